```python
import math
import jax
import jax.numpy as jnp
from jax import lax
import numpy as np

D_MODEL = 2048
BATCH = 16
SEQ = 256
DEPTH = 2
DEC_BATCH = 4
DEC_SEQ = 1024
PAST_LEN = 256

GRID_W = 64
NORM_EPS = 1e-6
ATT_HD = 128
ATT_WIDTH = D_MODEL // 2
ATT_HEADS = ATT_WIDTH // ATT_HD
ATT_KV_HEADS = ATT_HEADS // 4
ATT_GROUPS = ATT_HEADS // ATT_KV_HEADS
ATT_KV_WIDTH = ATT_KV_HEADS * ATT_HD
Q_BLOCK = 128
ROPE_THETA = 10000.0
M_HD = 128
M_WIDTH = D_MODEL // 4
M_HEADS = M_WIDTH // M_HD
M_CHUNK = 64
M_INIT = -1e30
R_HD = 64
R_WIDTH = D_MODEL // 4
R_HEADS = R_WIDTH // R_HD
LORA_W = 64
LORA_A = 64
LORA_G = 128
GN_EPS = 64e-5
D_MIX = ATT_WIDTH + M_WIDTH + R_WIDTH
ATT_IN = ATT_WIDTH + 2 * ATT_KV_WIDTH
M_IN = 4 * M_WIDTH + 4 * M_HEADS
R_IN = 3 * R_WIDTH + LORA_W + LORA_A + LORA_G
N_IN = ATT_IN + M_IN + R_IN
N_EXPERTS = 16
N_EXPERT_GROUPS = 4
EXPERTS_PER_GROUP = N_EXPERTS // N_EXPERT_GROUPS
TOP_K = 2
D_EXPERT = D_MODEL // 4
MOE_BLOCK = 128

kernel_name = "hybrid_gqa_mlstm_rwkv7_grouped_moe_dit_step"


def rmsnorm(x, g, eps=NORM_EPS):
    xf = x.astype(jnp.float32)
    y = xf * lax.rsqrt(jnp.mean(xf * xf, axis=-1, keepdims=True) + eps)
    return (y * g.astype(jnp.float32)).astype(x.dtype)


def axial_rope_tables(n_tokens):
    rows = n_tokens // GRID_W
    row = jnp.repeat(jnp.arange(rows), GRID_W).astype(jnp.float32)
    col = jnp.tile(jnp.arange(GRID_W), rows).astype(jnp.float32)
    n_freq = ATT_HD // 4
    inv_freq = ROPE_THETA ** (-jnp.arange(n_freq, dtype=jnp.float32) / n_freq)
    ang_r = row[:, None] * inv_freq[None, :]
    ang_c = col[:, None] * inv_freq[None, :]
    return (jnp.cos(ang_r), jnp.sin(ang_r), jnp.cos(ang_c), jnp.sin(ang_c))


def _rotate(x, cos, sin):
    x1, x2 = jnp.split(x, 2, axis=-1)
    return jnp.concatenate([x1 * cos - x2 * sin, x2 * cos + x1 * sin], axis=-1)


def apply_axial_rope(x, rope):
    cos_r, sin_r, cos_c, sin_c = rope
    shp = (1, x.shape[1]) + (1,) * (x.ndim - 3) + (cos_r.shape[-1],)
    xf = x.astype(jnp.float32)
    half = ATT_HD // 2
    xr = _rotate(xf[..., :half], cos_r.reshape(shp), sin_r.reshape(shp))
    xc = _rotate(xf[..., half:], cos_c.reshape(shp), sin_c.reshape(shp))
    return jnp.concatenate([xr, xc], axis=-1).astype(x.dtype)


def attend(q, k, v):
    b, s = q.shape[:2]
    nb = s // Q_BLOCK
    qb = q.reshape(b, nb, Q_BLOCK, ATT_KV_HEADS, ATT_GROUPS, ATT_HD).transpose(1, 0, 2, 3, 4, 5)
    scale = ATT_HD ** -0.5

    def one_block(q_blk):
        sc = jnp.einsum('bqkgd,bskd->bkgqs', q_blk, k).astype(jnp.float32) * scale
        p = jax.nn.softmax(sc, axis=-1).astype(v.dtype)
        return jnp.einsum('bkgqs,bskd->bqkgd', p, v)

    o = lax.map(one_block, qb)
    return o.transpose(1, 0, 2, 3, 4, 5).reshape(b, s, ATT_WIDTH)


def mlstm_chunkwise(q, k, v, ig, lf, c0, n0, m0):
    b, s, h, d = q.shape
    nc = s // M_CHUNK
    to_c4 = lambda t: t.reshape(b, nc, M_CHUNK, h, d).transpose(1, 0, 3, 2, 4)
    to_c3 = lambda t: t.reshape(b, nc, M_CHUNK, h).transpose(1, 0, 3, 2)
    mask = jnp.tril(jnp.ones((M_CHUNK, M_CHUNK), dtype=bool))

    def step(carry, inp):
        c_st, n_st, m_st = carry
        qc, kc, vc, ic, lfc = inp
        bcum = jnp.cumsum(lfc, axis=-1)
        dmat = jnp.where(mask, bcum[..., :, None] - bcum[..., None, :] + ic[..., None, :], -jnp.inf)
        inter = bcum + m_st[..., None]
        m_t = jnp.maximum(inter, jnp.max(dmat, axis=-1))
        w_intra = jnp.exp(dmat - m_t[..., None])
        w_inter = jnp.exp(inter - m_t)
        s_qk = jnp.einsum('bhtd,bhsd->bhts', qc, kc) * w_intra
        num = w_inter[..., None] * jnp.einsum('bhtd,bhde->bhte', qc, c_st) + jnp.einsum('bhts,bhse->bhte', s_qk, vc)
        den = w_inter * jnp.einsum('bhtd,bhd->bht', qc, n_st) + jnp.sum(s_qk, axis=-1)
        h_out = num / jnp.maximum(jnp.abs(den), jnp.exp(-m_t))[..., None]
        b_last = bcum[..., -1]
        g_s = b_last[..., None] - bcum + ic
        m_new = jnp.maximum(b_last + m_st, jnp.max(g_s, axis=-1))
        carry_decay = jnp.exp(b_last + m_st - m_new)
        w_s = jnp.exp(g_s - m_new[..., None])
        c_new = carry_decay[..., None, None] * c_st + jnp.einsum('bhs,bhsd,bhse->bhde', w_s, kc, vc)
        n_new = carry_decay[..., None] * n_st + jnp.einsum('bhs,bhsd->bhd', w_s, kc)
        return (c_new, n_new, m_new), h_out

    xs = (to_c4(q), to_c4(k), to_c4(v), to_c3(ig), to_c3(lf))
    fin, hs = lax.scan(step, (c0, n0, m0), xs)
    hs = hs.transpose(1, 0, 3, 2, 4).reshape(b, s, h, d)
    return hs, fin


def mlstm_mixer(u, i_bias, f_bias, norm_g, c0, n0, m0):
    b, s, _ = u.shape
    heads = lambda t: t.reshape(b, s, M_HEADS, M_HD).astype(jnp.float32)
    q = heads(u[..., :M_WIDTH]) * (M_HD ** -0.5)
    k = heads(u[..., M_WIDTH:2 * M_WIDTH])
    v = heads(u[..., 2 * M_WIDTH:3 * M_WIDTH])
    o = u[..., 3 * M_WIDTH:4 * M_WIDTH].astype(jnp.float32)
    gates = u[..., 4 * M_WIDTH:].reshape(b, s, 2, 2, M_HEADS).astype(jnp.float32)
    ig = gates[:, :, :, 0] + i_bias.astype(jnp.float32)
    lf = jax.nn.log_sigmoid(gates[:, :, :, 1] + f_bias.astype(jnp.float32))
    c0 = c0.astype(jnp.float32)
    n0 = n0.astype(jnp.float32)
    m0 = m0.astype(jnp.float32)
    flip = lambda t: jnp.flip(t, axis=1)
    h_f, (c_f, n_f, m_f) = mlstm_chunkwise(q, k, v, ig[:, :, 0], lf[:, :, 0], c0[:, 0], n0[:, 0], m0[:, 0])
    h_b, (c_b, n_b, m_b) = mlstm_chunkwise(flip(q), flip(k), flip(v), flip(ig[:, :, 1]), flip(lf[:, :, 1]),
                                           c0[:, 1], n0[:, 1], m0[:, 1])
    h = h_f + flip(h_b)
    h = h * lax.rsqrt(jnp.mean(h * h, axis=-1, keepdims=True) + NORM_EPS)
    h = h.reshape(b, s, M_WIDTH) * norm_g.astype(jnp.float32)
    out = h * jax.nn.sigmoid(o)
    return out.astype(u.dtype), (jnp.stack([c_f, c_b], axis=1), jnp.stack([n_f, n_b], axis=1),
                                 jnp.stack([m_f, m_b], axis=1))


def token_shift_bidir(x):
    prev = jnp.pad(x[:, :-1], ((0, 0), (1, 0), (0, 0)))
    nxt = jnp.pad(x[:, 1:], ((0, 0), (0, 1), (0, 0)))
    return 0.5 * (prev + nxt)


def rwkv_scan(s0, r, w, k, v, kk, a, reverse):
    xs = tuple(jnp.moveaxis(t, 1, 0) for t in (r, w, k, v, kk, a))

    def step(st, inp):
        r_t, w_t, k_t, v_t, kk_t, a_t = inp
        sa = jnp.einsum('bhvk,bhk->bhv', st, kk_t)
        st = st * w_t[:, :, None, :] - sa[..., None] * (kk_t * a_t)[:, :, None, :] + v_t[..., None] * k_t[:, :, None, :]
        return st, jnp.einsum('bhvk,bhk->bhv', st, r_t)

    s_fin, ys = lax.scan(step, s0, xs, reverse=reverse)
    return s_fin, jnp.moveaxis(ys, 0, 1)


def rwkv_mixer(u, mu, w0, w_up, a0, a_up, g_up, k_k, k_a, r_k, ln_g, ln_b, s0):
    b, s, _ = u.shape
    u = u + mu * (token_shift_bidir(u) - u)
    uf = u.astype(jnp.float32)
    o1, o2, o3 = R_WIDTH, 2 * R_WIDTH, 3 * R_WIDTH
    o4, o5 = o3 + LORA_W, o3 + LORA_W + LORA_A
    r, k, v = uf[..., :o1], uf[..., o1:o2], uf[..., o2:o3]
    xw, xa, xg = uf[..., o3:o4], uf[..., o4:o5], uf[..., o5:]
    heads = lambda t: t.reshape(b, s, R_HEADS, R_HD)
    g = jax.nn.sigmoid(xg) @ g_up.astype(jnp.float32)
    kk = heads(k * k_k.astype(jnp.float32))
    kk = kk * lax.rsqrt(jnp.maximum(jnp.sum(kk * kk, axis=-1, keepdims=True), 1e-24))
    s0 = s0.astype(jnp.float32)
    ys, states = [], []
    for d, rev in ((0, False), (1, True)):
        wd = -jax.nn.softplus(-(w0[d].astype(jnp.float32) + jnp.tanh(xw) @ w_up[d].astype(jnp.float32))) - 0.5
        decay = jnp.exp(-jnp.exp(wd))
        ad = jax.nn.sigmoid(a0[d].astype(jnp.float32) + xa @ a_up[d].astype(jnp.float32))
        kd = k * (1.0 + (ad - 1.0) * k_a.astype(jnp.float32))
        s_fin, y = rwkv_scan(s0[:, d], heads(r), heads(decay), heads(kd), heads(v), kk, heads(ad), rev)
        ys.append(y)
        states.append(s_fin)
    y = ys[0] + ys[1]
    mean = jnp.mean(y, axis=-1, keepdims=True)
    var = jnp.mean(jnp.square(y - mean), axis=-1, keepdims=True)
    y = ((y - mean) * lax.rsqrt(var + GN_EPS)).reshape(b, s, R_WIDTH)
    y = y * ln_g.astype(jnp.float32) + ln_b.astype(jnp.float32)
    bonus = jnp.sum(heads(r) * heads(k) * r_k.astype(jnp.float32), axis=-1, keepdims=True) * heads(v)
    out = (y + bonus.reshape(b, s, R_WIDTH)) * g
    return out.astype(u.dtype), jnp.stack(states, axis=1)


def route(h, router_w, router_b):
    t = h.shape[0]
    s = jax.nn.sigmoid((h @ router_w).astype(jnp.float32))
    s_sel = s + router_b.astype(jnp.float32)
    grp = s_sel.reshape(t, N_EXPERT_GROUPS, EXPERTS_PER_GROUP)
    grp_score = jnp.sum(lax.top_k(grp, 2)[0], axis=-1)
    g_idx = lax.top_k(grp_score, 1)[1]
    gmask = jax.nn.one_hot(g_idx[:, 0], N_EXPERT_GROUPS, dtype=jnp.float32)
    in_group = jnp.sum(grp * gmask[:, :, None], axis=1)
    loc = lax.top_k(in_group, TOP_K)[1]
    idx = (g_idx * EXPERTS_PER_GROUP + loc).astype(jnp.int32)
    w = jnp.take_along_axis(s, idx, axis=1)
    w = w / jnp.sum(w, axis=-1, keepdims=True)
    return idx, w


def moe_ffn(h, router_w, router_b, w1, w3, w2):
    b, s, d = h.shape
    t = b * s
    ht = h.reshape(t, d)
    idx, gate = route(ht, router_w, router_b)
    n_assign = t * TOP_K
    flat_e = idx.reshape(-1)
    flat_tok = jnp.repeat(jnp.arange(t, dtype=jnp.int32), TOP_K)
    flat_gate = gate.reshape(-1)
    order = jnp.argsort(flat_e)
    e_sorted = flat_e[order]
    tok_sorted = flat_tok[order]
    gate_sorted = flat_gate[order]
    counts = jnp.bincount(flat_e, length=N_EXPERTS)
    padded = (counts + MOE_BLOCK - 1) // MOE_BLOCK * MOE_BLOCK
    pad_end = jnp.cumsum(padded)
    pad_start = pad_end - padded
    start = jnp.cumsum(counts) - counts
    dest = pad_start[e_sorted] + (jnp.arange(n_assign, dtype=jnp.int32) - start[e_sorted])
    n_rows = -(-n_assign // MOE_BLOCK) * MOE_BLOCK + N_EXPERTS * MOE_BLOCK
    n_blocks = n_rows // MOE_BLOCK
    row_tok = jnp.zeros((n_rows,), jnp.int32).at[dest].set(tok_sorted)
    block_e = jnp.minimum(jnp.searchsorted(pad_end, jnp.arange(n_blocks, dtype=jnp.int32) * MOE_BLOCK, side='right'),
                          N_EXPERTS - 1)
    xb = ht[row_tok].reshape(n_blocks, MOE_BLOCK, d)

    def expert_block(args):
        x_blk, e = args
        return (jax.nn.silu(x_blk @ w1[e]) * (x_blk @ w3[e])) @ w2[e]

    yb = lax.map(expert_block, (xb, block_e)).reshape(n_rows, d)
    contrib = yb[dest] * gate_sorted[:, None].astype(h.dtype)
    out = jax.ops.segment_sum(contrib, tok_sorted, num_segments=t)
    return out.reshape(b, s, d)


def trunk_layer(x, mod, rope, ctx_kv, state0, p):
    b, s, _ = x.shape
    sh1, sc1, g1, sh2, sc2, g2 = jnp.split(mod, 6, axis=-1)
    h = rmsnorm(x, p['norm1_g']) * (1 + sc1) + sh1
    u = h @ p['w_in']
    u_att, u_m, u_r = u[..., :ATT_IN], u[..., ATT_IN:ATT_IN + M_IN], u[..., ATT_IN + M_IN:]
    q = u_att[..., :ATT_WIDTH].reshape(b, s, ATT_KV_HEADS, ATT_GROUPS, ATT_HD)
    k = u_att[..., ATT_WIDTH:ATT_WIDTH + ATT_KV_WIDTH].reshape(b, s, ATT_KV_HEADS, ATT_HD)
    v = u_att[..., ATT_WIDTH + ATT_KV_WIDTH:].reshape(b, s, ATT_KV_HEADS, ATT_HD)
    q = rmsnorm(q, p['attn_q_norm'])
    k = rmsnorm(k, p['attn_k_norm'])
    if rope is not None:
        q = apply_axial_rope(q, rope)
        k = apply_axial_rope(k, rope)
    if ctx_kv is None:
        k_all, v_all = k, v
    else:
        k_all = jnp.concatenate([ctx_kv[0].astype(k.dtype), k], axis=1)
        v_all = jnp.concatenate([ctx_kv[1].astype(v.dtype), v], axis=1)
    att = attend(q, k_all, v_all)
    m_out, (c_st, n_st, m_st) = mlstm_mixer(u_m, p['mlstm_i_bias'], p['mlstm_f_bias'], p['mlstm_norm_g'],
                                            state0[0], state0[1], state0[2])
    r_out, r_st = rwkv_mixer(u_r, p['rwkv_mu'], p['rwkv_w0'], p['rwkv_w_up'], p['rwkv_a0'], p['rwkv_a_up'],
                             p['rwkv_g_up'], p['rwkv_k_k'], p['rwkv_k_a'], p['rwkv_r_k'], p['rwkv_ln_g'],
                             p['rwkv_ln_b'], state0[3])
    mix = jnp.concatenate([att, m_out.astype(x.dtype), r_out.astype(x.dtype)], axis=-1) @ p['w_out']
    x = x + g1 * mix
    h2 = rmsnorm(x, p['norm2_g']) * (1 + sc2) + sh2
    x = x + g2 * moe_ffn(h2, p['router_w'], p['router_b'], p['exp_w1'], p['exp_w3'], p['exp_w2'])
    return x, (k, v, c_st, n_st, m_st, r_st)


def setup_inputs(seed: int = 0) -> dict:
    key = jax.random.key(seed)
    ks = iter(jax.random.split(key, 48))

    def nrm(shape, scale):
        return scale * jax.random.normal(next(ks), shape, jnp.float32)

    def gain(shape):
        return 1.0 + nrm(shape, 0.02)

    return {
        'x_prompt': nrm((BATCH, SEQ, D_MODEL), 1.0),
        'x_sample': nrm((DEC_BATCH, DEC_SEQ, D_MODEL), 1.0),
        'c': nrm((DEC_BATCH, D_MODEL), 1.0),
        'cache_attn_k': nrm((DEC_BATCH, DEPTH, PAST_LEN, ATT_KV_HEADS, ATT_HD), 1.0),
        'cache_attn_v': nrm((DEC_BATCH, DEPTH, PAST_LEN, ATT_KV_HEADS, ATT_HD), 1.0),
        'state_mlstm_C': nrm((DEC_BATCH, DEPTH, 2, M_HEADS, M_HD, M_HD), 0.5),
        'state_mlstm_n': nrm((DEC_BATCH, DEPTH, 2, M_HEADS, M_HD), 0.5),
        'state_mlstm_m': nrm((DEC_BATCH, DEPTH, 2, M_HEADS), 1.0),
        'state_rwkv': nrm((DEC_BATCH, DEPTH, 2, R_HEADS, R_HD, R_HD), 0.3),
        'c_ctx': nrm((D_MODEL,), 1.0),
        'norm1_g': gain((DEPTH, D_MODEL)),
        'norm2_g': gain((DEPTH, D_MODEL)),
        'w_mod': nrm((DEPTH, D_MODEL, 6 * D_MODEL), 0.2 * D_MODEL ** -0.5),
        'b_mod': nrm((DEPTH, 6 * D_MODEL), 0.02),
        'w_in': nrm((DEPTH, D_MODEL, N_IN), D_MODEL ** -0.5),
        'w_out': nrm((DEPTH, D_MIX, D_MODEL), D_MIX ** -0.5),
        'attn_q_norm': gain((DEPTH, ATT_HD)),
        'attn_k_norm': gain((DEPTH, ATT_HD)),
        'mlstm_i_bias': nrm((DEPTH, 2, M_HEADS), 0.1),
        'mlstm_f_bias': 3.0 + nrm((DEPTH, 2, M_HEADS), 0.5),
        'mlstm_norm_g': gain((DEPTH, M_WIDTH)),
        'rwkv_mu': jax.random.uniform(next(ks), (DEPTH, R_IN), jnp.float32),
        'rwkv_w0': nrm((DEPTH, 2, R_WIDTH), 0.5),
        'rwkv_w_up': nrm((DEPTH, 2, LORA_W, R_WIDTH), 0.5 * LORA_W ** -0.5),
        'rwkv_a0': nrm((DEPTH, 2, R_WIDTH), 0.5),
        'rwkv_a_up': nrm((DEPTH, 2, LORA_A, R_WIDTH), 0.5 * LORA_A ** -0.5),
        'rwkv_g_up': nrm((DEPTH, LORA_G, R_WIDTH), LORA_G ** -0.5),
        'rwkv_k_k': 0.85 + nrm((DEPTH, R_WIDTH), 0.05),
        'rwkv_k_a': 1.0 + nrm((DEPTH, R_WIDTH), 0.05),
        'rwkv_r_k': nrm((DEPTH, R_HEADS, R_HD), 0.1),
        'rwkv_ln_g': gain((DEPTH, R_WIDTH)),
        'rwkv_ln_b': nrm((DEPTH, R_WIDTH), 0.02),
        'router_w': nrm((D_MODEL, N_EXPERTS), D_MODEL ** -0.5),
        'router_b': nrm((N_EXPERTS,), 0.01),
        'exp_w1': nrm((DEPTH, N_EXPERTS, D_MODEL, D_EXPERT), D_MODEL ** -0.5),
        'exp_w3': nrm((DEPTH, N_EXPERTS, D_MODEL, D_EXPERT), D_MODEL ** -0.5),
        'exp_w2': nrm((DEPTH, N_EXPERTS, D_EXPERT, D_MODEL), D_EXPERT ** -0.5),
    }


def reference(x_prompt, x_sample, c, cache_attn_k, cache_attn_v, state_mlstm_C, state_mlstm_n, state_mlstm_m,
              state_rwkv, c_ctx, norm1_g, norm2_g, w_mod, b_mod, w_in, w_out, attn_q_norm, attn_k_norm,
              mlstm_i_bias, mlstm_f_bias, mlstm_norm_g, rwkv_mu, rwkv_w0, rwkv_w_up, rwkv_a0, rwkv_a_up,
              rwkv_g_up, rwkv_k_k, rwkv_k_a, rwkv_r_k, rwkv_ln_g, rwkv_ln_b, router_w, router_b,
              exp_w1, exp_w3, exp_w2):
    b_ctx = x_prompt.shape[0]
    rope_lat = axial_rope_tables(x_sample.shape[1])
    ctx_state0 = (jnp.zeros((b_ctx, 2, M_HEADS, M_HD, M_HD), jnp.float32),
                  jnp.zeros((b_ctx, 2, M_HEADS, M_HD), jnp.float32),
                  jnp.full((b_ctx, 2, M_HEADS), M_INIT, jnp.float32),
                  jnp.zeros((b_ctx, 2, R_HEADS, R_HD, R_HD), jnp.float32))
    x_ctx, x_lat = x_prompt, x_sample
    ks, vs, cs, ns, ms, rs = [], [], [], [], [], []
    for l in range(DEPTH):
        p = dict(norm1_g=norm1_g[l], norm2_g=norm2_g[l], w_in=w_in[l], w_out=w_out[l],
                 attn_q_norm=attn_q_norm[l], attn_k_norm=attn_k_norm[l],
                 mlstm_i_bias=mlstm_i_bias[l], mlstm_f_bias=mlstm_f_bias[l], mlstm_norm_g=mlstm_norm_g[l],
                 rwkv_mu=rwkv_mu[l], rwkv_w0=rwkv_w0[l], rwkv_w_up=rwkv_w_up[l], rwkv_a0=rwkv_a0[l],
                 rwkv_a_up=rwkv_a_up[l], rwkv_g_up=rwkv_g_up[l], rwkv_k_k=rwkv_k_k[l], rwkv_k_a=rwkv_k_a[l],
                 rwkv_r_k=rwkv_r_k[l], rwkv_ln_g=rwkv_ln_g[l], rwkv_ln_b=rwkv_ln_b[l],
                 router_w=router_w, router_b=router_b,
                 exp_w1=exp_w1[l], exp_w3=exp_w3[l], exp_w2=exp_w2[l])
        mod_ctx = (jax.nn.silu(c_ctx) @ w_mod[l] + b_mod[l])[None, None, :]
        mod_lat = (jax.nn.silu(c) @ w_mod[l] + b_mod[l])[:, None, :]
        x_ctx, (k_c, v_c, c_c, n_c, m_c, r_c) = trunk_layer(x_ctx, mod_ctx, None, None, ctx_state0, p)
        ks.append(k_c)
        vs.append(v_c)
        cs.append(c_c)
        ns.append(n_c)
        ms.append(m_c)
        rs.append(r_c)
        lat_state0 = (state_mlstm_C[:, l], state_mlstm_n[:, l], state_mlstm_m[:, l], state_rwkv[:, l])
        x_lat, _ = trunk_layer(x_lat, mod_lat, rope_lat, (cache_attn_k[:, l], cache_attn_v[:, l]), lat_state0, p)
    new_attn_k = jnp.stack(ks, axis=1)
    new_attn_v = jnp.stack(vs, axis=1)
    new_mlstm_C = jnp.stack(cs, axis=1)
    new_mlstm_n = jnp.stack(ns, axis=1)
    new_mlstm_m = jnp.stack(ms, axis=1)
    new_rwkv = jnp.stack(rs, axis=1)
    return (x_ctx, x_lat, new_attn_k, new_attn_v, new_mlstm_C, new_mlstm_n, new_mlstm_m, new_rwkv)
```

```python
import functools

import jax
import jax.numpy as jnp
from jax import lax
from jax.experimental import pallas as pl
from jax.experimental.pallas import tpu as pltpu

F32 = jnp.float32
BF16 = jnp.bfloat16
I32 = jnp.int32

NORM_EPS = 1e-6
GN_EPS = 64e-5
M_INIT = -1e30
GRID_W = 64
ROPE_THETA = 10000.0
ATT_HD = 128
ATT_GROUPS = 4
ATT_KV_HEADS = 2
M_HD = 128
M_HEADS = 4
R_HD = 64
R_HEADS = 8
N_EXPERTS = 16
N_EXPERT_GROUPS = 4
EXPERTS_PER_GROUP = 4
CHUNK = 64
LANES = 128
EXPERT_ROWS = 256
VMEM_LIMIT = 56 * 1024 * 1024

COL_ATT = 0
COL_M = 1536
COL_R = 3584
COL_GATE = 5376
N_IN_PAD = 5632


def _cparams(sem):
    return pltpu.CompilerParams(dimension_semantics=sem, vmem_limit_bytes=VMEM_LIMIT)


def _dot(a, b):
    return jnp.dot(a.astype(BF16), b.astype(BF16), preferred_element_type=F32)


def _dg(a, b, dims):
    return lax.dot_general(a, b, (dims, ((), ())), preferred_element_type=F32)


_NN = ((1,), (0,))
_NT = ((1,), (1,))
_TN = ((0,), (0,))


def _split(a):
    hi = a.astype(BF16)
    lo = (a - hi.astype(F32)).astype(BF16)
    return hi, lo


def _dot3(a, b, dims=_NN):
    ah, al = _split(a)
    bh, bl = _split(b)
    return _dg(ah, bh, dims) + (_dg(ah, bl, dims) + _dg(al, bh, dims))


def _dot1(a, b, dims=_NN):
    return _dg(a.astype(BF16), b.astype(BF16), dims)


def _rms(x, g):
    return x * lax.rsqrt(jnp.mean(x * x, axis=-1, keepdims=True) + NORM_EPS) * g


def _sigmoid(x):
    return 1.0 / (1.0 + jnp.exp(-x))


def _mod_kernel(c_ref, w_ref, b_ref, o_ref):
    c = c_ref[...]
    o_ref[...] = _dot(c * _sigmoid(c), w_ref[...]) + b_ref[...]


def _modulation(c_all, w_mod, b_mod):
    depth, d, n = w_mod.shape
    tn = 1024
    return pl.pallas_call(
        _mod_kernel,
        grid=(depth, n // tn),
        in_specs=[pl.BlockSpec((8, d), lambda l, j: (0, 0)),
                  pl.BlockSpec((None, d, tn), lambda l, j: (l, 0, j)),
                  pl.BlockSpec((None, 1, tn), lambda l, j: (l, 0, j))],
        out_specs=pl.BlockSpec((None, 8, tn), lambda l, j: (l, 0, j)),
        out_shape=jax.ShapeDtypeStruct((depth, 8, n), F32),
        compiler_params=_cparams(("arbitrary", "arbitrary")),
        name="modulation",
    )(c_all, w_mod, b_mod.reshape(depth, 1, n))


def _mod_row(tok0, t_ctx, s_lat):
    return jnp.where(tok0 < t_ctx, 0, 1 + (tok0 - t_ctx) // s_lat)


def _in_kernel(x_ref, g_ref, sh_ref, sc_ref, w_ref, o_ref, h_ref):
    @pl.when(pl.program_id(1) == 0)
    def _():
        h = _rms(x_ref[...], g_ref[...]) * (1.0 + sc_ref[...]) + sh_ref[...]
        h_ref[...] = h.astype(BF16)

    o_ref[...] = jnp.dot(h_ref[...], w_ref[...], preferred_element_type=F32)


def _in_proj(x, g1, mod3, w_p, t_ctx, s_lat):
    ntok, d = x.shape
    n = w_p.shape[1]
    tm, tn = 512, 512
    row = lambda i: _mod_row(i * tm, t_ctx, s_lat)
    return pl.pallas_call(
        _in_kernel,
        grid=(ntok // tm, n // tn),
        in_specs=[pl.BlockSpec((tm, d), lambda i, j: (i, 0)),
                  pl.BlockSpec((1, d), lambda i, j: (0, 0)),
                  pl.BlockSpec((None, 1, d), lambda i, j: (row(i), 0, 0)),
                  pl.BlockSpec((None, 1, d), lambda i, j: (row(i), 0, 1)),
                  pl.BlockSpec((d, tn), lambda i, j: (0, j))],
        out_specs=pl.BlockSpec((tm, tn), lambda i, j: (i, j)),
        out_shape=jax.ShapeDtypeStruct((ntok, n), F32),
        scratch_shapes=[pltpu.VMEM((tm, d), BF16)],
        compiler_params=_cparams(("arbitrary", "arbitrary")),
        name="in_proj",
    )(x, g1, mod3, mod3, w_p)


def _softmax_av(q, kb, vb):
    s = _dg(q.astype(BF16), kb, _NT) * (ATT_HD ** -0.5)
    p = jnp.exp(s - jnp.max(s, axis=-1, keepdims=True))
    l = jnp.sum(p, axis=-1, keepdims=True)
    return jnp.dot(p.astype(BF16), vb, preferred_element_type=F32) / l


def _att_ctx_kernel(q_ref, k_ref, v_ref, qn_ref, kn_ref, alias_ref, o_ref, ko_ref):
    del alias_ref
    k = _rms(k_ref[...], kn_ref[...])
    ko_ref[...] = k
    kb = k.astype(BF16)
    vb = v_ref[...].astype(BF16)
    for g in range(ATT_GROUPS):
        q = _rms(q_ref[:, g * ATT_HD:(g + 1) * ATT_HD], qn_ref[...])
        o_ref[:, g * ATT_HD:(g + 1) * ATT_HD] = _softmax_av(q, kb, vb)


def _rope(x, cos, sin):
    lane = lax.broadcasted_iota(I32, x.shape, 1)
    first = (lane % (ATT_HD // 2)) < (ATT_HD // 4)
    partner = jnp.where(first, pltpu.roll(x, ATT_HD - ATT_HD // 4, 1), pltpu.roll(x, ATT_HD // 4, 1))
    return x * cos + partner * sin


def _att_lat_kernel(q_ref, k_ref, v_ref, ck_ref, cv_ref, qn_ref, kn_ref, cosq_ref, sinq_ref, cosk_ref, sink_ref,
                    alias_ref, o_ref, kb_ref, vb_ref, *, past):
    del alias_ref

    @pl.when(pl.program_id(2) == 0)
    def _():
        k = _rope(_rms(k_ref[...], kn_ref[...]), cosk_ref[...], sink_ref[...])
        kb_ref[0:past, :] = ck_ref[...].astype(BF16)
        kb_ref[past:, :] = k.astype(BF16)
        vb_ref[0:past, :] = cv_ref[...].astype(BF16)
        vb_ref[past:, :] = v_ref[...].astype(BF16)

    kb = kb_ref[...]
    vb = vb_ref[...]
    for g in range(ATT_GROUPS):
        q = _rope(_rms(q_ref[:, g * ATT_HD:(g + 1) * ATT_HD], qn_ref[...]), cosq_ref[...], sinq_ref[...])
        o_ref[:, g * ATT_HD:(g + 1) * ATT_HD] = _softmax_av(q, kb, vb)


def _attention(u, qn, kn, cache_k, cache_v, cos, sin, b_ctx, s_ctx, b_lat, s_lat):
    ntok = u.shape[0]
    t_ctx = b_ctx * s_ctx
    gw = ATT_GROUPS * ATT_HD
    kcol = (ATT_KV_HEADS * gw) // ATT_HD
    vcol = kcol + ATT_KV_HEADS
    att, k_ctx = pl.pallas_call(
        _att_ctx_kernel,
        grid=(b_ctx, ATT_KV_HEADS),
        in_specs=[pl.BlockSpec((s_ctx, gw), lambda b, h: (b, h)),
                  pl.BlockSpec((s_ctx, ATT_HD), lambda b, h: (b, kcol + h)),
                  pl.BlockSpec((s_ctx, ATT_HD), lambda b, h: (b, vcol + h)),
                  pl.BlockSpec((1, ATT_HD), lambda b, h: (0, 0)),
                  pl.BlockSpec((1, ATT_HD), lambda b, h: (0, 0)),
                  pl.BlockSpec(memory_space=pl.ANY)],
        out_specs=[pl.BlockSpec((s_ctx, gw), lambda b, h: (b, h)),
                   pl.BlockSpec((s_ctx, ATT_HD), lambda b, h: (b, h))],
        out_shape=[jax.ShapeDtypeStruct((ntok, ATT_KV_HEADS * gw), F32),
                   jax.ShapeDtypeStruct((t_ctx, ATT_KV_HEADS * ATT_HD), F32)],
        input_output_aliases={5: 0},
        compiler_params=_cparams(("arbitrary", "arbitrary")),
        name="att_ctx",
    )(u, u, u, qn, kn, jnp.zeros((ntok, ATT_KV_HEADS * gw), F32))

    tq = 256
    nqb = s_lat // tq
    past = cache_k.shape[1]
    qrow0 = t_ctx // tq
    krow0 = t_ctx // s_lat
    att = pl.pallas_call(
        functools.partial(_att_lat_kernel, past=past),
        grid=(b_lat, ATT_KV_HEADS, nqb),
        in_specs=[pl.BlockSpec((tq, gw), lambda b, h, i: (qrow0 + b * nqb + i, h)),
                  pl.BlockSpec((s_lat, ATT_HD), lambda b, h, i: (krow0 + b, kcol + h)),
                  pl.BlockSpec((s_lat, ATT_HD), lambda b, h, i: (krow0 + b, vcol + h)),
                  pl.BlockSpec((None, past, ATT_HD), lambda b, h, i: (b, 0, h)),
                  pl.BlockSpec((None, past, ATT_HD), lambda b, h, i: (b, 0, h)),
                  pl.BlockSpec((1, ATT_HD), lambda b, h, i: (0, 0)),
                  pl.BlockSpec((1, ATT_HD), lambda b, h, i: (0, 0)),
                  pl.BlockSpec((tq, ATT_HD), lambda b, h, i: (i, 0)),
                  pl.BlockSpec((tq, ATT_HD), lambda b, h, i: (i, 0)),
                  pl.BlockSpec((s_lat, ATT_HD), lambda b, h, i: (0, 0)),
                  pl.BlockSpec((s_lat, ATT_HD), lambda b, h, i: (0, 0)),
                  pl.BlockSpec(memory_space=pl.ANY)],
        out_specs=pl.BlockSpec((tq, gw), lambda b, h, i: (qrow0 + b * nqb + i, h)),
        out_shape=jax.ShapeDtypeStruct(att.shape, F32),
        scratch_shapes=[pltpu.VMEM((past + s_lat, ATT_HD), BF16), pltpu.VMEM((past + s_lat, ATT_HD), BF16)],
        input_output_aliases={11: 0},
        compiler_params=_cparams(("arbitrary", "arbitrary", "arbitrary")),
        name="att_lat",
    )(u, u, u, cache_k, cache_v, qn, kn, cos, sin, cos, sin, att)
    return att, k_ctx


def _rope_tables(n_tokens):
    pos = jnp.arange(n_tokens)
    row = (pos // GRID_W).astype(F32)
    col = (pos % GRID_W).astype(F32)
    n_freq = ATT_HD // 4
    inv_freq = ROPE_THETA ** (-jnp.arange(n_freq, dtype=F32) / n_freq)
    ang_r = row[:, None] * inv_freq[None, :]
    ang_c = col[:, None] * inv_freq[None, :]
    cos = jnp.concatenate([jnp.cos(ang_r), jnp.cos(ang_r), jnp.cos(ang_c), jnp.cos(ang_c)], axis=-1)
    sin = jnp.concatenate([-jnp.sin(ang_r), jnp.sin(ang_r), -jnp.sin(ang_c), jnp.sin(ang_c)], axis=-1)
    return cos, sin


def _log_sigmoid(x):
    return jnp.minimum(x, 0.0) - jnp.log1p(jnp.exp(-jnp.abs(x)))


def _mlstm_kernel(*refs, zero_init, seq):
    if zero_init:
        (q_ref, k_ref, v_ref, o_ref, gc_ref, gt_ref, b_ref, ng_ref, alias_ref,
         out_ref, co_ref, no_ref, mo_ref, h_scr, c_scr, n_scr, m_scr) = refs
    else:
        (q_ref, k_ref, v_ref, o_ref, gc_ref, gt_ref, b_ref, ng_ref, c0_ref, n0_ref, m0_ref, alias_ref,
         out_ref, co_ref, no_ref, mo_ref, h_scr, c_scr, n_scr, m_scr) = refs
    del alias_ref
    head = pl.program_id(1)
    nc = seq // CHUNK
    if zero_init:
        c_scr[...] = jnp.zeros(c_scr.shape, F32)
        n_scr[...] = jnp.zeros(n_scr.shape, F32)
        m_scr[...] = jnp.full(m_scr.shape, M_INIT, F32)
    else:
        c_scr[...] = c0_ref[...]
        n_scr[...] = n0_ref[...]
        m_scr[...] = m0_ref[...]

    ti = lax.broadcasted_iota(I32, (CHUNK, CHUNK), 0)
    si = lax.broadcasted_iota(I32, (CHUNK, CHUNK), 1)
    bias = b_ref[...]

    def chunk(c, carry):
        for d in (0, 1):
            cc = c if d == 0 else nc - 1 - c
            rows = pl.ds(pl.multiple_of(cc * CHUNK, CHUNK), CHUNK)
            causal = (si <= ti) if d == 0 else (si >= ti)
            causal_t = (ti <= si) if d == 0 else (ti >= si)
            q = q_ref[rows, :] * (M_HD ** -0.5)
            k = k_ref[rows, :]
            v = v_ref[rows, :]
            gcol = gc_ref[rows, :] + bias
            ig_c = gcol[:, 2 * d:2 * d + 1]
            lf_c = _log_sigmoid(gcol[:, 2 * d + 1:2 * d + 2])
            ig_r = gt_ref[cc, pl.ds(8 * d + head, 1), :] + bias[:, 2 * d:2 * d + 1]
            lf_r = _log_sigmoid(gt_ref[cc, pl.ds(8 * d + M_HEADS + head, 1), :] + bias[:, 2 * d + 1:2 * d + 2])
            bcum_c = jnp.sum(jnp.where(causal, lf_r, 0.0), axis=1, keepdims=True)
            bcum_r = jnp.sum(jnp.where(causal_t, lf_c, 0.0), axis=0, keepdims=True)
            m_st = m_scr[d][:, 0:1]
            c_st = c_scr[d]
            n_st = n_scr[d]
            dmat = jnp.where(causal, bcum_c - bcum_r + ig_r, -jnp.inf)
            inter = bcum_c + m_st
            m_t = jnp.maximum(inter, jnp.max(dmat, axis=1, keepdims=True))
            w_intra = jnp.exp(dmat - m_t)
            w_inter = jnp.exp(inter - m_t)
            qb = q.astype(BF16)
            s_qk = _dg(qb, k.astype(BF16), _NT) * w_intra
            num = w_inter * _dg(qb, c_st.astype(BF16), _NN) + _dot(s_qk, v)
            den = w_inter * jnp.sum(q * n_st, axis=1, keepdims=True) + jnp.sum(s_qk, axis=1, keepdims=True)
            hv = num / jnp.maximum(jnp.abs(den), jnp.exp(-m_t))
            h_scr[d, rows, :] = hv
            b_last = bcum_c[CHUNK - 1:CHUNK, :] if d == 0 else bcum_c[0:1, :]
            g_c = b_last - bcum_c + ig_c
            m_new = jnp.maximum(b_last + m_st, jnp.max(g_c, axis=0, keepdims=True))
            decay = jnp.exp(b_last + m_st - m_new)
            kw = k * jnp.exp(g_c - m_new)
            c_scr[d] = decay * c_st + _dg(kw.astype(BF16), v.astype(BF16), _TN)
            n_scr[d] = decay * n_st + jnp.sum(kw, axis=0, keepdims=True)
            m_scr[d] = jnp.broadcast_to(m_new, (1, M_HD))
        return carry

    lax.fori_loop(0, nc, chunk, 0)
    h = h_scr[0] + h_scr[1]
    h = h * lax.rsqrt(jnp.mean(h * h, axis=-1, keepdims=True) + NORM_EPS) * ng_ref[...]
    out_ref[...] = h * _sigmoid(o_ref[...])
    co_ref[...] = c_scr[...]
    no_ref[...] = n_scr[...]
    mo_ref[...] = m_scr[...]


def _mlstm(u, gcol, gt, bias, norm_g, states, prev_out, row0, batch, seq):
    ntok = u.shape[0]
    zero_init = states is None
    qc = COL_M // M_HD
    blk = lambda off: pl.BlockSpec((seq, M_HD), lambda b, h: (row0 + b, off + h))
    in_specs = [blk(qc), blk(qc + M_HEADS), blk(qc + 2 * M_HEADS), blk(qc + 3 * M_HEADS),
                pl.BlockSpec((None, seq, 4), lambda b, h: (h, row0 + b, 0)),
                pl.BlockSpec((seq // CHUNK, 16, CHUNK), lambda b, h: (row0 + b, 0, 0)),
                pl.BlockSpec((None, 1, 4), lambda b, h: (h, 0, 0)),
                pl.BlockSpec((1, M_HD), lambda b, h: (0, h))]
    args = [u, u, u, u, gcol, gt, bias, norm_g]
    if not zero_init:
        in_specs += [pl.BlockSpec((None, 2, None, M_HD, M_HD), lambda b, h: (b, 0, h, 0, 0)),
                     pl.BlockSpec((None, 2, None, 1, M_HD), lambda b, h: (b, 0, h, 0, 0)),
                     pl.BlockSpec((None, 2, None, 1, M_HD), lambda b, h: (b, 0, h, 0, 0))]
        args += list(states)
    in_specs.append(pl.BlockSpec(memory_space=pl.ANY))
    args.append(prev_out)
    return pl.pallas_call(
        functools.partial(_mlstm_kernel, zero_init=zero_init, seq=seq),
        grid=(batch, M_HEADS),
        in_specs=in_specs,
        out_specs=[pl.BlockSpec((seq, M_HD), lambda b, h: (row0 + b, h)),
                   pl.BlockSpec((None, 2, None, M_HD, M_HD), lambda b, h: (b, 0, h, 0, 0)),
                   pl.BlockSpec((None, 2, None, 1, M_HD), lambda b, h: (b, 0, h, 0, 0)),
                   pl.BlockSpec((None, 2, None, 1, M_HD), lambda b, h: (b, 0, h, 0, 0))],
        out_shape=[jax.ShapeDtypeStruct((ntok, M_HEADS * M_HD), F32),
                   jax.ShapeDtypeStruct((batch, 2, M_HEADS, M_HD, M_HD), F32),
                   jax.ShapeDtypeStruct((batch, 2, M_HEADS, 1, M_HD), F32),
                   jax.ShapeDtypeStruct((batch, 2, M_HEADS, 1, M_HD), F32)],
        scratch_shapes=[pltpu.VMEM((2, seq, M_HD), F32), pltpu.VMEM((2, M_HD, M_HD), F32),
                        pltpu.VMEM((2, 1, M_HD), F32), pltpu.VMEM((2, 1, M_HD), F32)],
        input_output_aliases={len(args) - 1: 0},
        compiler_params=_cparams(("arbitrary", "arbitrary")),
        name="mlstm_ctx" if zero_init else "mlstm_lat",
    )(*args)


def _softplus(x):
    return jnp.maximum(x, 0.0) + jnp.log1p(jnp.exp(-jnp.abs(x)))


def _rwkv_kernel(*refs, zero_init, seq):
    if zero_init:
        (r_ref, k_ref, v_ref, xl_ref, xg_ref, mur_ref, muk_ref, muv_ref, mul_ref, mug_ref,
         w0_ref, wup_ref, a0_ref, aup_ref, gup_ref, kk_ref, ka_ref, rk_ref, lng_ref, lnb_ref, alias_ref,
         out_ref, so_ref, r_scr, v_scr, kk_scr, g_scr, bonus_scr, lw_scr, kd_scr, b_scr, y_scr, s_scr) = refs
    else:
        (r_ref, k_ref, v_ref, xl_ref, xg_ref, mur_ref, muk_ref, muv_ref, mul_ref, mug_ref,
         w0_ref, wup_ref, a0_ref, aup_ref, gup_ref, kk_ref, ka_ref, rk_ref, lng_ref, lnb_ref, s0_ref, alias_ref,
         out_ref, so_ref, r_scr, v_scr, kk_scr, g_scr, bonus_scr, lw_scr, kd_scr, b_scr, y_scr, s_scr) = refs
    del alias_ref
    nc = seq // CHUNK
    hd = R_HD
    row = lax.broadcasted_iota(I32, (seq, LANES), 0)

    def tshift(x_ref, mu_ref):
        x = x_ref[...]
        prev = jnp.where(row == 0, 0.0, pltpu.roll(x, 1, 0))
        nxt = jnp.where(row == seq - 1, 0.0, pltpu.roll(x, seq - 1, 0))
        return x + mu_ref[...] * (0.5 * (prev + nxt) - x)

    r = tshift(r_ref, mur_ref)
    k = tshift(k_ref, muk_ref)
    v = tshift(v_ref, muv_ref)
    xl = tshift(xl_ref, mul_ref)
    xg = tshift(xg_ref, mug_ref)
    g = _dot(_sigmoid(xg), gup_ref[...])
    kkp = k * kk_ref[...]
    kk = jnp.concatenate(
        [kkp[:, i * hd:(i + 1) * hd]
         * lax.rsqrt(jnp.maximum(jnp.sum(kkp[:, i * hd:(i + 1) * hd] ** 2, axis=-1, keepdims=True), 1e-24))
         for i in range(2)], axis=1)
    tw = jnp.tanh(xl[:, 0:hd])
    xa = xl[:, hd:2 * hd]
    r_scr[...] = r
    v_scr[...] = v
    kk_scr[...] = kk
    g_scr[...] = g
    rkk = r * k * rk_ref[...]
    bonus_scr[...] = jnp.concatenate(
        [jnp.sum(rkk[:, i * hd:(i + 1) * hd], axis=-1, keepdims=True) * v[:, i * hd:(i + 1) * hd] for i in range(2)],
        axis=1)
    for d in (0, 1):
        wd = -_softplus(-(w0_ref[d] + _dot(tw, wup_ref[d]))) - 0.5
        ad = _sigmoid(a0_ref[d] + _dot(xa, aup_ref[d]))
        lw_scr[d] = -jnp.exp(wd)
        kd_scr[d] = k * (1.0 + (ad - 1.0) * ka_ref[...])
        b_scr[d] = kk * ad
    if zero_init:
        s_scr[...] = jnp.zeros(s_scr.shape, F32)
    else:
        s_scr[...] = s0_ref[...]

    ti = lax.broadcasted_iota(I32, (CHUNK, CHUNK), 0)
    si = lax.broadcasted_iota(I32, (CHUNK, CHUNK), 1)
    eye = (ti == si).astype(F32)

    def chunk(c, carry):
        for d in (0, 1):
            cc = c if d == 0 else nc - 1 - c
            rows = pl.ds(pl.multiple_of(cc * CHUNK, CHUNK), CHUNK)
            incl = (si <= ti) if d == 0 else (si >= ti)
            strict = (si < ti) if d == 0 else (si > ti)
            lw = lw_scr[d, rows, :]
            lw_hi, lw_lo = _split(lw)
            tri = jnp.where(incl, 1.0, 0.0).astype(BF16)
            lc = _dg(tri, lw_hi, _NN) + _dg(tri, lw_lo, _NN)
            l_last = lc[CHUNK - 1:CHUNK, :] if d == 0 else lc[0:1, :]
            e_neg = jnp.exp(-lc)
            e_end = jnp.exp(l_last - lc)
            rc = r_scr[rows, :]
            vc = v_scr[rows, :]
            kkc = kk_scr[rows, :]
            kdc = kd_scr[d, rows, :]
            bc = b_scr[d, rows, :]
            rt = rc * jnp.exp(lc)
            kkt = kkc * jnp.exp(lc - lw)
            kh = kdc * e_neg
            bh = bc * e_neg
            kbar = kdc * e_end
            bbar = bc * e_end
            w_end = jnp.exp(l_last)
            ys = []
            for i in range(2):
                sl = slice(i * hd, (i + 1) * hd)
                lhs = jnp.concatenate([kkt[:, sl], rt[:, sl]], axis=0)
                rhs = jnp.concatenate([bh[:, sl], kh[:, sl]], axis=0)
                ab = _dot3(lhs, rhs, _NT)
                a_kb = jnp.where(strict, ab[:CHUNK, :CHUNK], 0.0)
                a_kk = jnp.where(strict, ab[:CHUNK, CHUNK:], 0.0)
                b_rb = jnp.where(incl, ab[CHUNK:, :CHUNK], 0.0)
                b_rk = jnp.where(incl, ab[CHUNK:, CHUNK:], 0.0)
                p = _dot3(a_kb, a_kb)
                inv = eye - a_kb
                inv = inv + _dot3(inv, p)
                span = 4
                while span < CHUNK:
                    p = _dot1(p, p)
                    inv = inv + _dot1(inv, p)
                    span *= 2
                s0 = s_scr[d, i]
                proj = _dot3(lhs, s0, _NT)
                vv = vc[:, sl]
                abv = _dot3(jnp.concatenate([a_kk, b_rk], axis=0), vv)
                u = _dot3(inv, proj[:CHUNK] + abv[:CHUNK])
                ys.append(proj[CHUNK:] + abv[CHUNK:] - _dot3(b_rb, u))
                upd = _dot3(jnp.concatenate([vv, -u], axis=0),
                            jnp.concatenate([kbar[:, sl], bbar[:, sl]], axis=0), _TN)
                s_scr[d, i] = s0 * w_end[:, sl] + upd
            y_scr[d, rows, :] = jnp.concatenate(ys, axis=1)
        return carry

    lax.fori_loop(0, nc, chunk, 0)

    y = y_scr[0] + y_scr[1]
    outs = []
    for i in range(2):
        yh = y[:, i * hd:(i + 1) * hd]
        mean = jnp.mean(yh, axis=-1, keepdims=True)
        var = jnp.mean(jnp.square(yh - mean), axis=-1, keepdims=True)
        outs.append((yh - mean) * lax.rsqrt(var + GN_EPS))
    yn = jnp.concatenate(outs, axis=1)
    out_ref[...] = (yn * lng_ref[...] + lnb_ref[...] + bonus_scr[...]) * g_scr[...]
    so_ref[...] = s_scr[...]


def _rwkv(u, p, state, prev_out, row0, batch, seq):
    ntok = u.shape[0]
    zero_init = state is None
    rc = COL_R // LANES
    nhp = R_HEADS // 2
    ublk = lambda off: pl.BlockSpec((seq, LANES), lambda b, h: (row0 + b, rc + off + h))
    ufix = lambda off: pl.BlockSpec((seq, LANES), lambda b, h: (row0 + b, rc + off))
    mblk = lambda off: pl.BlockSpec((1, LANES), lambda b, h: (0, off + h))
    mfix = lambda off: pl.BlockSpec((1, LANES), lambda b, h: (0, off))
    vec = pl.BlockSpec((1, LANES), lambda b, h: (0, h))
    in_specs = [ublk(0), ublk(nhp), ublk(2 * nhp), ufix(3 * nhp), ufix(3 * nhp + 1),
                mblk(0), mblk(nhp), mblk(2 * nhp), mfix(3 * nhp), mfix(3 * nhp + 1),
                pl.BlockSpec((2, 1, LANES), lambda b, h: (0, 0, h)),
                pl.BlockSpec((2, R_HD, LANES), lambda b, h: (0, 0, h)),
                pl.BlockSpec((2, 1, LANES), lambda b, h: (0, 0, h)),
                pl.BlockSpec((2, R_HD, LANES), lambda b, h: (0, 0, h)),
                pl.BlockSpec((LANES, LANES), lambda b, h: (0, h)),
                vec, vec, vec, vec, vec]
    args = [u, u, u, u, u, p['mu'], p['mu'], p['mu'], p['mu'], p['mu'],
            p['w0'], p['w_up'], p['a0'], p['a_up'], p['g_up'], p['k_k'], p['k_a'], p['r_k'], p['ln_g'], p['ln_b']]
    if not zero_init:
        in_specs.append(pl.BlockSpec((None, 2, 2, R_HD, R_HD), lambda b, h: (b, 0, h, 0, 0)))
        args.append(state)
    in_specs.append(pl.BlockSpec(memory_space=pl.ANY))
    args.append(prev_out)
    big = lambda n: pltpu.VMEM((n, seq, LANES), F32)
    return pl.pallas_call(
        functools.partial(_rwkv_kernel, zero_init=zero_init, seq=seq),
        grid=(batch, nhp),
        in_specs=in_specs,
        out_specs=[pl.BlockSpec((seq, LANES), lambda b, h: (row0 + b, h)),
                   pl.BlockSpec((None, 2, 2, R_HD, R_HD), lambda b, h: (b, 0, h, 0, 0))],
        out_shape=[jax.ShapeDtypeStruct((ntok, R_HEADS * R_HD), F32),
                   jax.ShapeDtypeStruct((batch, 2, R_HEADS, R_HD, R_HD), F32)],
        scratch_shapes=[pltpu.VMEM((seq, LANES), F32)] * 5
                       + [big(2), big(2), big(2), big(2), pltpu.VMEM((2, 2, R_HD, R_HD), F32)],
        input_output_aliases={len(args) - 1: 0},
        compiler_params=_cparams(("arbitrary", "arbitrary")),
        name="rwkv_ctx" if zero_init else "rwkv_lat",
    )(*args)


def _top2_sum(a, b, c, d):
    m1, n1 = jnp.maximum(a, b), jnp.minimum(a, b)
    m2, n2 = jnp.maximum(c, d), jnp.minimum(c, d)
    return jnp.maximum(m1, m2) + jnp.maximum(jnp.minimum(m1, m2), jnp.maximum(n1, n2))


def _first_argmax(vals):
    best = functools.reduce(jnp.maximum, vals)
    idx = jnp.full(best.shape, len(vals) - 1, I32)
    for j in range(len(vals) - 2, -1, -1):
        idx = jnp.where(vals[j] == best, j, idx)
    return best, idx


def _out_kernel(att_ref, m_ref, r_ref, x_ref, w_ref, g1_ref, sh2_ref, sc2_ref, n2_ref, rw_ref, rb_ref,
                x1_ref, h2_ref, route_ref):
    na = att_ref.shape[1]
    nm = m_ref.shape[1]
    mix = (jnp.dot(att_ref[...].astype(BF16), w_ref[0:na, :], preferred_element_type=F32)
           + jnp.dot(m_ref[...].astype(BF16), w_ref[na:na + nm, :], preferred_element_type=F32)
           + jnp.dot(r_ref[...].astype(BF16), w_ref[na + nm:, :], preferred_element_type=F32))
    x1 = x_ref[...] + g1_ref[...] * mix
    x1_ref[...] = x1
    h2 = _rms(x1, n2_ref[...]) * (1.0 + sc2_ref[...]) + sh2_ref[...]
    h2_ref[...] = h2
    logits = _dot3(rw_ref[...], h2, _NT)
    s = _sigmoid(logits)
    ssel = s + rb_ref[...]
    srow = [s[e:e + 1, :] for e in range(N_EXPERTS)]
    brow = [ssel[e:e + 1, :] for e in range(N_EXPERTS)]
    gscore = [_top2_sum(*brow[EXPERTS_PER_GROUP * g:EXPERTS_PER_GROUP * (g + 1)]) for g in range(N_EXPERT_GROUPS)]
    _, gidx = _first_argmax(gscore)
    pick = lambda rows, j: functools.reduce(
        lambda acc, g: jnp.where(gidx == g, rows[EXPERTS_PER_GROUP * g + j], acc),
        range(N_EXPERT_GROUPS - 2, -1, -1), rows[EXPERTS_PER_GROUP * (N_EXPERT_GROUPS - 1) + j])
    ing = [pick(brow, j) for j in range(EXPERTS_PER_GROUP)]
    sin_ = [pick(srow, j) for j in range(EXPERTS_PER_GROUP)]
    _, l1 = _first_argmax(ing)
    _, l2 = _first_argmax([jnp.where(l1 == j, -jnp.inf, ing[j]) for j in range(EXPERTS_PER_GROUP)])
    sel = lambda l: functools.reduce(lambda acc, j: jnp.where(l == j, sin_[j], acc),
                                     range(EXPERTS_PER_GROUP - 2, -1, -1), sin_[EXPERTS_PER_GROUP - 1])
    w1, w2 = sel(l1), sel(l2)
    tot = w1 + w2
    e1 = (gidx * EXPERTS_PER_GROUP + l1).astype(F32)
    e2 = (gidx * EXPERTS_PER_GROUP + l2).astype(F32)
    zero = jnp.zeros_like(w1)
    route_ref[...] = jnp.concatenate([e1, e2, w1 / tot, w2 / tot, zero, zero, zero, zero], axis=0)


def _out_proj(att, m_out, r_out, x, w_out, mod3, n2g, rw_t, rb, t_ctx, s_lat):
    ntok, d = x.shape
    tm = 256
    row = lambda i: _mod_row(i * tm, t_ctx, s_lat)
    modblk = lambda j: pl.BlockSpec((None, 1, d), lambda i: (row(i), 0, j))
    return pl.pallas_call(
        _out_kernel,
        grid=(ntok // tm,),
        in_specs=[pl.BlockSpec((tm, att.shape[1]), lambda i: (i, 0)),
                  pl.BlockSpec((tm, m_out.shape[1]), lambda i: (i, 0)),
                  pl.BlockSpec((tm, r_out.shape[1]), lambda i: (i, 0)),
                  pl.BlockSpec((tm, d), lambda i: (i, 0)),
                  pl.BlockSpec(w_out.shape, lambda i: (0, 0)),
                  modblk(2), modblk(3), modblk(4),
                  pl.BlockSpec((1, d), lambda i: (0, 0)),
                  pl.BlockSpec(rw_t.shape, lambda i: (0, 0)),
                  pl.BlockSpec(rb.shape, lambda i: (0, 0))],
        out_specs=[pl.BlockSpec((tm, d), lambda i: (i, 0)),
                   pl.BlockSpec((tm, d), lambda i: (i, 0)),
                   pl.BlockSpec((8, tm), lambda i: (0, i))],
        out_shape=[jax.ShapeDtypeStruct((ntok, d), F32), jax.ShapeDtypeStruct((ntok, d), F32),
                   jax.ShapeDtypeStruct((8, ntok), F32)],
        compiler_params=_cparams(("arbitrary",)),
        name="out_proj",
    )(att, m_out, r_out, x, w_out, mod3, mod3, mod3, n2g, rw_t, rb)


def _expert_kernel(be_ref, tok_ref, nused_ref, h_hbm, w1_ref, w3_ref, w2_ref, y_ref, xbuf, w1b, w3b, w2b, sem):
    i = pl.program_id(0)

    @pl.when(i < nused_ref[0])
    def _():
        base = i * EXPERT_ROWS

        def issue(j, c):
            pltpu.make_async_copy(h_hbm.at[pl.ds(tok_ref[base + j], 1), :], xbuf.at[pl.ds(j, 1), :], sem).start()
            return c

        lax.fori_loop(0, EXPERT_ROWS, issue, 0)

        @pl.when(jnp.logical_or(i == 0, be_ref[i] != be_ref[jnp.maximum(i - 1, 0)]))
        def _():
            w1b[...] = w1_ref[...].astype(BF16)
            w3b[...] = w3_ref[...].astype(BF16)
            w2b[...] = w2_ref[...].astype(BF16)

        pltpu.make_async_copy(h_hbm.at[pl.ds(0, EXPERT_ROWS), :], xbuf, sem).wait()
        xb = xbuf[...].astype(BF16)
        a = jnp.dot(xb, w1b[...], preferred_element_type=F32)
        b = jnp.dot(xb, w3b[...], preferred_element_type=F32)
        hmid = (a * _sigmoid(a)) * b
        y_ref[...] = jnp.dot(hmid.astype(BF16), w2b[...], preferred_element_type=F32)

    @pl.when(i >= nused_ref[0])
    def _():
        y_ref[...] = jnp.zeros(y_ref.shape, F32)


def _experts(h2, block_e, row_tok, n_used, w1, w3, w2):
    ntok, d = h2.shape
    de = w1.shape[2]
    n_rows = row_tok.shape[0]
    nb = n_rows // EXPERT_ROWS
    return pl.pallas_call(
        _expert_kernel,
        grid_spec=pltpu.PrefetchScalarGridSpec(
            num_scalar_prefetch=3,
            grid=(nb,),
            in_specs=[pl.BlockSpec(memory_space=pl.ANY),
                      pl.BlockSpec((None, d, de), lambda i, be, tok, nu: (be[i], 0, 0)),
                      pl.BlockSpec((None, d, de), lambda i, be, tok, nu: (be[i], 0, 0)),
                      pl.BlockSpec((None, de, d), lambda i, be, tok, nu: (be[i], 0, 0))],
            out_specs=pl.BlockSpec((EXPERT_ROWS, d), lambda i, be, tok, nu: (i, 0)),
            scratch_shapes=[pltpu.VMEM((EXPERT_ROWS, d), F32), pltpu.VMEM((d, de), BF16), pltpu.VMEM((d, de), BF16),
                            pltpu.VMEM((de, d), BF16), pltpu.SemaphoreType.DMA(())]),
        out_shape=jax.ShapeDtypeStruct((n_rows, d), F32),
        compiler_params=_cparams(("arbitrary",)),
        name="experts",
    )(block_e, row_tok, n_used, h2, w1, w3, w2)


def _combine_kernel(d1_ref, d2_ref, y_hbm, x1_ref, g2_ref, gate_ref, o_ref, ybuf, sem):
    tm = x1_ref.shape[0]
    base = pl.program_id(0) * tm

    def issue(j, c):
        pltpu.make_async_copy(y_hbm.at[pl.ds(d1_ref[base + j], 1), :], ybuf.at[0, pl.ds(j, 1), :], sem).start()
        pltpu.make_async_copy(y_hbm.at[pl.ds(d2_ref[base + j], 1), :], ybuf.at[1, pl.ds(j, 1), :], sem).start()
        return c

    lax.fori_loop(0, tm, issue, 0)
    pltpu.make_async_copy(y_hbm.at[pl.ds(0, tm), :], ybuf.at[0], sem).wait()
    pltpu.make_async_copy(y_hbm.at[pl.ds(0, tm), :], ybuf.at[1], sem).wait()
    gate = gate_ref[...]
    moe = ybuf[0] * gate[:, 0:1] + ybuf[1] * gate[:, 1:2]
    o_ref[...] = x1_ref[...] + g2_ref[...] * moe


def _combine(y, x1, mod3, gate, dest1, dest2, t_ctx, s_lat):
    ntok, d = x1.shape
    tm = 256
    row = lambda i: _mod_row(i * tm, t_ctx, s_lat)
    return pl.pallas_call(
        _combine_kernel,
        grid_spec=pltpu.PrefetchScalarGridSpec(
            num_scalar_prefetch=2,
            grid=(ntok // tm,),
            in_specs=[pl.BlockSpec(memory_space=pl.ANY),
                      pl.BlockSpec((tm, d), lambda i, a, b: (i, 0)),
                      pl.BlockSpec((None, 1, d), lambda i, a, b: (row(i), 0, 5)),
                      pl.BlockSpec((tm, 2), lambda i, a, b: (i, 0))],
            out_specs=pl.BlockSpec((tm, d), lambda i, a, b: (i, 0)),
            scratch_shapes=[pltpu.VMEM((2, tm, d), F32), pltpu.SemaphoreType.DMA(())]),
        out_shape=jax.ShapeDtypeStruct((ntok, d), F32),
        compiler_params=_cparams(("arbitrary",)),
        name="combine",
    )(dest1, dest2, y, x1, mod3, gate)


def _dispatch(route):
    ntok = route.shape[1]
    e = route[0:2].astype(I32).T
    gate = route[2:4].T
    flat_e = e.reshape(-1)
    n_assign = flat_e.shape[0]
    onehot = (flat_e[:, None] == jnp.arange(N_EXPERTS, dtype=I32)[None, :]).astype(I32)
    csum = jnp.cumsum(onehot, axis=0)
    pos = jnp.sum(csum * onehot, axis=1) - 1
    counts = csum[-1]
    padded = (counts + EXPERT_ROWS - 1) // EXPERT_ROWS * EXPERT_ROWS
    pad_end = jnp.cumsum(padded)
    pad_start = pad_end - padded
    dest = (pad_start[flat_e] + pos).astype(I32)
    n_rows = -(-n_assign // EXPERT_ROWS) * EXPERT_ROWS + N_EXPERTS * EXPERT_ROWS
    nb = n_rows // EXPERT_ROWS
    row_tok = jnp.zeros((n_rows,), I32).at[dest].set(jnp.arange(n_assign, dtype=I32) // 2)
    block_e = jnp.minimum(jnp.searchsorted(pad_end, jnp.arange(nb, dtype=I32) * EXPERT_ROWS, side='right'),
                          N_EXPERTS - 1).astype(I32)
    n_used = (pad_end[-1:] // EXPERT_ROWS).astype(I32)
    dest2 = dest.reshape(ntok, 2)
    return row_tok, block_e, n_used, gate, dest2[:, 0], dest2[:, 1]


def kernel(x_prompt, x_sample, c, cache_attn_k, cache_attn_v, state_mlstm_C, state_mlstm_n, state_mlstm_m, state_rwkv, c_ctx, norm1_g, norm2_g, w_mod, b_mod, w_in, w_out, attn_q_norm, attn_k_norm, mlstm_i_bias, mlstm_f_bias, mlstm_norm_g, rwkv_mu, rwkv_w0, rwkv_w_up, rwkv_a0, rwkv_a_up, rwkv_g_up, rwkv_k_k, rwkv_k_a, rwkv_r_k, rwkv_ln_g, rwkv_ln_b, router_w, router_b, exp_w1, exp_w3, exp_w2):
    b_ctx, s_ctx, d = x_prompt.shape
    b_lat, s_lat, _ = x_sample.shape
    depth = w_in.shape[0]
    t_ctx = b_ctx * s_ctx
    ntok = t_ctx + b_lat * s_lat
    assert b_lat + 1 <= 8 and s_lat % 512 == 0 and t_ctx % 512 == 0 and t_ctx % s_lat == 0
    past = cache_attn_k.shape[2]

    x = jnp.concatenate([x_prompt.reshape(t_ctx, d), x_sample.reshape(b_lat * s_lat, d)], axis=0)
    c_all = jnp.zeros((8, d), F32).at[0].set(c_ctx).at[1:1 + b_lat].set(c)
    mod = _modulation(c_all, w_mod, b_mod)
    cos, sin = _rope_tables(s_lat)
    rw_t = router_w.T
    rb = router_b.reshape(N_EXPERTS, 1)
    m_width = M_HEADS * M_HD
    r_width = R_HEADS * R_HD
    n_in = w_in.shape[2]
    gate_lo = COL_M + 4 * m_width

    ks, vs, cs, ns, ms, rs = [], [], [], [], [], []
    for l in range(depth):
        w = w_in[l]
        w_p = jnp.concatenate([w[:, :gate_lo], w[:, gate_lo + 4 * M_HEADS:], w[:, gate_lo:gate_lo + 4 * M_HEADS],
                               jnp.zeros((d, N_IN_PAD - n_in), F32)], axis=1).astype(BF16)
        mod3 = mod[l].reshape(8, 1, 6 * d)
        u = _in_proj(x, norm1_g[l][None], mod3, w_p, t_ctx, s_lat)

        ck = cache_attn_k[:, l].reshape(b_lat, past, ATT_KV_HEADS * ATT_HD)
        cv = cache_attn_v[:, l].reshape(b_lat, past, ATT_KV_HEADS * ATT_HD)
        att, k_ctx = _attention(u, attn_q_norm[l][None], attn_k_norm[l][None], ck, cv, cos, sin,
                                b_ctx, s_ctx, b_lat, s_lat)
        ks.append(k_ctx.reshape(b_ctx, s_ctx, ATT_KV_HEADS, ATT_HD))
        vcol = ATT_GROUPS * ATT_KV_HEADS * ATT_HD + ATT_KV_HEADS * ATT_HD
        vs.append(u[:t_ctx, vcol:vcol + ATT_KV_HEADS * ATT_HD].reshape(b_ctx, s_ctx, ATT_KV_HEADS, ATT_HD))

        gates = u[:, COL_GATE:COL_GATE + 4 * M_HEADS]
        gcol = gates.reshape(ntok, 2, 2, M_HEADS).transpose(3, 0, 1, 2).reshape(M_HEADS, ntok, 4)
        gt = gates.reshape(ntok // CHUNK, CHUNK, 4 * M_HEADS).transpose(0, 2, 1)
        bias = jnp.stack([mlstm_i_bias[l], mlstm_f_bias[l]], axis=1)
        bias = bias.transpose(2, 0, 1).reshape(M_HEADS, 1, 4)
        ng = mlstm_norm_g[l][None]
        m_out = jnp.zeros((ntok, m_width), F32)
        m_out, c_c, n_c, m_c = _mlstm(u, gcol, gt, bias, ng, None, m_out, 0, b_ctx, s_ctx)
        lat_states = (state_mlstm_C[:, l],
                      state_mlstm_n[:, l].reshape(b_lat, 2, M_HEADS, 1, M_HD),
                      jnp.broadcast_to(state_mlstm_m[:, l][..., None, None], (b_lat, 2, M_HEADS, 1, M_HD)))
        m_out, _, _, _ = _mlstm(u, gcol, gt, bias, ng, lat_states, m_out, t_ctx // s_lat, b_lat, s_lat)
        cs.append(c_c)
        ns.append(n_c.reshape(b_ctx, 2, M_HEADS, M_HD))
        ms.append(m_c[:, :, :, 0, 0])

        rp = dict(mu=rwkv_mu[l][None], w0=rwkv_w0[l].reshape(2, 1, r_width), w_up=rwkv_w_up[l],
                  a0=rwkv_a0[l].reshape(2, 1, r_width), a_up=rwkv_a_up[l], g_up=rwkv_g_up[l],
                  k_k=rwkv_k_k[l][None], k_a=rwkv_k_a[l][None], r_k=rwkv_r_k[l].reshape(1, r_width),
                  ln_g=rwkv_ln_g[l][None], ln_b=rwkv_ln_b[l][None])
        r_out = jnp.zeros((ntok, r_width), F32)
        r_out, r_c = _rwkv(u, rp, None, r_out, 0, b_ctx, s_ctx)
        r_out, _ = _rwkv(u, rp, state_rwkv[:, l], r_out, t_ctx // s_lat, b_lat, s_lat)
        rs.append(r_c)

        x1, h2, route = _out_proj(att, m_out, r_out, x, w_out[l].astype(BF16), mod3, norm2_g[l][None], rw_t, rb,
                                  t_ctx, s_lat)
        row_tok, block_e, n_used, gate, dest1, dest2 = _dispatch(route)
        y = _experts(h2, block_e, row_tok, n_used, exp_w1[l], exp_w3[l], exp_w2[l])
        x = _combine(y, x1, mod3, gate, dest1, dest2, t_ctx, s_lat)

    y_prompt = x[:t_ctx].reshape(b_ctx, s_ctx, d)
    y_sample = x[t_ctx:].reshape(b_lat, s_lat, d)
    return (y_prompt, y_sample, jnp.stack(ks, axis=1), jnp.stack(vs, axis=1), jnp.stack(cs, axis=1),
            jnp.stack(ns, axis=1), jnp.stack(ms, axis=1), jnp.stack(rs, axis=1))
```

```python
import functools

import jax
import jax.numpy as jnp
from jax import lax
from jax.experimental import pallas as pl
from jax.experimental.pallas import tpu as pltpu

F32 = jnp.float32
BF16 = jnp.bfloat16
I32 = jnp.int32

NORM_EPS = 1e-6
GN_EPS = 64e-5
M_INIT = -1e30
GRID_W = 64
ROPE_THETA = 10000.0
ATT_HD = 128
ATT_GROUPS = 4
ATT_KV_HEADS = 2
M_HD = 128
M_HEADS = 4
R_HD = 64
R_HEADS = 8
N_EXPERTS = 16
N_EXPERT_GROUPS = 4
EXPERTS_PER_GROUP = 4
CHUNK = 64
LANES = 128
EXPERT_ROWS = 256
VMEM_LIMIT = 56 * 1024 * 1024

COL_ATT = 0
COL_M = 1536
COL_R = 3584
COL_GATE = 5376
N_IN_PAD = 5632


def _cparams(sem):
    return pltpu.CompilerParams(dimension_semantics=sem, vmem_limit_bytes=VMEM_LIMIT)


def _dot(a, b):
    return jnp.dot(a.astype(BF16), b.astype(BF16), preferred_element_type=F32)


def _dg(a, b, dims):
    return lax.dot_general(a, b, (dims, ((), ())), preferred_element_type=F32)


_NN = ((1,), (0,))
_NT = ((1,), (1,))
_TN = ((0,), (0,))


def _split(a):
    hi = a.astype(BF16)
    lo = (a - hi.astype(F32)).astype(BF16)
    return hi, lo


def _dot3(a, b, dims=_NN):
    ah, al = _split(a)
    bh, bl = _split(b)
    return _dg(ah, bh, dims) + (_dg(ah, bl, dims) + _dg(al, bh, dims))


def _dot1(a, b, dims=_NN):
    return _dg(a.astype(BF16), b.astype(BF16), dims)


def _rms(x, g):
    return x * lax.rsqrt(jnp.mean(x * x, axis=-1, keepdims=True) + NORM_EPS) * g


def _sigmoid(x):
    return 1.0 / (1.0 + jnp.exp(-x))


def _mod_kernel(c_ref, w_ref, b_ref, o_ref):
    c = c_ref[...]
    o_ref[...] = _dot(c * _sigmoid(c), w_ref[...]) + b_ref[...]


def _modulation(c_all, w_mod, b_mod):
    depth, d, n = w_mod.shape
    tn = 1024
    return pl.pallas_call(
        _mod_kernel,
        grid=(depth, n // tn),
        in_specs=[pl.BlockSpec((8, d), lambda l, j: (0, 0)),
                  pl.BlockSpec((None, d, tn), lambda l, j: (l, 0, j)),
                  pl.BlockSpec((None, 1, tn), lambda l, j: (l, 0, j))],
        out_specs=pl.BlockSpec((None, 8, tn), lambda l, j: (l, 0, j)),
        out_shape=jax.ShapeDtypeStruct((depth, 8, n), F32),
        compiler_params=_cparams(("arbitrary", "arbitrary")),
        name="modulation",
    )(c_all, w_mod, b_mod.reshape(depth, 1, n))


def _mod_row(tok0, t_ctx, s_lat):
    return jnp.where(tok0 < t_ctx, 0, 1 + (tok0 - t_ctx) // s_lat)


def _in_kernel(x_ref, g_ref, sh_ref, sc_ref, w_ref, o_ref, h_ref):
    @pl.when(pl.program_id(1) == 0)
    def _():
        h = _rms(x_ref[...], g_ref[...]) * (1.0 + sc_ref[...]) + sh_ref[...]
        h_ref[...] = h.astype(BF16)

    o_ref[...] = jnp.dot(h_ref[...], w_ref[...], preferred_element_type=F32)


def _in_proj(x, g1, mod3, w_p, t_ctx, s_lat):
    ntok, d = x.shape
    n = w_p.shape[1]
    tm, tn = 512, 512
    row = lambda i: _mod_row(i * tm, t_ctx, s_lat)
    return pl.pallas_call(
        _in_kernel,
        grid=(ntok // tm, n // tn),
        in_specs=[pl.BlockSpec((tm, d), lambda i, j: (i, 0)),
                  pl.BlockSpec((1, d), lambda i, j: (0, 0)),
                  pl.BlockSpec((None, 1, d), lambda i, j: (row(i), 0, 0)),
                  pl.BlockSpec((None, 1, d), lambda i, j: (row(i), 0, 1)),
                  pl.BlockSpec((d, tn), lambda i, j: (0, j))],
        out_specs=pl.BlockSpec((tm, tn), lambda i, j: (i, j)),
        out_shape=jax.ShapeDtypeStruct((ntok, n), F32),
        scratch_shapes=[pltpu.VMEM((tm, d), BF16)],
        compiler_params=_cparams(("arbitrary", "arbitrary")),
        name="in_proj",
    )(x, g1, mod3, mod3, w_p)


def _softmax_av(q, kb, vb):
    s = _dg(q.astype(BF16), kb, _NT) * (ATT_HD ** -0.5)
    p = jnp.exp(s - jnp.max(s, axis=-1, keepdims=True))
    l = jnp.sum(p, axis=-1, keepdims=True)
    return jnp.dot(p.astype(BF16), vb, preferred_element_type=F32) / l


def _att_ctx_kernel(q_ref, k_ref, v_ref, qn_ref, kn_ref, alias_ref, o_ref, ko_ref):
    del alias_ref
    k = _rms(k_ref[...], kn_ref[...])
    ko_ref[...] = k
    kb = k.astype(BF16)
    vb = v_ref[...].astype(BF16)
    for g in range(ATT_GROUPS):
        q = _rms(q_ref[:, g * ATT_HD:(g + 1) * ATT_HD], qn_ref[...])
        o_ref[:, g * ATT_HD:(g + 1) * ATT_HD] = _softmax_av(q, kb, vb)


def _rope(x, cos, sin):
    lane = lax.broadcasted_iota(I32, x.shape, 1)
    first = (lane % (ATT_HD // 2)) < (ATT_HD // 4)
    partner = jnp.where(first, pltpu.roll(x, ATT_HD - ATT_HD // 4, 1), pltpu.roll(x, ATT_HD // 4, 1))
    return x * cos + partner * sin


def _att_lat_kernel(q_ref, k_ref, v_ref, ck_ref, cv_ref, qn_ref, kn_ref, cosq_ref, sinq_ref, cosk_ref, sink_ref,
                    alias_ref, o_ref, kb_ref, vb_ref, *, past):
    del alias_ref

    @pl.when(pl.program_id(2) == 0)
    def _():
        k = _rope(_rms(k_ref[...], kn_ref[...]), cosk_ref[...], sink_ref[...])
        kb_ref[0:past, :] = ck_ref[...].astype(BF16)
        kb_ref[past:, :] = k.astype(BF16)
        vb_ref[0:past, :] = cv_ref[...].astype(BF16)
        vb_ref[past:, :] = v_ref[...].astype(BF16)

    kb = kb_ref[...]
    vb = vb_ref[...]
    for g in range(ATT_GROUPS):
        q = _rope(_rms(q_ref[:, g * ATT_HD:(g + 1) * ATT_HD], qn_ref[...]), cosq_ref[...], sinq_ref[...])
        o_ref[:, g * ATT_HD:(g + 1) * ATT_HD] = _softmax_av(q, kb, vb)


def _attention(u, qn, kn, cache_k, cache_v, cos, sin, b_ctx, s_ctx, b_lat, s_lat):
    ntok = u.shape[0]
    t_ctx = b_ctx * s_ctx
    gw = ATT_GROUPS * ATT_HD
    kcol = (ATT_KV_HEADS * gw) // ATT_HD
    vcol = kcol + ATT_KV_HEADS
    att, k_ctx = pl.pallas_call(
        _att_ctx_kernel,
        grid=(b_ctx, ATT_KV_HEADS),
        in_specs=[pl.BlockSpec((s_ctx, gw), lambda b, h: (b, h)),
                  pl.BlockSpec((s_ctx, ATT_HD), lambda b, h: (b, kcol + h)),
                  pl.BlockSpec((s_ctx, ATT_HD), lambda b, h: (b, vcol + h)),
                  pl.BlockSpec((1, ATT_HD), lambda b, h: (0, 0)),
                  pl.BlockSpec((1, ATT_HD), lambda b, h: (0, 0)),
                  pl.BlockSpec(memory_space=pl.ANY)],
        out_specs=[pl.BlockSpec((s_ctx, gw), lambda b, h: (b, h)),
                   pl.BlockSpec((s_ctx, ATT_HD), lambda b, h: (b, h))],
        out_shape=[jax.ShapeDtypeStruct((ntok, ATT_KV_HEADS * gw), F32),
                   jax.ShapeDtypeStruct((t_ctx, ATT_KV_HEADS * ATT_HD), F32)],
        input_output_aliases={5: 0},
        compiler_params=_cparams(("arbitrary", "arbitrary")),
        name="att_ctx",
    )(u, u, u, qn, kn, jnp.zeros((ntok, ATT_KV_HEADS * gw), F32))

    tq = 256
    nqb = s_lat // tq
    past = cache_k.shape[1]
    qrow0 = t_ctx // tq
    krow0 = t_ctx // s_lat
    att = pl.pallas_call(
        functools.partial(_att_lat_kernel, past=past),
        grid=(b_lat, ATT_KV_HEADS, nqb),
        in_specs=[pl.BlockSpec((tq, gw), lambda b, h, i: (qrow0 + b * nqb + i, h)),
                  pl.BlockSpec((s_lat, ATT_HD), lambda b, h, i: (krow0 + b, kcol + h)),
                  pl.BlockSpec((s_lat, ATT_HD), lambda b, h, i: (krow0 + b, vcol + h)),
                  pl.BlockSpec((None, past, ATT_HD), lambda b, h, i: (b, 0, h)),
                  pl.BlockSpec((None, past, ATT_HD), lambda b, h, i: (b, 0, h)),
                  pl.BlockSpec((1, ATT_HD), lambda b, h, i: (0, 0)),
                  pl.BlockSpec((1, ATT_HD), lambda b, h, i: (0, 0)),
                  pl.BlockSpec((tq, ATT_HD), lambda b, h, i: (i, 0)),
                  pl.BlockSpec((tq, ATT_HD), lambda b, h, i: (i, 0)),
                  pl.BlockSpec((s_lat, ATT_HD), lambda b, h, i: (0, 0)),
                  pl.BlockSpec((s_lat, ATT_HD), lambda b, h, i: (0, 0)),
                  pl.BlockSpec(memory_space=pl.ANY)],
        out_specs=pl.BlockSpec((tq, gw), lambda b, h, i: (qrow0 + b * nqb + i, h)),
        out_shape=jax.ShapeDtypeStruct(att.shape, F32),
        scratch_shapes=[pltpu.VMEM((past + s_lat, ATT_HD), BF16), pltpu.VMEM((past + s_lat, ATT_HD), BF16)],
        input_output_aliases={11: 0},
        compiler_params=_cparams(("arbitrary", "arbitrary", "arbitrary")),
        name="att_lat",
    )(u, u, u, cache_k, cache_v, qn, kn, cos, sin, cos, sin, att)
    return att, k_ctx


def _rope_tables(n_tokens):
    pos = jnp.arange(n_tokens)
    row = (pos // GRID_W).astype(F32)
    col = (pos % GRID_W).astype(F32)
    n_freq = ATT_HD // 4
    inv_freq = ROPE_THETA ** (-jnp.arange(n_freq, dtype=F32) / n_freq)
    ang_r = row[:, None] * inv_freq[None, :]
    ang_c = col[:, None] * inv_freq[None, :]
    cos = jnp.concatenate([jnp.cos(ang_r), jnp.cos(ang_r), jnp.cos(ang_c), jnp.cos(ang_c)], axis=-1)
    sin = jnp.concatenate([-jnp.sin(ang_r), jnp.sin(ang_r), -jnp.sin(ang_c), jnp.sin(ang_c)], axis=-1)
    return cos, sin


def _log_sigmoid(x):
    return jnp.minimum(x, 0.0) - jnp.log1p(jnp.exp(-jnp.abs(x)))


def _mlstm_kernel(*refs, zero_init, seq):
    if zero_init:
        (q_ref, k_ref, v_ref, o_ref, gc_ref, gt_ref, b_ref, ng_ref, alias_ref,
         out_ref, co_ref, no_ref, mo_ref, h_scr, c_scr, n_scr, m_scr) = refs
    else:
        (q_ref, k_ref, v_ref, o_ref, gc_ref, gt_ref, b_ref, ng_ref, c0_ref, n0_ref, m0_ref, alias_ref,
         out_ref, co_ref, no_ref, mo_ref, h_scr, c_scr, n_scr, m_scr) = refs
    del alias_ref
    head = pl.program_id(1)
    nc = seq // CHUNK
    if zero_init:
        c_scr[...] = jnp.zeros(c_scr.shape, F32)
        n_scr[...] = jnp.zeros(n_scr.shape, F32)
        m_scr[...] = jnp.full(m_scr.shape, M_INIT, F32)
    else:
        c_scr[...] = c0_ref[...]
        n_scr[...] = n0_ref[...]
        m_scr[...] = m0_ref[...]

    ti = lax.broadcasted_iota(I32, (CHUNK, CHUNK), 0)
    si = lax.broadcasted_iota(I32, (CHUNK, CHUNK), 1)
    bias = b_ref[...]

    def chunk(c, carry):
        ch = []
        for d in (0, 1):
            cc = c if d == 0 else nc - 1 - c
            rows = pl.ds(pl.multiple_of(cc * CHUNK, CHUNK), CHUNK)
            causal = (si <= ti) if d == 0 else (si >= ti)
            causal_t = (ti <= si) if d == 0 else (ti >= si)
            q = q_ref[rows, :] * (M_HD ** -0.5)
            k = k_ref[rows, :]
            v = v_ref[rows, :]
            gcol = gc_ref[rows, :] + bias
            ig_c = gcol[:, 2 * d:2 * d + 1]
            lf_c = _log_sigmoid(gcol[:, 2 * d + 1:2 * d + 2])
            ig_r = gt_ref[cc, pl.ds(8 * d + head, 1), :] + bias[:, 2 * d:2 * d + 1]
            lf_r = _log_sigmoid(gt_ref[cc, pl.ds(8 * d + M_HEADS + head, 1), :] + bias[:, 2 * d + 1:2 * d + 2])
            bcum_c = jnp.sum(jnp.where(causal, lf_r, 0.0), axis=1, keepdims=True)
            bcum_r = jnp.sum(jnp.where(causal_t, lf_c, 0.0), axis=0, keepdims=True)
            m_st = m_scr[d][:, 0:1]
            dmat = jnp.where(causal, bcum_c - bcum_r + ig_r, -jnp.inf)
            inter = bcum_c + m_st
            m_t = jnp.maximum(inter, jnp.max(dmat, axis=1, keepdims=True))
            b_last = bcum_c[CHUNK - 1:CHUNK, :] if d == 0 else bcum_c[0:1, :]
            g_c = b_last - bcum_c + ig_c
            m_new = jnp.maximum(b_last + m_st, jnp.max(g_c, axis=0, keepdims=True))
            ch.append(dict(d=d, rows=rows, q=q, qb=q.astype(BF16), kb=k.astype(BF16), vb=v.astype(BF16),
                           c_st=c_scr[d], n_st=n_scr[d], m_t=m_t, m_new=m_new,
                           w_intra=jnp.exp(dmat - m_t), w_inter=jnp.exp(inter - m_t),
                           decay=jnp.exp(b_last + m_st - m_new), kw=k * jnp.exp(g_c - m_new)))
        for x in ch:
            x['s_qk'] = _dg(x['qb'], x['kb'], _NT) * x['w_intra']
        for x in ch:
            x['qc'] = _dg(x['qb'], x['c_st'].astype(BF16), _NN)
        for x in ch:
            x['upd'] = _dg(x['kw'].astype(BF16), x['vb'], _TN)
        for x in ch:
            x['sv'] = _dg(x['s_qk'].astype(BF16), x['vb'], _NN)
        for x in ch:
            d = x['d']
            num = x['w_inter'] * x['qc'] + x['sv']
            den = (x['w_inter'] * jnp.sum(x['q'] * x['n_st'], axis=1, keepdims=True)
                   + jnp.sum(x['s_qk'], axis=1, keepdims=True))
            h_scr[d, x['rows'], :] = num / jnp.maximum(jnp.abs(den), jnp.exp(-x['m_t']))
            c_scr[d] = x['decay'] * x['c_st'] + x['upd']
            n_scr[d] = x['decay'] * x['n_st'] + jnp.sum(x['kw'], axis=0, keepdims=True)
            m_scr[d] = jnp.broadcast_to(x['m_new'], (1, M_HD))
        return carry

    lax.fori_loop(0, nc, chunk, 0)
    h = h_scr[0] + h_scr[1]
    h = h * lax.rsqrt(jnp.mean(h * h, axis=-1, keepdims=True) + NORM_EPS) * ng_ref[...]
    out_ref[...] = h * _sigmoid(o_ref[...])
    co_ref[...] = c_scr[...]
    no_ref[...] = n_scr[...]
    mo_ref[...] = m_scr[...]


def _mlstm(u, gcol, gt, bias, norm_g, states, prev_out, row0, batch, seq):
    ntok = u.shape[0]
    zero_init = states is None
    qc = COL_M // M_HD
    blk = lambda off: pl.BlockSpec((seq, M_HD), lambda b, h: (row0 + b, off + h))
    in_specs = [blk(qc), blk(qc + M_HEADS), blk(qc + 2 * M_HEADS), blk(qc + 3 * M_HEADS),
                pl.BlockSpec((None, seq, 4), lambda b, h: (h, row0 + b, 0)),
                pl.BlockSpec((seq // CHUNK, 16, CHUNK), lambda b, h: (row0 + b, 0, 0)),
                pl.BlockSpec((None, 1, 4), lambda b, h: (h, 0, 0)),
                pl.BlockSpec((1, M_HD), lambda b, h: (0, h))]
    args = [u, u, u, u, gcol, gt, bias, norm_g]
    if not zero_init:
        in_specs += [pl.BlockSpec((None, 2, None, M_HD, M_HD), lambda b, h: (b, 0, h, 0, 0)),
                     pl.BlockSpec((None, 2, None, 1, M_HD), lambda b, h: (b, 0, h, 0, 0)),
                     pl.BlockSpec((None, 2, None, 1, M_HD), lambda b, h: (b, 0, h, 0, 0))]
        args += list(states)
    in_specs.append(pl.BlockSpec(memory_space=pl.ANY))
    args.append(prev_out)
    return pl.pallas_call(
        functools.partial(_mlstm_kernel, zero_init=zero_init, seq=seq),
        grid=(batch, M_HEADS),
        in_specs=in_specs,
        out_specs=[pl.BlockSpec((seq, M_HD), lambda b, h: (row0 + b, h)),
                   pl.BlockSpec((None, 2, None, M_HD, M_HD), lambda b, h: (b, 0, h, 0, 0)),
                   pl.BlockSpec((None, 2, None, 1, M_HD), lambda b, h: (b, 0, h, 0, 0)),
                   pl.BlockSpec((None, 2, None, 1, M_HD), lambda b, h: (b, 0, h, 0, 0))],
        out_shape=[jax.ShapeDtypeStruct((ntok, M_HEADS * M_HD), F32),
                   jax.ShapeDtypeStruct((batch, 2, M_HEADS, M_HD, M_HD), F32),
                   jax.ShapeDtypeStruct((batch, 2, M_HEADS, 1, M_HD), F32),
                   jax.ShapeDtypeStruct((batch, 2, M_HEADS, 1, M_HD), F32)],
        scratch_shapes=[pltpu.VMEM((2, seq, M_HD), F32), pltpu.VMEM((2, M_HD, M_HD), F32),
                        pltpu.VMEM((2, 1, M_HD), F32), pltpu.VMEM((2, 1, M_HD), F32)],
        input_output_aliases={len(args) - 1: 0},
        compiler_params=_cparams(("arbitrary", "arbitrary")),
        name="mlstm_ctx" if zero_init else "mlstm_lat",
    )(*args)


def _softplus(x):
    return jnp.maximum(x, 0.0) + jnp.log1p(jnp.exp(-jnp.abs(x)))


def _rwkv_kernel(*refs, zero_init, seq):
    if zero_init:
        (r_ref, k_ref, v_ref, xl_ref, xg_ref, mur_ref, muk_ref, muv_ref, mul_ref, mug_ref,
         w0_ref, wup_ref, a0_ref, aup_ref, gup_ref, kk_ref, ka_ref, rk_ref, lng_ref, lnb_ref, alias_ref,
         out_ref, so_ref, r_scr, v_scr, kk_scr, g_scr, bonus_scr, lw_scr, kd_scr, b_scr, y_scr, s_scr) = refs
    else:
        (r_ref, k_ref, v_ref, xl_ref, xg_ref, mur_ref, muk_ref, muv_ref, mul_ref, mug_ref,
         w0_ref, wup_ref, a0_ref, aup_ref, gup_ref, kk_ref, ka_ref, rk_ref, lng_ref, lnb_ref, s0_ref, alias_ref,
         out_ref, so_ref, r_scr, v_scr, kk_scr, g_scr, bonus_scr, lw_scr, kd_scr, b_scr, y_scr, s_scr) = refs
    del alias_ref
    nc = seq // CHUNK
    hd = R_HD
    row = lax.broadcasted_iota(I32, (seq, LANES), 0)

    def tshift(x_ref, mu_ref):
        x = x_ref[...]
        prev = jnp.where(row == 0, 0.0, pltpu.roll(x, 1, 0))
        nxt = jnp.where(row == seq - 1, 0.0, pltpu.roll(x, seq - 1, 0))
        return x + mu_ref[...] * (0.5 * (prev + nxt) - x)

    r = tshift(r_ref, mur_ref)
    k = tshift(k_ref, muk_ref)
    v = tshift(v_ref, muv_ref)
    xl = tshift(xl_ref, mul_ref)
    xg = tshift(xg_ref, mug_ref)
    g = _dot(_sigmoid(xg), gup_ref[...])
    kkp = k * kk_ref[...]
    kk = jnp.concatenate(
        [kkp[:, i * hd:(i + 1) * hd]
         * lax.rsqrt(jnp.maximum(jnp.sum(kkp[:, i * hd:(i + 1) * hd] ** 2, axis=-1, keepdims=True), 1e-24))
         for i in range(2)], axis=1)
    tw = jnp.tanh(xl[:, 0:hd])
    xa = xl[:, hd:2 * hd]
    r_scr[...] = r
    v_scr[...] = v
    kk_scr[...] = kk
    g_scr[...] = g
    rkk = r * k * rk_ref[...]
    bonus_scr[...] = jnp.concatenate(
        [jnp.sum(rkk[:, i * hd:(i + 1) * hd], axis=-1, keepdims=True) * v[:, i * hd:(i + 1) * hd] for i in range(2)],
        axis=1)
    for d in (0, 1):
        wd = -_softplus(-(w0_ref[d] + _dot(tw, wup_ref[d]))) - 0.5
        ad = _sigmoid(a0_ref[d] + _dot(xa, aup_ref[d]))
        lw_scr[d] = -jnp.exp(wd)
        kd_scr[d] = k * (1.0 + (ad - 1.0) * ka_ref[...])
        b_scr[d] = kk * ad
    if zero_init:
        s_scr[...] = jnp.zeros(s_scr.shape, F32)
    else:
        s_scr[...] = s0_ref[...]

    ti = lax.broadcasted_iota(I32, (CHUNK, CHUNK), 0)
    si = lax.broadcasted_iota(I32, (CHUNK, CHUNK), 1)
    eye = (ti == si).astype(F32)

    def chunk(c, carry):
        chains = []
        rows_d = []
        for d in (0, 1):
            cc = c if d == 0 else nc - 1 - c
            rows = pl.ds(pl.multiple_of(cc * CHUNK, CHUNK), CHUNK)
            rows_d.append(rows)
            incl = (si <= ti) if d == 0 else (si >= ti)
            strict = (si < ti) if d == 0 else (si > ti)
            lw = lw_scr[d, rows, :]
            lw_hi, lw_lo = _split(lw)
            tri = jnp.where(incl, 1.0, 0.0).astype(BF16)
            lc = _dg(tri, lw_hi, _NN) + _dg(tri, lw_lo, _NN)
            l_last = lc[CHUNK - 1:CHUNK, :] if d == 0 else lc[0:1, :]
            e_neg = jnp.exp(-lc)
            e_end = jnp.exp(l_last - lc)
            vc = v_scr[rows, :]
            kdc = kd_scr[d, rows, :]
            bc = b_scr[d, rows, :]
            rt = r_scr[rows, :] * jnp.exp(lc)
            kkt = kk_scr[rows, :] * jnp.exp(lc - lw)
            kh = kdc * e_neg
            bh = bc * e_neg
            kbar = kdc * e_end
            bbar = bc * e_end
            w_end = jnp.exp(l_last)
            for i in range(2):
                sl = slice(i * hd, (i + 1) * hd)
                chains.append(dict(
                    d=d, i=i, incl=incl, strict=strict,
                    lhs=jnp.concatenate([kkt[:, sl], rt[:, sl]], axis=0),
                    rhs=jnp.concatenate([bh[:, sl], kh[:, sl]], axis=0),
                    end=jnp.concatenate([kbar[:, sl], bbar[:, sl]], axis=0),
                    v=vc[:, sl], w_end=w_end[:, sl], s0=s_scr[d, i]))
        for ch in chains:
            ch['ab'] = _dot3(ch['lhs'], ch['rhs'], _NT)
        for ch in chains:
            ch['proj'] = _dot3(ch['lhs'], ch['s0'], _NT)
        for ch in chains:
            ab = ch['ab']
            ch['a_kb'] = jnp.where(ch['strict'], ab[:CHUNK, :CHUNK], 0.0)
            ch['b_rb'] = jnp.where(ch['incl'], ab[CHUNK:, :CHUNK], 0.0)
            ch['akk_brk'] = jnp.concatenate([jnp.where(ch['strict'], ab[:CHUNK, CHUNK:], 0.0),
                                             jnp.where(ch['incl'], ab[CHUNK:, CHUNK:], 0.0)], axis=0)
        for ch in chains:
            ch['p'] = _dot3(ch['a_kb'], ch['a_kb'])
        for ch in chains:
            ch['abv'] = _dot3(ch['akk_brk'], ch['v'])
        for ch in chains:
            inv = eye - ch['a_kb']
            ch['inv'] = inv + _dot3(inv, ch['p'])
        span = 4
        while span < CHUNK:
            for ch in chains:
                ch['p'] = _dot1(ch['p'], ch['p'])
            for ch in chains:
                ch['inv'] = ch['inv'] + _dot1(ch['inv'], ch['p'])
            span *= 2
        for ch in chains:
            ch['u'] = _dot3(ch['inv'], ch['proj'][:CHUNK] + ch['abv'][:CHUNK])
        for ch in chains:
            ch['y'] = ch['proj'][CHUNK:] + ch['abv'][CHUNK:] - _dot3(ch['b_rb'], ch['u'])
        for ch in chains:
            upd = _dot3(jnp.concatenate([ch['v'], -ch['u']], axis=0), ch['end'], _TN)
            s_scr[ch['d'], ch['i']] = ch['s0'] * ch['w_end'] + upd
        for d in (0, 1):
            y_scr[d, rows_d[d], :] = jnp.concatenate([chains[2 * d]['y'], chains[2 * d + 1]['y']], axis=1)
        return carry

    lax.fori_loop(0, nc, chunk, 0)

    y = y_scr[0] + y_scr[1]
    outs = []
    for i in range(2):
        yh = y[:, i * hd:(i + 1) * hd]
        mean = jnp.mean(yh, axis=-1, keepdims=True)
        var = jnp.mean(jnp.square(yh - mean), axis=-1, keepdims=True)
        outs.append((yh - mean) * lax.rsqrt(var + GN_EPS))
    yn = jnp.concatenate(outs, axis=1)
    out_ref[...] = (yn * lng_ref[...] + lnb_ref[...] + bonus_scr[...]) * g_scr[...]
    so_ref[...] = s_scr[...]


def _rwkv(u, p, state, prev_out, row0, batch, seq):
    ntok = u.shape[0]
    zero_init = state is None
    rc = COL_R // LANES
    nhp = R_HEADS // 2
    ublk = lambda off: pl.BlockSpec((seq, LANES), lambda b, h: (row0 + b, rc + off + h))
    ufix = lambda off: pl.BlockSpec((seq, LANES), lambda b, h: (row0 + b, rc + off))
    mblk = lambda off: pl.BlockSpec((1, LANES), lambda b, h: (0, off + h))
    mfix = lambda off: pl.BlockSpec((1, LANES), lambda b, h: (0, off))
    vec = pl.BlockSpec((1, LANES), lambda b, h: (0, h))
    in_specs = [ublk(0), ublk(nhp), ublk(2 * nhp), ufix(3 * nhp), ufix(3 * nhp + 1),
                mblk(0), mblk(nhp), mblk(2 * nhp), mfix(3 * nhp), mfix(3 * nhp + 1),
                pl.BlockSpec((2, 1, LANES), lambda b, h: (0, 0, h)),
                pl.BlockSpec((2, R_HD, LANES), lambda b, h: (0, 0, h)),
                pl.BlockSpec((2, 1, LANES), lambda b, h: (0, 0, h)),
                pl.BlockSpec((2, R_HD, LANES), lambda b, h: (0, 0, h)),
                pl.BlockSpec((LANES, LANES), lambda b, h: (0, h)),
                vec, vec, vec, vec, vec]
    args = [u, u, u, u, u, p['mu'], p['mu'], p['mu'], p['mu'], p['mu'],
            p['w0'], p['w_up'], p['a0'], p['a_up'], p['g_up'], p['k_k'], p['k_a'], p['r_k'], p['ln_g'], p['ln_b']]
    if not zero_init:
        in_specs.append(pl.BlockSpec((None, 2, 2, R_HD, R_HD), lambda b, h: (b, 0, h, 0, 0)))
        args.append(state)
    in_specs.append(pl.BlockSpec(memory_space=pl.ANY))
    args.append(prev_out)
    big = lambda n: pltpu.VMEM((n, seq, LANES), F32)
    return pl.pallas_call(
        functools.partial(_rwkv_kernel, zero_init=zero_init, seq=seq),
        grid=(batch, nhp),
        in_specs=in_specs,
        out_specs=[pl.BlockSpec((seq, LANES), lambda b, h: (row0 + b, h)),
                   pl.BlockSpec((None, 2, 2, R_HD, R_HD), lambda b, h: (b, 0, h, 0, 0))],
        out_shape=[jax.ShapeDtypeStruct((ntok, R_HEADS * R_HD), F32),
                   jax.ShapeDtypeStruct((batch, 2, R_HEADS, R_HD, R_HD), F32)],
        scratch_shapes=[pltpu.VMEM((seq, LANES), F32)] * 5
                       + [big(2), big(2), big(2), big(2), pltpu.VMEM((2, 2, R_HD, R_HD), F32)],
        input_output_aliases={len(args) - 1: 0},
        compiler_params=_cparams(("arbitrary", "arbitrary")),
        name="rwkv_ctx" if zero_init else "rwkv_lat",
    )(*args)


def _top2_sum(a, b, c, d):
    m1, n1 = jnp.maximum(a, b), jnp.minimum(a, b)
    m2, n2 = jnp.maximum(c, d), jnp.minimum(c, d)
    return jnp.maximum(m1, m2) + jnp.maximum(jnp.minimum(m1, m2), jnp.maximum(n1, n2))


def _first_argmax(vals):
    best = functools.reduce(jnp.maximum, vals)
    idx = jnp.full(best.shape, len(vals) - 1, I32)
    for j in range(len(vals) - 2, -1, -1):
        idx = jnp.where(vals[j] == best, j, idx)
    return best, idx


def _out_kernel(att_ref, m_ref, r_ref, x_ref, w_ref, g1_ref, sh2_ref, sc2_ref, n2_ref, rw_ref, rb_ref,
                x1_ref, h2_ref, route_ref):
    na = att_ref.shape[1]
    nm = m_ref.shape[1]
    mix = (jnp.dot(att_ref[...].astype(BF16), w_ref[0:na, :], preferred_element_type=F32)
           + jnp.dot(m_ref[...].astype(BF16), w_ref[na:na + nm, :], preferred_element_type=F32)
           + jnp.dot(r_ref[...].astype(BF16), w_ref[na + nm:, :], preferred_element_type=F32))
    x1 = x_ref[...] + g1_ref[...] * mix
    x1_ref[...] = x1
    h2 = _rms(x1, n2_ref[...]) * (1.0 + sc2_ref[...]) + sh2_ref[...]
    h2_ref[...] = h2
    logits = _dot3(rw_ref[...], h2, _NT)
    s = _sigmoid(logits)
    ssel = s + rb_ref[...]
    srow = [s[e:e + 1, :] for e in range(N_EXPERTS)]
    brow = [ssel[e:e + 1, :] for e in range(N_EXPERTS)]
    gscore = [_top2_sum(*brow[EXPERTS_PER_GROUP * g:EXPERTS_PER_GROUP * (g + 1)]) for g in range(N_EXPERT_GROUPS)]
    _, gidx = _first_argmax(gscore)
    pick = lambda rows, j: functools.reduce(
        lambda acc, g: jnp.where(gidx == g, rows[EXPERTS_PER_GROUP * g + j], acc),
        range(N_EXPERT_GROUPS - 2, -1, -1), rows[EXPERTS_PER_GROUP * (N_EXPERT_GROUPS - 1) + j])
    ing = [pick(brow, j) for j in range(EXPERTS_PER_GROUP)]
    sin_ = [pick(srow, j) for j in range(EXPERTS_PER_GROUP)]
    _, l1 = _first_argmax(ing)
    _, l2 = _first_argmax([jnp.where(l1 == j, -jnp.inf, ing[j]) for j in range(EXPERTS_PER_GROUP)])
    sel = lambda l: functools.reduce(lambda acc, j: jnp.where(l == j, sin_[j], acc),
                                     range(EXPERTS_PER_GROUP - 2, -1, -1), sin_[EXPERTS_PER_GROUP - 1])
    w1, w2 = sel(l1), sel(l2)
    tot = w1 + w2
    e1 = (gidx * EXPERTS_PER_GROUP + l1).astype(F32)
    e2 = (gidx * EXPERTS_PER_GROUP + l2).astype(F32)
    zero = jnp.zeros_like(w1)
    route_ref[...] = jnp.concatenate([e1, e2, w1 / tot, w2 / tot, zero, zero, zero, zero], axis=0)


def _out_proj(att, m_out, r_out, x, w_out, mod3, n2g, rw_t, rb, t_ctx, s_lat):
    ntok, d = x.shape
    tm = 256
    row = lambda i: _mod_row(i * tm, t_ctx, s_lat)
    modblk = lambda j: pl.BlockSpec((None, 1, d), lambda i: (row(i), 0, j))
    return pl.pallas_call(
        _out_kernel,
        grid=(ntok // tm,),
        in_specs=[pl.BlockSpec((tm, att.shape[1]), lambda i: (i, 0)),
                  pl.BlockSpec((tm, m_out.shape[1]), lambda i: (i, 0)),
                  pl.BlockSpec((tm, r_out.shape[1]), lambda i: (i, 0)),
                  pl.BlockSpec((tm, d), lambda i: (i, 0)),
                  pl.BlockSpec(w_out.shape, lambda i: (0, 0)),
                  modblk(2), modblk(3), modblk(4),
                  pl.BlockSpec((1, d), lambda i: (0, 0)),
                  pl.BlockSpec(rw_t.shape, lambda i: (0, 0)),
                  pl.BlockSpec(rb.shape, lambda i: (0, 0))],
        out_specs=[pl.BlockSpec((tm, d), lambda i: (i, 0)),
                   pl.BlockSpec((tm, d), lambda i: (i, 0)),
                   pl.BlockSpec((8, tm), lambda i: (0, i))],
        out_shape=[jax.ShapeDtypeStruct((ntok, d), F32), jax.ShapeDtypeStruct((ntok, d), F32),
                   jax.ShapeDtypeStruct((8, ntok), F32)],
        compiler_params=_cparams(("arbitrary",)),
        name="out_proj",
    )(att, m_out, r_out, x, w_out, mod3, mod3, mod3, n2g, rw_t, rb)


def _expert_kernel(be_ref, tok_ref, nused_ref, h_hbm, w1_ref, w3_ref, w2_ref, y_ref, xbuf, w1b, w3b, w2b, sem):
    i = pl.program_id(0)

    @pl.when(i < nused_ref[0])
    def _():
        base = i * EXPERT_ROWS

        def issue(j, c):
            pltpu.make_async_copy(h_hbm.at[pl.ds(tok_ref[base + j], 1), :], xbuf.at[pl.ds(j, 1), :], sem).start()
            return c

        lax.fori_loop(0, EXPERT_ROWS, issue, 0)

        @pl.when(jnp.logical_or(i == 0, be_ref[i] != be_ref[jnp.maximum(i - 1, 0)]))
        def _():
            w1b[...] = w1_ref[...].astype(BF16)
            w3b[...] = w3_ref[...].astype(BF16)
            w2b[...] = w2_ref[...].astype(BF16)

        pltpu.make_async_copy(h_hbm.at[pl.ds(0, EXPERT_ROWS), :], xbuf, sem).wait()
        xb = xbuf[...].astype(BF16)
        a = jnp.dot(xb, w1b[...], preferred_element_type=F32)
        b = jnp.dot(xb, w3b[...], preferred_element_type=F32)
        hmid = (a * _sigmoid(a)) * b
        y_ref[...] = jnp.dot(hmid.astype(BF16), w2b[...], preferred_element_type=F32)

    @pl.when(i >= nused_ref[0])
    def _():
        y_ref[...] = jnp.zeros(y_ref.shape, F32)


def _experts(h2, block_e, row_tok, n_used, w1, w3, w2):
    ntok, d = h2.shape
    de = w1.shape[2]
    n_rows = row_tok.shape[0]
    nb = n_rows // EXPERT_ROWS
    return pl.pallas_call(
        _expert_kernel,
        grid_spec=pltpu.PrefetchScalarGridSpec(
            num_scalar_prefetch=3,
            grid=(nb,),
            in_specs=[pl.BlockSpec(memory_space=pl.ANY),
                      pl.BlockSpec((None, d, de), lambda i, be, tok, nu: (be[i], 0, 0)),
                      pl.BlockSpec((None, d, de), lambda i, be, tok, nu: (be[i], 0, 0)),
                      pl.BlockSpec((None, de, d), lambda i, be, tok, nu: (be[i], 0, 0))],
            out_specs=pl.BlockSpec((EXPERT_ROWS, d), lambda i, be, tok, nu: (i, 0)),
            scratch_shapes=[pltpu.VMEM((EXPERT_ROWS, d), F32), pltpu.VMEM((d, de), BF16), pltpu.VMEM((d, de), BF16),
                            pltpu.VMEM((de, d), BF16), pltpu.SemaphoreType.DMA(())]),
        out_shape=jax.ShapeDtypeStruct((n_rows, d), F32),
        compiler_params=_cparams(("arbitrary",)),
        name="experts",
    )(block_e, row_tok, n_used, h2, w1, w3, w2)


def _combine_kernel(d1_ref, d2_ref, y_hbm, x1_ref, g2_ref, gate_ref, o_ref, ybuf, sem):
    tm = x1_ref.shape[0]
    base = pl.program_id(0) * tm

    def issue(j, c):
        pltpu.make_async_copy(y_hbm.at[pl.ds(d1_ref[base + j], 1), :], ybuf.at[0, pl.ds(j, 1), :], sem).start()
        pltpu.make_async_copy(y_hbm.at[pl.ds(d2_ref[base + j], 1), :], ybuf.at[1, pl.ds(j, 1), :], sem).start()
        return c

    lax.fori_loop(0, tm, issue, 0)
    pltpu.make_async_copy(y_hbm.at[pl.ds(0, tm), :], ybuf.at[0], sem).wait()
    pltpu.make_async_copy(y_hbm.at[pl.ds(0, tm), :], ybuf.at[1], sem).wait()
    gate = gate_ref[...]
    moe = ybuf[0] * gate[:, 0:1] + ybuf[1] * gate[:, 1:2]
    o_ref[...] = x1_ref[...] + g2_ref[...] * moe


def _combine(y, x1, mod3, gate, dest1, dest2, t_ctx, s_lat):
    ntok, d = x1.shape
    tm = 256
    row = lambda i: _mod_row(i * tm, t_ctx, s_lat)
    return pl.pallas_call(
        _combine_kernel,
        grid_spec=pltpu.PrefetchScalarGridSpec(
            num_scalar_prefetch=2,
            grid=(ntok // tm,),
            in_specs=[pl.BlockSpec(memory_space=pl.ANY),
                      pl.BlockSpec((tm, d), lambda i, a, b: (i, 0)),
                      pl.BlockSpec((None, 1, d), lambda i, a, b: (row(i), 0, 5)),
                      pl.BlockSpec((tm, 2), lambda i, a, b: (i, 0))],
            out_specs=pl.BlockSpec((tm, d), lambda i, a, b: (i, 0)),
            scratch_shapes=[pltpu.VMEM((2, tm, d), F32), pltpu.SemaphoreType.DMA(())]),
        out_shape=jax.ShapeDtypeStruct((ntok, d), F32),
        compiler_params=_cparams(("arbitrary",)),
        name="combine",
    )(dest1, dest2, y, x1, mod3, gate)


def _dispatch(route):
    ntok = route.shape[1]
    e = route[0:2].astype(I32).T
    gate = route[2:4].T
    flat_e = e.reshape(-1)
    n_assign = flat_e.shape[0]
    onehot = (flat_e[:, None] == jnp.arange(N_EXPERTS, dtype=I32)[None, :]).astype(I32)
    csum = jnp.cumsum(onehot, axis=0)
    pos = jnp.sum(csum * onehot, axis=1) - 1
    counts = csum[-1]
    padded = (counts + EXPERT_ROWS - 1) // EXPERT_ROWS * EXPERT_ROWS
    pad_end = jnp.cumsum(padded)
    pad_start = pad_end - padded
    dest = (pad_start[flat_e] + pos).astype(I32)
    n_rows = -(-n_assign // EXPERT_ROWS) * EXPERT_ROWS + N_EXPERTS * EXPERT_ROWS
    nb = n_rows // EXPERT_ROWS
    row_tok = jnp.zeros((n_rows,), I32).at[dest].set(jnp.arange(n_assign, dtype=I32) // 2)
    block_e = jnp.minimum(jnp.searchsorted(pad_end, jnp.arange(nb, dtype=I32) * EXPERT_ROWS, side='right'),
                          N_EXPERTS - 1).astype(I32)
    n_used = (pad_end[-1:] // EXPERT_ROWS).astype(I32)
    dest2 = dest.reshape(ntok, 2)
    return row_tok, block_e, n_used, gate, dest2[:, 0], dest2[:, 1]


def kernel(x_prompt, x_sample, c, cache_attn_k, cache_attn_v, state_mlstm_C, state_mlstm_n, state_mlstm_m, state_rwkv, c_ctx, norm1_g, norm2_g, w_mod, b_mod, w_in, w_out, attn_q_norm, attn_k_norm, mlstm_i_bias, mlstm_f_bias, mlstm_norm_g, rwkv_mu, rwkv_w0, rwkv_w_up, rwkv_a0, rwkv_a_up, rwkv_g_up, rwkv_k_k, rwkv_k_a, rwkv_r_k, rwkv_ln_g, rwkv_ln_b, router_w, router_b, exp_w1, exp_w3, exp_w2):
    b_ctx, s_ctx, d = x_prompt.shape
    b_lat, s_lat, _ = x_sample.shape
    depth = w_in.shape[0]
    t_ctx = b_ctx * s_ctx
    ntok = t_ctx + b_lat * s_lat
    assert b_lat + 1 <= 8 and s_lat % 512 == 0 and t_ctx % 512 == 0 and t_ctx % s_lat == 0
    past = cache_attn_k.shape[2]

    x = jnp.concatenate([x_prompt.reshape(t_ctx, d), x_sample.reshape(b_lat * s_lat, d)], axis=0)
    c_all = jnp.zeros((8, d), F32).at[0].set(c_ctx).at[1:1 + b_lat].set(c)
    mod = _modulation(c_all, w_mod, b_mod)
    cos, sin = _rope_tables(s_lat)
    rw_t = router_w.T
    rb = router_b.reshape(N_EXPERTS, 1)
    m_width = M_HEADS * M_HD
    r_width = R_HEADS * R_HD
    n_in = w_in.shape[2]
    gate_lo = COL_M + 4 * m_width

    ks, vs, cs, ns, ms, rs = [], [], [], [], [], []
    for l in range(depth):
        w = w_in[l]
        w_p = jnp.concatenate([w[:, :gate_lo], w[:, gate_lo + 4 * M_HEADS:], w[:, gate_lo:gate_lo + 4 * M_HEADS],
                               jnp.zeros((d, N_IN_PAD - n_in), F32)], axis=1).astype(BF16)
        mod3 = mod[l].reshape(8, 1, 6 * d)
        u = _in_proj(x, norm1_g[l][None], mod3, w_p, t_ctx, s_lat)

        ck = cache_attn_k[:, l].reshape(b_lat, past, ATT_KV_HEADS * ATT_HD)
        cv = cache_attn_v[:, l].reshape(b_lat, past, ATT_KV_HEADS * ATT_HD)
        att, k_ctx = _attention(u, attn_q_norm[l][None], attn_k_norm[l][None], ck, cv, cos, sin,
                                b_ctx, s_ctx, b_lat, s_lat)
        ks.append(k_ctx.reshape(b_ctx, s_ctx, ATT_KV_HEADS, ATT_HD))
        vcol = ATT_GROUPS * ATT_KV_HEADS * ATT_HD + ATT_KV_HEADS * ATT_HD
        vs.append(u[:t_ctx, vcol:vcol + ATT_KV_HEADS * ATT_HD].reshape(b_ctx, s_ctx, ATT_KV_HEADS, ATT_HD))

        gates = u[:, COL_GATE:COL_GATE + 4 * M_HEADS]
        gcol = gates.reshape(ntok, 2, 2, M_HEADS).transpose(3, 0, 1, 2).reshape(M_HEADS, ntok, 4)
        gt = gates.reshape(ntok // CHUNK, CHUNK, 4 * M_HEADS).transpose(0, 2, 1)
        bias = jnp.stack([mlstm_i_bias[l], mlstm_f_bias[l]], axis=1)
        bias = bias.transpose(2, 0, 1).reshape(M_HEADS, 1, 4)
        ng = mlstm_norm_g[l][None]
        m_out = jnp.zeros((ntok, m_width), F32)
        m_out, c_c, n_c, m_c = _mlstm(u, gcol, gt, bias, ng, None, m_out, 0, b_ctx, s_ctx)
        lat_states = (state_mlstm_C[:, l],
                      state_mlstm_n[:, l].reshape(b_lat, 2, M_HEADS, 1, M_HD),
                      jnp.broadcast_to(state_mlstm_m[:, l][..., None, None], (b_lat, 2, M_HEADS, 1, M_HD)))
        m_out, _, _, _ = _mlstm(u, gcol, gt, bias, ng, lat_states, m_out, t_ctx // s_lat, b_lat, s_lat)
        cs.append(c_c)
        ns.append(n_c.reshape(b_ctx, 2, M_HEADS, M_HD))
        ms.append(m_c[:, :, :, 0, 0])

        rp = dict(mu=rwkv_mu[l][None], w0=rwkv_w0[l].reshape(2, 1, r_width), w_up=rwkv_w_up[l],
                  a0=rwkv_a0[l].reshape(2, 1, r_width), a_up=rwkv_a_up[l], g_up=rwkv_g_up[l],
                  k_k=rwkv_k_k[l][None], k_a=rwkv_k_a[l][None], r_k=rwkv_r_k[l].reshape(1, r_width),
                  ln_g=rwkv_ln_g[l][None], ln_b=rwkv_ln_b[l][None])
        r_out = jnp.zeros((ntok, r_width), F32)
        r_out, r_c = _rwkv(u, rp, None, r_out, 0, b_ctx, s_ctx)
        r_out, _ = _rwkv(u, rp, state_rwkv[:, l], r_out, t_ctx // s_lat, b_lat, s_lat)
        rs.append(r_c)

        x1, h2, route = _out_proj(att, m_out, r_out, x, w_out[l].astype(BF16), mod3, norm2_g[l][None], rw_t, rb,
                                  t_ctx, s_lat)
        row_tok, block_e, n_used, gate, dest1, dest2 = _dispatch(route)
        y = _experts(h2, block_e, row_tok, n_used, exp_w1[l], exp_w3[l], exp_w2[l])
        x = _combine(y, x1, mod3, gate, dest1, dest2, t_ctx, s_lat)

    y_prompt = x[:t_ctx].reshape(b_ctx, s_ctx, d)
    y_sample = x[t_ctx:].reshape(b_lat, s_lat, d)
    return (y_prompt, y_sample, jnp.stack(ks, axis=1), jnp.stack(vs, axis=1), jnp.stack(cs, axis=1),
            jnp.stack(ns, axis=1), jnp.stack(ms, axis=1), jnp.stack(rs, axis=1))
```

```python
import functools

import jax
import jax.numpy as jnp
from jax import lax
from jax.experimental import pallas as pl
from jax.experimental.pallas import tpu as pltpu

F32 = jnp.float32
BF16 = jnp.bfloat16
I32 = jnp.int32

NORM_EPS = 1e-6
GN_EPS = 64e-5
M_INIT = -1e30
GRID_W = 64
ROPE_THETA = 10000.0
ATT_HD = 128
ATT_GROUPS = 4
ATT_KV_HEADS = 2
M_HD = 128
M_HEADS = 4
R_HD = 64
R_HEADS = 8
N_EXPERTS = 16
N_EXPERT_GROUPS = 4
EXPERTS_PER_GROUP = 4
CHUNK = 64
LANES = 128
EXPERT_ROWS = 256
VMEM_LIMIT = 56 * 1024 * 1024

COL_ATT = 0
COL_M = 1536
COL_R = 3584
COL_GATE = 5376
N_IN_PAD = 5632


def _cparams(sem):
    return pltpu.CompilerParams(dimension_semantics=sem, vmem_limit_bytes=VMEM_LIMIT)


def _dot(a, b):
    return jnp.dot(a.astype(BF16), b.astype(BF16), preferred_element_type=F32)


def _dg(a, b, dims):
    return lax.dot_general(a, b, (dims, ((), ())), preferred_element_type=F32)


_NN = ((1,), (0,))
_NT = ((1,), (1,))
_TN = ((0,), (0,))


def _split(a):
    hi = a.astype(BF16)
    lo = (a - hi.astype(F32)).astype(BF16)
    return hi, lo


def _dot3(a, b, dims=_NN):
    ah, al = _split(a)
    bh, bl = _split(b)
    return _dg(ah, bh, dims) + (_dg(ah, bl, dims) + _dg(al, bh, dims))


def _dot1(a, b, dims=_NN):
    return _dg(a.astype(BF16), b.astype(BF16), dims)


def _rms(x, g):
    return x * lax.rsqrt(jnp.mean(x * x, axis=-1, keepdims=True) + NORM_EPS) * g


def _sigmoid(x):
    return 1.0 / (1.0 + jnp.exp(-x))


def _mod_kernel(c_ref, w_ref, b_ref, o_ref):
    c = c_ref[...]
    o_ref[...] = _dot(c * _sigmoid(c), w_ref[...]) + b_ref[...]


def _modulation(c_all, w_mod, b_mod):
    depth, d, n = w_mod.shape
    tn = 1024
    return pl.pallas_call(
        _mod_kernel,
        grid=(depth, n // tn),
        in_specs=[pl.BlockSpec((8, d), lambda l, j: (0, 0)),
                  pl.BlockSpec((None, d, tn), lambda l, j: (l, 0, j)),
                  pl.BlockSpec((None, 1, tn), lambda l, j: (l, 0, j))],
        out_specs=pl.BlockSpec((None, 8, tn), lambda l, j: (l, 0, j)),
        out_shape=jax.ShapeDtypeStruct((depth, 8, n), F32),
        compiler_params=_cparams(("arbitrary", "arbitrary")),
        name="modulation",
    )(c_all, w_mod, b_mod.reshape(depth, 1, n))


def _mod_row(tok0, t_ctx, s_lat):
    return jnp.where(tok0 < t_ctx, 0, 1 + (tok0 - t_ctx) // s_lat)


def _in_kernel(x_ref, g_ref, sh_ref, sc_ref, w_ref, o_ref, h_ref):
    @pl.when(pl.program_id(1) == 0)
    def _():
        h = _rms(x_ref[...], g_ref[...]) * (1.0 + sc_ref[...]) + sh_ref[...]
        h_ref[...] = h.astype(BF16)

    o_ref[...] = jnp.dot(h_ref[...], w_ref[...], preferred_element_type=F32)


def _in_proj(x, g1, mod3, w_p, t_ctx, s_lat):
    ntok, d = x.shape
    n = w_p.shape[1]
    tm, tn = 512, 512
    row = lambda i: _mod_row(i * tm, t_ctx, s_lat)
    return pl.pallas_call(
        _in_kernel,
        grid=(ntok // tm, n // tn),
        in_specs=[pl.BlockSpec((tm, d), lambda i, j: (i, 0)),
                  pl.BlockSpec((1, d), lambda i, j: (0, 0)),
                  pl.BlockSpec((None, 1, d), lambda i, j: (row(i), 0, 0)),
                  pl.BlockSpec((None, 1, d), lambda i, j: (row(i), 0, 1)),
                  pl.BlockSpec((d, tn), lambda i, j: (0, j))],
        out_specs=pl.BlockSpec((tm, tn), lambda i, j: (i, j)),
        out_shape=jax.ShapeDtypeStruct((ntok, n), F32),
        scratch_shapes=[pltpu.VMEM((tm, d), BF16)],
        compiler_params=_cparams(("arbitrary", "arbitrary")),
        name="in_proj",
    )(x, g1, mod3, mod3, w_p)


def _softmax_av(q, kb, vb):
    s = _dg(q.astype(BF16), kb, _NT) * (ATT_HD ** -0.5)
    p = jnp.exp(s - jnp.max(s, axis=-1, keepdims=True))
    l = jnp.sum(p, axis=-1, keepdims=True)
    return jnp.dot(p.astype(BF16), vb, preferred_element_type=F32) / l


def _att_ctx_kernel(q_ref, k_ref, v_ref, qn_ref, kn_ref, alias_ref, o_ref, ko_ref):
    del alias_ref
    k = _rms(k_ref[...], kn_ref[...])
    ko_ref[...] = k
    kb = k.astype(BF16)
    vb = v_ref[...].astype(BF16)
    for g in range(ATT_GROUPS):
        q = _rms(q_ref[:, g * ATT_HD:(g + 1) * ATT_HD], qn_ref[...])
        o_ref[:, g * ATT_HD:(g + 1) * ATT_HD] = _softmax_av(q, kb, vb)


def _rope(x, cos, sin):
    lane = lax.broadcasted_iota(I32, x.shape, 1)
    first = (lane % (ATT_HD // 2)) < (ATT_HD // 4)
    partner = jnp.where(first, pltpu.roll(x, ATT_HD - ATT_HD // 4, 1), pltpu.roll(x, ATT_HD // 4, 1))
    return x * cos + partner * sin


def _att_lat_kernel(q_ref, k_ref, v_ref, ck_ref, cv_ref, qn_ref, kn_ref, cosq_ref, sinq_ref, cosk_ref, sink_ref,
                    alias_ref, o_ref, kb_ref, vb_ref, *, past):
    del alias_ref

    @pl.when(pl.program_id(2) == 0)
    def _():
        k = _rope(_rms(k_ref[...], kn_ref[...]), cosk_ref[...], sink_ref[...])
        kb_ref[0:past, :] = ck_ref[...].astype(BF16)
        kb_ref[past:, :] = k.astype(BF16)
        vb_ref[0:past, :] = cv_ref[...].astype(BF16)
        vb_ref[past:, :] = v_ref[...].astype(BF16)

    kb = kb_ref[...]
    vb = vb_ref[...]
    for g in range(ATT_GROUPS):
        q = _rope(_rms(q_ref[:, g * ATT_HD:(g + 1) * ATT_HD], qn_ref[...]), cosq_ref[...], sinq_ref[...])
        o_ref[:, g * ATT_HD:(g + 1) * ATT_HD] = _softmax_av(q, kb, vb)


def _attention(u, qn, kn, cache_k, cache_v, cos, sin, b_ctx, s_ctx, b_lat, s_lat):
    ntok = u.shape[0]
    t_ctx = b_ctx * s_ctx
    gw = ATT_GROUPS * ATT_HD
    kcol = (ATT_KV_HEADS * gw) // ATT_HD
    vcol = kcol + ATT_KV_HEADS
    att, k_ctx = pl.pallas_call(
        _att_ctx_kernel,
        grid=(b_ctx, ATT_KV_HEADS),
        in_specs=[pl.BlockSpec((s_ctx, gw), lambda b, h: (b, h)),
                  pl.BlockSpec((s_ctx, ATT_HD), lambda b, h: (b, kcol + h)),
                  pl.BlockSpec((s_ctx, ATT_HD), lambda b, h: (b, vcol + h)),
                  pl.BlockSpec((1, ATT_HD), lambda b, h: (0, 0)),
                  pl.BlockSpec((1, ATT_HD), lambda b, h: (0, 0)),
                  pl.BlockSpec(memory_space=pl.ANY)],
        out_specs=[pl.BlockSpec((s_ctx, gw), lambda b, h: (b, h)),
                   pl.BlockSpec((s_ctx, ATT_HD), lambda b, h: (b, h))],
        out_shape=[jax.ShapeDtypeStruct((ntok, ATT_KV_HEADS * gw), F32),
                   jax.ShapeDtypeStruct((t_ctx, ATT_KV_HEADS * ATT_HD), F32)],
        input_output_aliases={5: 0},
        compiler_params=_cparams(("arbitrary", "arbitrary")),
        name="att_ctx",
    )(u, u, u, qn, kn, jnp.zeros((ntok, ATT_KV_HEADS * gw), F32))

    tq = 256
    nqb = s_lat // tq
    past = cache_k.shape[1]
    qrow0 = t_ctx // tq
    krow0 = t_ctx // s_lat
    att = pl.pallas_call(
        functools.partial(_att_lat_kernel, past=past),
        grid=(b_lat, ATT_KV_HEADS, nqb),
        in_specs=[pl.BlockSpec((tq, gw), lambda b, h, i: (qrow0 + b * nqb + i, h)),
                  pl.BlockSpec((s_lat, ATT_HD), lambda b, h, i: (krow0 + b, kcol + h)),
                  pl.BlockSpec((s_lat, ATT_HD), lambda b, h, i: (krow0 + b, vcol + h)),
                  pl.BlockSpec((None, past, ATT_HD), lambda b, h, i: (b, 0, h)),
                  pl.BlockSpec((None, past, ATT_HD), lambda b, h, i: (b, 0, h)),
                  pl.BlockSpec((1, ATT_HD), lambda b, h, i: (0, 0)),
                  pl.BlockSpec((1, ATT_HD), lambda b, h, i: (0, 0)),
                  pl.BlockSpec((tq, ATT_HD), lambda b, h, i: (i, 0)),
                  pl.BlockSpec((tq, ATT_HD), lambda b, h, i: (i, 0)),
                  pl.BlockSpec((s_lat, ATT_HD), lambda b, h, i: (0, 0)),
                  pl.BlockSpec((s_lat, ATT_HD), lambda b, h, i: (0, 0)),
                  pl.BlockSpec(memory_space=pl.ANY)],
        out_specs=pl.BlockSpec((tq, gw), lambda b, h, i: (qrow0 + b * nqb + i, h)),
        out_shape=jax.ShapeDtypeStruct(att.shape, F32),
        scratch_shapes=[pltpu.VMEM((past + s_lat, ATT_HD), BF16), pltpu.VMEM((past + s_lat, ATT_HD), BF16)],
        input_output_aliases={11: 0},
        compiler_params=_cparams(("arbitrary", "arbitrary", "arbitrary")),
        name="att_lat",
    )(u, u, u, cache_k, cache_v, qn, kn, cos, sin, cos, sin, att)
    return att, k_ctx


def _rope_tables(n_tokens):
    pos = jnp.arange(n_tokens)
    row = (pos // GRID_W).astype(F32)
    col = (pos % GRID_W).astype(F32)
    n_freq = ATT_HD // 4
    inv_freq = ROPE_THETA ** (-jnp.arange(n_freq, dtype=F32) / n_freq)
    ang_r = row[:, None] * inv_freq[None, :]
    ang_c = col[:, None] * inv_freq[None, :]
    cos = jnp.concatenate([jnp.cos(ang_r), jnp.cos(ang_r), jnp.cos(ang_c), jnp.cos(ang_c)], axis=-1)
    sin = jnp.concatenate([-jnp.sin(ang_r), jnp.sin(ang_r), -jnp.sin(ang_c), jnp.sin(ang_c)], axis=-1)
    return cos, sin


def _log_sigmoid(x):
    return jnp.minimum(x, 0.0) - jnp.log1p(jnp.exp(-jnp.abs(x)))


def _mlstm_kernel(*refs, zero_init, seq):
    if zero_init:
        (q_ref, k_ref, v_ref, o_ref, gc_ref, gt_ref, b_ref, ng_ref, alias_ref,
         out_ref, co_ref, no_ref, mo_ref, h_scr, c_scr, n_scr, m_scr) = refs
    else:
        (q_ref, k_ref, v_ref, o_ref, gc_ref, gt_ref, b_ref, ng_ref, c0_ref, n0_ref, m0_ref, alias_ref,
         out_ref, co_ref, no_ref, mo_ref, h_scr, c_scr, n_scr, m_scr) = refs
    del alias_ref
    head = pl.program_id(1)
    nc = seq // CHUNK
    if zero_init:
        c_scr[...] = jnp.zeros(c_scr.shape, F32)
        n_scr[...] = jnp.zeros(n_scr.shape, F32)
        m_scr[...] = jnp.full(m_scr.shape, M_INIT, F32)
    else:
        c_scr[...] = c0_ref[...]
        n_scr[...] = n0_ref[...]
        m_scr[...] = m0_ref[...]

    ti = lax.broadcasted_iota(I32, (CHUNK, CHUNK), 0)
    si = lax.broadcasted_iota(I32, (CHUNK, CHUNK), 1)
    bias = b_ref[...]

    def chunk(c, carry):
        ch = []
        for d in (0, 1):
            cc = c if d == 0 else nc - 1 - c
            rows = pl.ds(pl.multiple_of(cc * CHUNK, CHUNK), CHUNK)
            causal = (si <= ti) if d == 0 else (si >= ti)
            causal_t = (ti <= si) if d == 0 else (ti >= si)
            q = q_ref[rows, :] * (M_HD ** -0.5)
            k = k_ref[rows, :]
            v = v_ref[rows, :]
            gcol = gc_ref[rows, :] + bias
            ig_c = gcol[:, 2 * d:2 * d + 1]
            lf_c = _log_sigmoid(gcol[:, 2 * d + 1:2 * d + 2])
            ig_r = gt_ref[cc, pl.ds(8 * d + head, 1), :] + bias[:, 2 * d:2 * d + 1]
            lf_r = _log_sigmoid(gt_ref[cc, pl.ds(8 * d + M_HEADS + head, 1), :] + bias[:, 2 * d + 1:2 * d + 2])
            bcum_c = jnp.sum(jnp.where(causal, lf_r, 0.0), axis=1, keepdims=True)
            bcum_r = jnp.sum(jnp.where(causal_t, lf_c, 0.0), axis=0, keepdims=True)
            m_st = m_scr[d][:, 0:1]
            dmat = jnp.where(causal, bcum_c - bcum_r + ig_r, -jnp.inf)
            inter = bcum_c + m_st
            m_t = jnp.maximum(inter, jnp.max(dmat, axis=1, keepdims=True))
            b_last = bcum_c[CHUNK - 1:CHUNK, :] if d == 0 else bcum_c[0:1, :]
            g_c = b_last - bcum_c + ig_c
            m_new = jnp.maximum(b_last + m_st, jnp.max(g_c, axis=0, keepdims=True))
            ch.append(dict(d=d, rows=rows, q=q, qb=q.astype(BF16), kb=k.astype(BF16), vb=v.astype(BF16),
                           c_st=c_scr[d], n_st=n_scr[d], m_t=m_t, m_new=m_new,
                           w_intra=jnp.exp(dmat - m_t), w_inter=jnp.exp(inter - m_t),
                           decay=jnp.exp(b_last + m_st - m_new), kw=k * jnp.exp(g_c - m_new)))
        for x in ch:
            x['s_qk'] = _dg(x['qb'], x['kb'], _NT) * x['w_intra']
        for x in ch:
            x['qc'] = _dg(x['qb'], x['c_st'].astype(BF16), _NN)
        for x in ch:
            x['upd'] = _dg(x['kw'].astype(BF16), x['vb'], _TN)
        for x in ch:
            x['sv'] = _dg(x['s_qk'].astype(BF16), x['vb'], _NN)
        for x in ch:
            d = x['d']
            num = x['w_inter'] * x['qc'] + x['sv']
            den = (x['w_inter'] * jnp.sum(x['q'] * x['n_st'], axis=1, keepdims=True)
                   + jnp.sum(x['s_qk'], axis=1, keepdims=True))
            h_scr[d, x['rows'], :] = num / jnp.maximum(jnp.abs(den), jnp.exp(-x['m_t']))
            c_scr[d] = x['decay'] * x['c_st'] + x['upd']
            n_scr[d] = x['decay'] * x['n_st'] + jnp.sum(x['kw'], axis=0, keepdims=True)
            m_scr[d] = jnp.broadcast_to(x['m_new'], (1, M_HD))
        return carry

    lax.fori_loop(0, nc, chunk, 0)
    h = h_scr[0] + h_scr[1]
    h = h * lax.rsqrt(jnp.mean(h * h, axis=-1, keepdims=True) + NORM_EPS) * ng_ref[...]
    out_ref[...] = h * _sigmoid(o_ref[...])
    co_ref[...] = c_scr[...]
    no_ref[...] = n_scr[...]
    mo_ref[...] = m_scr[...]


def _mlstm(u, gcol, gt, bias, norm_g, states, prev_out, row0, batch, seq):
    ntok = u.shape[0]
    zero_init = states is None
    qc = COL_M // M_HD
    blk = lambda off: pl.BlockSpec((seq, M_HD), lambda b, h: (row0 + b, off + h))
    in_specs = [blk(qc), blk(qc + M_HEADS), blk(qc + 2 * M_HEADS), blk(qc + 3 * M_HEADS),
                pl.BlockSpec((None, seq, 4), lambda b, h: (h, row0 + b, 0)),
                pl.BlockSpec((seq // CHUNK, 16, CHUNK), lambda b, h: (row0 + b, 0, 0)),
                pl.BlockSpec((None, 1, 4), lambda b, h: (h, 0, 0)),
                pl.BlockSpec((1, M_HD), lambda b, h: (0, h))]
    args = [u, u, u, u, gcol, gt, bias, norm_g]
    if not zero_init:
        in_specs += [pl.BlockSpec((None, 2, None, M_HD, M_HD), lambda b, h: (b, 0, h, 0, 0)),
                     pl.BlockSpec((None, 2, None, 1, M_HD), lambda b, h: (b, 0, h, 0, 0)),
                     pl.BlockSpec((None, 2, None, 1, M_HD), lambda b, h: (b, 0, h, 0, 0))]
        args += list(states)
    in_specs.append(pl.BlockSpec(memory_space=pl.ANY))
    args.append(prev_out)
    return pl.pallas_call(
        functools.partial(_mlstm_kernel, zero_init=zero_init, seq=seq),
        grid=(batch, M_HEADS),
        in_specs=in_specs,
        out_specs=[pl.BlockSpec((seq, M_HD), lambda b, h: (row0 + b, h)),
                   pl.BlockSpec((None, 2, None, M_HD, M_HD), lambda b, h: (b, 0, h, 0, 0)),
                   pl.BlockSpec((None, 2, None, 1, M_HD), lambda b, h: (b, 0, h, 0, 0)),
                   pl.BlockSpec((None, 2, None, 1, M_HD), lambda b, h: (b, 0, h, 0, 0))],
        out_shape=[jax.ShapeDtypeStruct((ntok, M_HEADS * M_HD), F32),
                   jax.ShapeDtypeStruct((batch, 2, M_HEADS, M_HD, M_HD), F32),
                   jax.ShapeDtypeStruct((batch, 2, M_HEADS, 1, M_HD), F32),
                   jax.ShapeDtypeStruct((batch, 2, M_HEADS, 1, M_HD), F32)],
        scratch_shapes=[pltpu.VMEM((2, seq, M_HD), F32), pltpu.VMEM((2, M_HD, M_HD), F32),
                        pltpu.VMEM((2, 1, M_HD), F32), pltpu.VMEM((2, 1, M_HD), F32)],
        input_output_aliases={len(args) - 1: 0},
        compiler_params=_cparams(("arbitrary", "arbitrary")),
        name="mlstm_ctx" if zero_init else "mlstm_lat",
    )(*args)


def _softplus(x):
    return jnp.maximum(x, 0.0) + jnp.log1p(jnp.exp(-jnp.abs(x)))


def _rwkv_kernel(*refs, zero_init, seq):
    if zero_init:
        (r_ref, k_ref, v_ref, xl_ref, xg_ref, mur_ref, muk_ref, muv_ref, mul_ref, mug_ref,
         w0_ref, wup_ref, a0_ref, aup_ref, gup_ref, kk_ref, ka_ref, rk_ref, lng_ref, lnb_ref, alias_ref,
         out_ref, so_ref, r_scr, v_scr, kk_scr, g_scr, bonus_scr, lw_scr, kd_scr, b_scr, y_scr, s_scr) = refs
    else:
        (r_ref, k_ref, v_ref, xl_ref, xg_ref, mur_ref, muk_ref, muv_ref, mul_ref, mug_ref,
         w0_ref, wup_ref, a0_ref, aup_ref, gup_ref, kk_ref, ka_ref, rk_ref, lng_ref, lnb_ref, s0_ref, alias_ref,
         out_ref, so_ref, r_scr, v_scr, kk_scr, g_scr, bonus_scr, lw_scr, kd_scr, b_scr, y_scr, s_scr) = refs
    del alias_ref
    nc = seq // CHUNK
    hd = R_HD
    row = lax.broadcasted_iota(I32, (seq, LANES), 0)

    def tshift(x_ref, mu_ref):
        x = x_ref[...]
        prev = jnp.where(row == 0, 0.0, pltpu.roll(x, 1, 0))
        nxt = jnp.where(row == seq - 1, 0.0, pltpu.roll(x, seq - 1, 0))
        return x + mu_ref[...] * (0.5 * (prev + nxt) - x)

    r = tshift(r_ref, mur_ref)
    k = tshift(k_ref, muk_ref)
    v = tshift(v_ref, muv_ref)
    xl = tshift(xl_ref, mul_ref)
    xg = tshift(xg_ref, mug_ref)
    g = _dot(_sigmoid(xg), gup_ref[...])
    kkp = k * kk_ref[...]
    kk = jnp.concatenate(
        [kkp[:, i * hd:(i + 1) * hd]
         * lax.rsqrt(jnp.maximum(jnp.sum(kkp[:, i * hd:(i + 1) * hd] ** 2, axis=-1, keepdims=True), 1e-24))
         for i in range(2)], axis=1)
    tw = jnp.tanh(xl[:, 0:hd])
    xa = xl[:, hd:2 * hd]
    r_scr[...] = r
    v_scr[...] = v
    kk_scr[...] = kk
    g_scr[...] = g
    rkk = r * k * rk_ref[...]
    bonus_scr[...] = jnp.concatenate(
        [jnp.sum(rkk[:, i * hd:(i + 1) * hd], axis=-1, keepdims=True) * v[:, i * hd:(i + 1) * hd] for i in range(2)],
        axis=1)
    for d in (0, 1):
        wd = -_softplus(-(w0_ref[d] + _dot(tw, wup_ref[d]))) - 0.5
        ad = _sigmoid(a0_ref[d] + _dot(xa, aup_ref[d]))
        lw_scr[d] = -jnp.exp(wd)
        kd_scr[d] = k * (1.0 + (ad - 1.0) * ka_ref[...])
        b_scr[d] = kk * ad
    if zero_init:
        s_scr[...] = jnp.zeros(s_scr.shape, F32)
    else:
        s_scr[...] = s0_ref[...]

    ti = lax.broadcasted_iota(I32, (CHUNK, CHUNK), 0)
    si = lax.broadcasted_iota(I32, (CHUNK, CHUNK), 1)
    eye = (ti == si).astype(F32)

    def chunk(c, carry):
        chains = []
        rows_d = []
        for d in (0, 1):
            cc = c if d == 0 else nc - 1 - c
            rows = pl.ds(pl.multiple_of(cc * CHUNK, CHUNK), CHUNK)
            rows_d.append(rows)
            incl = (si <= ti) if d == 0 else (si >= ti)
            strict = (si < ti) if d == 0 else (si > ti)
            lw = lw_scr[d, rows, :]
            lw_hi, lw_lo = _split(lw)
            tri = jnp.where(incl, 1.0, 0.0).astype(BF16)
            lc = _dg(tri, lw_hi, _NN) + _dg(tri, lw_lo, _NN)
            l_last = lc[CHUNK - 1:CHUNK, :] if d == 0 else lc[0:1, :]
            e_neg = jnp.exp(-lc)
            e_end = jnp.exp(l_last - lc)
            vc = v_scr[rows, :]
            kdc = kd_scr[d, rows, :]
            bc = b_scr[d, rows, :]
            rt = r_scr[rows, :] * jnp.exp(lc)
            kkt = kk_scr[rows, :] * jnp.exp(lc - lw)
            kh = kdc * e_neg
            bh = bc * e_neg
            kbar = kdc * e_end
            bbar = bc * e_end
            w_end = jnp.exp(l_last)
            for i in range(2):
                sl = slice(i * hd, (i + 1) * hd)
                chains.append(dict(
                    d=d, i=i, incl=incl, strict=strict,
                    lhs=jnp.concatenate([kkt[:, sl], rt[:, sl]], axis=0),
                    rhs=jnp.concatenate([bh[:, sl], kh[:, sl]], axis=0),
                    end=jnp.concatenate([kbar[:, sl], bbar[:, sl]], axis=0),
                    v=vc[:, sl], w_end=w_end[:, sl], s0=s_scr[d, i]))
        for ch in chains:
            ch['ab'] = _dot3(ch['lhs'], ch['rhs'], _NT)
        for ch in chains:
            ch['proj'] = _dot3(ch['lhs'], ch['s0'], _NT)
        for ch in chains:
            ab = ch['ab']
            ch['a_kb'] = jnp.where(ch['strict'], ab[:CHUNK, :CHUNK], 0.0)
            ch['b_rb'] = jnp.where(ch['incl'], ab[CHUNK:, :CHUNK], 0.0)
            ch['akk_brk'] = jnp.concatenate([jnp.where(ch['strict'], ab[:CHUNK, CHUNK:], 0.0),
                                             jnp.where(ch['incl'], ab[CHUNK:, CHUNK:], 0.0)], axis=0)
        for ch in chains:
            ch['p'] = _dot3(ch['a_kb'], ch['a_kb'])
        for ch in chains:
            ch['abv'] = _dot3(ch['akk_brk'], ch['v'])
        for ch in chains:
            inv = eye - ch['a_kb']
            ch['inv'] = inv + _dot3(inv, ch['p'])
        span = 4
        while span < CHUNK:
            for ch in chains:
                ch['p'] = _dot1(ch['p'], ch['p'])
            for ch in chains:
                ch['inv'] = ch['inv'] + _dot1(ch['inv'], ch['p'])
            span *= 2
        for ch in chains:
            ch['u'] = _dot3(ch['inv'], ch['proj'][:CHUNK] + ch['abv'][:CHUNK])
        for ch in chains:
            ch['y'] = ch['proj'][CHUNK:] + ch['abv'][CHUNK:] - _dot3(ch['b_rb'], ch['u'])
        for ch in chains:
            upd = _dot3(jnp.concatenate([ch['v'], -ch['u']], axis=0), ch['end'], _TN)
            s_scr[ch['d'], ch['i']] = ch['s0'] * ch['w_end'] + upd
        for d in (0, 1):
            y_scr[d, rows_d[d], :] = jnp.concatenate([chains[2 * d]['y'], chains[2 * d + 1]['y']], axis=1)
        return carry

    lax.fori_loop(0, nc, chunk, 0)

    y = y_scr[0] + y_scr[1]
    outs = []
    for i in range(2):
        yh = y[:, i * hd:(i + 1) * hd]
        mean = jnp.mean(yh, axis=-1, keepdims=True)
        var = jnp.mean(jnp.square(yh - mean), axis=-1, keepdims=True)
        outs.append((yh - mean) * lax.rsqrt(var + GN_EPS))
    yn = jnp.concatenate(outs, axis=1)
    out_ref[...] = (yn * lng_ref[...] + lnb_ref[...] + bonus_scr[...]) * g_scr[...]
    so_ref[...] = s_scr[...]


def _rwkv(u, p, state, prev_out, row0, batch, seq):
    ntok = u.shape[0]
    zero_init = state is None
    rc = COL_R // LANES
    nhp = R_HEADS // 2
    ublk = lambda off: pl.BlockSpec((seq, LANES), lambda b, h: (row0 + b, rc + off + h))
    ufix = lambda off: pl.BlockSpec((seq, LANES), lambda b, h: (row0 + b, rc + off))
    mblk = lambda off: pl.BlockSpec((1, LANES), lambda b, h: (0, off + h))
    mfix = lambda off: pl.BlockSpec((1, LANES), lambda b, h: (0, off))
    vec = pl.BlockSpec((1, LANES), lambda b, h: (0, h))
    in_specs = [ublk(0), ublk(nhp), ublk(2 * nhp), ufix(3 * nhp), ufix(3 * nhp + 1),
                mblk(0), mblk(nhp), mblk(2 * nhp), mfix(3 * nhp), mfix(3 * nhp + 1),
                pl.BlockSpec((2, 1, LANES), lambda b, h: (0, 0, h)),
                pl.BlockSpec((2, R_HD, LANES), lambda b, h: (0, 0, h)),
                pl.BlockSpec((2, 1, LANES), lambda b, h: (0, 0, h)),
                pl.BlockSpec((2, R_HD, LANES), lambda b, h: (0, 0, h)),
                pl.BlockSpec((LANES, LANES), lambda b, h: (0, h)),
                vec, vec, vec, vec, vec]
    args = [u, u, u, u, u, p['mu'], p['mu'], p['mu'], p['mu'], p['mu'],
            p['w0'], p['w_up'], p['a0'], p['a_up'], p['g_up'], p['k_k'], p['k_a'], p['r_k'], p['ln_g'], p['ln_b']]
    if not zero_init:
        in_specs.append(pl.BlockSpec((None, 2, 2, R_HD, R_HD), lambda b, h: (b, 0, h, 0, 0)))
        args.append(state)
    in_specs.append(pl.BlockSpec(memory_space=pl.ANY))
    args.append(prev_out)
    big = lambda n: pltpu.VMEM((n, seq, LANES), F32)
    return pl.pallas_call(
        functools.partial(_rwkv_kernel, zero_init=zero_init, seq=seq),
        grid=(batch, nhp),
        in_specs=in_specs,
        out_specs=[pl.BlockSpec((seq, LANES), lambda b, h: (row0 + b, h)),
                   pl.BlockSpec((None, 2, 2, R_HD, R_HD), lambda b, h: (b, 0, h, 0, 0))],
        out_shape=[jax.ShapeDtypeStruct((ntok, R_HEADS * R_HD), F32),
                   jax.ShapeDtypeStruct((batch, 2, R_HEADS, R_HD, R_HD), F32)],
        scratch_shapes=[pltpu.VMEM((seq, LANES), F32)] * 5
                       + [big(2), big(2), big(2), big(2), pltpu.VMEM((2, 2, R_HD, R_HD), F32)],
        input_output_aliases={len(args) - 1: 0},
        compiler_params=_cparams(("arbitrary", "arbitrary")),
        name="rwkv_ctx" if zero_init else "rwkv_lat",
    )(*args)


def _top2_sum(a, b, c, d):
    m1, n1 = jnp.maximum(a, b), jnp.minimum(a, b)
    m2, n2 = jnp.maximum(c, d), jnp.minimum(c, d)
    return jnp.maximum(m1, m2) + jnp.maximum(jnp.minimum(m1, m2), jnp.maximum(n1, n2))


def _first_argmax(vals):
    best = functools.reduce(jnp.maximum, vals)
    idx = jnp.full(best.shape, len(vals) - 1, I32)
    for j in range(len(vals) - 2, -1, -1):
        idx = jnp.where(vals[j] == best, j, idx)
    return best, idx


def _out_kernel(att_ref, m_ref, r_ref, x_ref, w_ref, g1_ref, sh2_ref, sc2_ref, n2_ref, rw_ref, rb_ref,
                x1_ref, h2_ref, route_ref, cnt_ref, cnt_scr):
    @pl.when(pl.program_id(0) == 0)
    def _():
        cnt_scr[...] = jnp.zeros(cnt_scr.shape, F32)

    na = att_ref.shape[1]
    nm = m_ref.shape[1]
    mix = (jnp.dot(att_ref[...].astype(BF16), w_ref[0:na, :], preferred_element_type=F32)
           + jnp.dot(m_ref[...].astype(BF16), w_ref[na:na + nm, :], preferred_element_type=F32)
           + jnp.dot(r_ref[...].astype(BF16), w_ref[na + nm:, :], preferred_element_type=F32))
    x1 = x_ref[...] + g1_ref[...] * mix
    x1_ref[...] = x1
    h2 = _rms(x1, n2_ref[...]) * (1.0 + sc2_ref[...]) + sh2_ref[...]
    h2_ref[...] = h2
    logits = _dot3(rw_ref[...], h2, _NT)
    s = _sigmoid(logits)
    ssel = s + rb_ref[...]
    srow = [s[e:e + 1, :] for e in range(N_EXPERTS)]
    brow = [ssel[e:e + 1, :] for e in range(N_EXPERTS)]
    gscore = [_top2_sum(*brow[EXPERTS_PER_GROUP * g:EXPERTS_PER_GROUP * (g + 1)]) for g in range(N_EXPERT_GROUPS)]
    _, gidx = _first_argmax(gscore)
    pick = lambda rows, j: functools.reduce(
        lambda acc, g: jnp.where(gidx == g, rows[EXPERTS_PER_GROUP * g + j], acc),
        range(N_EXPERT_GROUPS - 2, -1, -1), rows[EXPERTS_PER_GROUP * (N_EXPERT_GROUPS - 1) + j])
    ing = [pick(brow, j) for j in range(EXPERTS_PER_GROUP)]
    sin_ = [pick(srow, j) for j in range(EXPERTS_PER_GROUP)]
    _, l1 = _first_argmax(ing)
    _, l2 = _first_argmax([jnp.where(l1 == j, -jnp.inf, ing[j]) for j in range(EXPERTS_PER_GROUP)])
    sel = lambda l: functools.reduce(lambda acc, j: jnp.where(l == j, sin_[j], acc),
                                     range(EXPERTS_PER_GROUP - 2, -1, -1), sin_[EXPERTS_PER_GROUP - 1])
    w1, w2 = sel(l1), sel(l2)
    tot = w1 + w2
    e1 = gidx * EXPERTS_PER_GROUP + l1
    e2 = gidx * EXPERTS_PER_GROUP + l2
    tm = e1.shape[1]
    eid = lax.broadcasted_iota(I32, (N_EXPERTS, tm), 0)
    oh1 = eid == e1
    oh2 = eid == e2
    picked = jnp.where(jnp.logical_or(oh1, oh2), 1.0, 0.0)
    earlier = jnp.where(lax.broadcasted_iota(I32, (tm, tm), 0) < lax.broadcasted_iota(I32, (tm, tm), 1), 1.0, 0.0)
    rank = cnt_scr[:, 0:1] + jnp.dot(picked.astype(BF16), earlier.astype(BF16), preferred_element_type=F32)
    pos1 = jnp.sum(jnp.where(oh1, rank, 0.0), axis=0, keepdims=True)
    pos2 = jnp.sum(jnp.where(oh2, rank, 0.0), axis=0, keepdims=True)
    cnt = cnt_scr[...] + jnp.sum(picked, axis=1, keepdims=True)
    cnt_scr[...] = cnt
    cnt_ref[...] = cnt
    zero = jnp.zeros_like(w1)
    route_ref[...] = jnp.concatenate([e1.astype(F32), e2.astype(F32), w1 / tot, w2 / tot, pos1, pos2, zero, zero],
                                     axis=0)


def _out_proj(att, m_out, r_out, x, w_out, mod3, n2g, rw_t, rb, t_ctx, s_lat):
    ntok, d = x.shape
    tm = 256
    row = lambda i: _mod_row(i * tm, t_ctx, s_lat)
    modblk = lambda j: pl.BlockSpec((None, 1, d), lambda i: (row(i), 0, j))
    return pl.pallas_call(
        _out_kernel,
        grid=(ntok // tm,),
        in_specs=[pl.BlockSpec((tm, att.shape[1]), lambda i: (i, 0)),
                  pl.BlockSpec((tm, m_out.shape[1]), lambda i: (i, 0)),
                  pl.BlockSpec((tm, r_out.shape[1]), lambda i: (i, 0)),
                  pl.BlockSpec((tm, d), lambda i: (i, 0)),
                  pl.BlockSpec(w_out.shape, lambda i: (0, 0)),
                  modblk(2), modblk(3), modblk(4),
                  pl.BlockSpec((1, d), lambda i: (0, 0)),
                  pl.BlockSpec(rw_t.shape, lambda i: (0, 0)),
                  pl.BlockSpec(rb.shape, lambda i: (0, 0))],
        out_specs=[pl.BlockSpec((tm, d), lambda i: (i, 0)),
                   pl.BlockSpec((tm, d), lambda i: (i, 0)),
                   pl.BlockSpec((8, tm), lambda i: (0, i)),
                   pl.BlockSpec((N_EXPERTS, LANES), lambda i: (0, 0))],
        out_shape=[jax.ShapeDtypeStruct((ntok, d), F32), jax.ShapeDtypeStruct((ntok, d), F32),
                   jax.ShapeDtypeStruct((8, ntok), F32), jax.ShapeDtypeStruct((N_EXPERTS, LANES), F32)],
        scratch_shapes=[pltpu.VMEM((N_EXPERTS, LANES), F32)],
        compiler_params=_cparams(("arbitrary",)),
        name="out_proj",
    )(att, m_out, r_out, x, w_out, mod3, mod3, mod3, n2g, rw_t, rb)


def _row_gather(src_hbm, idx_ref, base, dst, sem, n, unrolled):
    def start(j):
        pltpu.make_async_copy(src_hbm.at[pl.ds(idx_ref[base + j], 1), :], dst.at[pl.ds(j, 1), :], sem).start()

    if unrolled:
        for j in range(n):
            start(j)
    else:
        def body(j, c):
            start(j)
            return c
        lax.fori_loop(0, n, body, 0)


def _expert_kernel(be_ref, tok_ref, nused_ref, h_hbm, w1_ref, w3_ref, w2_ref, y_ref, xbuf, w1b, w3b, w2b, sem):
    i = pl.program_id(0)
    n_used = nused_ref[0]
    wait = lambda s: pltpu.make_async_copy(h_hbm.at[pl.ds(0, EXPERT_ROWS), :], xbuf.at[s], sem.at[s]).wait()

    @pl.when(i < n_used)
    def _():
        slot = i % 2

        @pl.when(i == 0)
        def _():
            _row_gather(h_hbm, tok_ref, 0, xbuf.at[0], sem.at[0], EXPERT_ROWS, unrolled=False)

        @pl.when(jnp.logical_or(i == 0, be_ref[i] != be_ref[jnp.maximum(i - 1, 0)]))
        def _():
            w1b[...] = w1_ref[...].astype(BF16)
            w3b[...] = w3_ref[...].astype(BF16)
            w2b[...] = w2_ref[...].astype(BF16)

        wait(slot)
        _row_gather(h_hbm, tok_ref, (i + 1) * EXPERT_ROWS, xbuf.at[1 - slot], sem.at[1 - slot], EXPERT_ROWS,
                    unrolled=True)
        xb = xbuf[slot].astype(BF16)
        a = jnp.dot(xb, w1b[...], preferred_element_type=F32)
        b = jnp.dot(xb, w3b[...], preferred_element_type=F32)
        hmid = (a * _sigmoid(a)) * b
        y_ref[...] = jnp.dot(hmid.astype(BF16), w2b[...], preferred_element_type=F32)

        @pl.when(i == n_used - 1)
        def _():
            wait(1 - slot)

    @pl.when(i >= n_used)
    def _():
        y_ref[...] = jnp.zeros(y_ref.shape, F32)


def _experts(h2, block_e, row_tok, n_used, w1, w3, w2, layer):
    ntok, d = h2.shape
    de = w1.shape[3]
    n_rows = row_tok.shape[0] - EXPERT_ROWS
    nb = n_rows // EXPERT_ROWS
    return pl.pallas_call(
        _expert_kernel,
        grid_spec=pltpu.PrefetchScalarGridSpec(
            num_scalar_prefetch=3,
            grid=(nb,),
            in_specs=[pl.BlockSpec(memory_space=pl.ANY),
                      pl.BlockSpec((None, None, d, de), lambda i, be, tok, nu: (layer, be[i], 0, 0)),
                      pl.BlockSpec((None, None, d, de), lambda i, be, tok, nu: (layer, be[i], 0, 0)),
                      pl.BlockSpec((None, None, de, d), lambda i, be, tok, nu: (layer, be[i], 0, 0))],
            out_specs=pl.BlockSpec((EXPERT_ROWS, d), lambda i, be, tok, nu: (i, 0)),
            scratch_shapes=[pltpu.VMEM((2, EXPERT_ROWS, d), F32), pltpu.VMEM((d, de), BF16),
                            pltpu.VMEM((d, de), BF16), pltpu.VMEM((de, d), BF16), pltpu.SemaphoreType.DMA((2,))]),
        out_shape=jax.ShapeDtypeStruct((n_rows, d), F32),
        compiler_params=_cparams(("arbitrary",)),
        name="experts",
    )(block_e, row_tok, n_used, h2, w1, w3, w2)


def _combine_kernel(d1_ref, d2_ref, y_hbm, x1_ref, g2_ref, gate_ref, o_ref, ybuf, sem):
    tm = x1_ref.shape[0]
    i = pl.program_id(0)
    slot = i % 2

    def gather(tile, s, unrolled):
        _row_gather(y_hbm, d1_ref, tile * tm, ybuf.at[s, 0], sem.at[s], tm, unrolled)
        _row_gather(y_hbm, d2_ref, tile * tm, ybuf.at[s, 1], sem.at[s], tm, unrolled)

    def wait(s):
        pltpu.make_async_copy(y_hbm.at[pl.ds(0, tm), :], ybuf.at[s, 0], sem.at[s]).wait()
        pltpu.make_async_copy(y_hbm.at[pl.ds(0, tm), :], ybuf.at[s, 1], sem.at[s]).wait()

    @pl.when(i == 0)
    def _():
        gather(0, 0, False)

    wait(slot)
    gather(i + 1, 1 - slot, True)
    gate = gate_ref[...]
    moe = ybuf[slot, 0] * gate[:, 0:1] + ybuf[slot, 1] * gate[:, 1:2]
    o_ref[...] = x1_ref[...] + g2_ref[...] * moe

    @pl.when(i == pl.num_programs(0) - 1)
    def _():
        wait(1 - slot)


def _combine(y, x1, mod3, gate, dest1, dest2, t_ctx, s_lat):
    ntok, d = x1.shape
    tm = 256
    row = lambda i: _mod_row(i * tm, t_ctx, s_lat)
    return pl.pallas_call(
        _combine_kernel,
        grid_spec=pltpu.PrefetchScalarGridSpec(
            num_scalar_prefetch=2,
            grid=(ntok // tm,),
            in_specs=[pl.BlockSpec(memory_space=pl.ANY),
                      pl.BlockSpec((tm, d), lambda i, a, b: (i, 0)),
                      pl.BlockSpec((None, 1, d), lambda i, a, b: (row(i), 0, 5)),
                      pl.BlockSpec((tm, 2), lambda i, a, b: (i, 0))],
            out_specs=pl.BlockSpec((tm, d), lambda i, a, b: (i, 0)),
            scratch_shapes=[pltpu.VMEM((2, 2, tm, d), F32), pltpu.SemaphoreType.DMA((2,))]),
        out_shape=jax.ShapeDtypeStruct((ntok, d), F32),
        compiler_params=_cparams(("arbitrary",)),
        name="combine",
    )(dest1, dest2, y, x1, mod3, gate)


def _dispatch(route, counts, tile):
    ntok = route.shape[1]
    e = route[0:2].astype(I32)
    pos = route[4:6].astype(I32)
    gate = route[2:4].T
    counts = counts[:, 0].astype(I32)
    padded = (counts + EXPERT_ROWS - 1) // EXPERT_ROWS * EXPERT_ROWS
    pad_end = jnp.cumsum(padded)
    pad_start = pad_end - padded
    dest = jnp.take(pad_start, e) + pos
    n_rows = -(-(2 * ntok) // EXPERT_ROWS) * EXPERT_ROWS + N_EXPERTS * EXPERT_ROWS
    nb = n_rows // EXPERT_ROWS
    tok = jnp.broadcast_to(jnp.arange(ntok, dtype=I32)[None, :], (2, ntok))
    row_tok = jnp.zeros((n_rows + EXPERT_ROWS,), I32).at[dest.reshape(-1)].set(tok.reshape(-1))
    blk_start = jnp.arange(nb, dtype=I32) * EXPERT_ROWS
    block_e = jnp.minimum(jnp.sum((pad_end[None, :] <= blk_start[:, None]).astype(I32), axis=1), N_EXPERTS - 1)
    n_used = pad_end[-1:] // EXPERT_ROWS
    dest = jnp.pad(dest, ((0, 0), (0, tile)))
    return row_tok, block_e, n_used, gate, dest[0], dest[1]


def kernel(x_prompt, x_sample, c, cache_attn_k, cache_attn_v, state_mlstm_C, state_mlstm_n, state_mlstm_m, state_rwkv, c_ctx, norm1_g, norm2_g, w_mod, b_mod, w_in, w_out, attn_q_norm, attn_k_norm, mlstm_i_bias, mlstm_f_bias, mlstm_norm_g, rwkv_mu, rwkv_w0, rwkv_w_up, rwkv_a0, rwkv_a_up, rwkv_g_up, rwkv_k_k, rwkv_k_a, rwkv_r_k, rwkv_ln_g, rwkv_ln_b, router_w, router_b, exp_w1, exp_w3, exp_w2):
    b_ctx, s_ctx, d = x_prompt.shape
    b_lat, s_lat, _ = x_sample.shape
    depth = w_in.shape[0]
    t_ctx = b_ctx * s_ctx
    ntok = t_ctx + b_lat * s_lat
    assert b_lat + 1 <= 8 and s_lat % 512 == 0 and t_ctx % 512 == 0 and t_ctx % s_lat == 0
    past = cache_attn_k.shape[2]

    x = jnp.concatenate([x_prompt.reshape(t_ctx, d), x_sample.reshape(b_lat * s_lat, d)], axis=0)
    c_all = jnp.zeros((8, d), F32).at[0].set(c_ctx).at[1:1 + b_lat].set(c)
    mod = _modulation(c_all, w_mod, b_mod)
    cos, sin = _rope_tables(s_lat)
    rw_t = router_w.T
    rb = router_b.reshape(N_EXPERTS, 1)
    m_width = M_HEADS * M_HD
    r_width = R_HEADS * R_HD
    n_in = w_in.shape[2]
    gate_lo = COL_M + 4 * m_width

    ks, vs, cs, ns, ms, rs = [], [], [], [], [], []
    for l in range(depth):
        w = w_in[l]
        w_p = jnp.concatenate([w[:, :gate_lo], w[:, gate_lo + 4 * M_HEADS:], w[:, gate_lo:gate_lo + 4 * M_HEADS],
                               jnp.zeros((d, N_IN_PAD - n_in), F32)], axis=1).astype(BF16)
        mod3 = mod[l].reshape(8, 1, 6 * d)
        u = _in_proj(x, norm1_g[l][None], mod3, w_p, t_ctx, s_lat)

        ck = cache_attn_k[:, l].reshape(b_lat, past, ATT_KV_HEADS * ATT_HD)
        cv = cache_attn_v[:, l].reshape(b_lat, past, ATT_KV_HEADS * ATT_HD)
        att, k_ctx = _attention(u, attn_q_norm[l][None], attn_k_norm[l][None], ck, cv, cos, sin,
                                b_ctx, s_ctx, b_lat, s_lat)
        ks.append(k_ctx.reshape(b_ctx, s_ctx, ATT_KV_HEADS, ATT_HD))
        vcol = ATT_GROUPS * ATT_KV_HEADS * ATT_HD + ATT_KV_HEADS * ATT_HD
        vs.append(u[:t_ctx, vcol:vcol + ATT_KV_HEADS * ATT_HD].reshape(b_ctx, s_ctx, ATT_KV_HEADS, ATT_HD))

        gates = u[:, COL_GATE:COL_GATE + 4 * M_HEADS]
        gcol = gates.reshape(ntok, 2, 2, M_HEADS).transpose(3, 0, 1, 2).reshape(M_HEADS, ntok, 4)
        gt = gates.reshape(ntok // CHUNK, CHUNK, 4 * M_HEADS).transpose(0, 2, 1)
        bias = jnp.stack([mlstm_i_bias[l], mlstm_f_bias[l]], axis=1)
        bias = bias.transpose(2, 0, 1).reshape(M_HEADS, 1, 4)
        ng = mlstm_norm_g[l][None]
        m_out = jnp.zeros((ntok, m_width), F32)
        m_out, c_c, n_c, m_c = _mlstm(u, gcol, gt, bias, ng, None, m_out, 0, b_ctx, s_ctx)
        lat_states = (state_mlstm_C[:, l],
                      state_mlstm_n[:, l].reshape(b_lat, 2, M_HEADS, 1, M_HD),
                      jnp.broadcast_to(state_mlstm_m[:, l][..., None, None], (b_lat, 2, M_HEADS, 1, M_HD)))
        m_out, _, _, _ = _mlstm(u, gcol, gt, bias, ng, lat_states, m_out, t_ctx // s_lat, b_lat, s_lat)
        cs.append(c_c)
        ns.append(n_c.reshape(b_ctx, 2, M_HEADS, M_HD))
        ms.append(m_c[:, :, :, 0, 0])

        rp = dict(mu=rwkv_mu[l][None], w0=rwkv_w0[l].reshape(2, 1, r_width), w_up=rwkv_w_up[l],
                  a0=rwkv_a0[l].reshape(2, 1, r_width), a_up=rwkv_a_up[l], g_up=rwkv_g_up[l],
                  k_k=rwkv_k_k[l][None], k_a=rwkv_k_a[l][None], r_k=rwkv_r_k[l].reshape(1, r_width),
                  ln_g=rwkv_ln_g[l][None], ln_b=rwkv_ln_b[l][None])
        r_out = jnp.zeros((ntok, r_width), F32)
        r_out, r_c = _rwkv(u, rp, None, r_out, 0, b_ctx, s_ctx)
        r_out, _ = _rwkv(u, rp, state_rwkv[:, l], r_out, t_ctx // s_lat, b_lat, s_lat)
        rs.append(r_c)

        x1, h2, route, counts = _out_proj(att, m_out, r_out, x, w_out[l].astype(BF16), mod3, norm2_g[l][None],
                                          rw_t, rb, t_ctx, s_lat)
        row_tok, block_e, n_used, gate, dest1, dest2 = _dispatch(route, counts, 256)
        y = _experts(h2, block_e, row_tok, n_used, exp_w1, exp_w3, exp_w2, l)
        x = _combine(y, x1, mod3, gate, dest1, dest2, t_ctx, s_lat)

    y_prompt = x[:t_ctx].reshape(b_ctx, s_ctx, d)
    y_sample = x[t_ctx:].reshape(b_lat, s_lat, d)
    return (y_prompt, y_sample, jnp.stack(ks, axis=1), jnp.stack(vs, axis=1), jnp.stack(cs, axis=1),
            jnp.stack(ns, axis=1), jnp.stack(ms, axis=1), jnp.stack(rs, axis=1))
```

```python
import functools

import jax
import jax.numpy as jnp
from jax import lax
from jax.experimental import pallas as pl
from jax.experimental.pallas import tpu as pltpu

F32 = jnp.float32
BF16 = jnp.bfloat16
I32 = jnp.int32

NORM_EPS = 1e-6
GN_EPS = 64e-5
M_INIT = -1e30
GRID_W = 64
ROPE_THETA = 10000.0
ATT_HD = 128
ATT_GROUPS = 4
ATT_KV_HEADS = 2
M_HD = 128
M_HEADS = 4
R_HD = 64
R_HEADS = 8
N_EXPERTS = 16
N_EXPERT_GROUPS = 4
EXPERTS_PER_GROUP = 4
CHUNK = 64
LANES = 128
EXPERT_ROWS = 256
VMEM_LIMIT = 56 * 1024 * 1024
RWKV_HEADS_PER_STEP_CTX = 8
RWKV_HEADS_PER_STEP_LAT = 8

COL_ATT = 0
COL_M = 1536
COL_R = 3584
COL_GATE = 5376
N_IN_PAD = 5632


def _cparams(sem):
    return pltpu.CompilerParams(dimension_semantics=sem, vmem_limit_bytes=VMEM_LIMIT)


def _dot(a, b):
    return jnp.dot(a.astype(BF16), b.astype(BF16), preferred_element_type=F32)


def _dg(a, b, dims):
    return lax.dot_general(a, b, (dims, ((), ())), preferred_element_type=F32)


_NN = ((1,), (0,))
_NT = ((1,), (1,))
_TN = ((0,), (0,))


def _split(a):
    hi = a.astype(BF16)
    lo = (a - hi.astype(F32)).astype(BF16)
    return hi, lo


def _dot3(a, b, dims=_NN):
    ah, al = _split(a)
    bh, bl = _split(b)
    return _dg(ah, bh, dims) + (_dg(ah, bl, dims) + _dg(al, bh, dims))


def _dot1(a, b, dims=_NN):
    return _dg(a.astype(BF16), b.astype(BF16), dims)


def _rms(x, g):
    return x * lax.rsqrt(jnp.mean(x * x, axis=-1, keepdims=True) + NORM_EPS) * g


def _sigmoid(x):
    return 1.0 / (1.0 + jnp.exp(-x))


def _mod_kernel(c_ref, w_ref, b_ref, o_ref):
    c = c_ref[...]
    o_ref[...] = _dot(c * _sigmoid(c), w_ref[...]) + b_ref[...]


def _modulation(c_all, w_mod, b_mod):
    depth, d, n = w_mod.shape
    tn = 1024
    return pl.pallas_call(
        _mod_kernel,
        grid=(depth, n // tn),
        in_specs=[pl.BlockSpec((8, d), lambda l, j: (0, 0)),
                  pl.BlockSpec((None, d, tn), lambda l, j: (l, 0, j)),
                  pl.BlockSpec((None, 1, tn), lambda l, j: (l, 0, j))],
        out_specs=pl.BlockSpec((None, 8, tn), lambda l, j: (l, 0, j)),
        out_shape=jax.ShapeDtypeStruct((depth, 8, n), F32),
        compiler_params=_cparams(("arbitrary", "arbitrary")),
        name="modulation",
    )(c_all, w_mod, b_mod.reshape(depth, 1, n))


def _mod_row(tok0, t_ctx, s_lat):
    return jnp.where(tok0 < t_ctx, 0, 1 + (tok0 - t_ctx) // s_lat)


def _in_kernel(x_ref, g_ref, sh_ref, sc_ref, w_ref, o_ref, h_ref):
    @pl.when(pl.program_id(1) == 0)
    def _():
        h = _rms(x_ref[...], g_ref[...]) * (1.0 + sc_ref[...]) + sh_ref[...]
        h_ref[...] = h.astype(BF16)

    o_ref[...] = jnp.dot(h_ref[...], w_ref[...], preferred_element_type=F32)


def _in_proj(x, g1, mod3, w_p, t_ctx, s_lat):
    ntok, d = x.shape
    n = w_p.shape[1]
    tm, tn = 512, 512
    row = lambda i: _mod_row(i * tm, t_ctx, s_lat)
    return pl.pallas_call(
        _in_kernel,
        grid=(ntok // tm, n // tn),
        in_specs=[pl.BlockSpec((tm, d), lambda i, j: (i, 0)),
                  pl.BlockSpec((1, d), lambda i, j: (0, 0)),
                  pl.BlockSpec((None, 1, d), lambda i, j: (row(i), 0, 0)),
                  pl.BlockSpec((None, 1, d), lambda i, j: (row(i), 0, 1)),
                  pl.BlockSpec((d, tn), lambda i, j: (0, j))],
        out_specs=pl.BlockSpec((tm, tn), lambda i, j: (i, j)),
        out_shape=jax.ShapeDtypeStruct((ntok, n), F32),
        scratch_shapes=[pltpu.VMEM((tm, d), BF16)],
        compiler_params=_cparams(("arbitrary", "arbitrary")),
        name="in_proj",
    )(x, g1, mod3, mod3, w_p)


def _softmax_av(q, kb, vb):
    s = _dg(q.astype(BF16), kb, _NT) * (ATT_HD ** -0.5)
    p = jnp.exp(s - jnp.max(s, axis=-1, keepdims=True))
    l = jnp.sum(p, axis=-1, keepdims=True)
    return jnp.dot(p.astype(BF16), vb, preferred_element_type=F32) / l


def _att_ctx_kernel(q_ref, k_ref, v_ref, qn_ref, kn_ref, alias_ref, o_ref, ko_ref):
    del alias_ref
    k = _rms(k_ref[...], kn_ref[...])
    ko_ref[...] = k
    kb = k.astype(BF16)
    vb = v_ref[...].astype(BF16)
    for g in range(ATT_GROUPS):
        q = _rms(q_ref[:, g * ATT_HD:(g + 1) * ATT_HD], qn_ref[...])
        o_ref[:, g * ATT_HD:(g + 1) * ATT_HD] = _softmax_av(q, kb, vb)


def _rope(x, cos, sin):
    lane = lax.broadcasted_iota(I32, x.shape, 1)
    first = (lane % (ATT_HD // 2)) < (ATT_HD // 4)
    partner = jnp.where(first, pltpu.roll(x, ATT_HD - ATT_HD // 4, 1), pltpu.roll(x, ATT_HD // 4, 1))
    return x * cos + partner * sin


def _att_lat_kernel(q_ref, k_ref, v_ref, ck_ref, cv_ref, qn_ref, kn_ref, cosq_ref, sinq_ref, cosk_ref, sink_ref,
                    alias_ref, o_ref, kb_ref, vb_ref, *, past):
    del alias_ref

    @pl.when(pl.program_id(2) == 0)
    def _():
        k = _rope(_rms(k_ref[...], kn_ref[...]), cosk_ref[...], sink_ref[...])
        kb_ref[0:past, :] = ck_ref[...].astype(BF16)
        kb_ref[past:, :] = k.astype(BF16)
        vb_ref[0:past, :] = cv_ref[...].astype(BF16)
        vb_ref[past:, :] = v_ref[...].astype(BF16)

    kb = kb_ref[...]
    vb = vb_ref[...]
    for g in range(ATT_GROUPS):
        q = _rope(_rms(q_ref[:, g * ATT_HD:(g + 1) * ATT_HD], qn_ref[...]), cosq_ref[...], sinq_ref[...])
        o_ref[:, g * ATT_HD:(g + 1) * ATT_HD] = _softmax_av(q, kb, vb)


def _attention(u, qn, kn, cache_k, cache_v, cos, sin, b_ctx, s_ctx, b_lat, s_lat):
    ntok = u.shape[0]
    t_ctx = b_ctx * s_ctx
    gw = ATT_GROUPS * ATT_HD
    kcol = (ATT_KV_HEADS * gw) // ATT_HD
    vcol = kcol + ATT_KV_HEADS
    att, k_ctx = pl.pallas_call(
        _att_ctx_kernel,
        grid=(b_ctx, ATT_KV_HEADS),
        in_specs=[pl.BlockSpec((s_ctx, gw), lambda b, h: (b, h)),
                  pl.BlockSpec((s_ctx, ATT_HD), lambda b, h: (b, kcol + h)),
                  pl.BlockSpec((s_ctx, ATT_HD), lambda b, h: (b, vcol + h)),
                  pl.BlockSpec((1, ATT_HD), lambda b, h: (0, 0)),
                  pl.BlockSpec((1, ATT_HD), lambda b, h: (0, 0)),
                  pl.BlockSpec(memory_space=pl.ANY)],
        out_specs=[pl.BlockSpec((s_ctx, gw), lambda b, h: (b, h)),
                   pl.BlockSpec((s_ctx, ATT_HD), lambda b, h: (b, h))],
        out_shape=[jax.ShapeDtypeStruct((ntok, ATT_KV_HEADS * gw), F32),
                   jax.ShapeDtypeStruct((t_ctx, ATT_KV_HEADS * ATT_HD), F32)],
        input_output_aliases={5: 0},
        compiler_params=_cparams(("arbitrary", "arbitrary")),
        name="att_ctx",
    )(u, u, u, qn, kn, jnp.zeros((ntok, ATT_KV_HEADS * gw), F32))

    tq = 256
    nqb = s_lat // tq
    past = cache_k.shape[1]
    qrow0 = t_ctx // tq
    krow0 = t_ctx // s_lat
    att = pl.pallas_call(
        functools.partial(_att_lat_kernel, past=past),
        grid=(b_lat, ATT_KV_HEADS, nqb),
        in_specs=[pl.BlockSpec((tq, gw), lambda b, h, i: (qrow0 + b * nqb + i, h)),
                  pl.BlockSpec((s_lat, ATT_HD), lambda b, h, i: (krow0 + b, kcol + h)),
                  pl.BlockSpec((s_lat, ATT_HD), lambda b, h, i: (krow0 + b, vcol + h)),
                  pl.BlockSpec((None, past, ATT_HD), lambda b, h, i: (b, 0, h)),
                  pl.BlockSpec((None, past, ATT_HD), lambda b, h, i: (b, 0, h)),
                  pl.BlockSpec((1, ATT_HD), lambda b, h, i: (0, 0)),
                  pl.BlockSpec((1, ATT_HD), lambda b, h, i: (0, 0)),
                  pl.BlockSpec((tq, ATT_HD), lambda b, h, i: (i, 0)),
                  pl.BlockSpec((tq, ATT_HD), lambda b, h, i: (i, 0)),
                  pl.BlockSpec((s_lat, ATT_HD), lambda b, h, i: (0, 0)),
                  pl.BlockSpec((s_lat, ATT_HD), lambda b, h, i: (0, 0)),
                  pl.BlockSpec(memory_space=pl.ANY)],
        out_specs=pl.BlockSpec((tq, gw), lambda b, h, i: (qrow0 + b * nqb + i, h)),
        out_shape=jax.ShapeDtypeStruct(att.shape, F32),
        scratch_shapes=[pltpu.VMEM((past + s_lat, ATT_HD), BF16), pltpu.VMEM((past + s_lat, ATT_HD), BF16)],
        input_output_aliases={11: 0},
        compiler_params=_cparams(("arbitrary", "arbitrary", "arbitrary")),
        name="att_lat",
    )(u, u, u, cache_k, cache_v, qn, kn, cos, sin, cos, sin, att)
    return att, k_ctx


def _rope_tables(n_tokens):
    pos = jnp.arange(n_tokens)
    row = (pos // GRID_W).astype(F32)
    col = (pos % GRID_W).astype(F32)
    n_freq = ATT_HD // 4
    inv_freq = ROPE_THETA ** (-jnp.arange(n_freq, dtype=F32) / n_freq)
    ang_r = row[:, None] * inv_freq[None, :]
    ang_c = col[:, None] * inv_freq[None, :]
    cos = jnp.concatenate([jnp.cos(ang_r), jnp.cos(ang_r), jnp.cos(ang_c), jnp.cos(ang_c)], axis=-1)
    sin = jnp.concatenate([-jnp.sin(ang_r), jnp.sin(ang_r), -jnp.sin(ang_c), jnp.sin(ang_c)], axis=-1)
    return cos, sin


def _log_sigmoid(x):
    return jnp.minimum(x, 0.0) - jnp.log1p(jnp.exp(-jnp.abs(x)))


def _mlstm_kernel(*refs, zero_init, seq):
    if zero_init:
        (q_ref, k_ref, v_ref, o_ref, gc_ref, gt_ref, br_ref, bc_ref, ng_ref, alias_ref,
         out_ref, co_ref, no_ref, mo_ref, h_scr, c_scr, n_scr, m_scr) = refs
    else:
        (q_ref, k_ref, v_ref, o_ref, gc_ref, gt_ref, br_ref, bc_ref, ng_ref, c0_ref, n0_ref, m0_ref, alias_ref,
         out_ref, co_ref, no_ref, mo_ref, h_scr, c_scr, n_scr, m_scr) = refs
    del alias_ref
    nc = seq // CHUNK
    if zero_init:
        c_scr[...] = jnp.zeros(c_scr.shape, F32)
        n_scr[...] = jnp.zeros(n_scr.shape, F32)
        m_scr[...] = jnp.full(m_scr.shape, M_INIT, F32)
    else:
        c_scr[...] = c0_ref[...]
        n_scr[...] = n0_ref[...]
        m_scr[...] = m0_ref[...]

    ti = lax.broadcasted_iota(I32, (CHUNK, CHUNK), 0)
    si = lax.broadcasted_iota(I32, (CHUNK, CHUNK), 1)
    gate_i = lambda d, h: 2 * M_HEADS * d + h
    gate_f = lambda d, h: 2 * M_HEADS * d + M_HEADS + h

    def chunk(c, carry):
        ch = []
        for d in (0, 1):
            cc = c if d == 0 else nc - 1 - c
            rows = pl.ds(pl.multiple_of(cc * CHUNK, CHUNK), CHUNK)
            causal = (si <= ti) if d == 0 else (si >= ti)
            causal_t = (ti <= si) if d == 0 else (ti >= si)
            q_all = q_ref[rows, :] * (M_HD ** -0.5)
            k_all = k_ref[rows, :]
            v_all = v_ref[rows, :]
            gcol = gc_ref[rows, :] + br_ref[...]
            grow = gt_ref[cc] + bc_ref[...]
            lfc_all = _log_sigmoid(gcol)
            lfr_all = _log_sigmoid(grow)
            for h in range(M_HEADS):
                sl = slice(h * M_HD, (h + 1) * M_HD)
                q, k, v = q_all[:, sl], k_all[:, sl], v_all[:, sl]
                ig_c = gcol[:, gate_i(d, h):gate_i(d, h) + 1]
                lf_c = lfc_all[:, gate_f(d, h):gate_f(d, h) + 1]
                ig_r = grow[gate_i(d, h):gate_i(d, h) + 1, :]
                lf_r = lfr_all[gate_f(d, h):gate_f(d, h) + 1, :]
                bcum_c = jnp.sum(jnp.where(causal, lf_r, 0.0), axis=1, keepdims=True)
                bcum_r = jnp.sum(jnp.where(causal_t, lf_c, 0.0), axis=0, keepdims=True)
                m_st = m_scr[d, h][:, 0:1]
                dmat = jnp.where(causal, bcum_c - bcum_r + ig_r, -jnp.inf)
                inter = bcum_c + m_st
                m_t = jnp.maximum(inter, jnp.max(dmat, axis=1, keepdims=True))
                b_last = bcum_c[CHUNK - 1:CHUNK, :] if d == 0 else bcum_c[0:1, :]
                g_c = b_last - bcum_c + ig_c
                m_new = jnp.maximum(b_last + m_st, jnp.max(g_c, axis=0, keepdims=True))
                ch.append(dict(d=d, h=h, rows=rows, sl=sl, q=q, qb=q.astype(BF16), kb=k.astype(BF16),
                               vb=v.astype(BF16), c_st=c_scr[d, h], n_st=n_scr[d, h], m_t=m_t, m_new=m_new,
                               w_intra=jnp.exp(dmat - m_t), w_inter=jnp.exp(inter - m_t),
                               decay=jnp.exp(b_last + m_st - m_new), kw=k * jnp.exp(g_c - m_new)))
        for x in ch:
            x['s_qk'] = _dg(x['qb'], x['kb'], _NT) * x['w_intra']
        for x in ch:
            x['qc'] = _dg(x['qb'], x['c_st'].astype(BF16), _NN)
        for x in ch:
            x['upd'] = _dg(x['kw'].astype(BF16), x['vb'], _TN)
        for x in ch:
            x['sv'] = _dg(x['s_qk'].astype(BF16), x['vb'], _NN)
        for x in ch:
            d, h = x['d'], x['h']
            num = x['w_inter'] * x['qc'] + x['sv']
            den = (x['w_inter'] * jnp.sum(x['q'] * x['n_st'], axis=1, keepdims=True)
                   + jnp.sum(x['s_qk'], axis=1, keepdims=True))
            h_scr[d, x['rows'], x['sl']] = num / jnp.maximum(jnp.abs(den), jnp.exp(-x['m_t']))
            c_scr[d, h] = x['decay'] * x['c_st'] + x['upd']
            n_scr[d, h] = x['decay'] * x['n_st'] + jnp.sum(x['kw'], axis=0, keepdims=True)
            m_scr[d, h] = jnp.broadcast_to(x['m_new'], (1, M_HD))
        return carry

    lax.fori_loop(0, nc, chunk, 0)
    hsum = h_scr[0] + h_scr[1]
    hn = jnp.concatenate(
        [hsum[:, h * M_HD:(h + 1) * M_HD]
         * lax.rsqrt(jnp.mean(hsum[:, h * M_HD:(h + 1) * M_HD] ** 2, axis=-1, keepdims=True) + NORM_EPS)
         for h in range(M_HEADS)], axis=1)
    out_ref[...] = hn * ng_ref[...] * _sigmoid(o_ref[...])
    co_ref[...] = c_scr[...]
    no_ref[...] = n_scr[...]
    mo_ref[...] = m_scr[...]


def _mlstm(u, gcol, gt, bias, norm_g, states, prev_out, row0, batch, seq):
    ntok = u.shape[0]
    zero_init = states is None
    width = M_HEADS * M_HD
    qc = COL_M // width
    ngate = 4 * M_HEADS
    blk = lambda off: pl.BlockSpec((seq, width), lambda b: (row0 + b, off))
    st_c = pl.BlockSpec((None, 2, M_HEADS, M_HD, M_HD), lambda b: (b, 0, 0, 0, 0))
    st_v = pl.BlockSpec((None, 2, M_HEADS, 1, M_HD), lambda b: (b, 0, 0, 0, 0))
    in_specs = [blk(qc), blk(qc + 1), blk(qc + 2), blk(qc + 3),
                pl.BlockSpec((seq, ngate), lambda b: (row0 + b, 0)),
                pl.BlockSpec((seq // CHUNK, ngate, CHUNK), lambda b: (row0 + b, 0, 0)),
                pl.BlockSpec((1, ngate), lambda b: (0, 0)),
                pl.BlockSpec((ngate, 1), lambda b: (0, 0)),
                pl.BlockSpec((1, width), lambda b: (0, 0))]
    args = [u, u, u, u, gcol, gt, bias.reshape(1, ngate), bias.reshape(ngate, 1), norm_g]
    if not zero_init:
        in_specs += [st_c, st_v, st_v]
        args += list(states)
    in_specs.append(pl.BlockSpec(memory_space=pl.ANY))
    args.append(prev_out)
    return pl.pallas_call(
        functools.partial(_mlstm_kernel, zero_init=zero_init, seq=seq),
        grid=(batch,),
        in_specs=in_specs,
        out_specs=[pl.BlockSpec((seq, width), lambda b: (row0 + b, 0)), st_c, st_v, st_v],
        out_shape=[jax.ShapeDtypeStruct((ntok, width), F32),
                   jax.ShapeDtypeStruct((batch, 2, M_HEADS, M_HD, M_HD), F32),
                   jax.ShapeDtypeStruct((batch, 2, M_HEADS, 1, M_HD), F32),
                   jax.ShapeDtypeStruct((batch, 2, M_HEADS, 1, M_HD), F32)],
        scratch_shapes=[pltpu.VMEM((2, seq, width), F32), pltpu.VMEM((2, M_HEADS, M_HD, M_HD), F32),
                        pltpu.VMEM((2, M_HEADS, 1, M_HD), F32), pltpu.VMEM((2, M_HEADS, 1, M_HD), F32)],
        input_output_aliases={len(args) - 1: 0},
        compiler_params=_cparams(("arbitrary",)),
        name="mlstm_ctx" if zero_init else "mlstm_lat",
    )(*args)


def _softplus(x):
    return jnp.maximum(x, 0.0) + jnp.log1p(jnp.exp(-jnp.abs(x)))


def _rwkv_kernel(*refs, zero_init, seq):
    if zero_init:
        (r_ref, k_ref, v_ref, xl_ref, xg_ref, mur_ref, muk_ref, muv_ref, mul_ref, mug_ref,
         w0_ref, wup_ref, a0_ref, aup_ref, gup_ref, kk_ref, ka_ref, rk_ref, lng_ref, lnb_ref, alias_ref,
         out_ref, so_ref, r_scr, v_scr, kk_scr, g_scr, bonus_scr, lw_scr, kd_scr, b_scr, y_scr, s_scr) = refs
    else:
        (r_ref, k_ref, v_ref, xl_ref, xg_ref, mur_ref, muk_ref, muv_ref, mul_ref, mug_ref,
         w0_ref, wup_ref, a0_ref, aup_ref, gup_ref, kk_ref, ka_ref, rk_ref, lng_ref, lnb_ref, s0_ref, alias_ref,
         out_ref, so_ref, r_scr, v_scr, kk_scr, g_scr, bonus_scr, lw_scr, kd_scr, b_scr, y_scr, s_scr) = refs
    del alias_ref
    nc = seq // CHUNK
    hd = R_HD
    nh = r_ref.shape[1] // hd

    def tshift(x_ref, mu_ref):
        x = x_ref[...]
        row = lax.broadcasted_iota(I32, x.shape, 0)
        prev = jnp.where(row == 0, 0.0, pltpu.roll(x, 1, 0))
        nxt = jnp.where(row == seq - 1, 0.0, pltpu.roll(x, seq - 1, 0))
        return x + mu_ref[...] * (0.5 * (prev + nxt) - x)

    r = tshift(r_ref, mur_ref)
    k = tshift(k_ref, muk_ref)
    v = tshift(v_ref, muv_ref)
    xl = tshift(xl_ref, mul_ref)
    xg = tshift(xg_ref, mug_ref)
    g = _dot(_sigmoid(xg), gup_ref[...])
    kkp = k * kk_ref[...]
    kk = jnp.concatenate(
        [kkp[:, i * hd:(i + 1) * hd]
         * lax.rsqrt(jnp.maximum(jnp.sum(kkp[:, i * hd:(i + 1) * hd] ** 2, axis=-1, keepdims=True), 1e-24))
         for i in range(nh)], axis=1)
    tw = jnp.tanh(xl[:, 0:hd])
    xa = xl[:, hd:2 * hd]
    r_scr[...] = r
    v_scr[...] = v
    kk_scr[...] = kk
    g_scr[...] = g
    rkk = r * k * rk_ref[...]
    bonus_scr[...] = jnp.concatenate(
        [jnp.sum(rkk[:, i * hd:(i + 1) * hd], axis=-1, keepdims=True) * v[:, i * hd:(i + 1) * hd] for i in range(nh)],
        axis=1)
    for d in (0, 1):
        wd = -_softplus(-(w0_ref[d] + _dot(tw, wup_ref[d]))) - 0.5
        ad = _sigmoid(a0_ref[d] + _dot(xa, aup_ref[d]))
        lw_scr[d] = -jnp.exp(wd)
        kd_scr[d] = k * (1.0 + (ad - 1.0) * ka_ref[...])
        b_scr[d] = kk * ad
    if zero_init:
        s_scr[...] = jnp.zeros(s_scr.shape, F32)
    else:
        s_scr[...] = s0_ref[...]

    ti = lax.broadcasted_iota(I32, (CHUNK, CHUNK), 0)
    si = lax.broadcasted_iota(I32, (CHUNK, CHUNK), 1)
    eye = (ti == si).astype(F32)

    def chunk(c, carry):
        chains = []
        rows_d = []
        for d in (0, 1):
            cc = c if d == 0 else nc - 1 - c
            rows = pl.ds(pl.multiple_of(cc * CHUNK, CHUNK), CHUNK)
            rows_d.append(rows)
            incl = (si <= ti) if d == 0 else (si >= ti)
            strict = (si < ti) if d == 0 else (si > ti)
            lw = lw_scr[d, rows, :]
            lw_hi, lw_lo = _split(lw)
            tri = jnp.where(incl, 1.0, 0.0).astype(BF16)
            lc = _dg(tri, lw_hi, _NN) + _dg(tri, lw_lo, _NN)
            l_last = lc[CHUNK - 1:CHUNK, :] if d == 0 else lc[0:1, :]
            e_neg = jnp.exp(-lc)
            e_end = jnp.exp(l_last - lc)
            vc = v_scr[rows, :]
            kdc = kd_scr[d, rows, :]
            bc = b_scr[d, rows, :]
            rt = r_scr[rows, :] * jnp.exp(lc)
            kkt = kk_scr[rows, :] * jnp.exp(lc - lw)
            kh = kdc * e_neg
            bh = bc * e_neg
            kbar = kdc * e_end
            bbar = bc * e_end
            w_end = jnp.exp(l_last)
            for i in range(nh):
                sl = slice(i * hd, (i + 1) * hd)
                chains.append(dict(
                    d=d, i=i, incl=incl, strict=strict,
                    lhs=jnp.concatenate([kkt[:, sl], rt[:, sl]], axis=0),
                    rhs=jnp.concatenate([bh[:, sl], kh[:, sl]], axis=0),
                    end=jnp.concatenate([kbar[:, sl], bbar[:, sl]], axis=0),
                    v=vc[:, sl], w_end=w_end[:, sl], s0=s_scr[d, i]))
        for ch in chains:
            ch['ab'] = _dot1(ch['lhs'], ch['rhs'], _NT)
        for ch in chains:
            ch['proj'] = _dot1(ch['lhs'], ch['s0'], _NT)
        for ch in chains:
            ab = ch['ab']
            ch['a_kb'] = jnp.where(ch['strict'], ab[:CHUNK, :CHUNK], 0.0)
            ch['b_rb'] = jnp.where(ch['incl'], ab[CHUNK:, :CHUNK], 0.0)
            ch['akk_brk'] = jnp.concatenate([jnp.where(ch['strict'], ab[:CHUNK, CHUNK:], 0.0),
                                             jnp.where(ch['incl'], ab[CHUNK:, CHUNK:], 0.0)], axis=0)
        for ch in chains:
            ch['p'] = _dot1(ch['a_kb'], ch['a_kb'])
        for ch in chains:
            ch['abv'] = _dot1(ch['akk_brk'], ch['v'])
        for ch in chains:
            inv = eye - ch['a_kb']
            ch['inv'] = inv + _dot1(inv, ch['p'])
        span = 4
        while span < CHUNK:
            for ch in chains:
                ch['p'] = _dot1(ch['p'], ch['p'])
            for ch in chains:
                ch['inv'] = ch['inv'] + _dot1(ch['inv'], ch['p'])
            span *= 2
        for ch in chains:
            ch['u'] = _dot1(ch['inv'], ch['proj'][:CHUNK] + ch['abv'][:CHUNK])
        for ch in chains:
            ch['y'] = ch['proj'][CHUNK:] + ch['abv'][CHUNK:] - _dot1(ch['b_rb'], ch['u'])
        for ch in chains:
            upd = _dot3(jnp.concatenate([ch['v'], -ch['u']], axis=0), ch['end'], _TN)
            s_scr[ch['d'], ch['i']] = ch['s0'] * ch['w_end'] + upd
        for d in (0, 1):
            y_scr[d, rows_d[d], :] = jnp.concatenate([ch['y'] for ch in chains[nh * d:nh * (d + 1)]], axis=1)
        return carry

    lax.fori_loop(0, nc, chunk, 0)

    y = y_scr[0] + y_scr[1]
    outs = []
    for i in range(nh):
        yh = y[:, i * hd:(i + 1) * hd]
        mean = jnp.mean(yh, axis=-1, keepdims=True)
        var = jnp.mean(jnp.square(yh - mean), axis=-1, keepdims=True)
        outs.append((yh - mean) * lax.rsqrt(var + GN_EPS))
    yn = jnp.concatenate(outs, axis=1)
    out_ref[...] = (yn * lng_ref[...] + lnb_ref[...] + bonus_scr[...]) * g_scr[...]
    so_ref[...] = s_scr[...]


def _rwkv(u, p, state, prev_out, row0, batch, seq, heads_per_step):
    ntok = u.shape[0]
    zero_init = state is None
    nh = heads_per_step
    wd = nh * R_HD
    nsteps = R_HEADS // nh
    width = R_HEADS * R_HD
    rc = COL_R // wd
    sec = width // wd
    lc = (COL_R + 3 * width) // LANES
    ublk = lambda off: pl.BlockSpec((seq, wd), lambda b, h: (row0 + b, rc + off + h))
    ufix = lambda blk: pl.BlockSpec((seq, LANES), lambda b, h: (row0 + b, blk))
    mblk = lambda off: pl.BlockSpec((1, wd), lambda b, h: (0, off + h))
    mfix = lambda blk: pl.BlockSpec((1, LANES), lambda b, h: (0, blk))
    vec = pl.BlockSpec((1, wd), lambda b, h: (0, h))
    in_specs = [ublk(0), ublk(sec), ublk(2 * sec), ufix(lc), ufix(lc + 1),
                mblk(0), mblk(sec), mblk(2 * sec), mfix(lc - COL_R // LANES), mfix(lc - COL_R // LANES + 1),
                pl.BlockSpec((2, 1, wd), lambda b, h: (0, 0, h)),
                pl.BlockSpec((2, R_HD, wd), lambda b, h: (0, 0, h)),
                pl.BlockSpec((2, 1, wd), lambda b, h: (0, 0, h)),
                pl.BlockSpec((2, R_HD, wd), lambda b, h: (0, 0, h)),
                pl.BlockSpec((LANES, wd), lambda b, h: (0, h)),
                vec, vec, vec, vec, vec]
    args = [u, u, u, u, u, p['mu'], p['mu'], p['mu'], p['mu'], p['mu'],
            p['w0'], p['w_up'], p['a0'], p['a_up'], p['g_up'], p['k_k'], p['k_a'], p['r_k'], p['ln_g'], p['ln_b']]
    if not zero_init:
        in_specs.append(pl.BlockSpec((None, 2, nh, R_HD, R_HD), lambda b, h: (b, 0, h, 0, 0)))
        args.append(state)
    in_specs.append(pl.BlockSpec(memory_space=pl.ANY))
    args.append(prev_out)
    big = lambda n: pltpu.VMEM((n, seq, wd), F32)
    return pl.pallas_call(
        functools.partial(_rwkv_kernel, zero_init=zero_init, seq=seq),
        grid=(batch, nsteps),
        in_specs=in_specs,
        out_specs=[pl.BlockSpec((seq, wd), lambda b, h: (row0 + b, h)),
                   pl.BlockSpec((None, 2, nh, R_HD, R_HD), lambda b, h: (b, 0, h, 0, 0))],
        out_shape=[jax.ShapeDtypeStruct((ntok, width), F32),
                   jax.ShapeDtypeStruct((batch, 2, R_HEADS, R_HD, R_HD), F32)],
        scratch_shapes=[pltpu.VMEM((seq, wd), F32)] * 5
                       + [big(2), big(2), big(2), big(2), pltpu.VMEM((2, nh, R_HD, R_HD), F32)],
        input_output_aliases={len(args) - 1: 0},
        compiler_params=_cparams(("arbitrary", "arbitrary")),
        name="rwkv_ctx" if zero_init else "rwkv_lat",
    )(*args)


def _top2_sum(a, b, c, d):
    m1, n1 = jnp.maximum(a, b), jnp.minimum(a, b)
    m2, n2 = jnp.maximum(c, d), jnp.minimum(c, d)
    return jnp.maximum(m1, m2) + jnp.maximum(jnp.minimum(m1, m2), jnp.maximum(n1, n2))


def _first_argmax(vals):
    best = functools.reduce(jnp.maximum, vals)
    idx = jnp.full(best.shape, len(vals) - 1, I32)
    for j in range(len(vals) - 2, -1, -1):
        idx = jnp.where(vals[j] == best, j, idx)
    return best, idx


def _out_kernel(att_ref, m_ref, r_ref, x_ref, w_ref, g1_ref, sh2_ref, sc2_ref, n2_ref, rw_ref, rb_ref,
                x1_ref, h2_ref, route_ref, cnt_ref, cnt_scr):
    @pl.when(pl.program_id(0) == 0)
    def _():
        cnt_scr[...] = jnp.zeros(cnt_scr.shape, F32)

    na = att_ref.shape[1]
    nm = m_ref.shape[1]
    mix = (jnp.dot(att_ref[...].astype(BF16), w_ref[0:na, :], preferred_element_type=F32)
           + jnp.dot(m_ref[...].astype(BF16), w_ref[na:na + nm, :], preferred_element_type=F32)
           + jnp.dot(r_ref[...].astype(BF16), w_ref[na + nm:, :], preferred_element_type=F32))
    x1 = x_ref[...] + g1_ref[...] * mix
    x1_ref[...] = x1
    h2 = _rms(x1, n2_ref[...]) * (1.0 + sc2_ref[...]) + sh2_ref[...]
    h2_ref[...] = h2
    logits = _dot1(rw_ref[...], h2, _NT)
    s = _sigmoid(logits)
    ssel = s + rb_ref[...]
    srow = [s[e:e + 1, :] for e in range(N_EXPERTS)]
    brow = [ssel[e:e + 1, :] for e in range(N_EXPERTS)]
    gscore = [_top2_sum(*brow[EXPERTS_PER_GROUP * g:EXPERTS_PER_GROUP * (g + 1)]) for g in range(N_EXPERT_GROUPS)]
    _, gidx = _first_argmax(gscore)
    pick = lambda rows, j: functools.reduce(
        lambda acc, g: jnp.where(gidx == g, rows[EXPERTS_PER_GROUP * g + j], acc),
        range(N_EXPERT_GROUPS - 2, -1, -1), rows[EXPERTS_PER_GROUP * (N_EXPERT_GROUPS - 1) + j])
    ing = [pick(brow, j) for j in range(EXPERTS_PER_GROUP)]
    sin_ = [pick(srow, j) for j in range(EXPERTS_PER_GROUP)]
    _, l1 = _first_argmax(ing)
    _, l2 = _first_argmax([jnp.where(l1 == j, -jnp.inf, ing[j]) for j in range(EXPERTS_PER_GROUP)])
    sel = lambda l: functools.reduce(lambda acc, j: jnp.where(l == j, sin_[j], acc),
                                     range(EXPERTS_PER_GROUP - 2, -1, -1), sin_[EXPERTS_PER_GROUP - 1])
    w1, w2 = sel(l1), sel(l2)
    tot = w1 + w2
    e1 = gidx * EXPERTS_PER_GROUP + l1
    e2 = gidx * EXPERTS_PER_GROUP + l2
    tm = e1.shape[1]
    eid = lax.broadcasted_iota(I32, (N_EXPERTS, tm), 0)
    oh1 = eid == e1
    oh2 = eid == e2
    picked = jnp.where(jnp.logical_or(oh1, oh2), 1.0, 0.0)
    earlier = jnp.where(lax.broadcasted_iota(I32, (tm, tm), 0) < lax.broadcasted_iota(I32, (tm, tm), 1), 1.0, 0.0)
    rank = cnt_scr[:, 0:1] + jnp.dot(picked.astype(BF16), earlier.astype(BF16), preferred_element_type=F32)
    pos1 = jnp.sum(jnp.where(oh1, rank, 0.0), axis=0, keepdims=True)
    pos2 = jnp.sum(jnp.where(oh2, rank, 0.0), axis=0, keepdims=True)
    cnt = cnt_scr[...] + jnp.sum(picked, axis=1, keepdims=True)
    cnt_scr[...] = cnt
    cnt_ref[...] = cnt
    zero = jnp.zeros_like(w1)
    route_ref[...] = jnp.concatenate([e1.astype(F32), e2.astype(F32), w1 / tot, w2 / tot, pos1, pos2, zero, zero],
                                     axis=0)


def _out_proj(att, m_out, r_out, x, w_out, mod3, n2g, rw_t, rb, t_ctx, s_lat):
    ntok, d = x.shape
    tm = 256
    row = lambda i: _mod_row(i * tm, t_ctx, s_lat)
    modblk = lambda j: pl.BlockSpec((None, 1, d), lambda i: (row(i), 0, j))
    return pl.pallas_call(
        _out_kernel,
        grid=(ntok // tm,),
        in_specs=[pl.BlockSpec((tm, att.shape[1]), lambda i: (i, 0)),
                  pl.BlockSpec((tm, m_out.shape[1]), lambda i: (i, 0)),
                  pl.BlockSpec((tm, r_out.shape[1]), lambda i: (i, 0)),
                  pl.BlockSpec((tm, d), lambda i: (i, 0)),
                  pl.BlockSpec(w_out.shape, lambda i: (0, 0)),
                  modblk(2), modblk(3), modblk(4),
                  pl.BlockSpec((1, d), lambda i: (0, 0)),
                  pl.BlockSpec(rw_t.shape, lambda i: (0, 0)),
                  pl.BlockSpec(rb.shape, lambda i: (0, 0))],
        out_specs=[pl.BlockSpec((tm, d), lambda i: (i, 0)),
                   pl.BlockSpec((tm, d), lambda i: (i, 0)),
                   pl.BlockSpec((8, tm), lambda i: (0, i)),
                   pl.BlockSpec((N_EXPERTS, LANES), lambda i: (0, 0))],
        out_shape=[jax.ShapeDtypeStruct((ntok, d), F32), jax.ShapeDtypeStruct((ntok, d), F32),
                   jax.ShapeDtypeStruct((8, ntok), F32), jax.ShapeDtypeStruct((N_EXPERTS, LANES), F32)],
        scratch_shapes=[pltpu.VMEM((N_EXPERTS, LANES), F32)],
        compiler_params=_cparams(("arbitrary",)),
        name="out_proj",
    )(att, m_out, r_out, x, w_out, mod3, mod3, mod3, n2g, rw_t, rb)


def _row_gather(src_hbm, idx_ref, base, dst, sem, n, unrolled):
    def start(j):
        pltpu.make_async_copy(src_hbm.at[pl.ds(idx_ref[base + j], 1), :], dst.at[pl.ds(j, 1), :], sem).start()

    if unrolled:
        for j in range(n):
            start(j)
    else:
        def body(j, c):
            start(j)
            return c
        lax.fori_loop(0, n, body, 0)


def _expert_kernel(be_ref, tok_ref, nused_ref, h_hbm, w1_ref, w3_ref, w2_ref, y_ref, xbuf, w1b, w3b, w2b, sem):
    i = pl.program_id(0)
    n_used = nused_ref[0]
    wait = lambda s: pltpu.make_async_copy(h_hbm.at[pl.ds(0, EXPERT_ROWS), :], xbuf.at[s], sem.at[s]).wait()

    @pl.when(i < n_used)
    def _():
        slot = i % 2

        @pl.when(i == 0)
        def _():
            _row_gather(h_hbm, tok_ref, 0, xbuf.at[0], sem.at[0], EXPERT_ROWS, unrolled=False)

        @pl.when(jnp.logical_or(i == 0, be_ref[i] != be_ref[jnp.maximum(i - 1, 0)]))
        def _():
            w1b[...] = w1_ref[...].astype(BF16)
            w3b[...] = w3_ref[...].astype(BF16)
            w2b[...] = w2_ref[...].astype(BF16)

        wait(slot)
        _row_gather(h_hbm, tok_ref, (i + 1) * EXPERT_ROWS, xbuf.at[1 - slot], sem.at[1 - slot], EXPERT_ROWS,
                    unrolled=True)
        xb = xbuf[slot].astype(BF16)
        a = jnp.dot(xb, w1b[...], preferred_element_type=F32)
        b = jnp.dot(xb, w3b[...], preferred_element_type=F32)
        hmid = (a * _sigmoid(a)) * b
        y_ref[...] = jnp.dot(hmid.astype(BF16), w2b[...], preferred_element_type=F32)

        @pl.when(i == n_used - 1)
        def _():
            wait(1 - slot)

    @pl.when(i >= n_used)
    def _():
        y_ref[...] = jnp.zeros(y_ref.shape, F32)


def _experts(h2, block_e, row_tok, n_used, w1, w3, w2, layer):
    ntok, d = h2.shape
    de = w1.shape[3]
    n_rows = row_tok.shape[0] - EXPERT_ROWS
    nb = n_rows // EXPERT_ROWS
    return pl.pallas_call(
        _expert_kernel,
        grid_spec=pltpu.PrefetchScalarGridSpec(
            num_scalar_prefetch=3,
            grid=(nb,),
            in_specs=[pl.BlockSpec(memory_space=pl.ANY),
                      pl.BlockSpec((None, None, d, de), lambda i, be, tok, nu: (layer, be[i], 0, 0)),
                      pl.BlockSpec((None, None, d, de), lambda i, be, tok, nu: (layer, be[i], 0, 0)),
                      pl.BlockSpec((None, None, de, d), lambda i, be, tok, nu: (layer, be[i], 0, 0))],
            out_specs=pl.BlockSpec((EXPERT_ROWS, d), lambda i, be, tok, nu: (i, 0)),
            scratch_shapes=[pltpu.VMEM((2, EXPERT_ROWS, d), F32), pltpu.VMEM((d, de), BF16),
                            pltpu.VMEM((d, de), BF16), pltpu.VMEM((de, d), BF16), pltpu.SemaphoreType.DMA((2,))]),
        out_shape=jax.ShapeDtypeStruct((n_rows, d), F32),
        compiler_params=_cparams(("arbitrary",)),
        name="experts",
    )(block_e, row_tok, n_used, h2, w1, w3, w2)


def _combine_kernel(d1_ref, d2_ref, y_hbm, x1_ref, g2_ref, gate_ref, o_ref, ybuf, sem):
    tm = x1_ref.shape[0]
    i = pl.program_id(0)
    slot = i % 2

    def gather(tile, s, unrolled):
        _row_gather(y_hbm, d1_ref, tile * tm, ybuf.at[s, 0], sem.at[s], tm, unrolled)
        _row_gather(y_hbm, d2_ref, tile * tm, ybuf.at[s, 1], sem.at[s], tm, unrolled)

    def wait(s):
        pltpu.make_async_copy(y_hbm.at[pl.ds(0, tm), :], ybuf.at[s, 0], sem.at[s]).wait()
        pltpu.make_async_copy(y_hbm.at[pl.ds(0, tm), :], ybuf.at[s, 1], sem.at[s]).wait()

    @pl.when(i == 0)
    def _():
        gather(0, 0, False)

    wait(slot)
    gather(i + 1, 1 - slot, True)
    gate = gate_ref[...]
    moe = ybuf[slot, 0] * gate[:, 0:1] + ybuf[slot, 1] * gate[:, 1:2]
    o_ref[...] = x1_ref[...] + g2_ref[...] * moe

    @pl.when(i == pl.num_programs(0) - 1)
    def _():
        wait(1 - slot)


def _combine(y, x1, mod3, gate, dest1, dest2, t_ctx, s_lat):
    ntok, d = x1.shape
    tm = 256
    row = lambda i: _mod_row(i * tm, t_ctx, s_lat)
    return pl.pallas_call(
        _combine_kernel,
        grid_spec=pltpu.PrefetchScalarGridSpec(
            num_scalar_prefetch=2,
            grid=(ntok // tm,),
            in_specs=[pl.BlockSpec(memory_space=pl.ANY),
                      pl.BlockSpec((tm, d), lambda i, a, b: (i, 0)),
                      pl.BlockSpec((None, 1, d), lambda i, a, b: (row(i), 0, 5)),
                      pl.BlockSpec((tm, 2), lambda i, a, b: (i, 0))],
            out_specs=pl.BlockSpec((tm, d), lambda i, a, b: (i, 0)),
            scratch_shapes=[pltpu.VMEM((2, 2, tm, d), F32), pltpu.SemaphoreType.DMA((2,))]),
        out_shape=jax.ShapeDtypeStruct((ntok, d), F32),
        compiler_params=_cparams(("arbitrary",)),
        name="combine",
    )(dest1, dest2, y, x1, mod3, gate)


def _dispatch(route, counts, tile):
    ntok = route.shape[1]
    e = route[0:2].astype(I32)
    pos = route[4:6].astype(I32)
    gate = route[2:4].T
    counts = counts[:, 0].astype(I32)
    padded = (counts + EXPERT_ROWS - 1) // EXPERT_ROWS * EXPERT_ROWS
    pad_end = jnp.cumsum(padded)
    pad_start = pad_end - padded
    onehot = (e[:, :, None] == jnp.arange(N_EXPERTS, dtype=I32)).astype(I32)
    dest = jnp.sum(onehot * pad_start, axis=-1) + pos
    n_rows = -(-(2 * ntok) // EXPERT_ROWS) * EXPERT_ROWS + N_EXPERTS * EXPERT_ROWS
    nb = n_rows // EXPERT_ROWS
    tok = jnp.broadcast_to(jnp.arange(ntok, dtype=I32)[None, :], (2, ntok))
    row_tok = jnp.zeros((n_rows + EXPERT_ROWS,), I32).at[dest.reshape(-1)].set(tok.reshape(-1))
    blk_start = jnp.arange(nb, dtype=I32) * EXPERT_ROWS
    block_e = jnp.minimum(jnp.sum((pad_end[None, :] <= blk_start[:, None]).astype(I32), axis=1), N_EXPERTS - 1)
    n_used = pad_end[-1:] // EXPERT_ROWS
    dest = jnp.pad(dest, ((0, 0), (0, tile)))
    return row_tok, block_e, n_used, gate, dest[0], dest[1]


def kernel(x_prompt, x_sample, c, cache_attn_k, cache_attn_v, state_mlstm_C, state_mlstm_n, state_mlstm_m, state_rwkv, c_ctx, norm1_g, norm2_g, w_mod, b_mod, w_in, w_out, attn_q_norm, attn_k_norm, mlstm_i_bias, mlstm_f_bias, mlstm_norm_g, rwkv_mu, rwkv_w0, rwkv_w_up, rwkv_a0, rwkv_a_up, rwkv_g_up, rwkv_k_k, rwkv_k_a, rwkv_r_k, rwkv_ln_g, rwkv_ln_b, router_w, router_b, exp_w1, exp_w3, exp_w2):
    b_ctx, s_ctx, d = x_prompt.shape
    b_lat, s_lat, _ = x_sample.shape
    depth = w_in.shape[0]
    t_ctx = b_ctx * s_ctx
    ntok = t_ctx + b_lat * s_lat
    assert b_lat + 1 <= 8 and s_lat % 512 == 0 and t_ctx % 512 == 0 and t_ctx % s_lat == 0
    past = cache_attn_k.shape[2]

    x = jnp.concatenate([x_prompt.reshape(t_ctx, d), x_sample.reshape(b_lat * s_lat, d)], axis=0)
    c_all = jnp.zeros((8, d), F32).at[0].set(c_ctx).at[1:1 + b_lat].set(c)
    mod = _modulation(c_all, w_mod, b_mod)
    cos, sin = _rope_tables(s_lat)
    rw_t = router_w.T
    rb = router_b.reshape(N_EXPERTS, 1)
    m_width = M_HEADS * M_HD
    r_width = R_HEADS * R_HD
    n_in = w_in.shape[2]
    gate_lo = COL_M + 4 * m_width

    ks, vs, cs, ns, ms, rs = [], [], [], [], [], []
    for l in range(depth):
        w = w_in[l]
        w_p = jnp.concatenate([w[:, :gate_lo], w[:, gate_lo + 4 * M_HEADS:], w[:, gate_lo:gate_lo + 4 * M_HEADS],
                               jnp.zeros((d, N_IN_PAD - n_in), F32)], axis=1).astype(BF16)
        mod3 = mod[l].reshape(8, 1, 6 * d)
        u = _in_proj(x, norm1_g[l][None], mod3, w_p, t_ctx, s_lat)

        ck = cache_attn_k[:, l].reshape(b_lat, past, ATT_KV_HEADS * ATT_HD)
        cv = cache_attn_v[:, l].reshape(b_lat, past, ATT_KV_HEADS * ATT_HD)
        att, k_ctx = _attention(u, attn_q_norm[l][None], attn_k_norm[l][None], ck, cv, cos, sin,
                                b_ctx, s_ctx, b_lat, s_lat)
        ks.append(k_ctx.reshape(b_ctx, s_ctx, ATT_KV_HEADS, ATT_HD))
        vcol = ATT_GROUPS * ATT_KV_HEADS * ATT_HD + ATT_KV_HEADS * ATT_HD
        vs.append(u[:t_ctx, vcol:vcol + ATT_KV_HEADS * ATT_HD].reshape(b_ctx, s_ctx, ATT_KV_HEADS, ATT_HD))

        gcol = u[:, COL_GATE:COL_GATE + 4 * M_HEADS]
        gt = gcol.reshape(ntok // CHUNK, CHUNK, 4 * M_HEADS).transpose(0, 2, 1)
        bias = jnp.stack([mlstm_i_bias[l], mlstm_f_bias[l]], axis=1)
        ng = mlstm_norm_g[l][None]
        m_out = jnp.zeros((ntok, m_width), F32)
        m_out, c_c, n_c, m_c = _mlstm(u, gcol, gt, bias, ng, None, m_out, 0, b_ctx, s_ctx)
        lat_states = (state_mlstm_C[:, l],
                      state_mlstm_n[:, l].reshape(b_lat, 2, M_HEADS, 1, M_HD),
                      jnp.broadcast_to(state_mlstm_m[:, l][..., None, None], (b_lat, 2, M_HEADS, 1, M_HD)))
        m_out, _, _, _ = _mlstm(u, gcol, gt, bias, ng, lat_states, m_out, t_ctx // s_lat, b_lat, s_lat)
        cs.append(c_c)
        ns.append(n_c.reshape(b_ctx, 2, M_HEADS, M_HD))
        ms.append(m_c[:, :, :, 0, 0])

        rp = dict(mu=rwkv_mu[l][None], w0=rwkv_w0[l].reshape(2, 1, r_width), w_up=rwkv_w_up[l],
                  a0=rwkv_a0[l].reshape(2, 1, r_width), a_up=rwkv_a_up[l], g_up=rwkv_g_up[l],
                  k_k=rwkv_k_k[l][None], k_a=rwkv_k_a[l][None], r_k=rwkv_r_k[l].reshape(1, r_width),
                  ln_g=rwkv_ln_g[l][None], ln_b=rwkv_ln_b[l][None])
        r_out = jnp.zeros((ntok, r_width), F32)
        r_out, r_c = _rwkv(u, rp, None, r_out, 0, b_ctx, s_ctx, RWKV_HEADS_PER_STEP_CTX)
        r_out, _ = _rwkv(u, rp, state_rwkv[:, l], r_out, t_ctx // s_lat, b_lat, s_lat, RWKV_HEADS_PER_STEP_LAT)
        rs.append(r_c)

        x1, h2, route, counts = _out_proj(att, m_out, r_out, x, w_out[l].astype(BF16), mod3, norm2_g[l][None],
                                          rw_t, rb, t_ctx, s_lat)
        row_tok, block_e, n_used, gate, dest1, dest2 = _dispatch(route, counts, 256)
        y = _experts(h2, block_e, row_tok, n_used, exp_w1, exp_w3, exp_w2, l)
        x = _combine(y, x1, mod3, gate, dest1, dest2, t_ctx, s_lat)

    y_prompt = x[:t_ctx].reshape(b_ctx, s_ctx, d)
    y_sample = x[t_ctx:].reshape(b_lat, s_lat, d)
    return (y_prompt, y_sample, jnp.stack(ks, axis=1), jnp.stack(vs, axis=1), jnp.stack(cs, axis=1),
            jnp.stack(ns, axis=1), jnp.stack(ms, axis=1), jnp.stack(rs, axis=1))
```

```python
import functools

import jax
import jax.numpy as jnp
from jax import lax
from jax.experimental import pallas as pl
from jax.experimental.pallas import tpu as pltpu

F32 = jnp.float32
BF16 = jnp.bfloat16
I32 = jnp.int32

NORM_EPS = 1e-6
GN_EPS = 64e-5
M_INIT = -1e30
GRID_W = 64
ROPE_THETA = 10000.0
ATT_HD = 128
ATT_GROUPS = 4
ATT_KV_HEADS = 2
M_HD = 128
M_HEADS = 4
R_HD = 64
R_HEADS = 8
N_EXPERTS = 16
N_EXPERT_GROUPS = 4
EXPERTS_PER_GROUP = 4
CHUNK = 64
LANES = 128
EXPERT_ROWS = 256
VMEM_LIMIT = 56 * 1024 * 1024
RWKV_HEADS_PER_STEP_CTX = 8
RWKV_HEADS_PER_STEP_LAT = 8

COL_ATT = 0
COL_M = 1536
COL_R = 3584
COL_GATE = 5376
N_IN_PAD = 5632


def _cparams(sem):
    return pltpu.CompilerParams(dimension_semantics=sem, vmem_limit_bytes=VMEM_LIMIT)


def _dot(a, b):
    return jnp.dot(a.astype(BF16), b.astype(BF16), preferred_element_type=F32)


def _dg(a, b, dims):
    return lax.dot_general(a, b, (dims, ((), ())), preferred_element_type=F32)


_NN = ((1,), (0,))
_NT = ((1,), (1,))
_TN = ((0,), (0,))


def _split(a):
    hi = a.astype(BF16)
    lo = (a - hi.astype(F32)).astype(BF16)
    return hi, lo


def _dot3(a, b, dims=_NN):
    ah, al = _split(a)
    bh, bl = _split(b)
    return _dg(ah, bh, dims) + (_dg(ah, bl, dims) + _dg(al, bh, dims))


def _dot1(a, b, dims=_NN):
    return _dg(a.astype(BF16), b.astype(BF16), dims)


def _rms(x, g):
    return x * lax.rsqrt(jnp.mean(x * x, axis=-1, keepdims=True) + NORM_EPS) * g


def _sigmoid(x):
    return 1.0 / (1.0 + jnp.exp(-x))


def _mod_kernel(c_ref, w_ref, b_ref, o_ref):
    c = c_ref[...]
    o_ref[...] = _dot(c * _sigmoid(c), w_ref[...]) + b_ref[...]


def _modulation(c_all, w_mod, b_mod):
    depth, d, n = w_mod.shape
    tn = 1024
    return pl.pallas_call(
        _mod_kernel,
        grid=(depth, n // tn),
        in_specs=[pl.BlockSpec((8, d), lambda l, j: (0, 0)),
                  pl.BlockSpec((None, d, tn), lambda l, j: (l, 0, j)),
                  pl.BlockSpec((None, 1, tn), lambda l, j: (l, 0, j))],
        out_specs=pl.BlockSpec((None, 8, tn), lambda l, j: (l, 0, j)),
        out_shape=jax.ShapeDtypeStruct((depth, 8, n), F32),
        compiler_params=_cparams(("arbitrary", "arbitrary")),
        name="modulation",
    )(c_all, w_mod, b_mod.reshape(depth, 1, n))


def _mod_row(tok0, t_ctx, s_lat):
    return jnp.where(tok0 < t_ctx, 0, 1 + (tok0 - t_ctx) // s_lat)


def _in_kernel(x_ref, g_ref, sh_ref, sc_ref, w_ref, o_ref, h_ref):
    @pl.when(pl.program_id(1) == 0)
    def _():
        slab = 256
        for r0 in range(0, x_ref.shape[0], slab):
            h = _rms(x_ref[r0:r0 + slab, :], g_ref[...]) * (1.0 + sc_ref[...]) + sh_ref[...]
            h_ref[r0:r0 + slab, :] = h.astype(BF16)

    o_ref[...] = jnp.dot(h_ref[...], w_ref[...], preferred_element_type=F32)


def _in_proj(x, g1, mod3, w_p, t_ctx, s_lat):
    ntok, d = x.shape
    n = w_p.shape[1]
    tm, tn = 1024, 512
    row = lambda i: _mod_row(i * tm, t_ctx, s_lat)
    return pl.pallas_call(
        _in_kernel,
        grid=(ntok // tm, n // tn),
        in_specs=[pl.BlockSpec((tm, d), lambda i, j: (i, 0)),
                  pl.BlockSpec((1, d), lambda i, j: (0, 0)),
                  pl.BlockSpec((None, 1, d), lambda i, j: (row(i), 0, 0)),
                  pl.BlockSpec((None, 1, d), lambda i, j: (row(i), 0, 1)),
                  pl.BlockSpec((d, tn), lambda i, j: (0, j))],
        out_specs=pl.BlockSpec((tm, tn), lambda i, j: (i, j)),
        out_shape=jax.ShapeDtypeStruct((ntok, n), F32),
        scratch_shapes=[pltpu.VMEM((tm, d), BF16)],
        compiler_params=_cparams(("arbitrary", "arbitrary")),
        name="in_proj",
    )(x, g1, mod3, mod3, w_p)


def _softmax_av(q, kb, vb):
    s = _dg(q.astype(BF16), kb, _NT) * (ATT_HD ** -0.5)
    p = jnp.exp(s - jnp.max(s, axis=-1, keepdims=True))
    l = jnp.sum(p, axis=-1, keepdims=True)
    return jnp.dot(p.astype(BF16), vb, preferred_element_type=F32) / l


def _att_ctx_kernel(q_ref, k_ref, v_ref, qn_ref, kn_ref, alias_ref, o_ref, ko_ref):
    del alias_ref
    k = _rms(k_ref[...], kn_ref[...])
    ko_ref[...] = k
    kb = k.astype(BF16)
    vb = v_ref[...].astype(BF16)
    for g in range(ATT_GROUPS):
        q = _rms(q_ref[:, g * ATT_HD:(g + 1) * ATT_HD], qn_ref[...])
        o_ref[:, g * ATT_HD:(g + 1) * ATT_HD] = _softmax_av(q, kb, vb)


def _rope(x, cos, sin):
    lane = lax.broadcasted_iota(I32, x.shape, 1)
    first = (lane % (ATT_HD // 2)) < (ATT_HD // 4)
    partner = jnp.where(first, pltpu.roll(x, ATT_HD - ATT_HD // 4, 1), pltpu.roll(x, ATT_HD // 4, 1))
    return x * cos + partner * sin


def _att_lat_kernel(q_ref, k_ref, v_ref, ck_ref, cv_ref, qn_ref, kn_ref, cosq_ref, sinq_ref, cosk_ref, sink_ref,
                    alias_ref, o_ref, kb_ref, vb_ref, *, past):
    del alias_ref

    @pl.when(pl.program_id(2) == 0)
    def _():
        k = _rope(_rms(k_ref[...], kn_ref[...]), cosk_ref[...], sink_ref[...])
        kb_ref[0:past, :] = ck_ref[...].astype(BF16)
        kb_ref[past:, :] = k.astype(BF16)
        vb_ref[0:past, :] = cv_ref[...].astype(BF16)
        vb_ref[past:, :] = v_ref[...].astype(BF16)

    kb = kb_ref[...]
    vb = vb_ref[...]
    for g in range(ATT_GROUPS):
        q = _rope(_rms(q_ref[:, g * ATT_HD:(g + 1) * ATT_HD], qn_ref[...]), cosq_ref[...], sinq_ref[...])
        o_ref[:, g * ATT_HD:(g + 1) * ATT_HD] = _softmax_av(q, kb, vb)


def _attention(u, qn, kn, cache_k, cache_v, cos, sin, b_ctx, s_ctx, b_lat, s_lat):
    ntok = u.shape[0]
    t_ctx = b_ctx * s_ctx
    gw = ATT_GROUPS * ATT_HD
    kcol = (ATT_KV_HEADS * gw) // ATT_HD
    vcol = kcol + ATT_KV_HEADS
    att, k_ctx = pl.pallas_call(
        _att_ctx_kernel,
        grid=(b_ctx, ATT_KV_HEADS),
        in_specs=[pl.BlockSpec((s_ctx, gw), lambda b, h: (b, h)),
                  pl.BlockSpec((s_ctx, ATT_HD), lambda b, h: (b, kcol + h)),
                  pl.BlockSpec((s_ctx, ATT_HD), lambda b, h: (b, vcol + h)),
                  pl.BlockSpec((1, ATT_HD), lambda b, h: (0, 0)),
                  pl.BlockSpec((1, ATT_HD), lambda b, h: (0, 0)),
                  pl.BlockSpec(memory_space=pl.ANY)],
        out_specs=[pl.BlockSpec((s_ctx, gw), lambda b, h: (b, h)),
                   pl.BlockSpec((s_ctx, ATT_HD), lambda b, h: (b, h))],
        out_shape=[jax.ShapeDtypeStruct((ntok, ATT_KV_HEADS * gw), F32),
                   jax.ShapeDtypeStruct((t_ctx, ATT_KV_HEADS * ATT_HD), F32)],
        input_output_aliases={5: 0},
        compiler_params=_cparams(("arbitrary", "arbitrary")),
        name="att_ctx",
    )(u, u, u, qn, kn, jnp.zeros((ntok, ATT_KV_HEADS * gw), F32))

    tq = 256
    nqb = s_lat // tq
    past = cache_k.shape[1]
    qrow0 = t_ctx // tq
    krow0 = t_ctx // s_lat
    att = pl.pallas_call(
        functools.partial(_att_lat_kernel, past=past),
        grid=(b_lat, ATT_KV_HEADS, nqb),
        in_specs=[pl.BlockSpec((tq, gw), lambda b, h, i: (qrow0 + b * nqb + i, h)),
                  pl.BlockSpec((s_lat, ATT_HD), lambda b, h, i: (krow0 + b, kcol + h)),
                  pl.BlockSpec((s_lat, ATT_HD), lambda b, h, i: (krow0 + b, vcol + h)),
                  pl.BlockSpec((None, past, ATT_HD), lambda b, h, i: (b, 0, h)),
                  pl.BlockSpec((None, past, ATT_HD), lambda b, h, i: (b, 0, h)),
                  pl.BlockSpec((1, ATT_HD), lambda b, h, i: (0, 0)),
                  pl.BlockSpec((1, ATT_HD), lambda b, h, i: (0, 0)),
                  pl.BlockSpec((tq, ATT_HD), lambda b, h, i: (i, 0)),
                  pl.BlockSpec((tq, ATT_HD), lambda b, h, i: (i, 0)),
                  pl.BlockSpec((s_lat, ATT_HD), lambda b, h, i: (0, 0)),
                  pl.BlockSpec((s_lat, ATT_HD), lambda b, h, i: (0, 0)),
                  pl.BlockSpec(memory_space=pl.ANY)],
        out_specs=pl.BlockSpec((tq, gw), lambda b, h, i: (qrow0 + b * nqb + i, h)),
        out_shape=jax.ShapeDtypeStruct(att.shape, F32),
        scratch_shapes=[pltpu.VMEM((past + s_lat, ATT_HD), BF16), pltpu.VMEM((past + s_lat, ATT_HD), BF16)],
        input_output_aliases={11: 0},
        compiler_params=_cparams(("arbitrary", "arbitrary", "arbitrary")),
        name="att_lat",
    )(u, u, u, cache_k, cache_v, qn, kn, cos, sin, cos, sin, att)
    return att, k_ctx


def _rope_tables(n_tokens):
    pos = jnp.arange(n_tokens)
    row = (pos // GRID_W).astype(F32)
    col = (pos % GRID_W).astype(F32)
    n_freq = ATT_HD // 4
    inv_freq = ROPE_THETA ** (-jnp.arange(n_freq, dtype=F32) / n_freq)
    ang_r = row[:, None] * inv_freq[None, :]
    ang_c = col[:, None] * inv_freq[None, :]
    cos = jnp.concatenate([jnp.cos(ang_r), jnp.cos(ang_r), jnp.cos(ang_c), jnp.cos(ang_c)], axis=-1)
    sin = jnp.concatenate([-jnp.sin(ang_r), jnp.sin(ang_r), -jnp.sin(ang_c), jnp.sin(ang_c)], axis=-1)
    return cos, sin


def _log_sigmoid(x):
    return jnp.minimum(x, 0.0) - jnp.log1p(jnp.exp(-jnp.abs(x)))


def _mlstm_kernel(*refs, zero_init, seq):
    if zero_init:
        (q_ref, k_ref, v_ref, o_ref, gc_ref, gt_ref, br_ref, bc_ref, ng_ref, alias_ref,
         out_ref, co_ref, mo_ref, h_scr, c_scr, m_scr) = refs
    else:
        (q_ref, k_ref, v_ref, o_ref, gc_ref, gt_ref, br_ref, bc_ref, ng_ref, c0_ref, m0_ref, alias_ref,
         out_ref, co_ref, mo_ref, h_scr, c_scr, m_scr) = refs
    del alias_ref
    nc = seq // CHUNK
    if zero_init:
        c_scr[...] = jnp.zeros(c_scr.shape, F32)
        m_scr[...] = jnp.full(m_scr.shape, M_INIT, F32)
    else:
        c_scr[...] = c0_ref[...]
        m_scr[...] = m0_ref[...]

    ti = lax.broadcasted_iota(I32, (CHUNK, CHUNK), 0)
    si = lax.broadcasted_iota(I32, (CHUNK, CHUNK), 1)
    ones_col = jnp.where(lax.broadcasted_iota(I32, (CHUNK, M_HD), 1) == 0, 1.0, 0.0).astype(BF16)
    eye_d = jnp.where(lax.broadcasted_iota(I32, (M_HD, M_HD), 0) == lax.broadcasted_iota(I32, (M_HD, M_HD), 1),
                      1.0, 0.0).astype(BF16)

    def cumsum3(tri, x, tri_left):
        hi = x.astype(BF16)
        r1 = x - hi.astype(F32)
        mid = r1.astype(BF16)
        lo = (r1 - mid.astype(F32)).astype(BF16)
        if tri_left:
            return _dg(tri, hi, _NN) + (_dg(tri, mid, _NN) + _dg(tri, lo, _NN))
        return _dg(hi, tri, _NN) + (_dg(mid, tri, _NN) + _dg(lo, tri, _NN))
    gate_i = lambda d, h: 2 * M_HEADS * d + h
    gate_f = lambda d, h: 2 * M_HEADS * d + M_HEADS + h

    def chunk(c, carry):
        ch = []
        for d in (0, 1):
            cc = c if d == 0 else nc - 1 - c
            rows = pl.ds(pl.multiple_of(cc * CHUNK, CHUNK), CHUNK)
            causal = (si <= ti) if d == 0 else (si >= ti)
            tri = jnp.where(causal, 1.0, 0.0).astype(BF16)
            tri_t = jnp.where((ti <= si) if d == 0 else (ti >= si), 1.0, 0.0).astype(BF16)
            q_all = q_ref[rows, :] * (M_HD ** -0.5)
            k_all = k_ref[rows, :]
            v_all = v_ref[rows, :]
            gcol = gc_ref[rows, :] + br_ref[...]
            grow = gt_ref[cc] + bc_ref[...]
            bcum_c_all = cumsum3(tri, _log_sigmoid(gcol), True)
            bcum_r_all = cumsum3(tri_t, _log_sigmoid(grow), False)
            for h in range(M_HEADS):
                sl = slice(h * M_HD, (h + 1) * M_HD)
                q, k, v = q_all[:, sl], k_all[:, sl], v_all[:, sl]
                ig_c = gcol[:, gate_i(d, h):gate_i(d, h) + 1]
                ig_r = grow[gate_i(d, h):gate_i(d, h) + 1, :]
                bcum_c = bcum_c_all[:, gate_f(d, h):gate_f(d, h) + 1]
                bcum_r = bcum_r_all[gate_f(d, h):gate_f(d, h) + 1, :]
                m_st = m_scr[d, h][:, 0:1]
                dmat = jnp.where(causal, bcum_c - bcum_r + ig_r, -jnp.inf)
                inter = bcum_c + m_st
                m_t = jnp.maximum(inter, jnp.max(dmat, axis=1, keepdims=True))
                b_last = bcum_c[CHUNK - 1:CHUNK, :] if d == 0 else bcum_c[0:1, :]
                g_c = b_last - bcum_c + ig_c
                m_new = jnp.maximum(b_last + m_st, jnp.max(g_c, axis=0, keepdims=True))
                ch.append(dict(d=d, h=h, rows=rows, sl=sl, qb=q.astype(BF16), kb=k.astype(BF16),
                               vx=jnp.concatenate([v.astype(BF16), ones_col], axis=1),
                               c_st=c_scr[d, h], m_t=m_t, m_new=m_new,
                               w_intra=jnp.exp(dmat - m_t), w_inter=jnp.exp(inter - m_t),
                               decay=jnp.exp(b_last + m_st - m_new), kw=(k * jnp.exp(g_c - m_new)).astype(BF16)))
        for x in ch:
            x['s_qk'] = _dg(x['qb'], x['kb'], _NT) * x['w_intra']
        for x in ch:
            x['qc'] = _dg(x['qb'], x['c_st'].astype(BF16), _NN)
        for x in ch:
            x['kw_t'] = _dg(eye_d, x['kw'], _NT).astype(BF16)
        for x in ch:
            x['sv'] = _dg(x['s_qk'].astype(BF16), x['vx'], _NN)
        for x in ch:
            x['upd'] = _dg(x['kw_t'], x['vx'], _NN)
        for x in ch:
            d, h = x['d'], x['h']
            tot = x['w_inter'] * x['qc'] + x['sv']
            den = tot[:, M_HD:M_HD + 1]
            h_scr[d, x['rows'], x['sl']] = tot[:, :M_HD] / jnp.maximum(jnp.abs(den), jnp.exp(-x['m_t']))
            c_scr[d, h] = x['decay'] * x['c_st'] + x['upd']
            m_scr[d, h] = jnp.broadcast_to(x['m_new'], (1, M_HD))
        return carry

    lax.fori_loop(0, nc, chunk, 0)
    hsum = h_scr[0] + h_scr[1]
    hn = jnp.concatenate(
        [hsum[:, h * M_HD:(h + 1) * M_HD]
         * lax.rsqrt(jnp.mean(hsum[:, h * M_HD:(h + 1) * M_HD] ** 2, axis=-1, keepdims=True) + NORM_EPS)
         for h in range(M_HEADS)], axis=1)
    out_ref[...] = hn * ng_ref[...] * _sigmoid(o_ref[...])
    co_ref[...] = c_scr[...]
    mo_ref[...] = m_scr[...]


def _mlstm(u, gcol, gt, bias, norm_g, states, prev_out, row0, batch, seq):
    ntok = u.shape[0]
    zero_init = states is None
    width = M_HEADS * M_HD
    qc = COL_M // width
    ngate = 4 * M_HEADS
    blk = lambda off: pl.BlockSpec((seq, width), lambda b: (row0 + b, off))
    st_c = pl.BlockSpec((None, 2, M_HEADS, M_HD, 2 * M_HD), lambda b: (b, 0, 0, 0, 0))
    st_v = pl.BlockSpec((None, 2, M_HEADS, 1, M_HD), lambda b: (b, 0, 0, 0, 0))
    in_specs = [blk(qc), blk(qc + 1), blk(qc + 2), blk(qc + 3),
                pl.BlockSpec((seq, ngate), lambda b: (row0 + b, 0)),
                pl.BlockSpec((seq // CHUNK, ngate, CHUNK), lambda b: (row0 + b, 0, 0)),
                pl.BlockSpec((1, ngate), lambda b: (0, 0)),
                pl.BlockSpec((ngate, 1), lambda b: (0, 0)),
                pl.BlockSpec((1, width), lambda b: (0, 0))]
    args = [u, u, u, u, gcol, gt, bias.reshape(1, ngate), bias.reshape(ngate, 1), norm_g]
    if not zero_init:
        in_specs += [st_c, st_v]
        args += list(states)
    in_specs.append(pl.BlockSpec(memory_space=pl.ANY))
    args.append(prev_out)
    return pl.pallas_call(
        functools.partial(_mlstm_kernel, zero_init=zero_init, seq=seq),
        grid=(batch,),
        in_specs=in_specs,
        out_specs=[pl.BlockSpec((seq, width), lambda b: (row0 + b, 0)), st_c, st_v],
        out_shape=[jax.ShapeDtypeStruct((ntok, width), F32),
                   jax.ShapeDtypeStruct((batch, 2, M_HEADS, M_HD, 2 * M_HD), F32),
                   jax.ShapeDtypeStruct((batch, 2, M_HEADS, 1, M_HD), F32)],
        scratch_shapes=[pltpu.VMEM((2, seq, width), F32), pltpu.VMEM((2, M_HEADS, M_HD, 2 * M_HD), F32),
                        pltpu.VMEM((2, M_HEADS, 1, M_HD), F32)],
        input_output_aliases={len(args) - 1: 0},
        compiler_params=_cparams(("arbitrary",)),
        name="mlstm_ctx" if zero_init else "mlstm_lat",
    )(*args)


def _softplus(x):
    return jnp.maximum(x, 0.0) + jnp.log1p(jnp.exp(-jnp.abs(x)))


def _rwkv_kernel(*refs, zero_init, seq):
    if zero_init:
        (r_ref, k_ref, v_ref, xl_ref, xg_ref, mur_ref, muk_ref, muv_ref, mul_ref, mug_ref,
         w0_ref, wup_ref, a0_ref, aup_ref, gup_ref, kk_ref, ka_ref, rk_ref, lng_ref, lnb_ref, alias_ref,
         out_ref, so_ref, r_scr, v_scr, kk_scr, g_scr, bonus_scr, lw_scr, kd_scr, b_scr, y_scr, s_scr) = refs
    else:
        (r_ref, k_ref, v_ref, xl_ref, xg_ref, mur_ref, muk_ref, muv_ref, mul_ref, mug_ref,
         w0_ref, wup_ref, a0_ref, aup_ref, gup_ref, kk_ref, ka_ref, rk_ref, lng_ref, lnb_ref, s0_ref, alias_ref,
         out_ref, so_ref, r_scr, v_scr, kk_scr, g_scr, bonus_scr, lw_scr, kd_scr, b_scr, y_scr, s_scr) = refs
    del alias_ref
    nc = seq // CHUNK
    hd = R_HD
    nh = r_ref.shape[1] // hd

    def tshift(x_ref, mu_ref):
        x = x_ref[...]
        row = lax.broadcasted_iota(I32, x.shape, 0)
        prev = jnp.where(row == 0, 0.0, pltpu.roll(x, 1, 0))
        nxt = jnp.where(row == seq - 1, 0.0, pltpu.roll(x, seq - 1, 0))
        return x + mu_ref[...] * (0.5 * (prev + nxt) - x)

    r = tshift(r_ref, mur_ref)
    k = tshift(k_ref, muk_ref)
    v = tshift(v_ref, muv_ref)
    xl = tshift(xl_ref, mul_ref)
    xg = tshift(xg_ref, mug_ref)
    g = _dot(_sigmoid(xg), gup_ref[...])
    kkp = k * kk_ref[...]
    kk = jnp.concatenate(
        [kkp[:, i * hd:(i + 1) * hd]
         * lax.rsqrt(jnp.maximum(jnp.sum(kkp[:, i * hd:(i + 1) * hd] ** 2, axis=-1, keepdims=True), 1e-24))
         for i in range(nh)], axis=1)
    tw = jnp.tanh(xl[:, 0:hd])
    xa = xl[:, hd:2 * hd]
    r_scr[...] = r
    v_scr[...] = v
    kk_scr[...] = kk
    g_scr[...] = g
    rkk = r * k * rk_ref[...]
    bonus_scr[...] = jnp.concatenate(
        [jnp.sum(rkk[:, i * hd:(i + 1) * hd], axis=-1, keepdims=True) * v[:, i * hd:(i + 1) * hd] for i in range(nh)],
        axis=1)
    for d in (0, 1):
        wd = -_softplus(-(w0_ref[d] + _dot(tw, wup_ref[d]))) - 0.5
        ad = _sigmoid(a0_ref[d] + _dot(xa, aup_ref[d]))
        lw_scr[d] = -jnp.exp(wd)
        kd_scr[d] = k * (1.0 + (ad - 1.0) * ka_ref[...])
        b_scr[d] = kk * ad
    if zero_init:
        s_scr[...] = jnp.zeros(s_scr.shape, F32)
    else:
        s_scr[...] = s0_ref[...]

    ti = lax.broadcasted_iota(I32, (CHUNK, CHUNK), 0)
    si = lax.broadcasted_iota(I32, (CHUNK, CHUNK), 1)
    eye = (ti == si).astype(F32)

    def chunk(c, carry):
        chains = []
        rows_d = []
        for d in (0, 1):
            cc = c if d == 0 else nc - 1 - c
            rows = pl.ds(pl.multiple_of(cc * CHUNK, CHUNK), CHUNK)
            rows_d.append(rows)
            incl = (si <= ti) if d == 0 else (si >= ti)
            strict = (si < ti) if d == 0 else (si > ti)
            lw = lw_scr[d, rows, :]
            lw_hi, lw_lo = _split(lw)
            tri = jnp.where(incl, 1.0, 0.0).astype(BF16)
            lc = _dg(tri, lw_hi, _NN) + _dg(tri, lw_lo, _NN)
            l_last = lc[CHUNK - 1:CHUNK, :] if d == 0 else lc[0:1, :]
            e_neg = jnp.exp(-lc)
            e_end = jnp.exp(l_last - lc)
            vc = v_scr[rows, :]
            kdc = kd_scr[d, rows, :]
            bc = b_scr[d, rows, :]
            rt = r_scr[rows, :] * jnp.exp(lc)
            kkt = kk_scr[rows, :] * jnp.exp(lc - lw)
            kh = kdc * e_neg
            bh = bc * e_neg
            kbar = kdc * e_end
            bbar = bc * e_end
            w_end = jnp.exp(l_last)
            for i in range(nh):
                sl = slice(i * hd, (i + 1) * hd)
                chains.append(dict(
                    d=d, i=i, incl=incl, strict=strict,
                    lhs=jnp.concatenate([kkt[:, sl], rt[:, sl]], axis=0),
                    rhs=jnp.concatenate([bh[:, sl], kh[:, sl]], axis=0),
                    end=jnp.concatenate([kbar[:, sl], bbar[:, sl]], axis=0),
                    v=vc[:, sl], w_end=w_end[:, sl], s0=s_scr[d, i]))
        for ch in chains:
            ch['ab'] = _dot1(ch['lhs'], ch['rhs'], _NT)
        for ch in chains:
            ch['proj'] = _dot1(ch['lhs'], ch['s0'], _NT)
        for ch in chains:
            ab = ch['ab']
            ch['a_kb'] = jnp.where(ch['strict'], ab[:CHUNK, :CHUNK], 0.0)
            ch['b_rb'] = jnp.where(ch['incl'], ab[CHUNK:, :CHUNK], 0.0)
            ch['akk_brk'] = jnp.concatenate([jnp.where(ch['strict'], ab[:CHUNK, CHUNK:], 0.0),
                                             jnp.where(ch['incl'], ab[CHUNK:, CHUNK:], 0.0)], axis=0)
        for ch in chains:
            ch['p'] = _dot1(ch['a_kb'], ch['a_kb'])
        for ch in chains:
            ch['abv'] = _dot1(ch['akk_brk'], ch['v'])
        for ch in chains:
            inv = eye - ch['a_kb']
            ch['inv'] = inv + _dot1(inv, ch['p'])
        span = 4
        while span < CHUNK:
            for ch in chains:
                ch['p'] = _dot1(ch['p'], ch['p'])
            for ch in chains:
                ch['inv'] = ch['inv'] + _dot1(ch['inv'], ch['p'])
            span *= 2
        for ch in chains:
            ch['u'] = _dot1(ch['inv'], ch['proj'][:CHUNK] + ch['abv'][:CHUNK])
        for ch in chains:
            ch['y'] = ch['proj'][CHUNK:] + ch['abv'][CHUNK:] - _dot1(ch['b_rb'], ch['u'])
        for ch in chains:
            upd = _dot3(jnp.concatenate([ch['v'], -ch['u']], axis=0), ch['end'], _TN)
            s_scr[ch['d'], ch['i']] = ch['s0'] * ch['w_end'] + upd
        for d in (0, 1):
            y_scr[d, rows_d[d], :] = jnp.concatenate([ch['y'] for ch in chains[nh * d:nh * (d + 1)]], axis=1)
        return carry

    lax.fori_loop(0, nc, chunk, 0)

    y = y_scr[0] + y_scr[1]
    outs = []
    for i in range(nh):
        yh = y[:, i * hd:(i + 1) * hd]
        mean = jnp.mean(yh, axis=-1, keepdims=True)
        var = jnp.mean(jnp.square(yh - mean), axis=-1, keepdims=True)
        outs.append((yh - mean) * lax.rsqrt(var + GN_EPS))
    yn = jnp.concatenate(outs, axis=1)
    out_ref[...] = (yn * lng_ref[...] + lnb_ref[...] + bonus_scr[...]) * g_scr[...]
    so_ref[...] = s_scr[...]


def _rwkv(u, p, state, prev_out, row0, batch, seq, heads_per_step):
    ntok = u.shape[0]
    zero_init = state is None
    nh = heads_per_step
    wd = nh * R_HD
    nsteps = R_HEADS // nh
    width = R_HEADS * R_HD
    rc = COL_R // wd
    sec = width // wd
    lc = (COL_R + 3 * width) // LANES
    ublk = lambda off: pl.BlockSpec((seq, wd), lambda b, h: (row0 + b, rc + off + h))
    ufix = lambda blk: pl.BlockSpec((seq, LANES), lambda b, h: (row0 + b, blk))
    mblk = lambda off: pl.BlockSpec((1, wd), lambda b, h: (0, off + h))
    mfix = lambda blk: pl.BlockSpec((1, LANES), lambda b, h: (0, blk))
    vec = pl.BlockSpec((1, wd), lambda b, h: (0, h))
    in_specs = [ublk(0), ublk(sec), ublk(2 * sec), ufix(lc), ufix(lc + 1),
                mblk(0), mblk(sec), mblk(2 * sec), mfix(lc - COL_R // LANES), mfix(lc - COL_R // LANES + 1),
                pl.BlockSpec((2, 1, wd), lambda b, h: (0, 0, h)),
                pl.BlockSpec((2, R_HD, wd), lambda b, h: (0, 0, h)),
                pl.BlockSpec((2, 1, wd), lambda b, h: (0, 0, h)),
                pl.BlockSpec((2, R_HD, wd), lambda b, h: (0, 0, h)),
                pl.BlockSpec((LANES, wd), lambda b, h: (0, h)),
                vec, vec, vec, vec, vec]
    args = [u, u, u, u, u, p['mu'], p['mu'], p['mu'], p['mu'], p['mu'],
            p['w0'], p['w_up'], p['a0'], p['a_up'], p['g_up'], p['k_k'], p['k_a'], p['r_k'], p['ln_g'], p['ln_b']]
    if not zero_init:
        in_specs.append(pl.BlockSpec((None, 2, nh, R_HD, R_HD), lambda b, h: (b, 0, h, 0, 0)))
        args.append(state)
    in_specs.append(pl.BlockSpec(memory_space=pl.ANY))
    args.append(prev_out)
    big = lambda n: pltpu.VMEM((n, seq, wd), F32)
    return pl.pallas_call(
        functools.partial(_rwkv_kernel, zero_init=zero_init, seq=seq),
        grid=(batch, nsteps),
        in_specs=in_specs,
        out_specs=[pl.BlockSpec((seq, wd), lambda b, h: (row0 + b, h)),
                   pl.BlockSpec((None, 2, nh, R_HD, R_HD), lambda b, h: (b, 0, h, 0, 0))],
        out_shape=[jax.ShapeDtypeStruct((ntok, width), F32),
                   jax.ShapeDtypeStruct((batch, 2, R_HEADS, R_HD, R_HD), F32)],
        scratch_shapes=[pltpu.VMEM((seq, wd), F32)] * 5
                       + [big(2), big(2), big(2), big(2), pltpu.VMEM((2, nh, R_HD, R_HD), F32)],
        input_output_aliases={len(args) - 1: 0},
        compiler_params=_cparams(("arbitrary", "arbitrary")),
        name="rwkv_ctx" if zero_init else "rwkv_lat",
    )(*args)


def _top2_sum(a, b, c, d):
    m1, n1 = jnp.maximum(a, b), jnp.minimum(a, b)
    m2, n2 = jnp.maximum(c, d), jnp.minimum(c, d)
    return jnp.maximum(m1, m2) + jnp.maximum(jnp.minimum(m1, m2), jnp.maximum(n1, n2))


def _first_argmax(vals):
    best = functools.reduce(jnp.maximum, vals)
    idx = jnp.full(best.shape, len(vals) - 1, I32)
    for j in range(len(vals) - 2, -1, -1):
        idx = jnp.where(vals[j] == best, j, idx)
    return best, idx


def _out_kernel(att_ref, m_ref, r_ref, x_ref, w_ref, g1_ref, sh2_ref, sc2_ref, n2_ref, rw_ref, rb_ref,
                x1_ref, h2_ref, route_ref, cnt_ref, cnt_scr):
    @pl.when(pl.program_id(0) == 0)
    def _():
        cnt_scr[...] = jnp.zeros(cnt_scr.shape, F32)

    na = att_ref.shape[1]
    nm = m_ref.shape[1]
    mix = (jnp.dot(att_ref[...].astype(BF16), w_ref[0:na, :], preferred_element_type=F32)
           + jnp.dot(m_ref[...].astype(BF16), w_ref[na:na + nm, :], preferred_element_type=F32)
           + jnp.dot(r_ref[...].astype(BF16), w_ref[na + nm:, :], preferred_element_type=F32))
    x1 = x_ref[...] + g1_ref[...] * mix
    x1_ref[...] = x1
    h2 = _rms(x1, n2_ref[...]) * (1.0 + sc2_ref[...]) + sh2_ref[...]
    h2_ref[...] = h2
    logits = _dot1(rw_ref[...], h2, _NT)
    s = _sigmoid(logits)
    ssel = s + rb_ref[...]
    srow = [s[e:e + 1, :] for e in range(N_EXPERTS)]
    brow = [ssel[e:e + 1, :] for e in range(N_EXPERTS)]
    gscore = [_top2_sum(*brow[EXPERTS_PER_GROUP * g:EXPERTS_PER_GROUP * (g + 1)]) for g in range(N_EXPERT_GROUPS)]
    _, gidx = _first_argmax(gscore)
    pick = lambda rows, j: functools.reduce(
        lambda acc, g: jnp.where(gidx == g, rows[EXPERTS_PER_GROUP * g + j], acc),
        range(N_EXPERT_GROUPS - 2, -1, -1), rows[EXPERTS_PER_GROUP * (N_EXPERT_GROUPS - 1) + j])
    ing = [pick(brow, j) for j in range(EXPERTS_PER_GROUP)]
    sin_ = [pick(srow, j) for j in range(EXPERTS_PER_GROUP)]
    _, l1 = _first_argmax(ing)
    _, l2 = _first_argmax([jnp.where(l1 == j, -jnp.inf, ing[j]) for j in range(EXPERTS_PER_GROUP)])
    sel = lambda l: functools.reduce(lambda acc, j: jnp.where(l == j, sin_[j], acc),
                                     range(EXPERTS_PER_GROUP - 2, -1, -1), sin_[EXPERTS_PER_GROUP - 1])
    w1, w2 = sel(l1), sel(l2)
    tot = w1 + w2
    e1 = gidx * EXPERTS_PER_GROUP + l1
    e2 = gidx * EXPERTS_PER_GROUP + l2
    tm = e1.shape[1]
    eid = lax.broadcasted_iota(I32, (N_EXPERTS, tm), 0)
    oh1 = eid == e1
    oh2 = eid == e2
    picked = jnp.where(jnp.logical_or(oh1, oh2), 1.0, 0.0)
    earlier = jnp.where(lax.broadcasted_iota(I32, (tm, tm), 0) < lax.broadcasted_iota(I32, (tm, tm), 1), 1.0, 0.0)
    rank = cnt_scr[:, 0:1] + jnp.dot(picked.astype(BF16), earlier.astype(BF16), preferred_element_type=F32)
    pos1 = jnp.sum(jnp.where(oh1, rank, 0.0), axis=0, keepdims=True)
    pos2 = jnp.sum(jnp.where(oh2, rank, 0.0), axis=0, keepdims=True)
    cnt = cnt_scr[...] + jnp.sum(picked, axis=1, keepdims=True)
    cnt_scr[...] = cnt
    cnt_ref[...] = cnt
    zero = jnp.zeros_like(w1)
    route_ref[...] = jnp.concatenate([e1.astype(F32), e2.astype(F32), w1 / tot, w2 / tot, pos1, pos2, zero, zero],
                                     axis=0)


def _out_proj(att, m_out, r_out, x, w_out, mod3, n2g, rw_t, rb, t_ctx, s_lat):
    ntok, d = x.shape
    tm = 256
    row = lambda i: _mod_row(i * tm, t_ctx, s_lat)
    modblk = lambda j: pl.BlockSpec((None, 1, d), lambda i: (row(i), 0, j))
    return pl.pallas_call(
        _out_kernel,
        grid=(ntok // tm,),
        in_specs=[pl.BlockSpec((tm, att.shape[1]), lambda i: (i, 0)),
                  pl.BlockSpec((tm, m_out.shape[1]), lambda i: (i, 0)),
                  pl.BlockSpec((tm, r_out.shape[1]), lambda i: (i, 0)),
                  pl.BlockSpec((tm, d), lambda i: (i, 0)),
                  pl.BlockSpec(w_out.shape, lambda i: (0, 0)),
                  modblk(2), modblk(3), modblk(4),
                  pl.BlockSpec((1, d), lambda i: (0, 0)),
                  pl.BlockSpec(rw_t.shape, lambda i: (0, 0)),
                  pl.BlockSpec(rb.shape, lambda i: (0, 0))],
        out_specs=[pl.BlockSpec((tm, d), lambda i: (i, 0)),
                   pl.BlockSpec((tm, d), lambda i: (i, 0)),
                   pl.BlockSpec((8, tm), lambda i: (0, i)),
                   pl.BlockSpec((N_EXPERTS, LANES), lambda i: (0, 0))],
        out_shape=[jax.ShapeDtypeStruct((ntok, d), F32), jax.ShapeDtypeStruct((ntok, d), F32),
                   jax.ShapeDtypeStruct((8, ntok), F32), jax.ShapeDtypeStruct((N_EXPERTS, LANES), F32)],
        scratch_shapes=[pltpu.VMEM((N_EXPERTS, LANES), F32)],
        compiler_params=_cparams(("arbitrary",)),
        name="out_proj",
    )(att, m_out, r_out, x, w_out, mod3, mod3, mod3, n2g, rw_t, rb)


def _row_gather(src_hbm, idx_ref, base, dst, sem, n, unrolled):
    def start(j):
        pltpu.make_async_copy(src_hbm.at[pl.ds(idx_ref[base + j], 1), :], dst.at[pl.ds(j, 1), :], sem).start()

    if unrolled:
        for j in range(n):
            start(j)
    else:
        def body(j, c):
            start(j)
            return c
        lax.fori_loop(0, n, body, 0)


def _expert_kernel(be_ref, tok_ref, nused_ref, h_hbm, w1_ref, w3_ref, w2_ref, y_ref, xbuf, w1b, w3b, w2b, sem):
    i = pl.program_id(0)
    n_used = nused_ref[0]
    wait = lambda s: pltpu.make_async_copy(h_hbm.at[pl.ds(0, EXPERT_ROWS), :], xbuf.at[s], sem.at[s]).wait()

    @pl.when(i < n_used)
    def _():
        slot = i % 2

        @pl.when(i == 0)
        def _():
            _row_gather(h_hbm, tok_ref, 0, xbuf.at[0], sem.at[0], EXPERT_ROWS, unrolled=False)

        @pl.when(jnp.logical_or(i == 0, be_ref[i] != be_ref[jnp.maximum(i - 1, 0)]))
        def _():
            w1b[...] = w1_ref[...].astype(BF16)
            w3b[...] = w3_ref[...].astype(BF16)
            w2b[...] = w2_ref[...].astype(BF16)

        wait(slot)
        _row_gather(h_hbm, tok_ref, (i + 1) * EXPERT_ROWS, xbuf.at[1 - slot], sem.at[1 - slot], EXPERT_ROWS,
                    unrolled=True)
        xb = xbuf[slot].astype(BF16)
        a = jnp.dot(xb, w1b[...], preferred_element_type=F32)
        b = jnp.dot(xb, w3b[...], preferred_element_type=F32)
        hmid = (a * _sigmoid(a)) * b
        y_ref[...] = jnp.dot(hmid.astype(BF16), w2b[...], preferred_element_type=F32)

        @pl.when(i == n_used - 1)
        def _():
            wait(1 - slot)

    @pl.when(i >= n_used)
    def _():
        y_ref[...] = jnp.zeros(y_ref.shape, F32)


def _experts(h2, block_e, row_tok, n_used, w1, w3, w2, layer):
    ntok, d = h2.shape
    de = w1.shape[3]
    n_rows = row_tok.shape[0] - EXPERT_ROWS
    nb = n_rows // EXPERT_ROWS
    return pl.pallas_call(
        _expert_kernel,
        grid_spec=pltpu.PrefetchScalarGridSpec(
            num_scalar_prefetch=3,
            grid=(nb,),
            in_specs=[pl.BlockSpec(memory_space=pl.ANY),
                      pl.BlockSpec((None, None, d, de), lambda i, be, tok, nu: (layer, be[i], 0, 0)),
                      pl.BlockSpec((None, None, d, de), lambda i, be, tok, nu: (layer, be[i], 0, 0)),
                      pl.BlockSpec((None, None, de, d), lambda i, be, tok, nu: (layer, be[i], 0, 0))],
            out_specs=pl.BlockSpec((EXPERT_ROWS, d), lambda i, be, tok, nu: (i, 0)),
            scratch_shapes=[pltpu.VMEM((2, EXPERT_ROWS, d), F32), pltpu.VMEM((d, de), BF16),
                            pltpu.VMEM((d, de), BF16), pltpu.VMEM((de, d), BF16), pltpu.SemaphoreType.DMA((2,))]),
        out_shape=jax.ShapeDtypeStruct((n_rows, d), F32),
        compiler_params=_cparams(("arbitrary",)),
        name="experts",
    )(block_e, row_tok, n_used, h2, w1, w3, w2)


def _combine_kernel(d1_ref, d2_ref, y_hbm, x1_ref, g2_ref, gate_ref, o_ref, ybuf, sem):
    tm = x1_ref.shape[0]
    i = pl.program_id(0)
    slot = i % 2

    def gather(tile, s, unrolled):
        _row_gather(y_hbm, d1_ref, tile * tm, ybuf.at[s, 0], sem.at[s], tm, unrolled)
        _row_gather(y_hbm, d2_ref, tile * tm, ybuf.at[s, 1], sem.at[s], tm, unrolled)

    def wait(s):
        pltpu.make_async_copy(y_hbm.at[pl.ds(0, tm), :], ybuf.at[s, 0], sem.at[s]).wait()
        pltpu.make_async_copy(y_hbm.at[pl.ds(0, tm), :], ybuf.at[s, 1], sem.at[s]).wait()

    @pl.when(i == 0)
    def _():
        gather(0, 0, False)

    wait(slot)
    gather(i + 1, 1 - slot, True)
    gate = gate_ref[...]
    moe = ybuf[slot, 0] * gate[:, 0:1] + ybuf[slot, 1] * gate[:, 1:2]
    o_ref[...] = x1_ref[...] + g2_ref[...] * moe

    @pl.when(i == pl.num_programs(0) - 1)
    def _():
        wait(1 - slot)


def _combine(y, x1, mod3, gate, dest1, dest2, t_ctx, s_lat):
    ntok, d = x1.shape
    tm = 256
    row = lambda i: _mod_row(i * tm, t_ctx, s_lat)
    return pl.pallas_call(
        _combine_kernel,
        grid_spec=pltpu.PrefetchScalarGridSpec(
            num_scalar_prefetch=2,
            grid=(ntok // tm,),
            in_specs=[pl.BlockSpec(memory_space=pl.ANY),
                      pl.BlockSpec((tm, d), lambda i, a, b: (i, 0)),
                      pl.BlockSpec((None, 1, d), lambda i, a, b: (row(i), 0, 5)),
                      pl.BlockSpec((tm, 2), lambda i, a, b: (i, 0))],
            out_specs=pl.BlockSpec((tm, d), lambda i, a, b: (i, 0)),
            scratch_shapes=[pltpu.VMEM((2, 2, tm, d), F32), pltpu.SemaphoreType.DMA((2,))]),
        out_shape=jax.ShapeDtypeStruct((ntok, d), F32),
        compiler_params=_cparams(("arbitrary",)),
        name="combine",
    )(dest1, dest2, y, x1, mod3, gate)


def _dispatch(route, counts, tile):
    ntok = route.shape[1]
    e = route[0:2].astype(I32)
    pos = route[4:6].astype(I32)
    gate = route[2:4].T
    counts = counts[:, 0].astype(I32)
    padded = (counts + EXPERT_ROWS - 1) // EXPERT_ROWS * EXPERT_ROWS
    pad_end = jnp.cumsum(padded)
    pad_start = pad_end - padded
    onehot = (e[:, :, None] == jnp.arange(N_EXPERTS, dtype=I32)).astype(I32)
    dest = jnp.sum(onehot * pad_start, axis=-1) + pos
    n_rows = -(-(2 * ntok) // EXPERT_ROWS) * EXPERT_ROWS + N_EXPERTS * EXPERT_ROWS
    nb = n_rows // EXPERT_ROWS
    tok = jnp.broadcast_to(jnp.arange(ntok, dtype=I32)[None, :], (2, ntok))
    row_tok = jnp.zeros((n_rows + EXPERT_ROWS,), I32).at[dest.reshape(-1)].set(tok.reshape(-1))
    blk_start = jnp.arange(nb, dtype=I32) * EXPERT_ROWS
    block_e = jnp.minimum(jnp.sum((pad_end[None, :] <= blk_start[:, None]).astype(I32), axis=1), N_EXPERTS - 1)
    n_used = pad_end[-1:] // EXPERT_ROWS
    dest = jnp.pad(dest, ((0, 0), (0, tile)))
    return row_tok, block_e, n_used, gate, dest[0], dest[1]


def kernel(x_prompt, x_sample, c, cache_attn_k, cache_attn_v, state_mlstm_C, state_mlstm_n, state_mlstm_m, state_rwkv, c_ctx, norm1_g, norm2_g, w_mod, b_mod, w_in, w_out, attn_q_norm, attn_k_norm, mlstm_i_bias, mlstm_f_bias, mlstm_norm_g, rwkv_mu, rwkv_w0, rwkv_w_up, rwkv_a0, rwkv_a_up, rwkv_g_up, rwkv_k_k, rwkv_k_a, rwkv_r_k, rwkv_ln_g, rwkv_ln_b, router_w, router_b, exp_w1, exp_w3, exp_w2):
    b_ctx, s_ctx, d = x_prompt.shape
    b_lat, s_lat, _ = x_sample.shape
    depth = w_in.shape[0]
    t_ctx = b_ctx * s_ctx
    ntok = t_ctx + b_lat * s_lat
    assert b_lat + 1 <= 8 and s_lat % 1024 == 0 and t_ctx % 1024 == 0 and t_ctx % s_lat == 0
    past = cache_attn_k.shape[2]

    x = jnp.concatenate([x_prompt.reshape(t_ctx, d), x_sample.reshape(b_lat * s_lat, d)], axis=0)
    c_all = jnp.zeros((8, d), F32).at[0].set(c_ctx).at[1:1 + b_lat].set(c)
    mod = _modulation(c_all, w_mod, b_mod)
    cos, sin = _rope_tables(s_lat)
    rw_t = router_w.T
    rb = router_b.reshape(N_EXPERTS, 1)
    m_width = M_HEADS * M_HD
    r_width = R_HEADS * R_HD
    n_in = w_in.shape[2]
    gate_lo = COL_M + 4 * m_width

    ks, vs, cs, ns, ms, rs = [], [], [], [], [], []
    for l in range(depth):
        w = w_in[l]
        w_p = jnp.concatenate([w[:, :gate_lo], w[:, gate_lo + 4 * M_HEADS:], w[:, gate_lo:gate_lo + 4 * M_HEADS],
                               jnp.zeros((d, N_IN_PAD - n_in), F32)], axis=1).astype(BF16)
        mod3 = mod[l].reshape(8, 1, 6 * d)
        u = _in_proj(x, norm1_g[l][None], mod3, w_p, t_ctx, s_lat)

        ck = cache_attn_k[:, l].reshape(b_lat, past, ATT_KV_HEADS * ATT_HD)
        cv = cache_attn_v[:, l].reshape(b_lat, past, ATT_KV_HEADS * ATT_HD)
        att, k_ctx = _attention(u, attn_q_norm[l][None], attn_k_norm[l][None], ck, cv, cos, sin,
                                b_ctx, s_ctx, b_lat, s_lat)
        ks.append(k_ctx.reshape(b_ctx, s_ctx, ATT_KV_HEADS, ATT_HD))
        vcol = ATT_GROUPS * ATT_KV_HEADS * ATT_HD + ATT_KV_HEADS * ATT_HD
        vs.append(u[:t_ctx, vcol:vcol + ATT_KV_HEADS * ATT_HD].reshape(b_ctx, s_ctx, ATT_KV_HEADS, ATT_HD))

        gcol = u[:, COL_GATE:COL_GATE + 4 * M_HEADS]
        gt = gcol.reshape(ntok // CHUNK, CHUNK, 4 * M_HEADS).transpose(0, 2, 1)
        bias = jnp.stack([mlstm_i_bias[l], mlstm_f_bias[l]], axis=1)
        ng = mlstm_norm_g[l][None]
        m_out = jnp.zeros((ntok, m_width), F32)
        m_out, cx_c, m_c = _mlstm(u, gcol, gt, bias, ng, None, m_out, 0, b_ctx, s_ctx)
        n_col = jnp.pad(state_mlstm_n[:, l][..., None], ((0, 0),) * 4 + ((0, M_HD - 1),))
        lat_states = (jnp.concatenate([state_mlstm_C[:, l], n_col], axis=-1),
                      jnp.broadcast_to(state_mlstm_m[:, l][..., None, None], (b_lat, 2, M_HEADS, 1, M_HD)))
        m_out, _, _ = _mlstm(u, gcol, gt, bias, ng, lat_states, m_out, t_ctx // s_lat, b_lat, s_lat)
        cs.append(cx_c[..., :M_HD])
        ns.append(cx_c[..., M_HD])
        ms.append(m_c[:, :, :, 0, 0])

        rp = dict(mu=rwkv_mu[l][None], w0=rwkv_w0[l].reshape(2, 1, r_width), w_up=rwkv_w_up[l],
                  a0=rwkv_a0[l].reshape(2, 1, r_width), a_up=rwkv_a_up[l], g_up=rwkv_g_up[l],
                  k_k=rwkv_k_k[l][None], k_a=rwkv_k_a[l][None], r_k=rwkv_r_k[l].reshape(1, r_width),
                  ln_g=rwkv_ln_g[l][None], ln_b=rwkv_ln_b[l][None])
        r_out = jnp.zeros((ntok, r_width), F32)
        r_out, r_c = _rwkv(u, rp, None, r_out, 0, b_ctx, s_ctx, RWKV_HEADS_PER_STEP_CTX)
        r_out, _ = _rwkv(u, rp, state_rwkv[:, l], r_out, t_ctx // s_lat, b_lat, s_lat, RWKV_HEADS_PER_STEP_LAT)
        rs.append(r_c)

        x1, h2, route, counts = _out_proj(att, m_out, r_out, x, w_out[l].astype(BF16), mod3, norm2_g[l][None],
                                          rw_t, rb, t_ctx, s_lat)
        row_tok, block_e, n_used, gate, dest1, dest2 = _dispatch(route, counts, 256)
        y = _experts(h2, block_e, row_tok, n_used, exp_w1, exp_w3, exp_w2, l)
        x = _combine(y, x1, mod3, gate, dest1, dest2, t_ctx, s_lat)

    y_prompt = x[:t_ctx].reshape(b_ctx, s_ctx, d)
    y_sample = x[t_ctx:].reshape(b_lat, s_lat, d)
    return (y_prompt, y_sample, jnp.stack(ks, axis=1), jnp.stack(vs, axis=1), jnp.stack(cs, axis=1),
            jnp.stack(ns, axis=1), jnp.stack(ms, axis=1), jnp.stack(rs, axis=1))
```

```python
import functools

import jax
import jax.numpy as jnp
from jax import lax
from jax.experimental import pallas as pl
from jax.experimental.pallas import tpu as pltpu

F32 = jnp.float32
BF16 = jnp.bfloat16
I32 = jnp.int32

NORM_EPS = 1e-6
GN_EPS = 64e-5
M_INIT = -1e30
GRID_W = 64
ROPE_THETA = 10000.0
ATT_HD = 128
ATT_GROUPS = 4
ATT_KV_HEADS = 2
M_HD = 128
M_HEADS = 4
R_HD = 64
R_HEADS = 8
N_EXPERTS = 16
N_EXPERT_GROUPS = 4
EXPERTS_PER_GROUP = 4
CHUNK = 64
LANES = 128
EXPERT_ROWS = 256
VMEM_LIMIT = 56 * 1024 * 1024
RWKV_HEADS_PER_STEP_CTX = 8
RWKV_HEADS_PER_STEP_LAT = 8

COL_ATT = 0
COL_M = 1536
COL_R = 3584
COL_GATE = 5376
N_IN_PAD = 5632


def _cparams(sem):
    return pltpu.CompilerParams(dimension_semantics=sem, vmem_limit_bytes=VMEM_LIMIT)


def _dot(a, b):
    return jnp.dot(a.astype(BF16), b.astype(BF16), preferred_element_type=F32)


def _dg(a, b, dims):
    return lax.dot_general(a, b, (dims, ((), ())), preferred_element_type=F32)


_NN = ((1,), (0,))
_NT = ((1,), (1,))
_TN = ((0,), (0,))


def _split(a):
    hi = a.astype(BF16)
    lo = (a - hi.astype(F32)).astype(BF16)
    return hi, lo


def _dot3(a, b, dims=_NN):
    ah, al = _split(a)
    bh, bl = _split(b)
    return _dg(ah, bh, dims) + (_dg(ah, bl, dims) + _dg(al, bh, dims))


def _dot1(a, b, dims=_NN):
    return _dg(a.astype(BF16), b.astype(BF16), dims)


def _rms(x, g):
    return x * lax.rsqrt(jnp.mean(x * x, axis=-1, keepdims=True) + NORM_EPS) * g


def _sigmoid(x):
    return 1.0 / (1.0 + jnp.exp(-x))


def _mod_kernel(c_ref, w_ref, b_ref, o_ref):
    c = c_ref[...]
    o_ref[...] = _dot(c * _sigmoid(c), w_ref[...]) + b_ref[...]


def _modulation(c_all, w_mod, b_mod):
    depth, d, n = w_mod.shape
    tn = 1024
    return pl.pallas_call(
        _mod_kernel,
        grid=(depth, n // tn),
        in_specs=[pl.BlockSpec((8, d), lambda l, j: (0, 0)),
                  pl.BlockSpec((None, d, tn), lambda l, j: (l, 0, j)),
                  pl.BlockSpec((None, 1, tn), lambda l, j: (l, 0, j))],
        out_specs=pl.BlockSpec((None, 8, tn), lambda l, j: (l, 0, j)),
        out_shape=jax.ShapeDtypeStruct((depth, 8, n), F32),
        compiler_params=_cparams(("arbitrary", "arbitrary")),
        name="modulation",
    )(c_all, w_mod, b_mod.reshape(depth, 1, n))


def _mod_row(tok0, t_ctx, s_lat):
    return jnp.where(tok0 < t_ctx, 0, 1 + (tok0 - t_ctx) // s_lat)


def _in_kernel(xc_ref, xl_ref, g_ref, sh_ref, sc_ref, wa_ref, wb_ref, o_ref, h_ref, *, n_ctx_tiles, n_head_tiles):
    i = pl.program_id(0)
    j = pl.program_id(1)

    def normalise(x_ref):
        slab = 256
        for r0 in range(0, x_ref.shape[0], slab):
            h = _rms(x_ref[r0:r0 + slab, :], g_ref[...]) * (1.0 + sc_ref[...]) + sh_ref[...]
            h_ref[r0:r0 + slab, :] = h.astype(BF16)

    @pl.when(jnp.logical_and(j == 0, i < n_ctx_tiles))
    def _():
        normalise(xc_ref)

    @pl.when(jnp.logical_and(j == 0, i >= n_ctx_tiles))
    def _():
        normalise(xl_ref)

    @pl.when(j < n_head_tiles)
    def _():
        o_ref[...] = jnp.dot(h_ref[...], wa_ref[...].astype(BF16), preferred_element_type=F32)

    @pl.when(j >= n_head_tiles)
    def _():
        o_ref[...] = jnp.dot(h_ref[...], wb_ref[...], preferred_element_type=F32)


def _in_proj(xc, xl, g1, mod3, w_in, layer, w_tail, s_lat):
    t_ctx, d = xc.shape
    ntok = t_ctx + xl.shape[0]
    tm, tn = 1024, 512
    na = COL_R // tn
    n = COL_R + w_tail.shape[1]
    nct = t_ctx // tm
    row = lambda i: _mod_row(i * tm, t_ctx, s_lat)
    return pl.pallas_call(
        functools.partial(_in_kernel, n_ctx_tiles=nct, n_head_tiles=na),
        grid=(ntok // tm, n // tn),
        in_specs=[pl.BlockSpec((tm, d), lambda i, j: (jnp.minimum(i, nct - 1), 0)),
                  pl.BlockSpec((tm, d), lambda i, j: (jnp.maximum(i - nct, 0), 0)),
                  pl.BlockSpec((1, d), lambda i, j: (0, 0)),
                  pl.BlockSpec((None, 1, d), lambda i, j: (row(i), 0, 0)),
                  pl.BlockSpec((None, 1, d), lambda i, j: (row(i), 0, 1)),
                  pl.BlockSpec((None, d, tn), lambda i, j: (layer, 0, jnp.minimum(j, na - 1))),
                  pl.BlockSpec((d, tn), lambda i, j: (0, jnp.maximum(j - na, 0)))],
        out_specs=pl.BlockSpec((tm, tn), lambda i, j: (i, j)),
        out_shape=jax.ShapeDtypeStruct((ntok, n), F32),
        scratch_shapes=[pltpu.VMEM((tm, d), BF16)],
        compiler_params=_cparams(("arbitrary", "arbitrary")),
        name="in_proj",
    )(xc, xl, g1, mod3, mod3, w_in, w_tail)


def _softmax_av(q, kb, vb):
    s = _dg(q.astype(BF16), kb, _NT) * (ATT_HD ** -0.5)
    p = jnp.exp(s - jnp.max(s, axis=-1, keepdims=True))
    l = jnp.sum(p, axis=-1, keepdims=True)
    return jnp.dot(p.astype(BF16), vb, preferred_element_type=F32) / l


def _att_ctx_kernel(q_ref, k_ref, v_ref, qn_ref, kn_ref, o_ref, ko_ref, vo_ref):
    k = _rms(k_ref[...], kn_ref[...])
    ko_ref[...] = k
    vo_ref[...] = v_ref[...]
    kb = k.astype(BF16)
    vb = v_ref[...].astype(BF16)
    for g in range(ATT_GROUPS):
        q = _rms(q_ref[:, g * ATT_HD:(g + 1) * ATT_HD], qn_ref[...])
        o_ref[:, g * ATT_HD:(g + 1) * ATT_HD] = _softmax_av(q, kb, vb)


def _rope(x, cos, sin):
    lane = lax.broadcasted_iota(I32, x.shape, 1)
    first = (lane % (ATT_HD // 2)) < (ATT_HD // 4)
    partner = jnp.where(first, pltpu.roll(x, ATT_HD - ATT_HD // 4, 1), pltpu.roll(x, ATT_HD // 4, 1))
    return x * cos + partner * sin


def _att_lat_kernel(q_ref, k_ref, v_ref, ck_ref, cv_ref, qn_ref, kn_ref, cosq_ref, sinq_ref, cosk_ref, sink_ref,
                    o_ref, kb_ref, vb_ref, *, past):
    @pl.when(pl.program_id(2) == 0)
    def _():
        k = _rope(_rms(k_ref[...], kn_ref[...]), cosk_ref[...], sink_ref[...])
        kb_ref[0:past, :] = ck_ref[...].astype(BF16)
        kb_ref[past:, :] = k.astype(BF16)
        vb_ref[0:past, :] = cv_ref[...].astype(BF16)
        vb_ref[past:, :] = v_ref[...].astype(BF16)

    kb = kb_ref[...]
    vb = vb_ref[...]
    for g in range(ATT_GROUPS):
        q = _rope(_rms(q_ref[:, g * ATT_HD:(g + 1) * ATT_HD], qn_ref[...]), cosq_ref[...], sinq_ref[...])
        o_ref[:, g * ATT_HD:(g + 1) * ATT_HD] = _softmax_av(q, kb, vb)


def _attention(u, qn, kn, cache_k, cache_v, cos, sin, b_ctx, s_ctx, b_lat, s_lat):
    ntok = u.shape[0]
    t_ctx = b_ctx * s_ctx
    gw = ATT_GROUPS * ATT_HD
    kcol = (ATT_KV_HEADS * gw) // ATT_HD
    vcol = kcol + ATT_KV_HEADS
    kv_spec = pl.BlockSpec((s_ctx, ATT_HD), lambda b, h: (b, h))
    att_c, k_ctx, v_ctx = pl.pallas_call(
        _att_ctx_kernel,
        grid=(b_ctx, ATT_KV_HEADS),
        in_specs=[pl.BlockSpec((s_ctx, gw), lambda b, h: (b, h)),
                  pl.BlockSpec((s_ctx, ATT_HD), lambda b, h: (b, kcol + h)),
                  pl.BlockSpec((s_ctx, ATT_HD), lambda b, h: (b, vcol + h)),
                  pl.BlockSpec((1, ATT_HD), lambda b, h: (0, 0)),
                  pl.BlockSpec((1, ATT_HD), lambda b, h: (0, 0))],
        out_specs=[pl.BlockSpec((s_ctx, gw), lambda b, h: (b, h)), kv_spec, kv_spec],
        out_shape=[jax.ShapeDtypeStruct((t_ctx, ATT_KV_HEADS * gw), F32),
                   jax.ShapeDtypeStruct((t_ctx, ATT_KV_HEADS * ATT_HD), F32),
                   jax.ShapeDtypeStruct((t_ctx, ATT_KV_HEADS * ATT_HD), F32)],
        compiler_params=_cparams(("arbitrary", "arbitrary")),
        name="att_ctx",
    )(u, u, u, qn, kn)

    tq = 256
    nqb = s_lat // tq
    past = cache_k.shape[1]
    qrow0 = t_ctx // tq
    krow0 = t_ctx // s_lat
    att_l = pl.pallas_call(
        functools.partial(_att_lat_kernel, past=past),
        grid=(b_lat, ATT_KV_HEADS, nqb),
        in_specs=[pl.BlockSpec((tq, gw), lambda b, h, i: (qrow0 + b * nqb + i, h)),
                  pl.BlockSpec((s_lat, ATT_HD), lambda b, h, i: (krow0 + b, kcol + h)),
                  pl.BlockSpec((s_lat, ATT_HD), lambda b, h, i: (krow0 + b, vcol + h)),
                  pl.BlockSpec((None, past, ATT_HD), lambda b, h, i: (b, 0, h)),
                  pl.BlockSpec((None, past, ATT_HD), lambda b, h, i: (b, 0, h)),
                  pl.BlockSpec((1, ATT_HD), lambda b, h, i: (0, 0)),
                  pl.BlockSpec((1, ATT_HD), lambda b, h, i: (0, 0)),
                  pl.BlockSpec((tq, ATT_HD), lambda b, h, i: (i, 0)),
                  pl.BlockSpec((tq, ATT_HD), lambda b, h, i: (i, 0)),
                  pl.BlockSpec((s_lat, ATT_HD), lambda b, h, i: (0, 0)),
                  pl.BlockSpec((s_lat, ATT_HD), lambda b, h, i: (0, 0))],
        out_specs=pl.BlockSpec((tq, gw), lambda b, h, i: (b * nqb + i, h)),
        out_shape=jax.ShapeDtypeStruct((ntok - t_ctx, ATT_KV_HEADS * gw), F32),
        scratch_shapes=[pltpu.VMEM((past + s_lat, ATT_HD), BF16), pltpu.VMEM((past + s_lat, ATT_HD), BF16)],
        compiler_params=_cparams(("arbitrary", "arbitrary", "arbitrary")),
        name="att_lat",
    )(u, u, u, cache_k, cache_v, qn, kn, cos, sin, cos, sin)
    return att_c, att_l, k_ctx, v_ctx


def _rope_tables(n_tokens):
    pos = jnp.arange(n_tokens)
    row = (pos // GRID_W).astype(F32)
    col = (pos % GRID_W).astype(F32)
    n_freq = ATT_HD // 4
    inv_freq = ROPE_THETA ** (-jnp.arange(n_freq, dtype=F32) / n_freq)
    ang_r = row[:, None] * inv_freq[None, :]
    ang_c = col[:, None] * inv_freq[None, :]
    cos = jnp.concatenate([jnp.cos(ang_r), jnp.cos(ang_r), jnp.cos(ang_c), jnp.cos(ang_c)], axis=-1)
    sin = jnp.concatenate([-jnp.sin(ang_r), jnp.sin(ang_r), -jnp.sin(ang_c), jnp.sin(ang_c)], axis=-1)
    return cos, sin


def _log_sigmoid(x):
    return jnp.minimum(x, 0.0) - jnp.log1p(jnp.exp(-jnp.abs(x)))


def _mlstm_kernel(*refs, zero_init, seq):
    if zero_init:
        (q_ref, k_ref, v_ref, o_ref, gc_ref, gt_ref, br_ref, bc_ref, ng_ref,
         out_ref, co_ref, mo_ref, h_scr, c_scr, m_scr) = refs
    else:
        (q_ref, k_ref, v_ref, o_ref, gc_ref, gt_ref, br_ref, bc_ref, ng_ref, c0_ref, m0_ref,
         out_ref, co_ref, mo_ref, h_scr, c_scr, m_scr) = refs
    nc = seq // CHUNK
    if zero_init:
        c_scr[...] = jnp.zeros(c_scr.shape, F32)
        m_scr[...] = jnp.full(m_scr.shape, M_INIT, F32)
    else:
        c_scr[...] = c0_ref[...]
        m_scr[...] = m0_ref[...]

    ti = lax.broadcasted_iota(I32, (CHUNK, CHUNK), 0)
    si = lax.broadcasted_iota(I32, (CHUNK, CHUNK), 1)
    ones_col = jnp.where(lax.broadcasted_iota(I32, (CHUNK, M_HD), 1) == 0, 1.0, 0.0).astype(BF16)
    eye_d = jnp.where(lax.broadcasted_iota(I32, (M_HD, M_HD), 0) == lax.broadcasted_iota(I32, (M_HD, M_HD), 1),
                      1.0, 0.0).astype(BF16)

    def cumsum3(tri, x, tri_left):
        hi = x.astype(BF16)
        r1 = x - hi.astype(F32)
        mid = r1.astype(BF16)
        lo = (r1 - mid.astype(F32)).astype(BF16)
        if tri_left:
            return _dg(tri, hi, _NN) + (_dg(tri, mid, _NN) + _dg(tri, lo, _NN))
        return _dg(hi, tri, _NN) + (_dg(mid, tri, _NN) + _dg(lo, tri, _NN))
    gate_i = lambda d, h: 2 * M_HEADS * d + h
    gate_f = lambda d, h: 2 * M_HEADS * d + M_HEADS + h

    def chunk(c, carry):
        ch = []
        for d in (0, 1):
            cc = c if d == 0 else nc - 1 - c
            rows = pl.ds(pl.multiple_of(cc * CHUNK, CHUNK), CHUNK)
            causal = (si <= ti) if d == 0 else (si >= ti)
            tri = jnp.where(causal, 1.0, 0.0).astype(BF16)
            tri_t = jnp.where((ti <= si) if d == 0 else (ti >= si), 1.0, 0.0).astype(BF16)
            q_all = q_ref[rows, :] * (M_HD ** -0.5)
            k_all = k_ref[rows, :]
            v_all = v_ref[rows, :]
            gcol = gc_ref[rows, :] + br_ref[...]
            grow = gt_ref[cc] + bc_ref[...]
            bcum_c_all = cumsum3(tri, _log_sigmoid(gcol), True)
            bcum_r_all = cumsum3(tri_t, _log_sigmoid(grow), False)
            for h in range(M_HEADS):
                sl = slice(h * M_HD, (h + 1) * M_HD)
                q, k, v = q_all[:, sl], k_all[:, sl], v_all[:, sl]
                ig_c = gcol[:, gate_i(d, h):gate_i(d, h) + 1]
                ig_r = grow[gate_i(d, h):gate_i(d, h) + 1, :]
                bcum_c = bcum_c_all[:, gate_f(d, h):gate_f(d, h) + 1]
                bcum_r = bcum_r_all[gate_f(d, h):gate_f(d, h) + 1, :]
                m_st = m_scr[d, h][:, 0:1]
                dmat = jnp.where(causal, bcum_c - bcum_r + ig_r, -jnp.inf)
                inter = bcum_c + m_st
                m_t = jnp.maximum(inter, jnp.max(dmat, axis=1, keepdims=True))
                b_last = bcum_c[CHUNK - 1:CHUNK, :] if d == 0 else bcum_c[0:1, :]
                g_c = b_last - bcum_c + ig_c
                m_new = jnp.maximum(b_last + m_st, jnp.max(g_c, axis=0, keepdims=True))
                ch.append(dict(d=d, h=h, rows=rows, sl=sl, qb=q.astype(BF16), kb=k.astype(BF16),
                               vx=jnp.concatenate([v.astype(BF16), ones_col], axis=1),
                               c_st=c_scr[d, h], m_t=m_t, m_new=m_new,
                               w_intra=jnp.exp(dmat - m_t), w_inter=jnp.exp(inter - m_t),
                               decay=jnp.exp(b_last + m_st - m_new), kw=(k * jnp.exp(g_c - m_new)).astype(BF16)))
        for x in ch:
            x['s_qk'] = _dg(x['qb'], x['kb'], _NT) * x['w_intra']
        for x in ch:
            x['qc'] = _dg(x['qb'], x['c_st'].astype(BF16), _NN)
        for x in ch:
            x['kw_t'] = _dg(eye_d, x['kw'], _NT).astype(BF16)
        for x in ch:
            x['sv'] = _dg(x['s_qk'].astype(BF16), x['vx'], _NN)
        for x in ch:
            x['upd'] = _dg(x['kw_t'], x['vx'], _NN)
        for x in ch:
            d, h = x['d'], x['h']
            tot = x['w_inter'] * x['qc'] + x['sv']
            den = tot[:, M_HD:M_HD + 1]
            h_scr[d, x['rows'], x['sl']] = tot[:, :M_HD] / jnp.maximum(jnp.abs(den), jnp.exp(-x['m_t']))
            c_scr[d, h] = x['decay'] * x['c_st'] + x['upd']
            m_scr[d, h] = jnp.broadcast_to(x['m_new'], (1, M_HD))
        return carry

    lax.fori_loop(0, nc, chunk, 0)
    hsum = h_scr[0] + h_scr[1]
    hn = jnp.concatenate(
        [hsum[:, h * M_HD:(h + 1) * M_HD]
         * lax.rsqrt(jnp.mean(hsum[:, h * M_HD:(h + 1) * M_HD] ** 2, axis=-1, keepdims=True) + NORM_EPS)
         for h in range(M_HEADS)], axis=1)
    out_ref[...] = hn * ng_ref[...] * _sigmoid(o_ref[...])
    co_ref[...] = c_scr[...]
    mo_ref[...] = m_scr[...]


def _mlstm(u, gcol, gt, bias, norm_g, states, row0, batch, seq):
    zero_init = states is None
    width = M_HEADS * M_HD
    qc = COL_M // width
    ngate = 4 * M_HEADS
    blk = lambda off: pl.BlockSpec((seq, width), lambda b: (row0 + b, off))
    st_c = pl.BlockSpec((None, 2, M_HEADS, M_HD, 2 * M_HD), lambda b: (b, 0, 0, 0, 0))
    st_v = pl.BlockSpec((None, 2, M_HEADS, 1, M_HD), lambda b: (b, 0, 0, 0, 0))
    in_specs = [blk(qc), blk(qc + 1), blk(qc + 2), blk(qc + 3),
                pl.BlockSpec((seq, ngate), lambda b: (row0 + b, 0)),
                pl.BlockSpec((seq // CHUNK, ngate, CHUNK), lambda b: (row0 + b, 0, 0)),
                pl.BlockSpec((1, ngate), lambda b: (0, 0)),
                pl.BlockSpec((ngate, 1), lambda b: (0, 0)),
                pl.BlockSpec((1, width), lambda b: (0, 0))]
    args = [u, u, u, u, gcol, gt, bias.reshape(1, ngate), bias.reshape(ngate, 1), norm_g]
    if not zero_init:
        in_specs += [st_c, st_v]
        args += list(states)
    return pl.pallas_call(
        functools.partial(_mlstm_kernel, zero_init=zero_init, seq=seq),
        grid=(batch,),
        in_specs=in_specs,
        out_specs=[pl.BlockSpec((seq, width), lambda b: (b, 0)), st_c, st_v],
        out_shape=[jax.ShapeDtypeStruct((batch * seq, width), F32),
                   jax.ShapeDtypeStruct((batch, 2, M_HEADS, M_HD, 2 * M_HD), F32),
                   jax.ShapeDtypeStruct((batch, 2, M_HEADS, 1, M_HD), F32)],
        scratch_shapes=[pltpu.VMEM((2, seq, width), F32), pltpu.VMEM((2, M_HEADS, M_HD, 2 * M_HD), F32),
                        pltpu.VMEM((2, M_HEADS, 1, M_HD), F32)],
        compiler_params=_cparams(("arbitrary",)),
        name="mlstm_ctx" if zero_init else "mlstm_lat",
    )(*args)


def _softplus(x):
    return jnp.maximum(x, 0.0) + jnp.log1p(jnp.exp(-jnp.abs(x)))


def _rwkv_kernel(*refs, zero_init, seq):
    if zero_init:
        (r_ref, k_ref, v_ref, xl_ref, xg_ref, mur_ref, muk_ref, muv_ref, mul_ref, mug_ref,
         w0_ref, wup_ref, a0_ref, aup_ref, gup_ref, kk_ref, ka_ref, rk_ref, lng_ref, lnb_ref,
         out_ref, so_ref, r_scr, v_scr, kk_scr, g_scr, bonus_scr, lw_scr, kd_scr, b_scr, y_scr, s_scr) = refs
    else:
        (r_ref, k_ref, v_ref, xl_ref, xg_ref, mur_ref, muk_ref, muv_ref, mul_ref, mug_ref,
         w0_ref, wup_ref, a0_ref, aup_ref, gup_ref, kk_ref, ka_ref, rk_ref, lng_ref, lnb_ref, s0_ref,
         out_ref, so_ref, r_scr, v_scr, kk_scr, g_scr, bonus_scr, lw_scr, kd_scr, b_scr, y_scr, s_scr) = refs
    nc = seq // CHUNK
    hd = R_HD
    nh = r_ref.shape[1] // hd

    def tshift(x_ref, mu_ref):
        x = x_ref[...]
        row = lax.broadcasted_iota(I32, x.shape, 0)
        prev = jnp.where(row == 0, 0.0, pltpu.roll(x, 1, 0))
        nxt = jnp.where(row == seq - 1, 0.0, pltpu.roll(x, seq - 1, 0))
        return x + mu_ref[...] * (0.5 * (prev + nxt) - x)

    r = tshift(r_ref, mur_ref)
    k = tshift(k_ref, muk_ref)
    v = tshift(v_ref, muv_ref)
    xl = tshift(xl_ref, mul_ref)
    xg = tshift(xg_ref, mug_ref)
    g = _dot(_sigmoid(xg), gup_ref[...])
    kkp = k * kk_ref[...]
    kk = jnp.concatenate(
        [kkp[:, i * hd:(i + 1) * hd]
         * lax.rsqrt(jnp.maximum(jnp.sum(kkp[:, i * hd:(i + 1) * hd] ** 2, axis=-1, keepdims=True), 1e-24))
         for i in range(nh)], axis=1)
    tw = jnp.tanh(xl[:, 0:hd])
    xa = xl[:, hd:2 * hd]
    r_scr[...] = r
    v_scr[...] = v
    kk_scr[...] = kk
    g_scr[...] = g
    rkk = r * k * rk_ref[...]
    bonus_scr[...] = jnp.concatenate(
        [jnp.sum(rkk[:, i * hd:(i + 1) * hd], axis=-1, keepdims=True) * v[:, i * hd:(i + 1) * hd] for i in range(nh)],
        axis=1)
    for d in (0, 1):
        wd = -_softplus(-(w0_ref[d] + _dot(tw, wup_ref[d]))) - 0.5
        ad = _sigmoid(a0_ref[d] + _dot(xa, aup_ref[d]))
        lw_scr[d] = -jnp.exp(wd)
        kd_scr[d] = k * (1.0 + (ad - 1.0) * ka_ref[...])
        b_scr[d] = kk * ad
    if zero_init:
        s_scr[...] = jnp.zeros(s_scr.shape, F32)
    else:
        s_scr[...] = s0_ref[...]

    ti = lax.broadcasted_iota(I32, (CHUNK, CHUNK), 0)
    si = lax.broadcasted_iota(I32, (CHUNK, CHUNK), 1)
    eye = (ti == si).astype(F32)

    def chunk(c, carry):
        chains = []
        rows_d = []
        for d in (0, 1):
            cc = c if d == 0 else nc - 1 - c
            rows = pl.ds(pl.multiple_of(cc * CHUNK, CHUNK), CHUNK)
            rows_d.append(rows)
            incl = (si <= ti) if d == 0 else (si >= ti)
            strict = (si < ti) if d == 0 else (si > ti)
            lw = lw_scr[d, rows, :]
            lw_hi, lw_lo = _split(lw)
            tri = jnp.where(incl, 1.0, 0.0).astype(BF16)
            lc = _dg(tri, lw_hi, _NN) + _dg(tri, lw_lo, _NN)
            l_last = lc[CHUNK - 1:CHUNK, :] if d == 0 else lc[0:1, :]
            e_neg = jnp.exp(-lc)
            e_end = jnp.exp(l_last - lc)
            vc = v_scr[rows, :]
            kdc = kd_scr[d, rows, :]
            bc = b_scr[d, rows, :]
            rt = r_scr[rows, :] * jnp.exp(lc)
            kkt = kk_scr[rows, :] * jnp.exp(lc - lw)
            kh = kdc * e_neg
            bh = bc * e_neg
            kbar = kdc * e_end
            bbar = bc * e_end
            w_end = jnp.exp(l_last)
            for i in range(nh):
                sl = slice(i * hd, (i + 1) * hd)
                chains.append(dict(
                    d=d, i=i, incl=incl, strict=strict,
                    lhs=jnp.concatenate([kkt[:, sl], rt[:, sl]], axis=0),
                    rhs=jnp.concatenate([bh[:, sl], kh[:, sl]], axis=0),
                    end=jnp.concatenate([kbar[:, sl], bbar[:, sl]], axis=0),
                    v=vc[:, sl], w_end=w_end[:, sl], s0=s_scr[d, i]))
        for ch in chains:
            ch['ab'] = _dot1(ch['lhs'], ch['rhs'], _NT)
        for ch in chains:
            ch['proj'] = _dot1(ch['lhs'], ch['s0'], _NT)
        for ch in chains:
            ab = ch['ab']
            ch['a_kb'] = jnp.where(ch['strict'], ab[:CHUNK, :CHUNK], 0.0)
            ch['b_rb'] = jnp.where(ch['incl'], ab[CHUNK:, :CHUNK], 0.0)
            ch['akk_brk'] = jnp.concatenate([jnp.where(ch['strict'], ab[:CHUNK, CHUNK:], 0.0),
                                             jnp.where(ch['incl'], ab[CHUNK:, CHUNK:], 0.0)], axis=0)
        for ch in chains:
            ch['p'] = _dot1(ch['a_kb'], ch['a_kb'])
        for ch in chains:
            ch['abv'] = _dot1(ch['akk_brk'], ch['v'])
        for ch in chains:
            inv = eye - ch['a_kb']
            ch['inv'] = inv + _dot1(inv, ch['p'])
        span = 4
        while span < CHUNK:
            for ch in chains:
                ch['p'] = _dot1(ch['p'], ch['p'])
            for ch in chains:
                ch['inv'] = ch['inv'] + _dot1(ch['inv'], ch['p'])
            span *= 2
        for ch in chains:
            ch['u'] = _dot1(ch['inv'], ch['proj'][:CHUNK] + ch['abv'][:CHUNK])
        for ch in chains:
            ch['y'] = ch['proj'][CHUNK:] + ch['abv'][CHUNK:] - _dot1(ch['b_rb'], ch['u'])
        for ch in chains:
            upd = _dot3(jnp.concatenate([ch['v'], -ch['u']], axis=0), ch['end'], _TN)
            s_scr[ch['d'], ch['i']] = ch['s0'] * ch['w_end'] + upd
        for d in (0, 1):
            y_scr[d, rows_d[d], :] = jnp.concatenate([ch['y'] for ch in chains[nh * d:nh * (d + 1)]], axis=1)
        return carry

    lax.fori_loop(0, nc, chunk, 0)

    y = y_scr[0] + y_scr[1]
    outs = []
    for i in range(nh):
        yh = y[:, i * hd:(i + 1) * hd]
        mean = jnp.mean(yh, axis=-1, keepdims=True)
        var = jnp.mean(jnp.square(yh - mean), axis=-1, keepdims=True)
        outs.append((yh - mean) * lax.rsqrt(var + GN_EPS))
    yn = jnp.concatenate(outs, axis=1)
    out_ref[...] = (yn * lng_ref[...] + lnb_ref[...] + bonus_scr[...]) * g_scr[...]
    so_ref[...] = s_scr[...]


def _rwkv(u, p, state, row0, batch, seq, heads_per_step):
    zero_init = state is None
    nh = heads_per_step
    wd = nh * R_HD
    nsteps = R_HEADS // nh
    width = R_HEADS * R_HD
    rc = COL_R // wd
    sec = width // wd
    lc = (COL_R + 3 * width) // LANES
    ublk = lambda off: pl.BlockSpec((seq, wd), lambda b, h: (row0 + b, rc + off + h))
    ufix = lambda blk: pl.BlockSpec((seq, LANES), lambda b, h: (row0 + b, blk))
    mblk = lambda off: pl.BlockSpec((1, wd), lambda b, h: (0, off + h))
    mfix = lambda blk: pl.BlockSpec((1, LANES), lambda b, h: (0, blk))
    vec = pl.BlockSpec((1, wd), lambda b, h: (0, h))
    in_specs = [ublk(0), ublk(sec), ublk(2 * sec), ufix(lc), ufix(lc + 1),
                mblk(0), mblk(sec), mblk(2 * sec), mfix(lc - COL_R // LANES), mfix(lc - COL_R // LANES + 1),
                pl.BlockSpec((2, 1, wd), lambda b, h: (0, 0, h)),
                pl.BlockSpec((2, R_HD, wd), lambda b, h: (0, 0, h)),
                pl.BlockSpec((2, 1, wd), lambda b, h: (0, 0, h)),
                pl.BlockSpec((2, R_HD, wd), lambda b, h: (0, 0, h)),
                pl.BlockSpec((LANES, wd), lambda b, h: (0, h)),
                vec, vec, vec, vec, vec]
    args = [u, u, u, u, u, p['mu'], p['mu'], p['mu'], p['mu'], p['mu'],
            p['w0'], p['w_up'], p['a0'], p['a_up'], p['g_up'], p['k_k'], p['k_a'], p['r_k'], p['ln_g'], p['ln_b']]
    if not zero_init:
        in_specs.append(pl.BlockSpec((None, 2, nh, R_HD, R_HD), lambda b, h: (b, 0, h, 0, 0)))
        args.append(state)
    big = lambda n: pltpu.VMEM((n, seq, wd), F32)
    return pl.pallas_call(
        functools.partial(_rwkv_kernel, zero_init=zero_init, seq=seq),
        grid=(batch, nsteps),
        in_specs=in_specs,
        out_specs=[pl.BlockSpec((seq, wd), lambda b, h: (b, h)),
                   pl.BlockSpec((None, 2, nh, R_HD, R_HD), lambda b, h: (b, 0, h, 0, 0))],
        out_shape=[jax.ShapeDtypeStruct((batch * seq, width), F32),
                   jax.ShapeDtypeStruct((batch, 2, R_HEADS, R_HD, R_HD), F32)],
        scratch_shapes=[pltpu.VMEM((seq, wd), F32)] * 5
                       + [big(2), big(2), big(2), big(2), pltpu.VMEM((2, nh, R_HD, R_HD), F32)],
        compiler_params=_cparams(("arbitrary", "arbitrary")),
        name="rwkv_ctx" if zero_init else "rwkv_lat",
    )(*args)


def _top2_sum(a, b, c, d):
    m1, n1 = jnp.maximum(a, b), jnp.minimum(a, b)
    m2, n2 = jnp.maximum(c, d), jnp.minimum(c, d)
    return jnp.maximum(m1, m2) + jnp.maximum(jnp.minimum(m1, m2), jnp.maximum(n1, n2))


def _first_argmax(vals):
    best = functools.reduce(jnp.maximum, vals)
    idx = jnp.full(best.shape, len(vals) - 1, I32)
    for j in range(len(vals) - 2, -1, -1):
        idx = jnp.where(vals[j] == best, j, idx)
    return best, idx


def _out_kernel(attc_ref, attl_ref, mc_ref, ml_ref, rc_ref, rl_ref, xc_ref, xl_ref, w_ref, g1_ref, sh2_ref, sc2_ref,
                n2_ref, rw_ref, rb_ref, x1_ref, h2_ref, route_ref, cnt_ref, cnt_scr, *, n_ctx_tiles):
    @pl.when(pl.program_id(0) == 0)
    def _():
        cnt_scr[...] = jnp.zeros(cnt_scr.shape, F32)

    is_ctx = pl.program_id(0) < n_ctx_tiles
    both = lambda c_ref, l_ref: jnp.where(is_ctx, c_ref[...], l_ref[...])
    na = attc_ref.shape[1]
    nm = mc_ref.shape[1]
    mix = (jnp.dot(both(attc_ref, attl_ref).astype(BF16), w_ref[0:na, :], preferred_element_type=F32)
           + jnp.dot(both(mc_ref, ml_ref).astype(BF16), w_ref[na:na + nm, :], preferred_element_type=F32)
           + jnp.dot(both(rc_ref, rl_ref).astype(BF16), w_ref[na + nm:, :], preferred_element_type=F32))
    x1 = both(xc_ref, xl_ref) + g1_ref[...] * mix
    x1_ref[...] = x1
    h2 = _rms(x1, n2_ref[...]) * (1.0 + sc2_ref[...]) + sh2_ref[...]
    h2_ref[...] = h2
    logits = _dot1(rw_ref[...], h2, _NT)
    s = _sigmoid(logits)
    ssel = s + rb_ref[...]
    srow = [s[e:e + 1, :] for e in range(N_EXPERTS)]
    brow = [ssel[e:e + 1, :] for e in range(N_EXPERTS)]
    gscore = [_top2_sum(*brow[EXPERTS_PER_GROUP * g:EXPERTS_PER_GROUP * (g + 1)]) for g in range(N_EXPERT_GROUPS)]
    _, gidx = _first_argmax(gscore)
    pick = lambda rows, j: functools.reduce(
        lambda acc, g: jnp.where(gidx == g, rows[EXPERTS_PER_GROUP * g + j], acc),
        range(N_EXPERT_GROUPS - 2, -1, -1), rows[EXPERTS_PER_GROUP * (N_EXPERT_GROUPS - 1) + j])
    ing = [pick(brow, j) for j in range(EXPERTS_PER_GROUP)]
    sin_ = [pick(srow, j) for j in range(EXPERTS_PER_GROUP)]
    _, l1 = _first_argmax(ing)
    _, l2 = _first_argmax([jnp.where(l1 == j, -jnp.inf, ing[j]) for j in range(EXPERTS_PER_GROUP)])
    sel = lambda l: functools.reduce(lambda acc, j: jnp.where(l == j, sin_[j], acc),
                                     range(EXPERTS_PER_GROUP - 2, -1, -1), sin_[EXPERTS_PER_GROUP - 1])
    w1, w2 = sel(l1), sel(l2)
    tot = w1 + w2
    e1 = gidx * EXPERTS_PER_GROUP + l1
    e2 = gidx * EXPERTS_PER_GROUP + l2
    tm = e1.shape[1]
    eid = lax.broadcasted_iota(I32, (N_EXPERTS, tm), 0)
    oh1 = eid == e1
    oh2 = eid == e2
    picked = jnp.where(jnp.logical_or(oh1, oh2), 1.0, 0.0)
    earlier = jnp.where(lax.broadcasted_iota(I32, (tm, tm), 0) < lax.broadcasted_iota(I32, (tm, tm), 1), 1.0, 0.0)
    rank = cnt_scr[:, 0:1] + jnp.dot(picked.astype(BF16), earlier.astype(BF16), preferred_element_type=F32)
    pos1 = jnp.sum(jnp.where(oh1, rank, 0.0), axis=0, keepdims=True)
    pos2 = jnp.sum(jnp.where(oh2, rank, 0.0), axis=0, keepdims=True)
    cnt = cnt_scr[...] + jnp.sum(picked, axis=1, keepdims=True)
    cnt_scr[...] = cnt
    cnt_ref[...] = cnt
    zero = jnp.zeros_like(w1)
    route_ref[...] = jnp.concatenate([e1.astype(F32), e2.astype(F32), w1 / tot, w2 / tot, pos1, pos2, zero, zero],
                                     axis=0)


def _out_proj(att, m_out, r_out, x, w_out, mod3, n2g, rw_t, rb, s_lat):
    t_ctx, d = x[0].shape
    ntok = t_ctx + x[1].shape[0]
    tm = 256
    nct = t_ctx // tm
    row = lambda i: _mod_row(i * tm, t_ctx, s_lat)
    modblk = lambda j: pl.BlockSpec((None, 1, d), lambda i: (row(i), 0, j))
    pair = lambda a: [pl.BlockSpec((tm, a[0].shape[1]), lambda i: (jnp.minimum(i, nct - 1), 0)),
                      pl.BlockSpec((tm, a[1].shape[1]), lambda i: (jnp.maximum(i - nct, 0), 0))]
    return pl.pallas_call(
        functools.partial(_out_kernel, n_ctx_tiles=nct),
        grid=(ntok // tm,),
        in_specs=pair(att) + pair(m_out) + pair(r_out) + pair(x) + [
                  pl.BlockSpec(w_out.shape, lambda i: (0, 0)),
                  modblk(2), modblk(3), modblk(4),
                  pl.BlockSpec((1, d), lambda i: (0, 0)),
                  pl.BlockSpec(rw_t.shape, lambda i: (0, 0)),
                  pl.BlockSpec(rb.shape, lambda i: (0, 0))],
        out_specs=[pl.BlockSpec((tm, d), lambda i: (i, 0)),
                   pl.BlockSpec((tm, d), lambda i: (i, 0)),
                   pl.BlockSpec((8, tm), lambda i: (0, i)),
                   pl.BlockSpec((N_EXPERTS, LANES), lambda i: (0, 0))],
        out_shape=[jax.ShapeDtypeStruct((ntok, d), F32), jax.ShapeDtypeStruct((ntok, d), F32),
                   jax.ShapeDtypeStruct((8, ntok), F32), jax.ShapeDtypeStruct((N_EXPERTS, LANES), F32)],
        scratch_shapes=[pltpu.VMEM((N_EXPERTS, LANES), F32)],
        compiler_params=_cparams(("arbitrary",)),
        name="out_proj",
    )(*att, *m_out, *r_out, *x, w_out, mod3, mod3, mod3, n2g, rw_t, rb)


def _row_gather(src_hbm, idx_ref, base, dst, sem, n, unrolled):
    def start(j):
        pltpu.make_async_copy(src_hbm.at[pl.ds(idx_ref[base + j], 1), :], dst.at[pl.ds(j, 1), :], sem).start()

    if unrolled:
        for j in range(n):
            start(j)
    else:
        def body(j, c):
            start(j)
            return c
        lax.fori_loop(0, n, body, 0)


def _expert_kernel(be_ref, tok_ref, nused_ref, h_hbm, w1_ref, w3_ref, w2_ref, y_ref, xbuf, w1b, w3b, w2b, sem):
    i = pl.program_id(0)
    n_used = nused_ref[0]
    wait = lambda s: pltpu.make_async_copy(h_hbm.at[pl.ds(0, EXPERT_ROWS), :], xbuf.at[s], sem.at[s]).wait()

    @pl.when(i < n_used)
    def _():
        slot = i % 2

        @pl.when(i == 0)
        def _():
            _row_gather(h_hbm, tok_ref, 0, xbuf.at[0], sem.at[0], EXPERT_ROWS, unrolled=False)

        @pl.when(jnp.logical_or(i == 0, be_ref[i] != be_ref[jnp.maximum(i - 1, 0)]))
        def _():
            w1b[...] = w1_ref[...].astype(BF16)
            w3b[...] = w3_ref[...].astype(BF16)
            w2b[...] = w2_ref[...].astype(BF16)

        wait(slot)
        _row_gather(h_hbm, tok_ref, (i + 1) * EXPERT_ROWS, xbuf.at[1 - slot], sem.at[1 - slot], EXPERT_ROWS,
                    unrolled=True)
        xb = xbuf[slot].astype(BF16)
        a = jnp.dot(xb, w1b[...], preferred_element_type=F32)
        b = jnp.dot(xb, w3b[...], preferred_element_type=F32)
        hmid = (a * _sigmoid(a)) * b
        y_ref[...] = jnp.dot(hmid.astype(BF16), w2b[...], preferred_element_type=F32)

        @pl.when(i == n_used - 1)
        def _():
            wait(1 - slot)

    @pl.when(i >= n_used)
    def _():
        y_ref[...] = jnp.zeros(y_ref.shape, F32)


def _experts(h2, block_e, row_tok, n_used, w1, w3, w2, layer):
    ntok, d = h2.shape
    de = w1.shape[3]
    n_rows = row_tok.shape[0] - EXPERT_ROWS
    nb = n_rows // EXPERT_ROWS
    return pl.pallas_call(
        _expert_kernel,
        grid_spec=pltpu.PrefetchScalarGridSpec(
            num_scalar_prefetch=3,
            grid=(nb,),
            in_specs=[pl.BlockSpec(memory_space=pl.ANY),
                      pl.BlockSpec((None, None, d, de), lambda i, be, tok, nu: (layer, be[i], 0, 0)),
                      pl.BlockSpec((None, None, d, de), lambda i, be, tok, nu: (layer, be[i], 0, 0)),
                      pl.BlockSpec((None, None, de, d), lambda i, be, tok, nu: (layer, be[i], 0, 0))],
            out_specs=pl.BlockSpec((EXPERT_ROWS, d), lambda i, be, tok, nu: (i, 0)),
            scratch_shapes=[pltpu.VMEM((2, EXPERT_ROWS, d), F32), pltpu.VMEM((d, de), BF16),
                            pltpu.VMEM((d, de), BF16), pltpu.VMEM((de, d), BF16), pltpu.SemaphoreType.DMA((2,))]),
        out_shape=jax.ShapeDtypeStruct((n_rows, d), F32),
        compiler_params=_cparams(("arbitrary",)),
        name="experts",
    )(block_e, row_tok, n_used, h2, w1, w3, w2)


def _combine_kernel(d1_ref, d2_ref, y_hbm, x1_ref, g2_ref, gate_ref, oc_ref, ol_ref, ybuf, sem, *, n_ctx_tiles):
    tm = x1_ref.shape[0]
    i = pl.program_id(0)
    slot = i % 2

    def gather(tile, s, unrolled):
        _row_gather(y_hbm, d1_ref, tile * tm, ybuf.at[s, 0], sem.at[s], tm, unrolled)
        _row_gather(y_hbm, d2_ref, tile * tm, ybuf.at[s, 1], sem.at[s], tm, unrolled)

    def wait(s):
        pltpu.make_async_copy(y_hbm.at[pl.ds(0, tm), :], ybuf.at[s, 0], sem.at[s]).wait()
        pltpu.make_async_copy(y_hbm.at[pl.ds(0, tm), :], ybuf.at[s, 1], sem.at[s]).wait()

    @pl.when(i == 0)
    def _():
        gather(0, 0, False)

    wait(slot)
    gather(i + 1, 1 - slot, True)
    gate = gate_ref[...]
    moe = ybuf[slot, 0] * gate[:, 0:1] + ybuf[slot, 1] * gate[:, 1:2]
    new_x = x1_ref[...] + g2_ref[...] * moe

    @pl.when(i < n_ctx_tiles)
    def _():
        oc_ref[...] = new_x

    @pl.when(i >= n_ctx_tiles)
    def _():
        ol_ref[...] = new_x

    @pl.when(i == pl.num_programs(0) - 1)
    def _():
        wait(1 - slot)


def _combine(y, x1, mod3, gate, dest1, dest2, t_ctx, s_lat):
    ntok, d = x1.shape
    tm = 256
    nct = t_ctx // tm
    row = lambda i: _mod_row(i * tm, t_ctx, s_lat)
    return pl.pallas_call(
        functools.partial(_combine_kernel, n_ctx_tiles=nct),
        grid_spec=pltpu.PrefetchScalarGridSpec(
            num_scalar_prefetch=2,
            grid=(ntok // tm,),
            in_specs=[pl.BlockSpec(memory_space=pl.ANY),
                      pl.BlockSpec((tm, d), lambda i, a, b: (i, 0)),
                      pl.BlockSpec((None, 1, d), lambda i, a, b: (row(i), 0, 5)),
                      pl.BlockSpec((tm, 2), lambda i, a, b: (i, 0))],
            out_specs=[pl.BlockSpec((tm, d), lambda i, a, b: (jnp.minimum(i, nct - 1), 0)),
                       pl.BlockSpec((tm, d), lambda i, a, b: (jnp.maximum(i - nct, 0), 0))],
            scratch_shapes=[pltpu.VMEM((2, 2, tm, d), F32), pltpu.SemaphoreType.DMA((2,))]),
        out_shape=[jax.ShapeDtypeStruct((t_ctx, d), F32), jax.ShapeDtypeStruct((ntok - t_ctx, d), F32)],
        compiler_params=_cparams(("arbitrary",)),
        name="combine",
    )(dest1, dest2, y, x1, mod3, gate)


def _dispatch(route, counts, tile):
    ntok = route.shape[1]
    e = route[0:2].astype(I32)
    pos = route[4:6].astype(I32)
    gate = route[2:4].T
    counts = counts[:, 0].astype(I32)
    padded = (counts + EXPERT_ROWS - 1) // EXPERT_ROWS * EXPERT_ROWS
    pad_end = jnp.cumsum(padded)
    pad_start = pad_end - padded
    onehot = (e[:, :, None] == jnp.arange(N_EXPERTS, dtype=I32)).astype(I32)
    dest = jnp.sum(onehot * pad_start, axis=-1) + pos
    n_rows = -(-(2 * ntok) // EXPERT_ROWS) * EXPERT_ROWS + N_EXPERTS * EXPERT_ROWS
    nb = n_rows // EXPERT_ROWS
    tok = jnp.broadcast_to(jnp.arange(ntok, dtype=I32)[None, :], (2, ntok))
    row_tok = jnp.zeros((n_rows + EXPERT_ROWS,), I32).at[dest.reshape(-1)].set(tok.reshape(-1))
    blk_start = jnp.arange(nb, dtype=I32) * EXPERT_ROWS
    block_e = jnp.minimum(jnp.sum((pad_end[None, :] <= blk_start[:, None]).astype(I32), axis=1), N_EXPERTS - 1)
    n_used = pad_end[-1:] // EXPERT_ROWS
    dest = jnp.pad(dest, ((0, 0), (0, tile)))
    return row_tok, block_e, n_used, gate, dest[0], dest[1]


def kernel(x_prompt, x_sample, c, cache_attn_k, cache_attn_v, state_mlstm_C, state_mlstm_n, state_mlstm_m, state_rwkv, c_ctx, norm1_g, norm2_g, w_mod, b_mod, w_in, w_out, attn_q_norm, attn_k_norm, mlstm_i_bias, mlstm_f_bias, mlstm_norm_g, rwkv_mu, rwkv_w0, rwkv_w_up, rwkv_a0, rwkv_a_up, rwkv_g_up, rwkv_k_k, rwkv_k_a, rwkv_r_k, rwkv_ln_g, rwkv_ln_b, router_w, router_b, exp_w1, exp_w3, exp_w2):
    b_ctx, s_ctx, d = x_prompt.shape
    b_lat, s_lat, _ = x_sample.shape
    depth = w_in.shape[0]
    t_ctx = b_ctx * s_ctx
    ntok = t_ctx + b_lat * s_lat
    assert b_lat + 1 <= 8 and s_lat % 1024 == 0 and t_ctx % 1024 == 0 and t_ctx % s_lat == 0
    past = cache_attn_k.shape[2]

    x = (x_prompt.reshape(t_ctx, d), x_sample.reshape(b_lat * s_lat, d))
    c_all = jnp.zeros((8, d), F32).at[0].set(c_ctx).at[1:1 + b_lat].set(c)
    mod = _modulation(c_all, w_mod, b_mod)
    cos, sin = _rope_tables(s_lat)
    rw_t = router_w.T
    rb = router_b.reshape(N_EXPERTS, 1)
    r_width = R_HEADS * R_HD
    n_in = w_in.shape[2]
    gate_hi = COL_R + 4 * M_HEADS
    w_tail = jnp.concatenate([w_in[:, :, gate_hi:], w_in[:, :, COL_R:gate_hi],
                              jnp.zeros((depth, d, N_IN_PAD - n_in), F32)], axis=2).astype(BF16)

    ks, vs, cs, ns, ms, rs = [], [], [], [], [], []
    for l in range(depth):
        mod3 = mod[l].reshape(8, 1, 6 * d)
        u = _in_proj(x[0], x[1], norm1_g[l][None], mod3, w_in, l, w_tail[l], s_lat)

        ck = cache_attn_k[:, l].reshape(b_lat, past, ATT_KV_HEADS * ATT_HD)
        cv = cache_attn_v[:, l].reshape(b_lat, past, ATT_KV_HEADS * ATT_HD)
        att_c, att_l, k_ctx, v_ctx = _attention(u, attn_q_norm[l][None], attn_k_norm[l][None], ck, cv, cos, sin,
                                                b_ctx, s_ctx, b_lat, s_lat)
        ks.append(k_ctx.reshape(b_ctx, s_ctx, ATT_KV_HEADS, ATT_HD))
        vs.append(v_ctx.reshape(b_ctx, s_ctx, ATT_KV_HEADS, ATT_HD))

        gcol = u[:, COL_GATE:COL_GATE + 4 * M_HEADS]
        gt = gcol.reshape(ntok // CHUNK, CHUNK, 4 * M_HEADS).transpose(0, 2, 1)
        bias = jnp.stack([mlstm_i_bias[l], mlstm_f_bias[l]], axis=1)
        ng = mlstm_norm_g[l][None]
        m_c_out, cx_c, m_c = _mlstm(u, gcol, gt, bias, ng, None, 0, b_ctx, s_ctx)
        n_col = jnp.pad(state_mlstm_n[:, l][..., None], ((0, 0),) * 4 + ((0, M_HD - 1),))
        lat_states = (jnp.concatenate([state_mlstm_C[:, l], n_col], axis=-1),
                      jnp.broadcast_to(state_mlstm_m[:, l][..., None, None], (b_lat, 2, M_HEADS, 1, M_HD)))
        m_l_out, _, _ = _mlstm(u, gcol, gt, bias, ng, lat_states, t_ctx // s_lat, b_lat, s_lat)
        cs.append(cx_c[..., :M_HD])
        ns.append(cx_c[..., M_HD])
        ms.append(m_c[:, :, :, 0, 0])

        rp = dict(mu=rwkv_mu[l][None], w0=rwkv_w0[l].reshape(2, 1, r_width), w_up=rwkv_w_up[l],
                  a0=rwkv_a0[l].reshape(2, 1, r_width), a_up=rwkv_a_up[l], g_up=rwkv_g_up[l],
                  k_k=rwkv_k_k[l][None], k_a=rwkv_k_a[l][None], r_k=rwkv_r_k[l].reshape(1, r_width),
                  ln_g=rwkv_ln_g[l][None], ln_b=rwkv_ln_b[l][None])
        r_c_out, r_c = _rwkv(u, rp, None, 0, b_ctx, s_ctx, RWKV_HEADS_PER_STEP_CTX)
        r_l_out, _ = _rwkv(u, rp, state_rwkv[:, l], t_ctx // s_lat, b_lat, s_lat, RWKV_HEADS_PER_STEP_LAT)
        rs.append(r_c)

        x1, h2, route, counts = _out_proj((att_c, att_l), (m_c_out, m_l_out), (r_c_out, r_l_out), x,
                                          w_out[l].astype(BF16), mod3, norm2_g[l][None], rw_t, rb, s_lat)
        row_tok, block_e, n_used, gate, dest1, dest2 = _dispatch(route, counts, 256)
        y = _experts(h2, block_e, row_tok, n_used, exp_w1, exp_w3, exp_w2, l)
        x = _combine(y, x1, mod3, gate, dest1, dest2, t_ctx, s_lat)

    y_prompt = x[0].reshape(b_ctx, s_ctx, d)
    y_sample = x[1].reshape(b_lat, s_lat, d)
    return (y_prompt, y_sample, jnp.stack(ks, axis=1), jnp.stack(vs, axis=1), jnp.stack(cs, axis=1),
            jnp.stack(ns, axis=1), jnp.stack(ms, axis=1), jnp.stack(rs, axis=1))
```

```python
import functools
import math

import jax
import jax.numpy as jnp
from jax import lax
from jax.experimental import pallas as pl
from jax.experimental.pallas import tpu as pltpu

F32 = jnp.float32
BF16 = jnp.bfloat16
I32 = jnp.int32

NORM_EPS = 1e-6
GN_EPS = 64e-5
M_INIT = -1e30
GRID_W = 64
ROPE_THETA = 10000.0
ATT_HD = 128
ATT_GROUPS = 4
ATT_KV_HEADS = 2
M_HD = 128
M_HEADS = 4
R_HD = 64
R_HEADS = 8
N_EXPERTS = 16
N_EXPERT_GROUPS = 4
EXPERTS_PER_GROUP = 4
CHUNK = 64
LANES = 128
EXPERT_ROWS = 256
VMEM_LIMIT = 58 * 1024 * 1024
RWKV_HEADS_PER_STEP_CTX = 8
RWKV_HEADS_PER_STEP_LAT = 8

COL_ATT = 0
COL_M = 1536
COL_R = 3584
COL_GATE = 5376
N_IN_PAD = 5632


def _cparams(sem):
    return pltpu.CompilerParams(dimension_semantics=sem, vmem_limit_bytes=VMEM_LIMIT)


def _dot(a, b):
    return jnp.dot(a.astype(BF16), b.astype(BF16), preferred_element_type=F32)


def _dg(a, b, dims):
    return lax.dot_general(a, b, (dims, ((), ())), preferred_element_type=F32)


_NN = ((1,), (0,))
_NT = ((1,), (1,))
_TN = ((0,), (0,))


def _split(a):
    hi = a.astype(BF16)
    lo = (a - hi.astype(F32)).astype(BF16)
    return hi, lo


def _dot3(a, b, dims=_NN):
    ah, al = _split(a)
    bh, bl = _split(b)
    return _dg(ah, bh, dims) + (_dg(ah, bl, dims) + _dg(al, bh, dims))


def _dot1(a, b, dims=_NN):
    return _dg(a.astype(BF16), b.astype(BF16), dims)


def _rms(x, g):
    return x * lax.rsqrt(jnp.mean(x * x, axis=-1, keepdims=True) + NORM_EPS) * g


def _sigmoid(x):
    return 1.0 / (1.0 + jnp.exp(-x))


def _mod_kernel(c_ref, w_ref, b_ref, o_ref):
    c = c_ref[...]
    o_ref[...] = _dot(c * _sigmoid(c), w_ref[...]) + b_ref[...]


def _modulation(c_all, w_mod, b_mod):
    depth, d, n = w_mod.shape
    tn = 1024
    return pl.pallas_call(
        _mod_kernel,
        grid=(depth, n // tn),
        in_specs=[pl.BlockSpec((8, d), lambda l, j: (0, 0)),
                  pl.BlockSpec((None, d, tn), lambda l, j: (l, 0, j)),
                  pl.BlockSpec((None, 1, tn), lambda l, j: (l, 0, j))],
        out_specs=pl.BlockSpec((None, 8, tn), lambda l, j: (l, 0, j)),
        out_shape=jax.ShapeDtypeStruct((depth, 8, n), F32),
        compiler_params=_cparams(("arbitrary", "arbitrary")),
        name="modulation",
    )(c_all, w_mod, b_mod.reshape(depth, 1, n))


def _mod_row(tok0, t_ctx, s_lat):
    return jnp.where(tok0 < t_ctx, 0, 1 + (tok0 - t_ctx) // s_lat)


def _in_kernel(xc_ref, xl_ref, g_ref, sh_ref, sc_ref, wa_ref, wb_ref, o_ref, h_ref, *, n_ctx_tiles, n_head_tiles):
    i = pl.program_id(0)
    j = pl.program_id(1)

    def normalise(x_ref):
        slab = 256
        for r0 in range(0, x_ref.shape[0], slab):
            h = _rms(x_ref[r0:r0 + slab, :], g_ref[...]) * (1.0 + sc_ref[...]) + sh_ref[...]
            h_ref[r0:r0 + slab, :] = h.astype(BF16)

    @pl.when(jnp.logical_and(j == 0, i < n_ctx_tiles))
    def _():
        normalise(xc_ref)

    @pl.when(jnp.logical_and(j == 0, i >= n_ctx_tiles))
    def _():
        normalise(xl_ref)

    @pl.when(j < n_head_tiles)
    def _():
        o_ref[...] = _dg(h_ref[...], wa_ref[...].astype(BF16), _NT)

    @pl.when(j >= n_head_tiles)
    def _():
        o_ref[...] = _dg(h_ref[...], wb_ref[...], _NT)


def _in_proj(xc, xl, g1, mod3, w_in_t, layer, w_tail_t, s_lat):
    t_ctx, d = xc.shape
    ntok = t_ctx + xl.shape[0]
    tm, tn = 1024, 512
    na = COL_R // tn
    n = COL_R + w_tail_t.shape[0]
    nct = t_ctx // tm
    row = lambda i: _mod_row(i * tm, t_ctx, s_lat)
    return pl.pallas_call(
        functools.partial(_in_kernel, n_ctx_tiles=nct, n_head_tiles=na),
        grid=(ntok // tm, n // tn),
        in_specs=[pl.BlockSpec((tm, d), lambda i, j: (jnp.minimum(i, nct - 1), 0)),
                  pl.BlockSpec((tm, d), lambda i, j: (jnp.maximum(i - nct, 0), 0)),
                  pl.BlockSpec((1, d), lambda i, j: (0, 0)),
                  pl.BlockSpec((None, 1, d), lambda i, j: (row(i), 0, 0)),
                  pl.BlockSpec((None, 1, d), lambda i, j: (row(i), 0, 1)),
                  pl.BlockSpec((None, tn, d), lambda i, j: (layer, jnp.minimum(j, na - 1), 0)),
                  pl.BlockSpec((tn, d), lambda i, j: (jnp.maximum(j - na, 0), 0))],
        out_specs=pl.BlockSpec((tm, tn), lambda i, j: (i, j)),
        out_shape=jax.ShapeDtypeStruct((ntok, n), F32),
        scratch_shapes=[pltpu.VMEM((tm, d), BF16)],
        compiler_params=_cparams(("arbitrary", "arbitrary")),
        name="in_proj",
    )(xc, xl, g1, mod3, mod3, w_in_t, w_tail_t)


def _softmax_av(q, kb, vb):
    s = _dg(q.astype(BF16), kb, _NT) * (ATT_HD ** -0.5)
    p = jnp.exp(s - jnp.max(s, axis=-1, keepdims=True))
    l = jnp.sum(p, axis=-1, keepdims=True)
    return jnp.dot(p.astype(BF16), vb, preferred_element_type=F32) / l


def _att_ctx_kernel(q_ref, k_ref, v_ref, qn_ref, kn_ref, o_ref, ko_ref, vo_ref):
    k = _rms(k_ref[...], kn_ref[...])
    ko_ref[...] = k
    vo_ref[...] = v_ref[...]
    kb = k.astype(BF16)
    vb = v_ref[...].astype(BF16)
    for g in range(ATT_GROUPS):
        q = _rms(q_ref[:, g * ATT_HD:(g + 1) * ATT_HD], qn_ref[...])
        o_ref[:, g * ATT_HD:(g + 1) * ATT_HD] = _softmax_av(q, kb, vb)


def _rope(x, cos, sin):
    lane = lax.broadcasted_iota(I32, x.shape, 1)
    first = (lane % (ATT_HD // 2)) < (ATT_HD // 4)
    partner = jnp.where(first, pltpu.roll(x, ATT_HD - ATT_HD // 4, 1), pltpu.roll(x, ATT_HD // 4, 1))
    return x * cos + partner * sin


def _att_lat_kernel(q_ref, k_ref, v_ref, ck_ref, cv_ref, qn_ref, kn_ref, cosq_ref, sinq_ref, cosk_ref, sink_ref,
                    o_ref, kb_ref, vb_ref, *, past):
    @pl.when(pl.program_id(2) == 0)
    def _():
        k = _rope(_rms(k_ref[...], kn_ref[...]), cosk_ref[...], sink_ref[...])
        kb_ref[0:past, :] = ck_ref[...].astype(BF16)
        kb_ref[past:, :] = k.astype(BF16)
        vb_ref[0:past, :] = cv_ref[...].astype(BF16)
        vb_ref[past:, :] = v_ref[...].astype(BF16)

    kb = kb_ref[...]
    vb = vb_ref[...]
    for g in range(ATT_GROUPS):
        q = _rope(_rms(q_ref[:, g * ATT_HD:(g + 1) * ATT_HD], qn_ref[...]), cosq_ref[...], sinq_ref[...])
        o_ref[:, g * ATT_HD:(g + 1) * ATT_HD] = _softmax_av(q, kb, vb)


def _attention(u, qn, kn, cache_k, cache_v, cos, sin, b_ctx, s_ctx, b_lat, s_lat):
    ntok = u.shape[0]
    t_ctx = b_ctx * s_ctx
    gw = ATT_GROUPS * ATT_HD
    kcol = (ATT_KV_HEADS * gw) // ATT_HD
    vcol = kcol + ATT_KV_HEADS
    kv_spec = pl.BlockSpec((s_ctx, ATT_HD), lambda b, h: (b, h))
    att_c, k_ctx, v_ctx = pl.pallas_call(
        _att_ctx_kernel,
        grid=(b_ctx, ATT_KV_HEADS),
        in_specs=[pl.BlockSpec((s_ctx, gw), lambda b, h: (b, h)),
                  pl.BlockSpec((s_ctx, ATT_HD), lambda b, h: (b, kcol + h)),
                  pl.BlockSpec((s_ctx, ATT_HD), lambda b, h: (b, vcol + h)),
                  pl.BlockSpec((1, ATT_HD), lambda b, h: (0, 0)),
                  pl.BlockSpec((1, ATT_HD), lambda b, h: (0, 0))],
        out_specs=[pl.BlockSpec((s_ctx, gw), lambda b, h: (b, h)), kv_spec, kv_spec],
        out_shape=[jax.ShapeDtypeStruct((t_ctx, ATT_KV_HEADS * gw), F32),
                   jax.ShapeDtypeStruct((t_ctx, ATT_KV_HEADS * ATT_HD), F32),
                   jax.ShapeDtypeStruct((t_ctx, ATT_KV_HEADS * ATT_HD), F32)],
        compiler_params=_cparams(("arbitrary", "arbitrary")),
        name="att_ctx",
    )(u, u, u, qn, kn)

    tq = 256
    nqb = s_lat // tq
    past = cache_k.shape[1]
    qrow0 = t_ctx // tq
    krow0 = t_ctx // s_lat
    att_l = pl.pallas_call(
        functools.partial(_att_lat_kernel, past=past),
        grid=(b_lat, ATT_KV_HEADS, nqb),
        in_specs=[pl.BlockSpec((tq, gw), lambda b, h, i: (qrow0 + b * nqb + i, h)),
                  pl.BlockSpec((s_lat, ATT_HD), lambda b, h, i: (krow0 + b, kcol + h)),
                  pl.BlockSpec((s_lat, ATT_HD), lambda b, h, i: (krow0 + b, vcol + h)),
                  pl.BlockSpec((None, past, ATT_HD), lambda b, h, i: (b, 0, h)),
                  pl.BlockSpec((None, past, ATT_HD), lambda b, h, i: (b, 0, h)),
                  pl.BlockSpec((1, ATT_HD), lambda b, h, i: (0, 0)),
                  pl.BlockSpec((1, ATT_HD), lambda b, h, i: (0, 0)),
                  pl.BlockSpec((tq, ATT_HD), lambda b, h, i: (i, 0)),
                  pl.BlockSpec((tq, ATT_HD), lambda b, h, i: (i, 0)),
                  pl.BlockSpec((s_lat, ATT_HD), lambda b, h, i: (0, 0)),
                  pl.BlockSpec((s_lat, ATT_HD), lambda b, h, i: (0, 0))],
        out_specs=pl.BlockSpec((tq, gw), lambda b, h, i: (b * nqb + i, h)),
        out_shape=jax.ShapeDtypeStruct((ntok - t_ctx, ATT_KV_HEADS * gw), F32),
        scratch_shapes=[pltpu.VMEM((past + s_lat, ATT_HD), BF16), pltpu.VMEM((past + s_lat, ATT_HD), BF16)],
        compiler_params=_cparams(("arbitrary", "arbitrary", "arbitrary")),
        name="att_lat",
    )(u, u, u, cache_k, cache_v, qn, kn, cos, sin, cos, sin)
    return att_c, att_l, k_ctx, v_ctx


def _rope_tables(n_tokens):
    pos = jnp.arange(n_tokens)
    row = (pos // GRID_W).astype(F32)
    col = (pos % GRID_W).astype(F32)
    n_freq = ATT_HD // 4
    inv_freq = ROPE_THETA ** (-jnp.arange(n_freq, dtype=F32) / n_freq)
    ang_r = row[:, None] * inv_freq[None, :]
    ang_c = col[:, None] * inv_freq[None, :]
    cos = jnp.concatenate([jnp.cos(ang_r), jnp.cos(ang_r), jnp.cos(ang_c), jnp.cos(ang_c)], axis=-1)
    sin = jnp.concatenate([-jnp.sin(ang_r), jnp.sin(ang_r), -jnp.sin(ang_c), jnp.sin(ang_c)], axis=-1)
    return cos, sin


def _log_sigmoid(x):
    return jnp.minimum(x, 0.0) - jnp.log1p(jnp.exp(-jnp.abs(x)))


def _mlstm_kernel(*refs, zero_init, seq):
    if zero_init:
        (q_ref, k_ref, v_ref, o_ref, gc_ref, gt_ref, br_ref, bc_ref, ng_ref,
         out_ref, co_ref, mo_ref, h_scr, c_scr, m_scr) = refs
    else:
        (q_ref, k_ref, v_ref, o_ref, gc_ref, gt_ref, br_ref, bc_ref, ng_ref, c0_ref, m0_ref,
         out_ref, co_ref, mo_ref, h_scr, c_scr, m_scr) = refs
    nc = seq // CHUNK
    if zero_init:
        c_scr[...] = jnp.zeros(c_scr.shape, F32)
        m_scr[...] = jnp.full(m_scr.shape, M_INIT, F32)
    else:
        c_scr[...] = c0_ref[...]
        m_scr[...] = m0_ref[...]

    ti = lax.broadcasted_iota(I32, (CHUNK, CHUNK), 0)
    si = lax.broadcasted_iota(I32, (CHUNK, CHUNK), 1)
    ones_col = jnp.where(lax.broadcasted_iota(I32, (CHUNK, M_HD), 1) == 0, 1.0, 0.0).astype(BF16)
    eye_d = jnp.where(lax.broadcasted_iota(I32, (M_HD, M_HD), 0) == lax.broadcasted_iota(I32, (M_HD, M_HD), 1),
                      1.0, 0.0).astype(BF16)

    def cumsum3(tri, x, tri_left):
        hi = x.astype(BF16)
        r1 = x - hi.astype(F32)
        mid = r1.astype(BF16)
        lo = (r1 - mid.astype(F32)).astype(BF16)
        if tri_left:
            return _dg(tri, hi, _NN) + (_dg(tri, mid, _NN) + _dg(tri, lo, _NN))
        return _dg(hi, tri, _NN) + (_dg(mid, tri, _NN) + _dg(lo, tri, _NN))
    gate_i = lambda d, h: 2 * M_HEADS * d + h
    gate_f = lambda d, h: 2 * M_HEADS * d + M_HEADS + h

    def chunk(c, carry):
        ch = []
        for d in (0, 1):
            cc = c if d == 0 else nc - 1 - c
            rows = pl.ds(pl.multiple_of(cc * CHUNK, CHUNK), CHUNK)
            causal = (si <= ti) if d == 0 else (si >= ti)
            tri = jnp.where(causal, 1.0, 0.0).astype(BF16)
            tri_t = jnp.where((ti <= si) if d == 0 else (ti >= si), 1.0, 0.0).astype(BF16)
            q_all = q_ref[rows, :] * (M_HD ** -0.5)
            k_all = k_ref[rows, :]
            v_all = v_ref[rows, :]
            gcol = gc_ref[rows, :] + br_ref[...]
            grow = gt_ref[cc] + bc_ref[...]
            bcum_c_all = cumsum3(tri, _log_sigmoid(gcol), True)
            bcum_r_all = cumsum3(tri_t, _log_sigmoid(grow), False)
            for h in range(M_HEADS):
                sl = slice(h * M_HD, (h + 1) * M_HD)
                q, k, v = q_all[:, sl], k_all[:, sl], v_all[:, sl]
                ig_c = gcol[:, gate_i(d, h):gate_i(d, h) + 1]
                ig_r = grow[gate_i(d, h):gate_i(d, h) + 1, :]
                bcum_c = bcum_c_all[:, gate_f(d, h):gate_f(d, h) + 1]
                bcum_r = bcum_r_all[gate_f(d, h):gate_f(d, h) + 1, :]
                m_st = m_scr[d, h][:, 0:1]
                dmat = jnp.where(causal, bcum_c - bcum_r + ig_r, -jnp.inf)
                inter = bcum_c + m_st
                m_t = jnp.maximum(inter, jnp.max(dmat, axis=1, keepdims=True))
                b_last = bcum_c[CHUNK - 1:CHUNK, :] if d == 0 else bcum_c[0:1, :]
                g_c = b_last - bcum_c + ig_c
                m_new = jnp.maximum(b_last + m_st, jnp.max(g_c, axis=0, keepdims=True))
                ch.append(dict(d=d, h=h, rows=rows, sl=sl, qb=q.astype(BF16), kb=k.astype(BF16),
                               vx=jnp.concatenate([v.astype(BF16), ones_col], axis=1),
                               c_st=c_scr[d, h], m_t=m_t, m_new=m_new,
                               w_intra=jnp.exp(dmat - m_t), w_inter=jnp.exp(inter - m_t),
                               decay=jnp.exp(b_last + m_st - m_new), kw=(k * jnp.exp(g_c - m_new)).astype(BF16)))
        for x in ch:
            x['s_qk'] = _dg(x['qb'], x['kb'], _NT) * x['w_intra']
        for x in ch:
            x['qc'] = _dg(x['qb'], x['c_st'].astype(BF16), _NN)
        for x in ch:
            x['kw_t'] = _dg(eye_d, x['kw'], _NT).astype(BF16)
        for x in ch:
            x['sv'] = _dg(x['s_qk'].astype(BF16), x['vx'], _NN)
        for x in ch:
            x['upd'] = _dg(x['kw_t'], x['vx'], _NN)
        for x in ch:
            d, h = x['d'], x['h']
            tot = x['w_inter'] * x['qc'] + x['sv']
            den = tot[:, M_HD:M_HD + 1]
            h_scr[d, x['rows'], x['sl']] = tot[:, :M_HD] / jnp.maximum(jnp.abs(den), jnp.exp(-x['m_t']))
            c_scr[d, h] = x['decay'] * x['c_st'] + x['upd']
            m_scr[d, h] = jnp.broadcast_to(x['m_new'], (1, M_HD))
        return carry

    lax.fori_loop(0, nc, chunk, 0)
    hsum = h_scr[0] + h_scr[1]
    hn = jnp.concatenate(
        [hsum[:, h * M_HD:(h + 1) * M_HD]
         * lax.rsqrt(jnp.mean(hsum[:, h * M_HD:(h + 1) * M_HD] ** 2, axis=-1, keepdims=True) + NORM_EPS)
         for h in range(M_HEADS)], axis=1)
    out_ref[...] = hn * ng_ref[...] * _sigmoid(o_ref[...])
    co_ref[...] = c_scr[...]
    mo_ref[...] = m_scr[...]


def _mlstm(u, gcol, gt, bias, norm_g, states, row0, batch, seq):
    zero_init = states is None
    width = M_HEADS * M_HD
    qc = COL_M // width
    ngate = 4 * M_HEADS
    blk = lambda off: pl.BlockSpec((seq, width), lambda b: (row0 + b, off))
    st_c = pl.BlockSpec((None, 2, M_HEADS, M_HD, 2 * M_HD), lambda b: (b, 0, 0, 0, 0))
    st_v = pl.BlockSpec((None, 2, M_HEADS, 1, M_HD), lambda b: (b, 0, 0, 0, 0))
    in_specs = [blk(qc), blk(qc + 1), blk(qc + 2), blk(qc + 3),
                pl.BlockSpec((seq, ngate), lambda b: (row0 + b, 0)),
                pl.BlockSpec((seq // CHUNK, ngate, CHUNK), lambda b: (row0 + b, 0, 0)),
                pl.BlockSpec((1, ngate), lambda b: (0, 0)),
                pl.BlockSpec((ngate, 1), lambda b: (0, 0)),
                pl.BlockSpec((1, width), lambda b: (0, 0))]
    args = [u, u, u, u, gcol, gt, bias.reshape(1, ngate), bias.reshape(ngate, 1), norm_g]
    if not zero_init:
        in_specs += [st_c, st_v]
        args += list(states)
    return pl.pallas_call(
        functools.partial(_mlstm_kernel, zero_init=zero_init, seq=seq),
        grid=(batch,),
        in_specs=in_specs,
        out_specs=[pl.BlockSpec((seq, width), lambda b: (b, 0)), st_c, st_v],
        out_shape=[jax.ShapeDtypeStruct((batch * seq, width), F32),
                   jax.ShapeDtypeStruct((batch, 2, M_HEADS, M_HD, 2 * M_HD), F32),
                   jax.ShapeDtypeStruct((batch, 2, M_HEADS, 1, M_HD), F32)],
        scratch_shapes=[pltpu.VMEM((2, seq, width), F32), pltpu.VMEM((2, M_HEADS, M_HD, 2 * M_HD), F32),
                        pltpu.VMEM((2, M_HEADS, 1, M_HD), F32)],
        compiler_params=_cparams(("arbitrary",)),
        name="mlstm_ctx" if zero_init else "mlstm_lat",
    )(*args)


def _rwkv_kernel(*refs, zero_init, seq):
    if zero_init:
        (r_ref, k_ref, v_ref, xl_ref, xg_ref, mur_ref, muk_ref, muv_ref, mul_ref, mug_ref,
         w0_ref, wup_ref, a0_ref, aup_ref, gup_ref, kk_ref, ka_ref, rk_ref, lng_ref, lnb_ref,
         out_ref, so_ref, r_scr, v_scr, kk_scr, g_scr, bonus_scr, lw_scr, kd_scr, b_scr, y_scr, s_scr) = refs
    else:
        (r_ref, k_ref, v_ref, xl_ref, xg_ref, mur_ref, muk_ref, muv_ref, mul_ref, mug_ref,
         w0_ref, wup_ref, a0_ref, aup_ref, gup_ref, kk_ref, ka_ref, rk_ref, lng_ref, lnb_ref, s0_ref,
         out_ref, so_ref, r_scr, v_scr, kk_scr, g_scr, bonus_scr, lw_scr, kd_scr, b_scr, y_scr, s_scr) = refs
    nc = seq // CHUNK
    hd = R_HD
    nh = r_ref.shape[1] // hd

    def tshift(x_ref, mu_ref):
        x = x_ref[...]
        row = lax.broadcasted_iota(I32, x.shape, 0)
        prev = jnp.where(row == 0, 0.0, pltpu.roll(x, 1, 0))
        nxt = jnp.where(row == seq - 1, 0.0, pltpu.roll(x, seq - 1, 0))
        return x + mu_ref[...] * (0.5 * (prev + nxt) - x)

    r = tshift(r_ref, mur_ref)
    k = tshift(k_ref, muk_ref)
    v = tshift(v_ref, muv_ref)
    xl = tshift(xl_ref, mul_ref)
    xg = tshift(xg_ref, mug_ref)
    g = _dot(_sigmoid(xg), gup_ref[...])
    wlanes = nh * hd
    same_head = (lax.broadcasted_iota(I32, (wlanes, wlanes), 0) // hd
                 == lax.broadcasted_iota(I32, (wlanes, wlanes), 1) // hd)
    head_ones = jnp.where(same_head, 1.0, 0.0).astype(BF16)

    def head_sum(x):
        hi = x.astype(BF16)
        r1 = x - hi.astype(F32)
        mid = r1.astype(BF16)
        lo = (r1 - mid.astype(F32)).astype(BF16)
        return _dg(hi, head_ones, _NN) + (_dg(mid, head_ones, _NN) + _dg(lo, head_ones, _NN))

    kkp = k * kk_ref[...]
    kk = kkp * lax.rsqrt(jnp.maximum(head_sum(kkp * kkp), 1e-24))
    tw = jnp.tanh(xl[:, 0:hd])
    xa = xl[:, hd:2 * hd]
    r_scr[...] = r
    v_scr[...] = v
    kk_scr[...] = kk
    g_scr[...] = g
    bonus_scr[...] = head_sum(r * k * rk_ref[...]) * v
    for d in (0, 1):
        lw_scr[d] = -math.exp(-0.5) * _sigmoid(w0_ref[d] + _dot(tw, wup_ref[d]))
        ad = _sigmoid(a0_ref[d] + _dot(xa, aup_ref[d]))
        kd_scr[d] = k * (1.0 + (ad - 1.0) * ka_ref[...])
        b_scr[d] = kk * ad
    if zero_init:
        s_scr[...] = jnp.zeros(s_scr.shape, F32)
    else:
        s_scr[...] = s0_ref[...]
    y_scr[...] = jnp.zeros(y_scr.shape, F32)

    ti = lax.broadcasted_iota(I32, (CHUNK, CHUNK), 0)
    si = lax.broadcasted_iota(I32, (CHUNK, CHUNK), 1)
    eye = (ti == si).astype(F32)

    def chunk(c, carry):
        chains = []
        rows_d = []
        for d in (0, 1):
            cc = c if d == 0 else nc - 1 - c
            rows = pl.ds(pl.multiple_of(cc * CHUNK, CHUNK), CHUNK)
            rows_d.append(rows)
            incl = (si <= ti) if d == 0 else (si >= ti)
            strict = (si < ti) if d == 0 else (si > ti)
            lw = lw_scr[d, rows, :]
            lw_hi, lw_lo = _split(lw)
            tri = jnp.where(incl, 1.0, 0.0).astype(BF16)
            lc = _dg(tri, lw_hi, _NN) + _dg(tri, lw_lo, _NN)
            l_last = lc[CHUNK - 1:CHUNK, :] if d == 0 else lc[0:1, :]
            e_neg = jnp.exp(-lc)
            e_end = jnp.exp(l_last - lc)
            vc = v_scr[rows, :]
            kdc = kd_scr[d, rows, :]
            bc = b_scr[d, rows, :]
            rt = r_scr[rows, :] * jnp.exp(lc)
            kkt = kk_scr[rows, :] * jnp.exp(lc - lw)
            kh = kdc * e_neg
            bh = bc * e_neg
            kbar = kdc * e_end
            bbar = bc * e_end
            w_end = jnp.exp(l_last)
            for i in range(nh):
                sl = slice(i * hd, (i + 1) * hd)
                chains.append(dict(
                    d=d, i=i, incl=incl, strict=strict,
                    lhs=jnp.concatenate([kkt[:, sl], rt[:, sl]], axis=0),
                    rhs=jnp.concatenate([bh[:, sl], kh[:, sl]], axis=0),
                    end=jnp.concatenate([kbar[:, sl], bbar[:, sl]], axis=0),
                    v=vc[:, sl], w_end=w_end[:, sl], s0=s_scr[d, i]))
        for ch in chains:
            ch['ab'] = _dot1(ch['lhs'], ch['rhs'], _NT)
        for ch in chains:
            ch['proj'] = _dot1(ch['lhs'], ch['s0'], _NT)
        for ch in chains:
            ab = ch['ab']
            ch['a_kb'] = jnp.where(ch['strict'], ab[:CHUNK, :CHUNK], 0.0)
            ch['b_rb'] = jnp.where(ch['incl'], ab[CHUNK:, :CHUNK], 0.0)
            ch['akk_brk'] = jnp.concatenate([jnp.where(ch['strict'], ab[:CHUNK, CHUNK:], 0.0),
                                             jnp.where(ch['incl'], ab[CHUNK:, CHUNK:], 0.0)], axis=0)
        for ch in chains:
            ch['p'] = _dot1(ch['a_kb'], ch['a_kb'])
        for ch in chains:
            ch['abv'] = _dot1(ch['akk_brk'], ch['v'])
        for ch in chains:
            inv = eye - ch['a_kb']
            ch['inv'] = inv + _dot1(inv, ch['p'])
        span = 4
        while span < CHUNK:
            for ch in chains:
                ch['p'] = _dot1(ch['p'], ch['p'])
            for ch in chains:
                ch['inv'] = ch['inv'] + _dot1(ch['inv'], ch['p'])
            span *= 2
        for ch in chains:
            ch['u'] = _dot1(ch['inv'], ch['proj'][:CHUNK] + ch['abv'][:CHUNK])
        for ch in chains:
            ch['y'] = ch['proj'][CHUNK:] + ch['abv'][CHUNK:] - _dot1(ch['b_rb'], ch['u'])
        for ch in chains:
            upd = _dot3(jnp.concatenate([ch['v'], -ch['u']], axis=0), ch['end'], _TN)
            s_scr[ch['d'], ch['i']] = ch['s0'] * ch['w_end'] + upd
        for d in (0, 1):
            y_scr[rows_d[d], :] += jnp.concatenate([ch['y'] for ch in chains[nh * d:nh * (d + 1)]], axis=1)
        return carry

    lax.fori_loop(0, nc, chunk, 0)

    y = y_scr[...]
    dev = y - head_sum(y) * (1.0 / hd)
    yn = dev * lax.rsqrt(head_sum(dev * dev) * (1.0 / hd) + GN_EPS)
    out_ref[...] = (yn * lng_ref[...] + lnb_ref[...] + bonus_scr[...]) * g_scr[...]
    so_ref[...] = s_scr[...]


def _rwkv(u, p, state, row0, batch, seq, heads_per_step):
    zero_init = state is None
    nh = heads_per_step
    wd = nh * R_HD
    nsteps = R_HEADS // nh
    width = R_HEADS * R_HD
    rc = COL_R // wd
    sec = width // wd
    lc = (COL_R + 3 * width) // LANES
    ublk = lambda off: pl.BlockSpec((seq, wd), lambda b, h: (row0 + b, rc + off + h))
    ufix = lambda blk: pl.BlockSpec((seq, LANES), lambda b, h: (row0 + b, blk))
    mblk = lambda off: pl.BlockSpec((1, wd), lambda b, h: (0, off + h))
    mfix = lambda blk: pl.BlockSpec((1, LANES), lambda b, h: (0, blk))
    vec = pl.BlockSpec((1, wd), lambda b, h: (0, h))
    in_specs = [ublk(0), ublk(sec), ublk(2 * sec), ufix(lc), ufix(lc + 1),
                mblk(0), mblk(sec), mblk(2 * sec), mfix(lc - COL_R // LANES), mfix(lc - COL_R // LANES + 1),
                pl.BlockSpec((2, 1, wd), lambda b, h: (0, 0, h)),
                pl.BlockSpec((2, R_HD, wd), lambda b, h: (0, 0, h)),
                pl.BlockSpec((2, 1, wd), lambda b, h: (0, 0, h)),
                pl.BlockSpec((2, R_HD, wd), lambda b, h: (0, 0, h)),
                pl.BlockSpec((LANES, wd), lambda b, h: (0, h)),
                vec, vec, vec, vec, vec]
    args = [u, u, u, u, u, p['mu'], p['mu'], p['mu'], p['mu'], p['mu'],
            p['w0'], p['w_up'], p['a0'], p['a_up'], p['g_up'], p['k_k'], p['k_a'], p['r_k'], p['ln_g'], p['ln_b']]
    if not zero_init:
        in_specs.append(pl.BlockSpec((None, 2, nh, R_HD, R_HD), lambda b, h: (b, 0, h, 0, 0)))
        args.append(state)
    big = lambda n: pltpu.VMEM((n, seq, wd), F32)
    return pl.pallas_call(
        functools.partial(_rwkv_kernel, zero_init=zero_init, seq=seq),
        grid=(batch, nsteps),
        in_specs=in_specs,
        out_specs=[pl.BlockSpec((seq, wd), lambda b, h: (b, h)),
                   pl.BlockSpec((None, 2, nh, R_HD, R_HD), lambda b, h: (b, 0, h, 0, 0))],
        out_shape=[jax.ShapeDtypeStruct((batch * seq, width), F32),
                   jax.ShapeDtypeStruct((batch, 2, R_HEADS, R_HD, R_HD), F32)],
        scratch_shapes=[pltpu.VMEM((seq, wd), F32)] * 5
                       + [big(2), big(2), big(2), pltpu.VMEM((seq, wd), F32), pltpu.VMEM((2, nh, R_HD, R_HD), F32)],
        compiler_params=_cparams(("arbitrary", "arbitrary")),
        name="rwkv_ctx" if zero_init else "rwkv_lat",
    )(*args)


def _top2_sum(a, b, c, d):
    m1, n1 = jnp.maximum(a, b), jnp.minimum(a, b)
    m2, n2 = jnp.maximum(c, d), jnp.minimum(c, d)
    return jnp.maximum(m1, m2) + jnp.maximum(jnp.minimum(m1, m2), jnp.maximum(n1, n2))


def _first_argmax(vals):
    best = functools.reduce(jnp.maximum, vals)
    idx = jnp.full(best.shape, len(vals) - 1, I32)
    for j in range(len(vals) - 2, -1, -1):
        idx = jnp.where(vals[j] == best, j, idx)
    return best, idx


def _out_kernel(attc_ref, attl_ref, mc_ref, ml_ref, rc_ref, rl_ref, xc_ref, xl_ref, w_ref, g1_ref, sh2_ref, sc2_ref,
                n2_ref, rw_ref, rb_ref, x1_ref, h2_ref, route_ref, cnt_ref, cnt_scr, *, n_ctx_tiles):
    @pl.when(pl.program_id(0) == 0)
    def _():
        cnt_scr[...] = jnp.zeros(cnt_scr.shape, F32)

    is_ctx = pl.program_id(0) < n_ctx_tiles
    both = lambda c_ref, l_ref: jnp.where(is_ctx, c_ref[...], l_ref[...])
    na = attc_ref.shape[1]
    nm = mc_ref.shape[1]
    mix = (jnp.dot(both(attc_ref, attl_ref).astype(BF16), w_ref[0:na, :], preferred_element_type=F32)
           + jnp.dot(both(mc_ref, ml_ref).astype(BF16), w_ref[na:na + nm, :], preferred_element_type=F32)
           + jnp.dot(both(rc_ref, rl_ref).astype(BF16), w_ref[na + nm:, :], preferred_element_type=F32))
    x1 = both(xc_ref, xl_ref) + g1_ref[...] * mix
    x1_ref[...] = x1
    h2 = _rms(x1, n2_ref[...]) * (1.0 + sc2_ref[...]) + sh2_ref[...]
    h2_ref[...] = h2
    logits = _dot1(rw_ref[...], h2, _NT)
    s = _sigmoid(logits)
    ssel = s + rb_ref[...]
    srow = [s[e:e + 1, :] for e in range(N_EXPERTS)]
    brow = [ssel[e:e + 1, :] for e in range(N_EXPERTS)]
    gscore = [_top2_sum(*brow[EXPERTS_PER_GROUP * g:EXPERTS_PER_GROUP * (g + 1)]) for g in range(N_EXPERT_GROUPS)]
    _, gidx = _first_argmax(gscore)
    pick = lambda rows, j: functools.reduce(
        lambda acc, g: jnp.where(gidx == g, rows[EXPERTS_PER_GROUP * g + j], acc),
        range(N_EXPERT_GROUPS - 2, -1, -1), rows[EXPERTS_PER_GROUP * (N_EXPERT_GROUPS - 1) + j])
    ing = [pick(brow, j) for j in range(EXPERTS_PER_GROUP)]
    sin_ = [pick(srow, j) for j in range(EXPERTS_PER_GROUP)]
    _, l1 = _first_argmax(ing)
    _, l2 = _first_argmax([jnp.where(l1 == j, -jnp.inf, ing[j]) for j in range(EXPERTS_PER_GROUP)])
    sel = lambda l: functools.reduce(lambda acc, j: jnp.where(l == j, sin_[j], acc),
                                     range(EXPERTS_PER_GROUP - 2, -1, -1), sin_[EXPERTS_PER_GROUP - 1])
    w1, w2 = sel(l1), sel(l2)
    tot = w1 + w2
    e1 = gidx * EXPERTS_PER_GROUP + l1
    e2 = gidx * EXPERTS_PER_GROUP + l2
    tm = e1.shape[1]
    eid = lax.broadcasted_iota(I32, (N_EXPERTS, tm), 0)
    oh1 = eid == e1
    oh2 = eid == e2
    picked = jnp.where(jnp.logical_or(oh1, oh2), 1.0, 0.0)
    earlier = jnp.where(lax.broadcasted_iota(I32, (tm, tm), 0) < lax.broadcasted_iota(I32, (tm, tm), 1), 1.0, 0.0)
    rank = cnt_scr[:, 0:1] + jnp.dot(picked.astype(BF16), earlier.astype(BF16), preferred_element_type=F32)
    pos1 = jnp.sum(jnp.where(oh1, rank, 0.0), axis=0, keepdims=True)
    pos2 = jnp.sum(jnp.where(oh2, rank, 0.0), axis=0, keepdims=True)
    cnt = cnt_scr[...] + jnp.sum(picked, axis=1, keepdims=True)
    cnt_scr[...] = cnt
    cnt_ref[...] = cnt
    zero = jnp.zeros_like(w1)
    route_ref[...] = jnp.concatenate([e1.astype(F32), e2.astype(F32), w1 / tot, w2 / tot, pos1, pos2, zero, zero],
                                     axis=0)


def _out_proj(att, m_out, r_out, x, w_out, mod3, n2g, rw_t, rb, s_lat):
    t_ctx, d = x[0].shape
    ntok = t_ctx + x[1].shape[0]
    tm = 256
    nct = t_ctx // tm
    row = lambda i: _mod_row(i * tm, t_ctx, s_lat)
    modblk = lambda j: pl.BlockSpec((None, 1, d), lambda i: (row(i), 0, j))
    pair = lambda a: [pl.BlockSpec((tm, a[0].shape[1]), lambda i: (jnp.minimum(i, nct - 1), 0)),
                      pl.BlockSpec((tm, a[1].shape[1]), lambda i: (jnp.maximum(i - nct, 0), 0))]
    return pl.pallas_call(
        functools.partial(_out_kernel, n_ctx_tiles=nct),
        grid=(ntok // tm,),
        in_specs=pair(att) + pair(m_out) + pair(r_out) + pair(x) + [
                  pl.BlockSpec(w_out.shape, lambda i: (0, 0)),
                  modblk(2), modblk(3), modblk(4),
                  pl.BlockSpec((1, d), lambda i: (0, 0)),
                  pl.BlockSpec(rw_t.shape, lambda i: (0, 0)),
                  pl.BlockSpec(rb.shape, lambda i: (0, 0))],
        out_specs=[pl.BlockSpec((tm, d), lambda i: (i, 0)),
                   pl.BlockSpec((tm, d), lambda i: (i, 0)),
                   pl.BlockSpec((8, tm), lambda i: (0, i)),
                   pl.BlockSpec((N_EXPERTS, LANES), lambda i: (0, 0))],
        out_shape=[jax.ShapeDtypeStruct((ntok, d), F32), jax.ShapeDtypeStruct((ntok, d), F32),
                   jax.ShapeDtypeStruct((8, ntok), F32), jax.ShapeDtypeStruct((N_EXPERTS, LANES), F32)],
        scratch_shapes=[pltpu.VMEM((N_EXPERTS, LANES), F32)],
        compiler_params=_cparams(("arbitrary",)),
        name="out_proj",
    )(*att, *m_out, *r_out, *x, w_out, mod3, mod3, mod3, n2g, rw_t, rb)


def _row_gather(src_hbm, idx_ref, base, dst, sem, n, unrolled):
    def start(j):
        pltpu.make_async_copy(src_hbm.at[pl.ds(idx_ref[base + j], 1), :], dst.at[pl.ds(j, 1), :], sem).start()

    if unrolled:
        for j in range(n):
            start(j)
    else:
        def body(j, c):
            start(j)
            return c
        lax.fori_loop(0, n, body, 0)


def _expert_kernel(be_ref, tok_ref, nused_ref, h_hbm, w1_ref, w3_ref, w2_ref, y_ref, xbuf, w1b, w3b, w2b, sem):
    i = pl.program_id(0)
    n_used = nused_ref[0]
    wait = lambda s: pltpu.make_async_copy(h_hbm.at[pl.ds(0, EXPERT_ROWS), :], xbuf.at[s], sem.at[s]).wait()

    @pl.when(i < n_used)
    def _():
        slot = i % 2

        @pl.when(i == 0)
        def _():
            _row_gather(h_hbm, tok_ref, 0, xbuf.at[0], sem.at[0], EXPERT_ROWS, unrolled=False)

        @pl.when(jnp.logical_or(i == 0, be_ref[i] != be_ref[jnp.maximum(i - 1, 0)]))
        def _():
            w1b[...] = w1_ref[...].astype(BF16)
            w3b[...] = w3_ref[...].astype(BF16)
            w2b[...] = w2_ref[...].astype(BF16)

        wait(slot)
        _row_gather(h_hbm, tok_ref, (i + 1) * EXPERT_ROWS, xbuf.at[1 - slot], sem.at[1 - slot], EXPERT_ROWS,
                    unrolled=True)
        xb = xbuf[slot].astype(BF16)
        a = jnp.dot(xb, w1b[...], preferred_element_type=F32)
        b = jnp.dot(xb, w3b[...], preferred_element_type=F32)
        hmid = (a * _sigmoid(a)) * b
        y_ref[...] = jnp.dot(hmid.astype(BF16), w2b[...], preferred_element_type=F32)

        @pl.when(i == n_used - 1)
        def _():
            wait(1 - slot)

    @pl.when(i >= n_used)
    def _():
        y_ref[...] = jnp.zeros(y_ref.shape, F32)


def _experts(h2, block_e, row_tok, n_used, w1, w3, w2, layer):
    ntok, d = h2.shape
    de = w1.shape[3]
    n_rows = row_tok.shape[0] - EXPERT_ROWS
    nb = n_rows // EXPERT_ROWS
    return pl.pallas_call(
        _expert_kernel,
        grid_spec=pltpu.PrefetchScalarGridSpec(
            num_scalar_prefetch=3,
            grid=(nb,),
            in_specs=[pl.BlockSpec(memory_space=pl.ANY),
                      pl.BlockSpec((None, None, d, de), lambda i, be, tok, nu: (layer, be[i], 0, 0)),
                      pl.BlockSpec((None, None, d, de), lambda i, be, tok, nu: (layer, be[i], 0, 0)),
                      pl.BlockSpec((None, None, de, d), lambda i, be, tok, nu: (layer, be[i], 0, 0))],
            out_specs=pl.BlockSpec((EXPERT_ROWS, d), lambda i, be, tok, nu: (i, 0)),
            scratch_shapes=[pltpu.VMEM((2, EXPERT_ROWS, d), F32), pltpu.VMEM((d, de), BF16),
                            pltpu.VMEM((d, de), BF16), pltpu.VMEM((de, d), BF16), pltpu.SemaphoreType.DMA((2,))]),
        out_shape=jax.ShapeDtypeStruct((n_rows, d), F32),
        compiler_params=_cparams(("arbitrary",)),
        name="experts",
    )(block_e, row_tok, n_used, h2, w1, w3, w2)


def _combine_kernel(d1_ref, d2_ref, y_hbm, x1_ref, g2_ref, gate_ref, oc_ref, ol_ref, ybuf, sem, *, n_ctx_tiles):
    tm = x1_ref.shape[0]
    i = pl.program_id(0)
    slot = i % 2

    def gather(tile, s, unrolled):
        _row_gather(y_hbm, d1_ref, tile * tm, ybuf.at[s, 0], sem.at[s], tm, unrolled)
        _row_gather(y_hbm, d2_ref, tile * tm, ybuf.at[s, 1], sem.at[s], tm, unrolled)

    def wait(s):
        pltpu.make_async_copy(y_hbm.at[pl.ds(0, tm), :], ybuf.at[s, 0], sem.at[s]).wait()
        pltpu.make_async_copy(y_hbm.at[pl.ds(0, tm), :], ybuf.at[s, 1], sem.at[s]).wait()

    @pl.when(i == 0)
    def _():
        gather(0, 0, False)

    wait(slot)
    gather(i + 1, 1 - slot, True)
    gate = gate_ref[...]
    moe = ybuf[slot, 0] * gate[:, 0:1] + ybuf[slot, 1] * gate[:, 1:2]
    new_x = x1_ref[...] + g2_ref[...] * moe

    @pl.when(i < n_ctx_tiles)
    def _():
        oc_ref[...] = new_x

    @pl.when(i >= n_ctx_tiles)
    def _():
        ol_ref[...] = new_x

    @pl.when(i == pl.num_programs(0) - 1)
    def _():
        wait(1 - slot)


def _combine(y, x1, mod3, gate, dest1, dest2, t_ctx, s_lat):
    ntok, d = x1.shape
    tm = 256
    nct = t_ctx // tm
    row = lambda i: _mod_row(i * tm, t_ctx, s_lat)
    return pl.pallas_call(
        functools.partial(_combine_kernel, n_ctx_tiles=nct),
        grid_spec=pltpu.PrefetchScalarGridSpec(
            num_scalar_prefetch=2,
            grid=(ntok // tm,),
            in_specs=[pl.BlockSpec(memory_space=pl.ANY),
                      pl.BlockSpec((tm, d), lambda i, a, b: (i, 0)),
                      pl.BlockSpec((None, 1, d), lambda i, a, b: (row(i), 0, 5)),
                      pl.BlockSpec((tm, 2), lambda i, a, b: (i, 0))],
            out_specs=[pl.BlockSpec((tm, d), lambda i, a, b: (jnp.minimum(i, nct - 1), 0)),
                       pl.BlockSpec((tm, d), lambda i, a, b: (jnp.maximum(i - nct, 0), 0))],
            scratch_shapes=[pltpu.VMEM((2, 2, tm, d), F32), pltpu.SemaphoreType.DMA((2,))]),
        out_shape=[jax.ShapeDtypeStruct((t_ctx, d), F32), jax.ShapeDtypeStruct((ntok - t_ctx, d), F32)],
        compiler_params=_cparams(("arbitrary",)),
        name="combine",
    )(dest1, dest2, y, x1, mod3, gate)


def _dispatch(route, counts, tile):
    ntok = route.shape[1]
    e = route[0:2].astype(I32)
    pos = route[4:6].astype(I32)
    gate = route[2:4].T
    counts = counts[:, 0].astype(I32)
    padded = (counts + EXPERT_ROWS - 1) // EXPERT_ROWS * EXPERT_ROWS
    pad_end = jnp.cumsum(padded)
    pad_start = pad_end - padded
    onehot = (e[:, :, None] == jnp.arange(N_EXPERTS, dtype=I32)).astype(I32)
    dest = jnp.sum(onehot * pad_start, axis=-1) + pos
    n_rows = -(-(2 * ntok) // EXPERT_ROWS) * EXPERT_ROWS + N_EXPERTS * EXPERT_ROWS
    nb = n_rows // EXPERT_ROWS
    tok = jnp.broadcast_to(jnp.arange(ntok, dtype=I32)[None, :], (2, ntok))
    row_tok = jnp.zeros((n_rows + EXPERT_ROWS,), I32).at[dest.reshape(-1)].set(tok.reshape(-1))
    blk_start = jnp.arange(nb, dtype=I32) * EXPERT_ROWS
    block_e = jnp.minimum(jnp.sum((pad_end[None, :] <= blk_start[:, None]).astype(I32), axis=1), N_EXPERTS - 1)
    n_used = pad_end[-1:] // EXPERT_ROWS
    dest = jnp.pad(dest, ((0, 0), (0, tile)))
    return row_tok, block_e, n_used, gate, dest[0], dest[1]


def kernel(x_prompt, x_sample, c, cache_attn_k, cache_attn_v, state_mlstm_C, state_mlstm_n, state_mlstm_m, state_rwkv, c_ctx, norm1_g, norm2_g, w_mod, b_mod, w_in, w_out, attn_q_norm, attn_k_norm, mlstm_i_bias, mlstm_f_bias, mlstm_norm_g, rwkv_mu, rwkv_w0, rwkv_w_up, rwkv_a0, rwkv_a_up, rwkv_g_up, rwkv_k_k, rwkv_k_a, rwkv_r_k, rwkv_ln_g, rwkv_ln_b, router_w, router_b, exp_w1, exp_w3, exp_w2):
    b_ctx, s_ctx, d = x_prompt.shape
    b_lat, s_lat, _ = x_sample.shape
    depth = w_in.shape[0]
    t_ctx = b_ctx * s_ctx
    ntok = t_ctx + b_lat * s_lat
    assert b_lat + 1 <= 8 and s_lat % 1024 == 0 and t_ctx % 1024 == 0 and t_ctx % s_lat == 0
    past = cache_attn_k.shape[2]

    x = (x_prompt.reshape(t_ctx, d), x_sample.reshape(b_lat * s_lat, d))
    c_all = jnp.zeros((8, d), F32).at[0].set(c_ctx).at[1:1 + b_lat].set(c)
    mod = _modulation(c_all, w_mod, b_mod)
    cos, sin = _rope_tables(s_lat)
    rw_t = router_w.T
    rb = router_b.reshape(N_EXPERTS, 1)
    r_width = R_HEADS * R_HD
    n_in = w_in.shape[2]
    gate_hi = COL_R + 4 * M_HEADS
    w_in_t = jnp.swapaxes(w_in, 1, 2)
    w_tail_t = jnp.concatenate([w_in_t[:, gate_hi:], w_in_t[:, COL_R:gate_hi],
                                jnp.zeros((depth, N_IN_PAD - n_in, d), F32)], axis=1).astype(BF16)

    ks, vs, cs, ns, ms, rs = [], [], [], [], [], []
    for l in range(depth):
        mod3 = mod[l].reshape(8, 1, 6 * d)
        u = _in_proj(x[0], x[1], norm1_g[l][None], mod3, w_in_t, l, w_tail_t[l], s_lat)

        ck = cache_attn_k[:, l].reshape(b_lat, past, ATT_KV_HEADS * ATT_HD)
        cv = cache_attn_v[:, l].reshape(b_lat, past, ATT_KV_HEADS * ATT_HD)
        att_c, att_l, k_ctx, v_ctx = _attention(u, attn_q_norm[l][None], attn_k_norm[l][None], ck, cv, cos, sin,
                                                b_ctx, s_ctx, b_lat, s_lat)
        ks.append(k_ctx.reshape(b_ctx, s_ctx, ATT_KV_HEADS, ATT_HD))
        vs.append(v_ctx.reshape(b_ctx, s_ctx, ATT_KV_HEADS, ATT_HD))

        gcol = u[:, COL_GATE:COL_GATE + 4 * M_HEADS]
        gt = gcol.reshape(ntok // CHUNK, CHUNK, 4 * M_HEADS).transpose(0, 2, 1)
        bias = jnp.stack([mlstm_i_bias[l], mlstm_f_bias[l]], axis=1)
        ng = mlstm_norm_g[l][None]
        m_c_out, cx_c, m_c = _mlstm(u, gcol, gt, bias, ng, None, 0, b_ctx, s_ctx)
        n_col = jnp.pad(state_mlstm_n[:, l][..., None], ((0, 0),) * 4 + ((0, M_HD - 1),))
        lat_states = (jnp.concatenate([state_mlstm_C[:, l], n_col], axis=-1),
                      jnp.broadcast_to(state_mlstm_m[:, l][..., None, None], (b_lat, 2, M_HEADS, 1, M_HD)))
        m_l_out, _, _ = _mlstm(u, gcol, gt, bias, ng, lat_states, t_ctx // s_lat, b_lat, s_lat)
        cs.append(cx_c[..., :M_HD])
        ns.append(cx_c[..., M_HD])
        ms.append(m_c[:, :, :, 0, 0])

        rp = dict(mu=rwkv_mu[l][None], w0=rwkv_w0[l].reshape(2, 1, r_width), w_up=rwkv_w_up[l],
                  a0=rwkv_a0[l].reshape(2, 1, r_width), a_up=rwkv_a_up[l], g_up=rwkv_g_up[l],
                  k_k=rwkv_k_k[l][None], k_a=rwkv_k_a[l][None], r_k=rwkv_r_k[l].reshape(1, r_width),
                  ln_g=rwkv_ln_g[l][None], ln_b=rwkv_ln_b[l][None])
        r_c_out, r_c = _rwkv(u, rp, None, 0, b_ctx, s_ctx, RWKV_HEADS_PER_STEP_CTX)
        r_l_out, _ = _rwkv(u, rp, state_rwkv[:, l], t_ctx // s_lat, b_lat, s_lat, RWKV_HEADS_PER_STEP_LAT)
        rs.append(r_c)

        x1, h2, route, counts = _out_proj((att_c, att_l), (m_c_out, m_l_out), (r_c_out, r_l_out), x,
                                          w_out[l].astype(BF16), mod3, norm2_g[l][None], rw_t, rb, s_lat)
        row_tok, block_e, n_used, gate, dest1, dest2 = _dispatch(route, counts, 256)
        y = _experts(h2, block_e, row_tok, n_used, exp_w1, exp_w3, exp_w2, l)
        x = _combine(y, x1, mod3, gate, dest1, dest2, t_ctx, s_lat)

    y_prompt = x[0].reshape(b_ctx, s_ctx, d)
    y_sample = x[1].reshape(b_lat, s_lat, d)
    return (y_prompt, y_sample, jnp.stack(ks, axis=1), jnp.stack(vs, axis=1), jnp.stack(cs, axis=1),
            jnp.stack(ns, axis=1), jnp.stack(ms, axis=1), jnp.stack(rs, axis=1))
```

```python
import functools
import math

import jax
import jax.numpy as jnp
from jax import lax
from jax.experimental import pallas as pl
from jax.experimental.pallas import tpu as pltpu

F32 = jnp.float32
BF16 = jnp.bfloat16
I32 = jnp.int32

NORM_EPS = 1e-6
GN_EPS = 64e-5
M_INIT = -1e30
GRID_W = 64
ROPE_THETA = 10000.0
ATT_HD = 128
ATT_GROUPS = 4
ATT_KV_HEADS = 2
M_HD = 128
M_HEADS = 4
R_HD = 64
R_HEADS = 8
N_EXPERTS = 16
N_EXPERT_GROUPS = 4
EXPERTS_PER_GROUP = 4
CHUNK = 64
LANES = 128
EXPERT_ROWS = 256
VMEM_LIMIT = 58 * 1024 * 1024
RWKV_HEADS_PER_STEP_CTX = 8
RWKV_HEADS_PER_STEP_LAT = 8

COL_ATT = 0
COL_M = 1536
COL_R = 3584
COL_GATE = 5376
N_IN_PAD = 5632


def _cparams(sem):
    return pltpu.CompilerParams(dimension_semantics=sem, vmem_limit_bytes=VMEM_LIMIT)


def _dot(a, b):
    return jnp.dot(a.astype(BF16), b.astype(BF16), preferred_element_type=F32)


def _dg(a, b, dims):
    return lax.dot_general(a, b, (dims, ((), ())), preferred_element_type=F32)


_NN = ((1,), (0,))
_NT = ((1,), (1,))
_TN = ((0,), (0,))


def _split(a):
    hi = a.astype(BF16)
    lo = (a - hi.astype(F32)).astype(BF16)
    return hi, lo


def _dot3(a, b, dims=_NN):
    ah, al = _split(a)
    bh, bl = _split(b)
    return _dg(ah, bh, dims) + (_dg(ah, bl, dims) + _dg(al, bh, dims))


def _dot1(a, b, dims=_NN):
    return _dg(a.astype(BF16), b.astype(BF16), dims)


def _rms(x, g):
    return x * lax.rsqrt(jnp.mean(x * x, axis=-1, keepdims=True) + NORM_EPS) * g


def _sigmoid(x):
    return 1.0 / (1.0 + jnp.exp(-x))


def _mod_kernel(c_ref, w_ref, b_ref, o_ref):
    c = c_ref[...]
    o_ref[...] = _dot(c * _sigmoid(c), w_ref[...]) + b_ref[...]


def _modulation(c_all, w_mod, b_mod):
    depth, d, n = w_mod.shape
    tn = 1024
    return pl.pallas_call(
        _mod_kernel,
        grid=(depth, n // tn),
        in_specs=[pl.BlockSpec((8, d), lambda l, j: (0, 0)),
                  pl.BlockSpec((None, d, tn), lambda l, j: (l, 0, j)),
                  pl.BlockSpec((None, 1, tn), lambda l, j: (l, 0, j))],
        out_specs=pl.BlockSpec((None, 8, tn), lambda l, j: (l, 0, j)),
        out_shape=jax.ShapeDtypeStruct((depth, 8, n), F32),
        compiler_params=_cparams(("arbitrary", "arbitrary")),
        name="modulation",
    )(c_all, w_mod, b_mod.reshape(depth, 1, n))


def _mod_row(tok0, t_ctx, s_lat):
    return jnp.where(tok0 < t_ctx, 0, 1 + (tok0 - t_ctx) // s_lat)


def _in_kernel(xc_ref, xl_ref, g_ref, sh_ref, sc_ref, wa_ref, wb_ref, o_ref, h_ref, *, n_ctx_tiles, n_head_tiles):
    i = pl.program_id(0)
    j = pl.program_id(1)

    def normalise(x_ref):
        slab = 256
        for r0 in range(0, x_ref.shape[0], slab):
            h = _rms(x_ref[r0:r0 + slab, :], g_ref[...]) * (1.0 + sc_ref[...]) + sh_ref[...]
            h_ref[r0:r0 + slab, :] = h.astype(BF16)

    @pl.when(jnp.logical_and(j == 0, i < n_ctx_tiles))
    def _():
        normalise(xc_ref)

    @pl.when(jnp.logical_and(j == 0, i >= n_ctx_tiles))
    def _():
        normalise(xl_ref)

    @pl.when(j < n_head_tiles)
    def _():
        o_ref[...] = _dg(h_ref[...], wa_ref[...].astype(BF16), _NT)

    @pl.when(j >= n_head_tiles)
    def _():
        o_ref[...] = _dg(h_ref[...], wb_ref[...], _NT)


def _in_proj(xc, xl, g1, mod3, w_in_t, layer, w_tail_t, s_lat):
    t_ctx, d = xc.shape
    ntok = t_ctx + xl.shape[0]
    tm, tn = 1024, 512
    na = COL_R // tn
    n = COL_R + w_tail_t.shape[0]
    nct = t_ctx // tm
    row = lambda i: _mod_row(i * tm, t_ctx, s_lat)
    return pl.pallas_call(
        functools.partial(_in_kernel, n_ctx_tiles=nct, n_head_tiles=na),
        grid=(ntok // tm, n // tn),
        in_specs=[pl.BlockSpec((tm, d), lambda i, j: (jnp.minimum(i, nct - 1), 0)),
                  pl.BlockSpec((tm, d), lambda i, j: (jnp.maximum(i - nct, 0), 0)),
                  pl.BlockSpec((1, d), lambda i, j: (0, 0)),
                  pl.BlockSpec((None, 1, d), lambda i, j: (row(i), 0, 0)),
                  pl.BlockSpec((None, 1, d), lambda i, j: (row(i), 0, 1)),
                  pl.BlockSpec((None, tn, d), lambda i, j: (layer, jnp.minimum(j, na - 1), 0)),
                  pl.BlockSpec((tn, d), lambda i, j: (jnp.maximum(j - na, 0), 0))],
        out_specs=pl.BlockSpec((tm, tn), lambda i, j: (i, j)),
        out_shape=jax.ShapeDtypeStruct((ntok, n), F32),
        scratch_shapes=[pltpu.VMEM((tm, d), BF16)],
        compiler_params=_cparams(("arbitrary", "arbitrary")),
        name="in_proj",
    )(xc, xl, g1, mod3, mod3, w_in_t, w_tail_t)


def _softmax_av(q, kb, vb):
    s = _dg(q.astype(BF16), kb, _NT) * (ATT_HD ** -0.5)
    p = jnp.exp(s - jnp.max(s, axis=-1, keepdims=True))
    l = jnp.sum(p, axis=-1, keepdims=True)
    return jnp.dot(p.astype(BF16), vb, preferred_element_type=F32) / l


def _att_ctx_kernel(q_ref, k_ref, v_ref, qn_ref, kn_ref, o_ref, ko_ref, vo_ref):
    k = _rms(k_ref[...], kn_ref[...])
    ko_ref[...] = k
    vo_ref[...] = v_ref[...]
    kb = k.astype(BF16)
    vb = v_ref[...].astype(BF16)
    for g in range(ATT_GROUPS):
        q = _rms(q_ref[:, g * ATT_HD:(g + 1) * ATT_HD], qn_ref[...])
        o_ref[:, g * ATT_HD:(g + 1) * ATT_HD] = _softmax_av(q, kb, vb)


def _rope(x, cos, sin):
    lane = lax.broadcasted_iota(I32, x.shape, 1)
    first = (lane % (ATT_HD // 2)) < (ATT_HD // 4)
    partner = jnp.where(first, pltpu.roll(x, ATT_HD - ATT_HD // 4, 1), pltpu.roll(x, ATT_HD // 4, 1))
    return x * cos + partner * sin


def _att_lat_kernel(q_ref, k_ref, v_ref, ck_ref, cv_ref, qn_ref, kn_ref, cosq_ref, sinq_ref, cosk_ref, sink_ref,
                    o_ref, kb_ref, vb_ref, *, past):
    @pl.when(pl.program_id(2) == 0)
    def _():
        k = _rope(_rms(k_ref[...], kn_ref[...]), cosk_ref[...], sink_ref[...])
        kb_ref[0:past, :] = ck_ref[...].astype(BF16)
        kb_ref[past:, :] = k.astype(BF16)
        vb_ref[0:past, :] = cv_ref[...].astype(BF16)
        vb_ref[past:, :] = v_ref[...].astype(BF16)

    kb = kb_ref[...]
    vb = vb_ref[...]
    for g in range(ATT_GROUPS):
        q = _rope(_rms(q_ref[:, g * ATT_HD:(g + 1) * ATT_HD], qn_ref[...]), cosq_ref[...], sinq_ref[...])
        o_ref[:, g * ATT_HD:(g + 1) * ATT_HD] = _softmax_av(q, kb, vb)


def _attention(u, qn, kn, cache_k, cache_v, cos, sin, b_ctx, s_ctx, b_lat, s_lat):
    ntok = u.shape[0]
    t_ctx = b_ctx * s_ctx
    gw = ATT_GROUPS * ATT_HD
    kcol = (ATT_KV_HEADS * gw) // ATT_HD
    vcol = kcol + ATT_KV_HEADS
    kv_spec = pl.BlockSpec((s_ctx, ATT_HD), lambda b, h: (b, h))
    att_c, k_ctx, v_ctx = pl.pallas_call(
        _att_ctx_kernel,
        grid=(b_ctx, ATT_KV_HEADS),
        in_specs=[pl.BlockSpec((s_ctx, gw), lambda b, h: (b, h)),
                  pl.BlockSpec((s_ctx, ATT_HD), lambda b, h: (b, kcol + h)),
                  pl.BlockSpec((s_ctx, ATT_HD), lambda b, h: (b, vcol + h)),
                  pl.BlockSpec((1, ATT_HD), lambda b, h: (0, 0)),
                  pl.BlockSpec((1, ATT_HD), lambda b, h: (0, 0))],
        out_specs=[pl.BlockSpec((s_ctx, gw), lambda b, h: (b, h)), kv_spec, kv_spec],
        out_shape=[jax.ShapeDtypeStruct((t_ctx, ATT_KV_HEADS * gw), F32),
                   jax.ShapeDtypeStruct((t_ctx, ATT_KV_HEADS * ATT_HD), F32),
                   jax.ShapeDtypeStruct((t_ctx, ATT_KV_HEADS * ATT_HD), F32)],
        compiler_params=_cparams(("arbitrary", "arbitrary")),
        name="att_ctx",
    )(u, u, u, qn, kn)

    tq = 256
    nqb = s_lat // tq
    past = cache_k.shape[1]
    qrow0 = t_ctx // tq
    krow0 = t_ctx // s_lat
    att_l = pl.pallas_call(
        functools.partial(_att_lat_kernel, past=past),
        grid=(b_lat, ATT_KV_HEADS, nqb),
        in_specs=[pl.BlockSpec((tq, gw), lambda b, h, i: (qrow0 + b * nqb + i, h)),
                  pl.BlockSpec((s_lat, ATT_HD), lambda b, h, i: (krow0 + b, kcol + h)),
                  pl.BlockSpec((s_lat, ATT_HD), lambda b, h, i: (krow0 + b, vcol + h)),
                  pl.BlockSpec((None, past, ATT_HD), lambda b, h, i: (b, 0, h)),
                  pl.BlockSpec((None, past, ATT_HD), lambda b, h, i: (b, 0, h)),
                  pl.BlockSpec((1, ATT_HD), lambda b, h, i: (0, 0)),
                  pl.BlockSpec((1, ATT_HD), lambda b, h, i: (0, 0)),
                  pl.BlockSpec((tq, ATT_HD), lambda b, h, i: (i, 0)),
                  pl.BlockSpec((tq, ATT_HD), lambda b, h, i: (i, 0)),
                  pl.BlockSpec((s_lat, ATT_HD), lambda b, h, i: (0, 0)),
                  pl.BlockSpec((s_lat, ATT_HD), lambda b, h, i: (0, 0))],
        out_specs=pl.BlockSpec((tq, gw), lambda b, h, i: (b * nqb + i, h)),
        out_shape=jax.ShapeDtypeStruct((ntok - t_ctx, ATT_KV_HEADS * gw), F32),
        scratch_shapes=[pltpu.VMEM((past + s_lat, ATT_HD), BF16), pltpu.VMEM((past + s_lat, ATT_HD), BF16)],
        compiler_params=_cparams(("arbitrary", "arbitrary", "arbitrary")),
        name="att_lat",
    )(u, u, u, cache_k, cache_v, qn, kn, cos, sin, cos, sin)
    return att_c, att_l, k_ctx, v_ctx


def _rope_tables(n_tokens):
    pos = jnp.arange(n_tokens)
    row = (pos // GRID_W).astype(F32)
    col = (pos % GRID_W).astype(F32)
    n_freq = ATT_HD // 4
    inv_freq = ROPE_THETA ** (-jnp.arange(n_freq, dtype=F32) / n_freq)
    ang_r = row[:, None] * inv_freq[None, :]
    ang_c = col[:, None] * inv_freq[None, :]
    cos = jnp.concatenate([jnp.cos(ang_r), jnp.cos(ang_r), jnp.cos(ang_c), jnp.cos(ang_c)], axis=-1)
    sin = jnp.concatenate([-jnp.sin(ang_r), jnp.sin(ang_r), -jnp.sin(ang_c), jnp.sin(ang_c)], axis=-1)
    return cos, sin


def _log_sigmoid(x):
    return jnp.minimum(x, 0.0) - jnp.log1p(jnp.exp(-jnp.abs(x)))


def _mlstm_kernel(*refs, zero_init, seq):
    if zero_init:
        (q_ref, k_ref, v_ref, o_ref, gc_ref, gt_ref, br_ref, bc_ref, ng_ref,
         out_ref, co_ref, mo_ref, h_scr, c_scr, m_scr) = refs
    else:
        (q_ref, k_ref, v_ref, o_ref, gc_ref, gt_ref, br_ref, bc_ref, ng_ref, c0_ref, m0_ref,
         out_ref, co_ref, mo_ref, h_scr, c_scr, m_scr) = refs
    nc = seq // CHUNK
    if zero_init:
        c_scr[...] = jnp.zeros(c_scr.shape, F32)
        m_scr[...] = jnp.full(m_scr.shape, M_INIT, F32)
    else:
        c_scr[...] = c0_ref[...]
        m_scr[...] = m0_ref[...]

    ti = lax.broadcasted_iota(I32, (CHUNK, CHUNK), 0)
    si = lax.broadcasted_iota(I32, (CHUNK, CHUNK), 1)
    ones_col = jnp.where(lax.broadcasted_iota(I32, (CHUNK, M_HD), 1) == 0, 1.0, 0.0).astype(BF16)
    eye_d = jnp.where(lax.broadcasted_iota(I32, (M_HD, M_HD), 0) == lax.broadcasted_iota(I32, (M_HD, M_HD), 1),
                      1.0, 0.0).astype(BF16)

    def cumsum3(tri, x, tri_left):
        hi = x.astype(BF16)
        r1 = x - hi.astype(F32)
        mid = r1.astype(BF16)
        lo = (r1 - mid.astype(F32)).astype(BF16)
        if tri_left:
            return _dg(tri, hi, _NN) + (_dg(tri, mid, _NN) + _dg(tri, lo, _NN))
        return _dg(hi, tri, _NN) + (_dg(mid, tri, _NN) + _dg(lo, tri, _NN))
    gate_i = lambda d, h: 2 * M_HEADS * d + h
    gate_f = lambda d, h: 2 * M_HEADS * d + M_HEADS + h

    def chunk(c, carry):
        ch = []
        for d in (0, 1):
            cc = c if d == 0 else nc - 1 - c
            rows = pl.ds(pl.multiple_of(cc * CHUNK, CHUNK), CHUNK)
            causal = (si <= ti) if d == 0 else (si >= ti)
            tri = jnp.where(causal, 1.0, 0.0).astype(BF16)
            tri_t = jnp.where((ti <= si) if d == 0 else (ti >= si), 1.0, 0.0).astype(BF16)
            q_all = q_ref[rows, :] * (M_HD ** -0.5)
            k_all = k_ref[rows, :]
            v_all = v_ref[rows, :]
            gcol = gc_ref[rows, :] + br_ref[...]
            grow = gt_ref[cc] + bc_ref[...]
            bcum_c_all = cumsum3(tri, _log_sigmoid(gcol), True)
            bcum_r_all = cumsum3(tri_t, _log_sigmoid(grow), False)
            for h in range(M_HEADS):
                sl = slice(h * M_HD, (h + 1) * M_HD)
                q, k, v = q_all[:, sl], k_all[:, sl], v_all[:, sl]
                ig_c = gcol[:, gate_i(d, h):gate_i(d, h) + 1]
                ig_r = grow[gate_i(d, h):gate_i(d, h) + 1, :]
                bcum_c = bcum_c_all[:, gate_f(d, h):gate_f(d, h) + 1]
                bcum_r = bcum_r_all[gate_f(d, h):gate_f(d, h) + 1, :]
                m_st = m_scr[d, h][:, 0:1]
                dmat = jnp.where(causal, bcum_c - bcum_r + ig_r, -jnp.inf)
                inter = bcum_c + m_st
                m_t = jnp.maximum(inter, jnp.max(dmat, axis=1, keepdims=True))
                b_last = bcum_c[CHUNK - 1:CHUNK, :] if d == 0 else bcum_c[0:1, :]
                g_c = b_last - bcum_c + ig_c
                m_new = jnp.maximum(b_last + m_st, jnp.max(g_c, axis=0, keepdims=True))
                ch.append(dict(d=d, h=h, rows=rows, sl=sl, qb=q.astype(BF16), kb=k.astype(BF16),
                               vx=jnp.concatenate([v.astype(BF16), ones_col], axis=1),
                               c_st=c_scr[d, h], m_t=m_t, m_new=m_new,
                               w_intra=jnp.exp(dmat - m_t), w_inter=jnp.exp(inter - m_t),
                               decay=jnp.exp(b_last + m_st - m_new), kw=(k * jnp.exp(g_c - m_new)).astype(BF16)))
        for x in ch:
            x['s_qk'] = _dg(x['qb'], x['kb'], _NT) * x['w_intra']
        for x in ch:
            x['qc'] = _dg(x['qb'], x['c_st'].astype(BF16), _NN)
        for x in ch:
            x['kw_t'] = _dg(eye_d, x['kw'], _NT).astype(BF16)
        for x in ch:
            x['sv'] = _dg(x['s_qk'].astype(BF16), x['vx'], _NN)
        for x in ch:
            x['upd'] = _dg(x['kw_t'], x['vx'], _NN)
        for x in ch:
            d, h = x['d'], x['h']
            tot = x['w_inter'] * x['qc'] + x['sv']
            den = tot[:, M_HD:M_HD + 1]
            h_scr[d, x['rows'], x['sl']] = tot[:, :M_HD] / jnp.maximum(jnp.abs(den), jnp.exp(-x['m_t']))
            c_scr[d, h] = x['decay'] * x['c_st'] + x['upd']
            m_scr[d, h] = jnp.broadcast_to(x['m_new'], (1, M_HD))
        return carry

    lax.fori_loop(0, nc, chunk, 0)
    hsum = h_scr[0] + h_scr[1]
    hn = jnp.concatenate(
        [hsum[:, h * M_HD:(h + 1) * M_HD]
         * lax.rsqrt(jnp.mean(hsum[:, h * M_HD:(h + 1) * M_HD] ** 2, axis=-1, keepdims=True) + NORM_EPS)
         for h in range(M_HEADS)], axis=1)
    out_ref[...] = hn * ng_ref[...] * _sigmoid(o_ref[...])
    co_ref[...] = c_scr[...]
    mo_ref[...] = m_scr[...]


def _mlstm(u, gcol, gt, bias, norm_g, states, row0, batch, seq):
    zero_init = states is None
    width = M_HEADS * M_HD
    qc = COL_M // width
    ngate = 4 * M_HEADS
    blk = lambda off: pl.BlockSpec((seq, width), lambda b: (row0 + b, off))
    st_c = pl.BlockSpec((None, 2, M_HEADS, M_HD, 2 * M_HD), lambda b: (b, 0, 0, 0, 0))
    st_v = pl.BlockSpec((None, 2, M_HEADS, 1, M_HD), lambda b: (b, 0, 0, 0, 0))
    in_specs = [blk(qc), blk(qc + 1), blk(qc + 2), blk(qc + 3),
                pl.BlockSpec((seq, ngate), lambda b: (row0 + b, 0)),
                pl.BlockSpec((seq // CHUNK, ngate, CHUNK), lambda b: (row0 + b, 0, 0)),
                pl.BlockSpec((1, ngate), lambda b: (0, 0)),
                pl.BlockSpec((ngate, 1), lambda b: (0, 0)),
                pl.BlockSpec((1, width), lambda b: (0, 0))]
    args = [u, u, u, u, gcol, gt, bias.reshape(1, ngate), bias.reshape(ngate, 1), norm_g]
    if not zero_init:
        in_specs += [st_c, st_v]
        args += list(states)
    return pl.pallas_call(
        functools.partial(_mlstm_kernel, zero_init=zero_init, seq=seq),
        grid=(batch,),
        in_specs=in_specs,
        out_specs=[pl.BlockSpec((seq, width), lambda b: (b, 0)), st_c, st_v],
        out_shape=[jax.ShapeDtypeStruct((batch * seq, width), F32),
                   jax.ShapeDtypeStruct((batch, 2, M_HEADS, M_HD, 2 * M_HD), F32),
                   jax.ShapeDtypeStruct((batch, 2, M_HEADS, 1, M_HD), F32)],
        scratch_shapes=[pltpu.VMEM((2, seq, width), F32), pltpu.VMEM((2, M_HEADS, M_HD, 2 * M_HD), F32),
                        pltpu.VMEM((2, M_HEADS, 1, M_HD), F32)],
        compiler_params=_cparams(("arbitrary",)),
        name="mlstm_ctx" if zero_init else "mlstm_lat",
    )(*args)


def _rwkv_kernel(*refs, zero_init, seq):
    if zero_init:
        (r_ref, k_ref, v_ref, xl_ref, xg_ref, mur_ref, muk_ref, muv_ref, mul_ref, mug_ref,
         w0_ref, wup_ref, a0_ref, aup_ref, gup_ref, kk_ref, ka_ref, rk_ref, lng_ref, lnb_ref,
         out_ref, so_ref, r_scr, v_scr, kk_scr, g_scr, bonus_scr, lw_scr, kd_scr, b_scr, y_scr, s_scr) = refs
    else:
        (r_ref, k_ref, v_ref, xl_ref, xg_ref, mur_ref, muk_ref, muv_ref, mul_ref, mug_ref,
         w0_ref, wup_ref, a0_ref, aup_ref, gup_ref, kk_ref, ka_ref, rk_ref, lng_ref, lnb_ref, s0_ref,
         out_ref, so_ref, r_scr, v_scr, kk_scr, g_scr, bonus_scr, lw_scr, kd_scr, b_scr, y_scr, s_scr) = refs
    nc = seq // CHUNK
    hd = R_HD
    nh = r_ref.shape[1] // hd

    def tshift(x_ref, mu_ref):
        x = x_ref[...]
        row = lax.broadcasted_iota(I32, x.shape, 0)
        prev = jnp.where(row == 0, 0.0, pltpu.roll(x, 1, 0))
        nxt = jnp.where(row == seq - 1, 0.0, pltpu.roll(x, seq - 1, 0))
        return x + mu_ref[...] * (0.5 * (prev + nxt) - x)

    r = tshift(r_ref, mur_ref)
    k = tshift(k_ref, muk_ref)
    v = tshift(v_ref, muv_ref)
    xl = tshift(xl_ref, mul_ref)
    xg = tshift(xg_ref, mug_ref)
    g = _dot(_sigmoid(xg), gup_ref[...])
    wlanes = nh * hd
    same_head = (lax.broadcasted_iota(I32, (wlanes, wlanes), 0) // hd
                 == lax.broadcasted_iota(I32, (wlanes, wlanes), 1) // hd)
    head_ones = jnp.where(same_head, 1.0, 0.0).astype(BF16)

    def head_sum(x):
        hi = x.astype(BF16)
        r1 = x - hi.astype(F32)
        mid = r1.astype(BF16)
        lo = (r1 - mid.astype(F32)).astype(BF16)
        return _dg(hi, head_ones, _NN) + (_dg(mid, head_ones, _NN) + _dg(lo, head_ones, _NN))

    kkp = k * kk_ref[...]
    kk = kkp * lax.rsqrt(jnp.maximum(head_sum(kkp * kkp), 1e-24))
    tw = jnp.tanh(xl[:, 0:hd])
    xa = xl[:, hd:2 * hd]
    r_scr[...] = r
    v_scr[...] = v
    kk_scr[...] = kk
    g_scr[...] = g
    bonus_scr[...] = head_sum(r * k * rk_ref[...]) * v
    for d in (0, 1):
        lw_scr[d] = -math.exp(-0.5) * _sigmoid(w0_ref[d] + _dot(tw, wup_ref[d]))
        ad = _sigmoid(a0_ref[d] + _dot(xa, aup_ref[d]))
        kd_scr[d] = k * (1.0 + (ad - 1.0) * ka_ref[...])
        b_scr[d] = kk * ad
    npair = nh // 2
    pw = 2 * hd
    zero_blk = jnp.zeros((hd, hd), F32)
    for d in (0, 1):
        for p in range(npair):
            if zero_init:
                s_scr[d, p] = jnp.zeros((pw, pw), F32)
            else:
                s_scr[d, p] = jnp.concatenate(
                    [jnp.concatenate([s0_ref[d, 2 * p], zero_blk], axis=1),
                     jnp.concatenate([zero_blk, s0_ref[d, 2 * p + 1]], axis=1)], axis=0)
    y_scr[...] = jnp.zeros(y_scr.shape, F32)

    ti = lax.broadcasted_iota(I32, (CHUNK, pw), 0)
    si = lax.broadcasted_iota(I32, (CHUNK, pw), 1) % CHUNK
    eye2 = (ti == si).astype(F32)
    tri_i = lax.broadcasted_iota(I32, (CHUNK, CHUNK), 0)
    tri_j = lax.broadcasted_iota(I32, (CHUNK, CHUNK), 1)
    first = lax.broadcasted_iota(I32, (1, pw), 1) < hd
    diag_blk = ((lax.broadcasted_iota(I32, (pw, pw), 0) < hd) == (lax.broadcasted_iota(I32, (pw, pw), 1) < hd))

    def bd(x):
        xb = x.astype(BF16)
        zero = jnp.zeros_like(xb)
        return jnp.concatenate([jnp.where(first, xb, zero), jnp.where(first, zero, xb)], axis=0)

    def chunk(c, carry):
        chains = []
        for d in (0, 1):
            cc = c if d == 0 else nc - 1 - c
            rows = pl.ds(pl.multiple_of(cc * CHUNK, CHUNK), CHUNK)
            incl = (si <= ti) if d == 0 else (si >= ti)
            strict = (si < ti) if d == 0 else (si > ti)
            lw = lw_scr[d, rows, :]
            lw_hi, lw_lo = _split(lw)
            tri = jnp.where((tri_j <= tri_i) if d == 0 else (tri_j >= tri_i), 1.0, 0.0).astype(BF16)
            lc = _dg(tri, lw_hi, _NN) + _dg(tri, lw_lo, _NN)
            l_last = lc[CHUNK - 1:CHUNK, :] if d == 0 else lc[0:1, :]
            e_neg = jnp.exp(-lc)
            e_end = jnp.exp(l_last - lc)
            vc = v_scr[rows, :]
            kdc = kd_scr[d, rows, :]
            bc = b_scr[d, rows, :]
            rt = r_scr[rows, :] * jnp.exp(lc)
            kkt = kk_scr[rows, :] * jnp.exp(lc - lw)
            kh = kdc * e_neg
            bh = bc * e_neg
            kbar = kdc * e_end
            bbar = bc * e_end
            w_end = jnp.exp(l_last)
            for p in range(npair):
                sl = slice(p * pw, (p + 1) * pw)
                chains.append(dict(
                    d=d, pair=p, sl=sl, rows=rows, incl=incl, strict=strict,
                    lhs=jnp.concatenate([kkt[:, sl], rt[:, sl]], axis=0).astype(BF16),
                    rhs=jnp.concatenate([bd(bh[:, sl]), bd(kh[:, sl])], axis=0),
                    end=jnp.concatenate([kbar[:, sl], bbar[:, sl]], axis=0),
                    v=vc[:, sl], w_end=w_end[:, sl], s0=s_scr[d, p]))
        for ch in chains:
            ch['ab'] = _dg(ch['lhs'], ch['rhs'], _NT)
        for ch in chains:
            ch['proj'] = _dg(ch['lhs'], ch['s0'].astype(BF16), _NT)
        for ch in chains:
            ab = ch['ab']
            ch['a_kb'] = jnp.where(ch['strict'], ab[:CHUNK, :pw], 0.0)
            ch['b_rb'] = jnp.where(ch['incl'], ab[CHUNK:, :pw], 0.0)
            ch['akk_brk'] = jnp.concatenate([jnp.where(ch['strict'], ab[:CHUNK, pw:], 0.0),
                                             jnp.where(ch['incl'], ab[CHUNK:, pw:], 0.0)], axis=0)
        for ch in chains:
            ch['p'] = _dg(ch['a_kb'].astype(BF16), bd(ch['a_kb']), _NN)
        for ch in chains:
            ch['abv'] = _dg(ch['akk_brk'].astype(BF16), bd(ch['v']), _NN)
        for ch in chains:
            inv = eye2 - ch['a_kb']
            ch['inv'] = inv + _dg(inv.astype(BF16), bd(ch['p']), _NN)
        span = 4
        while span < CHUNK:
            for ch in chains:
                ch['p'] = _dg(ch['p'].astype(BF16), bd(ch['p']), _NN)
            for ch in chains:
                ch['inv'] = ch['inv'] + _dg(ch['inv'].astype(BF16), bd(ch['p']), _NN)
            span *= 2
        for ch in chains:
            ch['u'] = _dg(ch['inv'].astype(BF16), bd(ch['proj'][:CHUNK] + ch['abv'][:CHUNK]), _NN)
        for ch in chains:
            y = ch['proj'][CHUNK:] + ch['abv'][CHUNK:] - _dg(ch['b_rb'].astype(BF16), bd(ch['u']), _NN)
            y_scr[ch['rows'], ch['sl']] += y
        for ch in chains:
            upd = _dot3(jnp.concatenate([ch['v'], -ch['u']], axis=0), ch['end'], _TN)
            s_scr[ch['d'], ch['pair']] = ch['s0'] * ch['w_end'] + jnp.where(diag_blk, upd, 0.0)
        return carry

    lax.fori_loop(0, nc, chunk, 0)

    y = y_scr[...]
    dev = y - head_sum(y) * (1.0 / hd)
    yn = dev * lax.rsqrt(head_sum(dev * dev) * (1.0 / hd) + GN_EPS)
    out_ref[...] = (yn * lng_ref[...] + lnb_ref[...] + bonus_scr[...]) * g_scr[...]
    for d in (0, 1):
        for p in range(npair):
            s = s_scr[d, p]
            so_ref[d, 2 * p] = s[:hd, :hd]
            so_ref[d, 2 * p + 1] = s[hd:, hd:]


def _rwkv(u, p, state, row0, batch, seq, heads_per_step):
    zero_init = state is None
    nh = heads_per_step
    wd = nh * R_HD
    nsteps = R_HEADS // nh
    width = R_HEADS * R_HD
    rc = COL_R // wd
    sec = width // wd
    lc = (COL_R + 3 * width) // LANES
    ublk = lambda off: pl.BlockSpec((seq, wd), lambda b, h: (row0 + b, rc + off + h))
    ufix = lambda blk: pl.BlockSpec((seq, LANES), lambda b, h: (row0 + b, blk))
    mblk = lambda off: pl.BlockSpec((1, wd), lambda b, h: (0, off + h))
    mfix = lambda blk: pl.BlockSpec((1, LANES), lambda b, h: (0, blk))
    vec = pl.BlockSpec((1, wd), lambda b, h: (0, h))
    in_specs = [ublk(0), ublk(sec), ublk(2 * sec), ufix(lc), ufix(lc + 1),
                mblk(0), mblk(sec), mblk(2 * sec), mfix(lc - COL_R // LANES), mfix(lc - COL_R // LANES + 1),
                pl.BlockSpec((2, 1, wd), lambda b, h: (0, 0, h)),
                pl.BlockSpec((2, R_HD, wd), lambda b, h: (0, 0, h)),
                pl.BlockSpec((2, 1, wd), lambda b, h: (0, 0, h)),
                pl.BlockSpec((2, R_HD, wd), lambda b, h: (0, 0, h)),
                pl.BlockSpec((LANES, wd), lambda b, h: (0, h)),
                vec, vec, vec, vec, vec]
    args = [u, u, u, u, u, p['mu'], p['mu'], p['mu'], p['mu'], p['mu'],
            p['w0'], p['w_up'], p['a0'], p['a_up'], p['g_up'], p['k_k'], p['k_a'], p['r_k'], p['ln_g'], p['ln_b']]
    if not zero_init:
        in_specs.append(pl.BlockSpec((None, 2, nh, R_HD, R_HD), lambda b, h: (b, 0, h, 0, 0)))
        args.append(state)
    big = lambda n: pltpu.VMEM((n, seq, wd), F32)
    return pl.pallas_call(
        functools.partial(_rwkv_kernel, zero_init=zero_init, seq=seq),
        grid=(batch, nsteps),
        in_specs=in_specs,
        out_specs=[pl.BlockSpec((seq, wd), lambda b, h: (b, h)),
                   pl.BlockSpec((None, 2, nh, R_HD, R_HD), lambda b, h: (b, 0, h, 0, 0))],
        out_shape=[jax.ShapeDtypeStruct((batch * seq, width), F32),
                   jax.ShapeDtypeStruct((batch, 2, R_HEADS, R_HD, R_HD), F32)],
        scratch_shapes=[pltpu.VMEM((seq, wd), F32)] * 5
                       + [big(2), big(2), big(2), pltpu.VMEM((seq, wd), F32),
                          pltpu.VMEM((2, nh // 2, 2 * R_HD, 2 * R_HD), F32)],
        compiler_params=_cparams(("arbitrary", "arbitrary")),
        name="rwkv_ctx" if zero_init else "rwkv_lat",
    )(*args)


def _top2_sum(a, b, c, d):
    m1, n1 = jnp.maximum(a, b), jnp.minimum(a, b)
    m2, n2 = jnp.maximum(c, d), jnp.minimum(c, d)
    return jnp.maximum(m1, m2) + jnp.maximum(jnp.minimum(m1, m2), jnp.maximum(n1, n2))


def _first_argmax(vals):
    best = functools.reduce(jnp.maximum, vals)
    idx = jnp.full(best.shape, len(vals) - 1, I32)
    for j in range(len(vals) - 2, -1, -1):
        idx = jnp.where(vals[j] == best, j, idx)
    return best, idx


def _out_kernel(attc_ref, attl_ref, mc_ref, ml_ref, rc_ref, rl_ref, xc_ref, xl_ref, w_ref, g1_ref, sh2_ref, sc2_ref,
                n2_ref, rw_ref, rb_ref, x1_ref, h2_ref, route_ref, cnt_ref, cnt_scr, *, n_ctx_tiles):
    @pl.when(pl.program_id(0) == 0)
    def _():
        cnt_scr[...] = jnp.zeros(cnt_scr.shape, F32)

    is_ctx = pl.program_id(0) < n_ctx_tiles
    both = lambda c_ref, l_ref: jnp.where(is_ctx, c_ref[...], l_ref[...])
    na = attc_ref.shape[1]
    nm = mc_ref.shape[1]
    mix = (jnp.dot(both(attc_ref, attl_ref).astype(BF16), w_ref[0:na, :], preferred_element_type=F32)
           + jnp.dot(both(mc_ref, ml_ref).astype(BF16), w_ref[na:na + nm, :], preferred_element_type=F32)
           + jnp.dot(both(rc_ref, rl_ref).astype(BF16), w_ref[na + nm:, :], preferred_element_type=F32))
    x1 = both(xc_ref, xl_ref) + g1_ref[...] * mix
    x1_ref[...] = x1
    h2 = _rms(x1, n2_ref[...]) * (1.0 + sc2_ref[...]) + sh2_ref[...]
    h2_ref[...] = h2
    logits = _dot1(rw_ref[...], h2, _NT)
    s = _sigmoid(logits)
    ssel = s + rb_ref[...]
    srow = [s[e:e + 1, :] for e in range(N_EXPERTS)]
    brow = [ssel[e:e + 1, :] for e in range(N_EXPERTS)]
    gscore = [_top2_sum(*brow[EXPERTS_PER_GROUP * g:EXPERTS_PER_GROUP * (g + 1)]) for g in range(N_EXPERT_GROUPS)]
    _, gidx = _first_argmax(gscore)
    pick = lambda rows, j: functools.reduce(
        lambda acc, g: jnp.where(gidx == g, rows[EXPERTS_PER_GROUP * g + j], acc),
        range(N_EXPERT_GROUPS - 2, -1, -1), rows[EXPERTS_PER_GROUP * (N_EXPERT_GROUPS - 1) + j])
    ing = [pick(brow, j) for j in range(EXPERTS_PER_GROUP)]
    sin_ = [pick(srow, j) for j in range(EXPERTS_PER_GROUP)]
    _, l1 = _first_argmax(ing)
    _, l2 = _first_argmax([jnp.where(l1 == j, -jnp.inf, ing[j]) for j in range(EXPERTS_PER_GROUP)])
    sel = lambda l: functools.reduce(lambda acc, j: jnp.where(l == j, sin_[j], acc),
                                     range(EXPERTS_PER_GROUP - 2, -1, -1), sin_[EXPERTS_PER_GROUP - 1])
    w1, w2 = sel(l1), sel(l2)
    tot = w1 + w2
    e1 = gidx * EXPERTS_PER_GROUP + l1
    e2 = gidx * EXPERTS_PER_GROUP + l2
    tm = e1.shape[1]
    eid = lax.broadcasted_iota(I32, (N_EXPERTS, tm), 0)
    oh1 = eid == e1
    oh2 = eid == e2
    picked = jnp.where(jnp.logical_or(oh1, oh2), 1.0, 0.0)
    earlier = jnp.where(lax.broadcasted_iota(I32, (tm, tm), 0) < lax.broadcasted_iota(I32, (tm, tm), 1), 1.0, 0.0)
    rank = cnt_scr[:, 0:1] + jnp.dot(picked.astype(BF16), earlier.astype(BF16), preferred_element_type=F32)
    pos1 = jnp.sum(jnp.where(oh1, rank, 0.0), axis=0, keepdims=True)
    pos2 = jnp.sum(jnp.where(oh2, rank, 0.0), axis=0, keepdims=True)
    cnt = cnt_scr[...] + jnp.sum(picked, axis=1, keepdims=True)
    cnt_scr[...] = cnt
    cnt_ref[...] = cnt
    zero = jnp.zeros_like(w1)
    route_ref[...] = jnp.concatenate([e1.astype(F32), e2.astype(F32), w1 / tot, w2 / tot, pos1, pos2, zero, zero],
                                     axis=0)


def _out_proj(att, m_out, r_out, x, w_out, mod3, n2g, rw_t, rb, s_lat):
    t_ctx, d = x[0].shape
    ntok = t_ctx + x[1].shape[0]
    tm = 256
    nct = t_ctx // tm
    row = lambda i: _mod_row(i * tm, t_ctx, s_lat)
    modblk = lambda j: pl.BlockSpec((None, 1, d), lambda i: (row(i), 0, j))
    pair = lambda a: [pl.BlockSpec((tm, a[0].shape[1]), lambda i: (jnp.minimum(i, nct - 1), 0)),
                      pl.BlockSpec((tm, a[1].shape[1]), lambda i: (jnp.maximum(i - nct, 0), 0))]
    return pl.pallas_call(
        functools.partial(_out_kernel, n_ctx_tiles=nct),
        grid=(ntok // tm,),
        in_specs=pair(att) + pair(m_out) + pair(r_out) + pair(x) + [
                  pl.BlockSpec(w_out.shape, lambda i: (0, 0)),
                  modblk(2), modblk(3), modblk(4),
                  pl.BlockSpec((1, d), lambda i: (0, 0)),
                  pl.BlockSpec(rw_t.shape, lambda i: (0, 0)),
                  pl.BlockSpec(rb.shape, lambda i: (0, 0))],
        out_specs=[pl.BlockSpec((tm, d), lambda i: (i, 0)),
                   pl.BlockSpec((tm, d), lambda i: (i, 0)),
                   pl.BlockSpec((8, tm), lambda i: (0, i)),
                   pl.BlockSpec((N_EXPERTS, LANES), lambda i: (0, 0))],
        out_shape=[jax.ShapeDtypeStruct((ntok, d), F32), jax.ShapeDtypeStruct((ntok, d), F32),
                   jax.ShapeDtypeStruct((8, ntok), F32), jax.ShapeDtypeStruct((N_EXPERTS, LANES), F32)],
        scratch_shapes=[pltpu.VMEM((N_EXPERTS, LANES), F32)],
        compiler_params=_cparams(("arbitrary",)),
        name="out_proj",
    )(*att, *m_out, *r_out, *x, w_out, mod3, mod3, mod3, n2g, rw_t, rb)


def _row_gather(src_hbm, idx_ref, base, dst, sem, n, unrolled):
    def start(j):
        pltpu.make_async_copy(src_hbm.at[pl.ds(idx_ref[base + j], 1), :], dst.at[pl.ds(j, 1), :], sem).start()

    if unrolled:
        for j in range(n):
            start(j)
    else:
        def body(j, c):
            start(j)
            return c
        lax.fori_loop(0, n, body, 0)


def _expert_kernel(be_ref, tok_ref, nused_ref, h_hbm, w1_ref, w3_ref, w2_ref, y_ref, xbuf, w1b, w3b, w2b, sem):
    i = pl.program_id(0)
    n_used = nused_ref[0]
    wait = lambda s: pltpu.make_async_copy(h_hbm.at[pl.ds(0, EXPERT_ROWS), :], xbuf.at[s], sem.at[s]).wait()

    @pl.when(i < n_used)
    def _():
        slot = i % 2

        @pl.when(i == 0)
        def _():
            _row_gather(h_hbm, tok_ref, 0, xbuf.at[0], sem.at[0], EXPERT_ROWS, unrolled=False)

        @pl.when(jnp.logical_or(i == 0, be_ref[i] != be_ref[jnp.maximum(i - 1, 0)]))
        def _():
            w1b[...] = w1_ref[...].astype(BF16)
            w3b[...] = w3_ref[...].astype(BF16)
            w2b[...] = w2_ref[...].astype(BF16)

        wait(slot)
        _row_gather(h_hbm, tok_ref, (i + 1) * EXPERT_ROWS, xbuf.at[1 - slot], sem.at[1 - slot], EXPERT_ROWS,
                    unrolled=True)
        xb = xbuf[slot].astype(BF16)
        a = jnp.dot(xb, w1b[...], preferred_element_type=F32)
        b = jnp.dot(xb, w3b[...], preferred_element_type=F32)
        hmid = (a * _sigmoid(a)) * b
        y_ref[...] = jnp.dot(hmid.astype(BF16), w2b[...], preferred_element_type=F32)

        @pl.when(i == n_used - 1)
        def _():
            wait(1 - slot)

    @pl.when(i >= n_used)
    def _():
        y_ref[...] = jnp.zeros(y_ref.shape, F32)


def _experts(h2, block_e, row_tok, n_used, w1, w3, w2, layer):
    ntok, d = h2.shape
    de = w1.shape[3]
    n_rows = row_tok.shape[0] - EXPERT_ROWS
    nb = n_rows // EXPERT_ROWS
    return pl.pallas_call(
        _expert_kernel,
        grid_spec=pltpu.PrefetchScalarGridSpec(
            num_scalar_prefetch=3,
            grid=(nb,),
            in_specs=[pl.BlockSpec(memory_space=pl.ANY),
                      pl.BlockSpec((None, None, d, de), lambda i, be, tok, nu: (layer, be[i], 0, 0)),
                      pl.BlockSpec((None, None, d, de), lambda i, be, tok, nu: (layer, be[i], 0, 0)),
                      pl.BlockSpec((None, None, de, d), lambda i, be, tok, nu: (layer, be[i], 0, 0))],
            out_specs=pl.BlockSpec((EXPERT_ROWS, d), lambda i, be, tok, nu: (i, 0)),
            scratch_shapes=[pltpu.VMEM((2, EXPERT_ROWS, d), F32), pltpu.VMEM((d, de), BF16),
                            pltpu.VMEM((d, de), BF16), pltpu.VMEM((de, d), BF16), pltpu.SemaphoreType.DMA((2,))]),
        out_shape=jax.ShapeDtypeStruct((n_rows, d), F32),
        compiler_params=_cparams(("arbitrary",)),
        name="experts",
    )(block_e, row_tok, n_used, h2, w1, w3, w2)


def _combine_kernel(d1_ref, d2_ref, y_hbm, x1_ref, g2_ref, gate_ref, oc_ref, ol_ref, ybuf, sem, *, n_ctx_tiles):
    tm = x1_ref.shape[0]
    i = pl.program_id(0)
    slot = i % 2

    def gather(tile, s, unrolled):
        _row_gather(y_hbm, d1_ref, tile * tm, ybuf.at[s, 0], sem.at[s], tm, unrolled)
        _row_gather(y_hbm, d2_ref, tile * tm, ybuf.at[s, 1], sem.at[s], tm, unrolled)

    def wait(s):
        pltpu.make_async_copy(y_hbm.at[pl.ds(0, tm), :], ybuf.at[s, 0], sem.at[s]).wait()
        pltpu.make_async_copy(y_hbm.at[pl.ds(0, tm), :], ybuf.at[s, 1], sem.at[s]).wait()

    @pl.when(i == 0)
    def _():
        gather(0, 0, False)

    wait(slot)
    gather(i + 1, 1 - slot, True)
    gate = gate_ref[...]
    moe = ybuf[slot, 0] * gate[:, 0:1] + ybuf[slot, 1] * gate[:, 1:2]
    new_x = x1_ref[...] + g2_ref[...] * moe

    @pl.when(i < n_ctx_tiles)
    def _():
        oc_ref[...] = new_x

    @pl.when(i >= n_ctx_tiles)
    def _():
        ol_ref[...] = new_x

    @pl.when(i == pl.num_programs(0) - 1)
    def _():
        wait(1 - slot)


def _combine(y, x1, mod3, gate, dest1, dest2, t_ctx, s_lat):
    ntok, d = x1.shape
    tm = 256
    nct = t_ctx // tm
    row = lambda i: _mod_row(i * tm, t_ctx, s_lat)
    return pl.pallas_call(
        functools.partial(_combine_kernel, n_ctx_tiles=nct),
        grid_spec=pltpu.PrefetchScalarGridSpec(
            num_scalar_prefetch=2,
            grid=(ntok // tm,),
            in_specs=[pl.BlockSpec(memory_space=pl.ANY),
                      pl.BlockSpec((tm, d), lambda i, a, b: (i, 0)),
                      pl.BlockSpec((None, 1, d), lambda i, a, b: (row(i), 0, 5)),
                      pl.BlockSpec((tm, 2), lambda i, a, b: (i, 0))],
            out_specs=[pl.BlockSpec((tm, d), lambda i, a, b: (jnp.minimum(i, nct - 1), 0)),
                       pl.BlockSpec((tm, d), lambda i, a, b: (jnp.maximum(i - nct, 0), 0))],
            scratch_shapes=[pltpu.VMEM((2, 2, tm, d), F32), pltpu.SemaphoreType.DMA((2,))]),
        out_shape=[jax.ShapeDtypeStruct((t_ctx, d), F32), jax.ShapeDtypeStruct((ntok - t_ctx, d), F32)],
        compiler_params=_cparams(("arbitrary",)),
        name="combine",
    )(dest1, dest2, y, x1, mod3, gate)


def _dispatch(route, counts, tile):
    ntok = route.shape[1]
    e = route[0:2].astype(I32)
    pos = route[4:6].astype(I32)
    gate = route[2:4].T
    counts = counts[:, 0].astype(I32)
    padded = (counts + EXPERT_ROWS - 1) // EXPERT_ROWS * EXPERT_ROWS
    pad_end = jnp.cumsum(padded)
    pad_start = pad_end - padded
    onehot = (e[:, :, None] == jnp.arange(N_EXPERTS, dtype=I32)).astype(I32)
    dest = jnp.sum(onehot * pad_start, axis=-1) + pos
    n_rows = -(-(2 * ntok) // EXPERT_ROWS) * EXPERT_ROWS + N_EXPERTS * EXPERT_ROWS
    nb = n_rows // EXPERT_ROWS
    tok = jnp.broadcast_to(jnp.arange(ntok, dtype=I32)[None, :], (2, ntok))
    row_tok = jnp.zeros((n_rows + EXPERT_ROWS,), I32).at[dest.reshape(-1)].set(tok.reshape(-1), unique_indices=True)
    blk_start = jnp.arange(nb, dtype=I32) * EXPERT_ROWS
    block_e = jnp.minimum(jnp.sum((pad_end[None, :] <= blk_start[:, None]).astype(I32), axis=1), N_EXPERTS - 1)
    n_used = pad_end[-1:] // EXPERT_ROWS
    dest = jnp.pad(dest, ((0, 0), (0, tile)))
    return row_tok, block_e, n_used, gate, dest[0], dest[1]


def kernel(x_prompt, x_sample, c, cache_attn_k, cache_attn_v, state_mlstm_C, state_mlstm_n, state_mlstm_m, state_rwkv, c_ctx, norm1_g, norm2_g, w_mod, b_mod, w_in, w_out, attn_q_norm, attn_k_norm, mlstm_i_bias, mlstm_f_bias, mlstm_norm_g, rwkv_mu, rwkv_w0, rwkv_w_up, rwkv_a0, rwkv_a_up, rwkv_g_up, rwkv_k_k, rwkv_k_a, rwkv_r_k, rwkv_ln_g, rwkv_ln_b, router_w, router_b, exp_w1, exp_w3, exp_w2):
    b_ctx, s_ctx, d = x_prompt.shape
    b_lat, s_lat, _ = x_sample.shape
    depth = w_in.shape[0]
    t_ctx = b_ctx * s_ctx
    ntok = t_ctx + b_lat * s_lat
    assert b_lat + 1 <= 8 and s_lat % 1024 == 0 and t_ctx % 1024 == 0 and t_ctx % s_lat == 0
    past = cache_attn_k.shape[2]

    x = (x_prompt.reshape(t_ctx, d), x_sample.reshape(b_lat * s_lat, d))
    c_all = jnp.zeros((8, d), F32).at[0].set(c_ctx).at[1:1 + b_lat].set(c)
    mod = _modulation(c_all, w_mod, b_mod)
    cos, sin = _rope_tables(s_lat)
    rw_t = router_w.T
    rb = router_b.reshape(N_EXPERTS, 1)
    r_width = R_HEADS * R_HD
    n_in = w_in.shape[2]
    gate_hi = COL_R + 4 * M_HEADS
    w_in_t = jnp.swapaxes(w_in, 1, 2)
    w_tail_t = jnp.concatenate([w_in_t[:, gate_hi:], w_in_t[:, COL_R:gate_hi],
                                jnp.zeros((depth, N_IN_PAD - n_in, d), F32)], axis=1).astype(BF16)

    ks, vs, cs, ns, ms, rs = [], [], [], [], [], []
    for l in range(depth):
        mod3 = mod[l].reshape(8, 1, 6 * d)
        u = _in_proj(x[0], x[1], norm1_g[l][None], mod3, w_in_t, l, w_tail_t[l], s_lat)

        ck = cache_attn_k[:, l].reshape(b_lat, past, ATT_KV_HEADS * ATT_HD)
        cv = cache_attn_v[:, l].reshape(b_lat, past, ATT_KV_HEADS * ATT_HD)
        att_c, att_l, k_ctx, v_ctx = _attention(u, attn_q_norm[l][None], attn_k_norm[l][None], ck, cv, cos, sin,
                                                b_ctx, s_ctx, b_lat, s_lat)
        ks.append(k_ctx.reshape(b_ctx, s_ctx, ATT_KV_HEADS, ATT_HD))
        vs.append(v_ctx.reshape(b_ctx, s_ctx, ATT_KV_HEADS, ATT_HD))

        gcol = u[:, COL_GATE:COL_GATE + 4 * M_HEADS]
        gt = gcol.reshape(ntok // CHUNK, CHUNK, 4 * M_HEADS).transpose(0, 2, 1)
        bias = jnp.stack([mlstm_i_bias[l], mlstm_f_bias[l]], axis=1)
        ng = mlstm_norm_g[l][None]
        m_c_out, cx_c, m_c = _mlstm(u, gcol, gt, bias, ng, None, 0, b_ctx, s_ctx)
        n_col = jnp.pad(state_mlstm_n[:, l][..., None], ((0, 0),) * 4 + ((0, M_HD - 1),))
        lat_states = (jnp.concatenate([state_mlstm_C[:, l], n_col], axis=-1),
                      jnp.broadcast_to(state_mlstm_m[:, l][..., None, None], (b_lat, 2, M_HEADS, 1, M_HD)))
        m_l_out, _, _ = _mlstm(u, gcol, gt, bias, ng, lat_states, t_ctx // s_lat, b_lat, s_lat)
        cs.append(cx_c[..., :M_HD])
        ns.append(cx_c[..., M_HD])
        ms.append(m_c[:, :, :, 0, 0])

        rp = dict(mu=rwkv_mu[l][None], w0=rwkv_w0[l].reshape(2, 1, r_width), w_up=rwkv_w_up[l],
                  a0=rwkv_a0[l].reshape(2, 1, r_width), a_up=rwkv_a_up[l], g_up=rwkv_g_up[l],
                  k_k=rwkv_k_k[l][None], k_a=rwkv_k_a[l][None], r_k=rwkv_r_k[l].reshape(1, r_width),
                  ln_g=rwkv_ln_g[l][None], ln_b=rwkv_ln_b[l][None])
        r_c_out, r_c = _rwkv(u, rp, None, 0, b_ctx, s_ctx, RWKV_HEADS_PER_STEP_CTX)
        r_l_out, _ = _rwkv(u, rp, state_rwkv[:, l], t_ctx // s_lat, b_lat, s_lat, RWKV_HEADS_PER_STEP_LAT)
        rs.append(r_c)

        x1, h2, route, counts = _out_proj((att_c, att_l), (m_c_out, m_l_out), (r_c_out, r_l_out), x,
                                          w_out[l].astype(BF16), mod3, norm2_g[l][None], rw_t, rb, s_lat)
        row_tok, block_e, n_used, gate, dest1, dest2 = _dispatch(route, counts, 256)
        y = _experts(h2, block_e, row_tok, n_used, exp_w1, exp_w3, exp_w2, l)
        x = _combine(y, x1, mod3, gate, dest1, dest2, t_ctx, s_lat)

    y_prompt = x[0].reshape(b_ctx, s_ctx, d)
    y_sample = x[1].reshape(b_lat, s_lat, d)
    return (y_prompt, y_sample, jnp.stack(ks, axis=1), jnp.stack(vs, axis=1), jnp.stack(cs, axis=1),
            jnp.stack(ns, axis=1), jnp.stack(ms, axis=1), jnp.stack(rs, axis=1))
```

```python
import functools
import math

import jax
import jax.numpy as jnp
from jax import lax
from jax.experimental import pallas as pl
from jax.experimental.pallas import tpu as pltpu

F32 = jnp.float32
BF16 = jnp.bfloat16
I32 = jnp.int32

NORM_EPS = 1e-6
GN_EPS = 64e-5
M_INIT = -1e30
GRID_W = 64
ROPE_THETA = 10000.0
ATT_HD = 128
ATT_GROUPS = 4
ATT_KV_HEADS = 2
M_HD = 128
M_HEADS = 4
R_HD = 64
R_HEADS = 8
N_EXPERTS = 16
N_EXPERT_GROUPS = 4
EXPERTS_PER_GROUP = 4
CHUNK = 64
LANES = 128
EXPERT_ROWS = 256
VMEM_LIMIT = 58 * 1024 * 1024
RWKV_HEADS_PER_STEP_CTX = 8
RWKV_HEADS_PER_STEP_LAT = 8

COL_ATT = 0
COL_M = 1536
COL_R = 3584
COL_GATE = 5376
N_IN_PAD = 5632


def _cparams(sem):
    return pltpu.CompilerParams(dimension_semantics=sem, vmem_limit_bytes=VMEM_LIMIT)


def _dot(a, b):
    return jnp.dot(a.astype(BF16), b.astype(BF16), preferred_element_type=F32)


def _dg(a, b, dims):
    return lax.dot_general(a, b, (dims, ((), ())), preferred_element_type=F32)


_NN = ((1,), (0,))
_NT = ((1,), (1,))
_TN = ((0,), (0,))


def _split(a):
    hi = a.astype(BF16)
    lo = (a - hi.astype(F32)).astype(BF16)
    return hi, lo


def _dot3(a, b, dims=_NN):
    ah, al = _split(a)
    bh, bl = _split(b)
    return _dg(ah, bh, dims) + (_dg(ah, bl, dims) + _dg(al, bh, dims))


def _dot1(a, b, dims=_NN):
    return _dg(a.astype(BF16), b.astype(BF16), dims)


def _rms(x, g):
    return x * lax.rsqrt(jnp.mean(x * x, axis=-1, keepdims=True) + NORM_EPS) * g


def _sigmoid(x):
    return 1.0 / (1.0 + jnp.exp(-x))


def _mod_kernel(c_ref, w_ref, b_ref, o_ref):
    c = c_ref[...]
    o_ref[...] = _dot(c * _sigmoid(c), w_ref[...]) + b_ref[...]


def _modulation(c_all, w_mod, b_mod):
    depth, d, n = w_mod.shape
    tn = 1024
    return pl.pallas_call(
        _mod_kernel,
        grid=(depth, n // tn),
        in_specs=[pl.BlockSpec((8, d), lambda l, j: (0, 0)),
                  pl.BlockSpec((None, d, tn), lambda l, j: (l, 0, j)),
                  pl.BlockSpec((None, 1, tn), lambda l, j: (l, 0, j))],
        out_specs=pl.BlockSpec((None, 8, tn), lambda l, j: (l, 0, j)),
        out_shape=jax.ShapeDtypeStruct((depth, 8, n), F32),
        compiler_params=_cparams(("arbitrary", "arbitrary")),
        name="modulation",
    )(c_all, w_mod, b_mod.reshape(depth, 1, n))


def _mod_row(tok0, t_ctx, s_lat):
    return jnp.where(tok0 < t_ctx, 0, 1 + (tok0 - t_ctx) // s_lat)


def _in_kernel(xc_ref, xl_ref, g_ref, sh_ref, sc_ref, wa_ref, wb_ref, o_ref, h_ref, *, n_ctx_tiles, n_head_tiles):
    i = pl.program_id(0)
    j = pl.program_id(1)

    def normalise(x_ref):
        slab = 256
        for r0 in range(0, x_ref.shape[0], slab):
            h = _rms(x_ref[r0:r0 + slab, :], g_ref[...]) * (1.0 + sc_ref[...]) + sh_ref[...]
            h_ref[r0:r0 + slab, :] = h.astype(BF16)

    @pl.when(jnp.logical_and(j == 0, i < n_ctx_tiles))
    def _():
        normalise(xc_ref)

    @pl.when(jnp.logical_and(j == 0, i >= n_ctx_tiles))
    def _():
        normalise(xl_ref)

    @pl.when(j < n_head_tiles)
    def _():
        o_ref[...] = _dg(h_ref[...], wa_ref[...].astype(BF16), _NT)

    @pl.when(j >= n_head_tiles)
    def _():
        o_ref[...] = _dg(h_ref[...], wb_ref[...], _NT)


def _in_proj(xc, xl, g1, mod3, w_in_t, layer, w_tail_t, s_lat):
    t_ctx, d = xc.shape
    ntok = t_ctx + xl.shape[0]
    tm, tn = 1024, 512
    na = COL_R // tn
    n = COL_R + w_tail_t.shape[0]
    nct = t_ctx // tm
    row = lambda i: _mod_row(i * tm, t_ctx, s_lat)
    return pl.pallas_call(
        functools.partial(_in_kernel, n_ctx_tiles=nct, n_head_tiles=na),
        grid=(ntok // tm, n // tn),
        in_specs=[pl.BlockSpec((tm, d), lambda i, j: (jnp.minimum(i, nct - 1), 0)),
                  pl.BlockSpec((tm, d), lambda i, j: (jnp.maximum(i - nct, 0), 0)),
                  pl.BlockSpec((1, d), lambda i, j: (0, 0)),
                  pl.BlockSpec((None, 1, d), lambda i, j: (row(i), 0, 0)),
                  pl.BlockSpec((None, 1, d), lambda i, j: (row(i), 0, 1)),
                  pl.BlockSpec((None, tn, d), lambda i, j: (layer, jnp.minimum(j, na - 1), 0)),
                  pl.BlockSpec((tn, d), lambda i, j: (jnp.maximum(j - na, 0), 0))],
        out_specs=pl.BlockSpec((tm, tn), lambda i, j: (i, j)),
        out_shape=jax.ShapeDtypeStruct((ntok, n), F32),
        scratch_shapes=[pltpu.VMEM((tm, d), BF16)],
        compiler_params=_cparams(("arbitrary", "arbitrary")),
        name="in_proj",
    )(xc, xl, g1, mod3, mod3, w_in_t, w_tail_t)


def _softmax_av(q, kb, vb):
    s = _dg(q.astype(BF16), kb, _NT) * (ATT_HD ** -0.5)
    p = jnp.exp(s - jnp.max(s, axis=-1, keepdims=True))
    l = jnp.sum(p, axis=-1, keepdims=True)
    return jnp.dot(p.astype(BF16), vb, preferred_element_type=F32) / l


def _att_ctx_kernel(q_ref, k_ref, v_ref, qn_ref, kn_ref, o_ref, ko_ref, vo_ref):
    k = _rms(k_ref[...], kn_ref[...])
    ko_ref[...] = k
    vo_ref[...] = v_ref[...]
    kb = k.astype(BF16)
    vb = v_ref[...].astype(BF16)
    for g in range(ATT_GROUPS):
        q = _rms(q_ref[:, g * ATT_HD:(g + 1) * ATT_HD], qn_ref[...])
        o_ref[:, g * ATT_HD:(g + 1) * ATT_HD] = _softmax_av(q, kb, vb)


def _rope(x, cos, sin):
    lane = lax.broadcasted_iota(I32, x.shape, 1)
    first = (lane % (ATT_HD // 2)) < (ATT_HD // 4)
    partner = jnp.where(first, pltpu.roll(x, ATT_HD - ATT_HD // 4, 1), pltpu.roll(x, ATT_HD // 4, 1))
    return x * cos + partner * sin


def _att_lat_kernel(q_ref, k_ref, v_ref, ck_ref, cv_ref, qn_ref, kn_ref, cosq_ref, sinq_ref, cosk_ref, sink_ref,
                    o_ref, kb_ref, vb_ref, *, past):
    @pl.when(pl.program_id(2) == 0)
    def _():
        k = _rope(_rms(k_ref[...], kn_ref[...]), cosk_ref[...], sink_ref[...])
        kb_ref[0:past, :] = ck_ref[...].astype(BF16)
        kb_ref[past:, :] = k.astype(BF16)
        vb_ref[0:past, :] = cv_ref[...].astype(BF16)
        vb_ref[past:, :] = v_ref[...].astype(BF16)

    kb = kb_ref[...]
    vb = vb_ref[...]
    for g in range(ATT_GROUPS):
        q = _rope(_rms(q_ref[:, g * ATT_HD:(g + 1) * ATT_HD], qn_ref[...]), cosq_ref[...], sinq_ref[...])
        o_ref[:, g * ATT_HD:(g + 1) * ATT_HD] = _softmax_av(q, kb, vb)


def _attention(u, qn, kn, cache_k, cache_v, cos, sin, b_ctx, s_ctx, b_lat, s_lat):
    ntok = u.shape[0]
    t_ctx = b_ctx * s_ctx
    gw = ATT_GROUPS * ATT_HD
    kcol = (ATT_KV_HEADS * gw) // ATT_HD
    vcol = kcol + ATT_KV_HEADS
    kv_spec = pl.BlockSpec((s_ctx, ATT_HD), lambda b, h: (b, h))
    att_c, k_ctx, v_ctx = pl.pallas_call(
        _att_ctx_kernel,
        grid=(b_ctx, ATT_KV_HEADS),
        in_specs=[pl.BlockSpec((s_ctx, gw), lambda b, h: (b, h)),
                  pl.BlockSpec((s_ctx, ATT_HD), lambda b, h: (b, kcol + h)),
                  pl.BlockSpec((s_ctx, ATT_HD), lambda b, h: (b, vcol + h)),
                  pl.BlockSpec((1, ATT_HD), lambda b, h: (0, 0)),
                  pl.BlockSpec((1, ATT_HD), lambda b, h: (0, 0))],
        out_specs=[pl.BlockSpec((s_ctx, gw), lambda b, h: (b, h)), kv_spec, kv_spec],
        out_shape=[jax.ShapeDtypeStruct((t_ctx, ATT_KV_HEADS * gw), F32),
                   jax.ShapeDtypeStruct((t_ctx, ATT_KV_HEADS * ATT_HD), F32),
                   jax.ShapeDtypeStruct((t_ctx, ATT_KV_HEADS * ATT_HD), F32)],
        compiler_params=_cparams(("arbitrary", "arbitrary")),
        name="att_ctx",
    )(u, u, u, qn, kn)

    tq = 256
    nqb = s_lat // tq
    past = cache_k.shape[1]
    qrow0 = t_ctx // tq
    krow0 = t_ctx // s_lat
    att_l = pl.pallas_call(
        functools.partial(_att_lat_kernel, past=past),
        grid=(b_lat, ATT_KV_HEADS, nqb),
        in_specs=[pl.BlockSpec((tq, gw), lambda b, h, i: (qrow0 + b * nqb + i, h)),
                  pl.BlockSpec((s_lat, ATT_HD), lambda b, h, i: (krow0 + b, kcol + h)),
                  pl.BlockSpec((s_lat, ATT_HD), lambda b, h, i: (krow0 + b, vcol + h)),
                  pl.BlockSpec((None, past, ATT_HD), lambda b, h, i: (b, 0, h)),
                  pl.BlockSpec((None, past, ATT_HD), lambda b, h, i: (b, 0, h)),
                  pl.BlockSpec((1, ATT_HD), lambda b, h, i: (0, 0)),
                  pl.BlockSpec((1, ATT_HD), lambda b, h, i: (0, 0)),
                  pl.BlockSpec((tq, ATT_HD), lambda b, h, i: (i, 0)),
                  pl.BlockSpec((tq, ATT_HD), lambda b, h, i: (i, 0)),
                  pl.BlockSpec((s_lat, ATT_HD), lambda b, h, i: (0, 0)),
                  pl.BlockSpec((s_lat, ATT_HD), lambda b, h, i: (0, 0))],
        out_specs=pl.BlockSpec((tq, gw), lambda b, h, i: (b * nqb + i, h)),
        out_shape=jax.ShapeDtypeStruct((ntok - t_ctx, ATT_KV_HEADS * gw), F32),
        scratch_shapes=[pltpu.VMEM((past + s_lat, ATT_HD), BF16), pltpu.VMEM((past + s_lat, ATT_HD), BF16)],
        compiler_params=_cparams(("arbitrary", "arbitrary", "arbitrary")),
        name="att_lat",
    )(u, u, u, cache_k, cache_v, qn, kn, cos, sin, cos, sin)
    return att_c, att_l, k_ctx, v_ctx


def _rope_tables(n_tokens):
    pos = jnp.arange(n_tokens)
    row = (pos // GRID_W).astype(F32)
    col = (pos % GRID_W).astype(F32)
    n_freq = ATT_HD // 4
    inv_freq = ROPE_THETA ** (-jnp.arange(n_freq, dtype=F32) / n_freq)
    ang_r = row[:, None] * inv_freq[None, :]
    ang_c = col[:, None] * inv_freq[None, :]
    cos = jnp.concatenate([jnp.cos(ang_r), jnp.cos(ang_r), jnp.cos(ang_c), jnp.cos(ang_c)], axis=-1)
    sin = jnp.concatenate([-jnp.sin(ang_r), jnp.sin(ang_r), -jnp.sin(ang_c), jnp.sin(ang_c)], axis=-1)
    return cos, sin


def _log_sigmoid(x):
    return jnp.minimum(x, 0.0) - jnp.log1p(jnp.exp(-jnp.abs(x)))


def _mlstm_kernel(*refs, zero_init, seq):
    if zero_init:
        (q_ref, k_ref, v_ref, o_ref, gc_ref, gt_ref, br_ref, bc_ref, ng_ref,
         out_ref, co_ref, mo_ref, h_scr, c_scr, m_scr) = refs
    else:
        (q_ref, k_ref, v_ref, o_ref, gc_ref, gt_ref, br_ref, bc_ref, ng_ref, c0_ref, m0_ref,
         out_ref, co_ref, mo_ref, h_scr, c_scr, m_scr) = refs
    nc = seq // CHUNK
    if zero_init:
        c_scr[...] = jnp.zeros(c_scr.shape, F32)
        m_scr[...] = jnp.full(m_scr.shape, M_INIT, F32)
    else:
        c_scr[...] = c0_ref[...]
        m_scr[...] = m0_ref[...]

    ti = lax.broadcasted_iota(I32, (CHUNK, CHUNK), 0)
    si = lax.broadcasted_iota(I32, (CHUNK, CHUNK), 1)
    ones_col = jnp.where(lax.broadcasted_iota(I32, (CHUNK, M_HD), 1) == 0, 1.0, 0.0).astype(BF16)
    eye_d = jnp.where(lax.broadcasted_iota(I32, (M_HD, M_HD), 0) == lax.broadcasted_iota(I32, (M_HD, M_HD), 1),
                      1.0, 0.0).astype(BF16)

    def cumsum3(tri, x, tri_left):
        hi = x.astype(BF16)
        r1 = x - hi.astype(F32)
        mid = r1.astype(BF16)
        lo = (r1 - mid.astype(F32)).astype(BF16)
        if tri_left:
            return _dg(tri, hi, _NN) + (_dg(tri, mid, _NN) + _dg(tri, lo, _NN))
        return _dg(hi, tri, _NN) + (_dg(mid, tri, _NN) + _dg(lo, tri, _NN))
    gate_i = lambda d, h: 2 * M_HEADS * d + h
    gate_f = lambda d, h: 2 * M_HEADS * d + M_HEADS + h

    def chunk(c, carry):
        ch = []
        for d in (0, 1):
            cc = c if d == 0 else nc - 1 - c
            rows = pl.ds(pl.multiple_of(cc * CHUNK, CHUNK), CHUNK)
            causal = (si <= ti) if d == 0 else (si >= ti)
            tri = jnp.where(causal, 1.0, 0.0).astype(BF16)
            tri_t = jnp.where((ti <= si) if d == 0 else (ti >= si), 1.0, 0.0).astype(BF16)
            q_all = q_ref[rows, :] * (M_HD ** -0.5)
            k_all = k_ref[rows, :]
            v_all = v_ref[rows, :]
            gcol = gc_ref[rows, :] + br_ref[...]
            grow = gt_ref[cc] + bc_ref[...]
            bcum_c_all = cumsum3(tri, _log_sigmoid(gcol), True)
            bcum_r_all = cumsum3(tri_t, _log_sigmoid(grow), False)
            for h in range(M_HEADS):
                sl = slice(h * M_HD, (h + 1) * M_HD)
                q, k, v = q_all[:, sl], k_all[:, sl], v_all[:, sl]
                ig_c = gcol[:, gate_i(d, h):gate_i(d, h) + 1]
                ig_r = grow[gate_i(d, h):gate_i(d, h) + 1, :]
                bcum_c = bcum_c_all[:, gate_f(d, h):gate_f(d, h) + 1]
                bcum_r = bcum_r_all[gate_f(d, h):gate_f(d, h) + 1, :]
                m_st = m_scr[d, h][:, 0:1]
                dmat = jnp.where(causal, bcum_c - bcum_r + ig_r, -jnp.inf)
                inter = bcum_c + m_st
                m_t = jnp.maximum(inter, jnp.max(dmat, axis=1, keepdims=True))
                b_last = bcum_c[CHUNK - 1:CHUNK, :] if d == 0 else bcum_c[0:1, :]
                g_c = b_last - bcum_c + ig_c
                m_new = jnp.maximum(b_last + m_st, jnp.max(g_c, axis=0, keepdims=True))
                ch.append(dict(d=d, h=h, rows=rows, sl=sl, qb=q.astype(BF16), kb=k.astype(BF16),
                               vx=jnp.concatenate([v.astype(BF16), ones_col], axis=1),
                               c_st=c_scr[d, h], m_t=m_t, m_new=m_new,
                               w_intra=jnp.exp(dmat - m_t), w_inter=jnp.exp(inter - m_t),
                               decay=jnp.exp(b_last + m_st - m_new), kw=(k * jnp.exp(g_c - m_new)).astype(BF16)))
        for x in ch:
            x['s_qk'] = _dg(x['qb'], x['kb'], _NT) * x['w_intra']
        for x in ch:
            x['qc'] = _dg(x['qb'], x['c_st'].astype(BF16), _NN)
        for x in ch:
            x['kw_t'] = _dg(eye_d, x['kw'], _NT).astype(BF16)
        for x in ch:
            x['sv'] = _dg(x['s_qk'].astype(BF16), x['vx'], _NN)
        for x in ch:
            x['upd'] = _dg(x['kw_t'], x['vx'], _NN)
        for x in ch:
            d, h = x['d'], x['h']
            tot = x['w_inter'] * x['qc'] + x['sv']
            den = tot[:, M_HD:M_HD + 1]
            h_scr[d, x['rows'], x['sl']] = tot[:, :M_HD] / jnp.maximum(jnp.abs(den), jnp.exp(-x['m_t']))
            c_scr[d, h] = x['decay'] * x['c_st'] + x['upd']
            m_scr[d, h] = jnp.broadcast_to(x['m_new'], (1, M_HD))
        return carry

    lax.fori_loop(0, nc, chunk, 0)
    hsum = h_scr[0] + h_scr[1]
    hn = jnp.concatenate(
        [hsum[:, h * M_HD:(h + 1) * M_HD]
         * lax.rsqrt(jnp.mean(hsum[:, h * M_HD:(h + 1) * M_HD] ** 2, axis=-1, keepdims=True) + NORM_EPS)
         for h in range(M_HEADS)], axis=1)
    out_ref[...] = hn * ng_ref[...] * _sigmoid(o_ref[...])
    co_ref[...] = c_scr[...]
    mo_ref[...] = m_scr[...]


def _mlstm(u, gcol, gt, bias, norm_g, states, row0, batch, seq):
    zero_init = states is None
    width = M_HEADS * M_HD
    qc = COL_M // width
    ngate = 4 * M_HEADS
    blk = lambda off: pl.BlockSpec((seq, width), lambda b: (row0 + b, off))
    st_c = pl.BlockSpec((None, 2, M_HEADS, M_HD, 2 * M_HD), lambda b: (b, 0, 0, 0, 0))
    st_v = pl.BlockSpec((None, 2, M_HEADS, 1, M_HD), lambda b: (b, 0, 0, 0, 0))
    in_specs = [blk(qc), blk(qc + 1), blk(qc + 2), blk(qc + 3),
                pl.BlockSpec((seq, ngate), lambda b: (row0 + b, 0)),
                pl.BlockSpec((seq // CHUNK, ngate, CHUNK), lambda b: (row0 + b, 0, 0)),
                pl.BlockSpec((1, ngate), lambda b: (0, 0)),
                pl.BlockSpec((ngate, 1), lambda b: (0, 0)),
                pl.BlockSpec((1, width), lambda b: (0, 0))]
    args = [u, u, u, u, gcol, gt, bias.reshape(1, ngate), bias.reshape(ngate, 1), norm_g]
    if not zero_init:
        in_specs += [st_c, st_v]
        args += list(states)
    return pl.pallas_call(
        functools.partial(_mlstm_kernel, zero_init=zero_init, seq=seq),
        grid=(batch,),
        in_specs=in_specs,
        out_specs=[pl.BlockSpec((seq, width), lambda b: (b, 0)), st_c, st_v],
        out_shape=[jax.ShapeDtypeStruct((batch * seq, width), F32),
                   jax.ShapeDtypeStruct((batch, 2, M_HEADS, M_HD, 2 * M_HD), F32),
                   jax.ShapeDtypeStruct((batch, 2, M_HEADS, 1, M_HD), F32)],
        scratch_shapes=[pltpu.VMEM((2, seq, width), F32), pltpu.VMEM((2, M_HEADS, M_HD, 2 * M_HD), F32),
                        pltpu.VMEM((2, M_HEADS, 1, M_HD), F32)],
        compiler_params=_cparams(("arbitrary",)),
        name="mlstm_ctx" if zero_init else "mlstm_lat",
    )(*args)


def _rwkv_kernel(*refs, zero_init, seq):
    if zero_init:
        (r_ref, k_ref, v_ref, xl_ref, xg_ref, mur_ref, muk_ref, muv_ref, mul_ref, mug_ref,
         w0_ref, wup_ref, a0_ref, aup_ref, gup_ref, kk_ref, ka_ref, rk_ref, lng_ref, lnb_ref,
         out_ref, so_ref, r_scr, v_scr, kk_scr, g_scr, bonus_scr, lw_scr, kd_scr, b_scr, y_scr, s_scr) = refs
    else:
        (r_ref, k_ref, v_ref, xl_ref, xg_ref, mur_ref, muk_ref, muv_ref, mul_ref, mug_ref,
         w0_ref, wup_ref, a0_ref, aup_ref, gup_ref, kk_ref, ka_ref, rk_ref, lng_ref, lnb_ref, s0_ref,
         out_ref, so_ref, r_scr, v_scr, kk_scr, g_scr, bonus_scr, lw_scr, kd_scr, b_scr, y_scr, s_scr) = refs
    nc = seq // CHUNK
    hd = R_HD
    nh = r_ref.shape[1] // hd

    def tshift(x_ref, mu_ref):
        x = x_ref[...]
        row = lax.broadcasted_iota(I32, x.shape, 0)
        prev = jnp.where(row == 0, 0.0, pltpu.roll(x, 1, 0))
        nxt = jnp.where(row == seq - 1, 0.0, pltpu.roll(x, seq - 1, 0))
        return x + mu_ref[...] * (0.5 * (prev + nxt) - x)

    r = tshift(r_ref, mur_ref)
    k = tshift(k_ref, muk_ref)
    v = tshift(v_ref, muv_ref)
    xl = tshift(xl_ref, mul_ref)
    xg = tshift(xg_ref, mug_ref)
    g = _dot(_sigmoid(xg), gup_ref[...])
    wlanes = nh * hd
    same_head = (lax.broadcasted_iota(I32, (wlanes, wlanes), 0) // hd
                 == lax.broadcasted_iota(I32, (wlanes, wlanes), 1) // hd)
    head_ones = jnp.where(same_head, 1.0, 0.0).astype(BF16)

    def head_sum(x):
        hi = x.astype(BF16)
        r1 = x - hi.astype(F32)
        mid = r1.astype(BF16)
        lo = (r1 - mid.astype(F32)).astype(BF16)
        return _dg(hi, head_ones, _NN) + (_dg(mid, head_ones, _NN) + _dg(lo, head_ones, _NN))

    kkp = k * kk_ref[...]
    kk = kkp * lax.rsqrt(jnp.maximum(head_sum(kkp * kkp), 1e-24))
    tw = jnp.tanh(xl[:, 0:hd])
    xa = xl[:, hd:2 * hd]
    r_scr[...] = r
    v_scr[...] = v
    kk_scr[...] = kk
    g_scr[...] = g
    bonus_scr[...] = head_sum(r * k * rk_ref[...]) * v
    for d in (0, 1):
        lw_scr[d] = -math.exp(-0.5) * _sigmoid(w0_ref[d] + _dot(tw, wup_ref[d]))
        ad = _sigmoid(a0_ref[d] + _dot(xa, aup_ref[d]))
        kd_scr[d] = k * (1.0 + (ad - 1.0) * ka_ref[...])
        b_scr[d] = kk * ad
    npair = nh // 2
    pw = 2 * hd
    zero_blk = jnp.zeros((hd, hd), F32)
    for d in (0, 1):
        for p in range(npair):
            if zero_init:
                s_scr[d, p] = jnp.zeros((pw, pw), F32)
            else:
                s_scr[d, p] = jnp.concatenate(
                    [jnp.concatenate([s0_ref[d, 2 * p], zero_blk], axis=1),
                     jnp.concatenate([zero_blk, s0_ref[d, 2 * p + 1]], axis=1)], axis=0)
    y_scr[...] = jnp.zeros(y_scr.shape, F32)

    ti = lax.broadcasted_iota(I32, (CHUNK, pw), 0)
    si = lax.broadcasted_iota(I32, (CHUNK, pw), 1) % CHUNK
    eye2 = (ti == si).astype(F32)
    tri_i = lax.broadcasted_iota(I32, (CHUNK, CHUNK), 0)
    tri_j = lax.broadcasted_iota(I32, (CHUNK, CHUNK), 1)
    first = lax.broadcasted_iota(I32, (1, pw), 1) < hd
    diag_blk = ((lax.broadcasted_iota(I32, (pw, pw), 0) < hd) == (lax.broadcasted_iota(I32, (pw, pw), 1) < hd))

    def bd(x):
        xb = x.astype(BF16)
        zero = jnp.zeros_like(xb)
        return jnp.concatenate([jnp.where(first, xb, zero), jnp.where(first, zero, xb)], axis=0)

    def chunk(c, carry):
        chains = []
        for d in (0, 1):
            cc = c if d == 0 else nc - 1 - c
            rows = pl.ds(pl.multiple_of(cc * CHUNK, CHUNK), CHUNK)
            incl = (si <= ti) if d == 0 else (si >= ti)
            strict = (si < ti) if d == 0 else (si > ti)
            lw = lw_scr[d, rows, :]
            lw_hi, lw_lo = _split(lw)
            tri = jnp.where((tri_j <= tri_i) if d == 0 else (tri_j >= tri_i), 1.0, 0.0).astype(BF16)
            lc = _dg(tri, lw_hi, _NN) + _dg(tri, lw_lo, _NN)
            l_last = lc[CHUNK - 1:CHUNK, :] if d == 0 else lc[0:1, :]
            e_neg = jnp.exp(-lc)
            e_end = jnp.exp(l_last - lc)
            vc = v_scr[rows, :]
            kdc = kd_scr[d, rows, :]
            bc = b_scr[d, rows, :]
            rt = r_scr[rows, :] * jnp.exp(lc)
            kkt = kk_scr[rows, :] * jnp.exp(lc - lw)
            kh = kdc * e_neg
            bh = bc * e_neg
            kbar = kdc * e_end
            bbar = bc * e_end
            w_end = jnp.exp(l_last)
            for p in range(npair):
                sl = slice(p * pw, (p + 1) * pw)
                chains.append(dict(
                    d=d, pair=p, sl=sl, rows=rows, incl=incl, strict=strict,
                    lhs=jnp.concatenate([kkt[:, sl], rt[:, sl]], axis=0).astype(BF16),
                    rhs=jnp.concatenate([bd(bh[:, sl]), bd(kh[:, sl])], axis=0),
                    end=jnp.concatenate([kbar[:, sl], bbar[:, sl]], axis=0),
                    v=vc[:, sl], w_end=w_end[:, sl], s0=s_scr[d, p]))
        for ch in chains:
            ch['ab'] = _dg(ch['lhs'], ch['rhs'], _NT)
        for ch in chains:
            ch['proj'] = _dg(ch['lhs'], ch['s0'].astype(BF16), _NT)
        for ch in chains:
            ab = ch['ab']
            ch['a_kb'] = jnp.where(ch['strict'], ab[:CHUNK, :pw], 0.0)
            ch['b_rb'] = jnp.where(ch['incl'], ab[CHUNK:, :pw], 0.0)
            ch['akk_brk'] = jnp.concatenate([jnp.where(ch['strict'], ab[:CHUNK, pw:], 0.0),
                                             jnp.where(ch['incl'], ab[CHUNK:, pw:], 0.0)], axis=0)
        for ch in chains:
            ch['p'] = _dg(ch['a_kb'].astype(BF16), bd(ch['a_kb']), _NN)
        for ch in chains:
            ch['abv'] = _dg(ch['akk_brk'].astype(BF16), bd(ch['v']), _NN)
        for ch in chains:
            inv = eye2 - ch['a_kb']
            ch['inv'] = inv + _dg(inv.astype(BF16), bd(ch['p']), _NN)
        span = 4
        while span < CHUNK:
            for ch in chains:
                ch['p'] = _dg(ch['p'].astype(BF16), bd(ch['p']), _NN)
            for ch in chains:
                ch['inv'] = ch['inv'] + _dg(ch['inv'].astype(BF16), bd(ch['p']), _NN)
            span *= 2
        for ch in chains:
            ch['u'] = _dg(ch['inv'].astype(BF16), bd(ch['proj'][:CHUNK] + ch['abv'][:CHUNK]), _NN)
        for ch in chains:
            y = ch['proj'][CHUNK:] + ch['abv'][CHUNK:] - _dg(ch['b_rb'].astype(BF16), bd(ch['u']), _NN)
            y_scr[ch['rows'], ch['sl']] += y
        for ch in chains:
            upd = _dot3(jnp.concatenate([ch['v'], -ch['u']], axis=0), ch['end'], _TN)
            s_scr[ch['d'], ch['pair']] = ch['s0'] * ch['w_end'] + jnp.where(diag_blk, upd, 0.0)
        return carry

    lax.fori_loop(0, nc, chunk, 0)

    y = y_scr[...]
    dev = y - head_sum(y) * (1.0 / hd)
    yn = dev * lax.rsqrt(head_sum(dev * dev) * (1.0 / hd) + GN_EPS)
    out_ref[...] = (yn * lng_ref[...] + lnb_ref[...] + bonus_scr[...]) * g_scr[...]
    for d in (0, 1):
        for p in range(npair):
            s = s_scr[d, p]
            so_ref[d, 2 * p] = s[:hd, :hd]
            so_ref[d, 2 * p + 1] = s[hd:, hd:]


def _rwkv(u, p, state, row0, batch, seq, heads_per_step):
    zero_init = state is None
    nh = heads_per_step
    wd = nh * R_HD
    nsteps = R_HEADS // nh
    width = R_HEADS * R_HD
    rc = COL_R // wd
    sec = width // wd
    lc = (COL_R + 3 * width) // LANES
    ublk = lambda off: pl.BlockSpec((seq, wd), lambda b, h: (row0 + b, rc + off + h))
    ufix = lambda blk: pl.BlockSpec((seq, LANES), lambda b, h: (row0 + b, blk))
    mblk = lambda off: pl.BlockSpec((1, wd), lambda b, h: (0, off + h))
    mfix = lambda blk: pl.BlockSpec((1, LANES), lambda b, h: (0, blk))
    vec = pl.BlockSpec((1, wd), lambda b, h: (0, h))
    in_specs = [ublk(0), ublk(sec), ublk(2 * sec), ufix(lc), ufix(lc + 1),
                mblk(0), mblk(sec), mblk(2 * sec), mfix(lc - COL_R // LANES), mfix(lc - COL_R // LANES + 1),
                pl.BlockSpec((2, 1, wd), lambda b, h: (0, 0, h)),
                pl.BlockSpec((2, R_HD, wd), lambda b, h: (0, 0, h)),
                pl.BlockSpec((2, 1, wd), lambda b, h: (0, 0, h)),
                pl.BlockSpec((2, R_HD, wd), lambda b, h: (0, 0, h)),
                pl.BlockSpec((LANES, wd), lambda b, h: (0, h)),
                vec, vec, vec, vec, vec]
    args = [u, u, u, u, u, p['mu'], p['mu'], p['mu'], p['mu'], p['mu'],
            p['w0'], p['w_up'], p['a0'], p['a_up'], p['g_up'], p['k_k'], p['k_a'], p['r_k'], p['ln_g'], p['ln_b']]
    if not zero_init:
        in_specs.append(pl.BlockSpec((None, 2, nh, R_HD, R_HD), lambda b, h: (b, 0, h, 0, 0)))
        args.append(state)
    big = lambda n: pltpu.VMEM((n, seq, wd), F32)
    return pl.pallas_call(
        functools.partial(_rwkv_kernel, zero_init=zero_init, seq=seq),
        grid=(batch, nsteps),
        in_specs=in_specs,
        out_specs=[pl.BlockSpec((seq, wd), lambda b, h: (b, h)),
                   pl.BlockSpec((None, 2, nh, R_HD, R_HD), lambda b, h: (b, 0, h, 0, 0))],
        out_shape=[jax.ShapeDtypeStruct((batch * seq, width), F32),
                   jax.ShapeDtypeStruct((batch, 2, R_HEADS, R_HD, R_HD), F32)],
        scratch_shapes=[pltpu.VMEM((seq, wd), F32)] * 5
                       + [big(2), big(2), big(2), pltpu.VMEM((seq, wd), F32),
                          pltpu.VMEM((2, nh // 2, 2 * R_HD, 2 * R_HD), F32)],
        compiler_params=_cparams(("arbitrary", "arbitrary")),
        name="rwkv_ctx" if zero_init else "rwkv_lat",
    )(*args)


def _top2_sum(a, b, c, d):
    m1, n1 = jnp.maximum(a, b), jnp.minimum(a, b)
    m2, n2 = jnp.maximum(c, d), jnp.minimum(c, d)
    return jnp.maximum(m1, m2) + jnp.maximum(jnp.minimum(m1, m2), jnp.maximum(n1, n2))


def _first_argmax(vals):
    best = functools.reduce(jnp.maximum, vals)
    idx = jnp.full(best.shape, len(vals) - 1, I32)
    for j in range(len(vals) - 2, -1, -1):
        idx = jnp.where(vals[j] == best, j, idx)
    return best, idx


def _out_kernel(attc_ref, attl_ref, mc_ref, ml_ref, rc_ref, rl_ref, xc_ref, xl_ref, w_ref, g1_ref, sh2_ref, sc2_ref,
                n2_ref, rw_ref, rb_ref, x1_ref, h2_ref, route_ref, cnt_ref, cnt_scr, *, n_ctx_tiles):
    @pl.when(pl.program_id(0) == 0)
    def _():
        cnt_scr[...] = jnp.zeros(cnt_scr.shape, F32)

    is_ctx = pl.program_id(0) < n_ctx_tiles
    both = lambda c_ref, l_ref: jnp.where(is_ctx, c_ref[...], l_ref[...])
    na = attc_ref.shape[1]
    nm = mc_ref.shape[1]
    mix = (jnp.dot(both(attc_ref, attl_ref).astype(BF16), w_ref[0:na, :], preferred_element_type=F32)
           + jnp.dot(both(mc_ref, ml_ref).astype(BF16), w_ref[na:na + nm, :], preferred_element_type=F32)
           + jnp.dot(both(rc_ref, rl_ref).astype(BF16), w_ref[na + nm:, :], preferred_element_type=F32))
    x1 = both(xc_ref, xl_ref) + g1_ref[...] * mix
    x1_ref[...] = x1
    h2 = _rms(x1, n2_ref[...]) * (1.0 + sc2_ref[...]) + sh2_ref[...]
    h2_ref[...] = h2
    logits = _dot1(rw_ref[...], h2, _NT)
    s = _sigmoid(logits)
    ssel = s + rb_ref[...]
    srow = [s[e:e + 1, :] for e in range(N_EXPERTS)]
    brow = [ssel[e:e + 1, :] for e in range(N_EXPERTS)]
    gscore = [_top2_sum(*brow[EXPERTS_PER_GROUP * g:EXPERTS_PER_GROUP * (g + 1)]) for g in range(N_EXPERT_GROUPS)]
    _, gidx = _first_argmax(gscore)
    pick = lambda rows, j: functools.reduce(
        lambda acc, g: jnp.where(gidx == g, rows[EXPERTS_PER_GROUP * g + j], acc),
        range(N_EXPERT_GROUPS - 2, -1, -1), rows[EXPERTS_PER_GROUP * (N_EXPERT_GROUPS - 1) + j])
    ing = [pick(brow, j) for j in range(EXPERTS_PER_GROUP)]
    sin_ = [pick(srow, j) for j in range(EXPERTS_PER_GROUP)]
    _, l1 = _first_argmax(ing)
    _, l2 = _first_argmax([jnp.where(l1 == j, -jnp.inf, ing[j]) for j in range(EXPERTS_PER_GROUP)])
    sel = lambda l: functools.reduce(lambda acc, j: jnp.where(l == j, sin_[j], acc),
                                     range(EXPERTS_PER_GROUP - 2, -1, -1), sin_[EXPERTS_PER_GROUP - 1])
    w1, w2 = sel(l1), sel(l2)
    tot = w1 + w2
    e1 = gidx * EXPERTS_PER_GROUP + l1
    e2 = gidx * EXPERTS_PER_GROUP + l2
    tm = e1.shape[1]
    eid = lax.broadcasted_iota(I32, (N_EXPERTS, tm), 0)
    oh1 = eid == e1
    oh2 = eid == e2
    picked = jnp.where(jnp.logical_or(oh1, oh2), 1.0, 0.0)
    earlier = jnp.where(lax.broadcasted_iota(I32, (tm, tm), 0) < lax.broadcasted_iota(I32, (tm, tm), 1), 1.0, 0.0)
    rank = cnt_scr[:, 0:1] + jnp.dot(picked.astype(BF16), earlier.astype(BF16), preferred_element_type=F32)
    pos1 = jnp.sum(jnp.where(oh1, rank, 0.0), axis=0, keepdims=True)
    pos2 = jnp.sum(jnp.where(oh2, rank, 0.0), axis=0, keepdims=True)
    cnt = cnt_scr[...] + jnp.sum(picked, axis=1, keepdims=True)
    cnt_scr[...] = cnt
    cnt_ref[...] = cnt
    zero = jnp.zeros_like(w1)
    route_ref[...] = jnp.concatenate([e1.astype(F32), e2.astype(F32), w1 / tot, w2 / tot, pos1, pos2, zero, zero],
                                     axis=0)


def _out_proj(att, m_out, r_out, x, w_out, mod3, n2g, rw_t, rb, s_lat):
    t_ctx, d = x[0].shape
    ntok = t_ctx + x[1].shape[0]
    tm = 256
    nct = t_ctx // tm
    row = lambda i: _mod_row(i * tm, t_ctx, s_lat)
    modblk = lambda j: pl.BlockSpec((None, 1, d), lambda i: (row(i), 0, j))
    pair = lambda a: [pl.BlockSpec((tm, a[0].shape[1]), lambda i: (jnp.minimum(i, nct - 1), 0)),
                      pl.BlockSpec((tm, a[1].shape[1]), lambda i: (jnp.maximum(i - nct, 0), 0))]
    return pl.pallas_call(
        functools.partial(_out_kernel, n_ctx_tiles=nct),
        grid=(ntok // tm,),
        in_specs=pair(att) + pair(m_out) + pair(r_out) + pair(x) + [
                  pl.BlockSpec(w_out.shape, lambda i: (0, 0)),
                  modblk(2), modblk(3), modblk(4),
                  pl.BlockSpec((1, d), lambda i: (0, 0)),
                  pl.BlockSpec(rw_t.shape, lambda i: (0, 0)),
                  pl.BlockSpec(rb.shape, lambda i: (0, 0))],
        out_specs=[pl.BlockSpec((tm, d), lambda i: (i, 0)),
                   pl.BlockSpec((tm, d), lambda i: (i, 0)),
                   pl.BlockSpec((8, tm), lambda i: (0, i)),
                   pl.BlockSpec((N_EXPERTS, LANES), lambda i: (0, 0))],
        out_shape=[jax.ShapeDtypeStruct((ntok, d), F32), jax.ShapeDtypeStruct((ntok, d), F32),
                   jax.ShapeDtypeStruct((8, ntok), F32), jax.ShapeDtypeStruct((N_EXPERTS, LANES), F32)],
        scratch_shapes=[pltpu.VMEM((N_EXPERTS, LANES), F32)],
        compiler_params=_cparams(("arbitrary",)),
        name="out_proj",
    )(*att, *m_out, *r_out, *x, w_out, mod3, mod3, mod3, n2g, rw_t, rb)


def _row_gather(src_hbm, idx_ref, base, dst, sem, n, unrolled, lo=0):
    def start(j):
        pltpu.make_async_copy(src_hbm.at[pl.ds(idx_ref[base + j], 1), :], dst.at[pl.ds(j, 1), :], sem).start()

    if unrolled:
        for j in range(lo, n):
            start(j)
    else:
        def body(j, c):
            start(j)
            return c
        lax.fori_loop(lo, n, body, 0)


def _expert_kernel(be_ref, tok_ref, nused_ref, h_hbm, w1_ref, w3_ref, w2_ref, y_ref, xbuf, w1b, w3b, w2b, sem):
    i = pl.program_id(0)
    n_used = nused_ref[0]
    wait = lambda s: pltpu.make_async_copy(h_hbm.at[pl.ds(0, EXPERT_ROWS), :], xbuf.at[s], sem.at[s]).wait()

    @pl.when(i < n_used)
    def _():
        slot = i % 2

        @pl.when(i == 0)
        def _():
            _row_gather(h_hbm, tok_ref, 0, xbuf.at[0], sem.at[0], EXPERT_ROWS, unrolled=False)

        @pl.when(jnp.logical_or(i == 0, be_ref[i] != be_ref[jnp.maximum(i - 1, 0)]))
        def _():
            w1b[...] = w1_ref[...].astype(BF16)
            w3b[...] = w3_ref[...].astype(BF16)
            w2b[...] = w2_ref[...].astype(BF16)

        wait(slot)
        de = w1b.shape[1]
        half = de // 2
        groups = 6
        per = EXPERT_ROWS // groups

        def next_rows(gi):
            hi = EXPERT_ROWS if gi == groups - 1 else (gi + 1) * per
            _row_gather(h_hbm, tok_ref, (i + 1) * EXPERT_ROWS, xbuf.at[1 - slot], sem.at[1 - slot], hi,
                        unrolled=True, lo=gi * per)

        xb = xbuf[slot].astype(BF16)
        hmid = []
        for n in range(2):
            cols = slice(n * half, (n + 1) * half)
            next_rows(2 * n)
            a = jnp.dot(xb, w1b[:, cols], preferred_element_type=F32)
            next_rows(2 * n + 1)
            b = jnp.dot(xb, w3b[:, cols], preferred_element_type=F32)
            hmid.append(((a * _sigmoid(a)) * b).astype(BF16))
        next_rows(4)
        y = jnp.dot(hmid[0], w2b[0:half, :], preferred_element_type=F32)
        next_rows(5)
        y_ref[...] = y + jnp.dot(hmid[1], w2b[half:, :], preferred_element_type=F32)

        @pl.when(i == n_used - 1)
        def _():
            wait(1 - slot)

    @pl.when(i >= n_used)
    def _():
        y_ref[...] = jnp.zeros(y_ref.shape, F32)


def _experts(h2, block_e, row_tok, n_used, w1, w3, w2, layer):
    ntok, d = h2.shape
    de = w1.shape[3]
    n_rows = row_tok.shape[0] - EXPERT_ROWS
    nb = n_rows // EXPERT_ROWS
    return pl.pallas_call(
        _expert_kernel,
        grid_spec=pltpu.PrefetchScalarGridSpec(
            num_scalar_prefetch=3,
            grid=(nb,),
            in_specs=[pl.BlockSpec(memory_space=pl.ANY),
                      pl.BlockSpec((None, None, d, de), lambda i, be, tok, nu: (layer, be[i], 0, 0)),
                      pl.BlockSpec((None, None, d, de), lambda i, be, tok, nu: (layer, be[i], 0, 0)),
                      pl.BlockSpec((None, None, de, d), lambda i, be, tok, nu: (layer, be[i], 0, 0))],
            out_specs=pl.BlockSpec((EXPERT_ROWS, d), lambda i, be, tok, nu: (i, 0)),
            scratch_shapes=[pltpu.VMEM((2, EXPERT_ROWS, d), F32), pltpu.VMEM((d, de), BF16),
                            pltpu.VMEM((d, de), BF16), pltpu.VMEM((de, d), BF16), pltpu.SemaphoreType.DMA((2,))]),
        out_shape=jax.ShapeDtypeStruct((n_rows, d), F32),
        compiler_params=_cparams(("arbitrary",)),
        name="experts",
    )(block_e, row_tok, n_used, h2, w1, w3, w2)


def _combine_kernel(d1_ref, d2_ref, y_hbm, x1_ref, g2_ref, gate_ref, oc_ref, ol_ref, ybuf, sem, *, n_ctx_tiles):
    tm = x1_ref.shape[0]
    i = pl.program_id(0)
    slot = i % 2

    def gather(tile, s, unrolled):
        _row_gather(y_hbm, d1_ref, tile * tm, ybuf.at[s, 0], sem.at[s], tm, unrolled)
        _row_gather(y_hbm, d2_ref, tile * tm, ybuf.at[s, 1], sem.at[s], tm, unrolled)

    def wait(s):
        pltpu.make_async_copy(y_hbm.at[pl.ds(0, tm), :], ybuf.at[s, 0], sem.at[s]).wait()
        pltpu.make_async_copy(y_hbm.at[pl.ds(0, tm), :], ybuf.at[s, 1], sem.at[s]).wait()

    @pl.when(i == 0)
    def _():
        gather(0, 0, False)

    wait(slot)
    gather(i + 1, 1 - slot, True)
    gate = gate_ref[...]
    moe = ybuf[slot, 0] * gate[:, 0:1] + ybuf[slot, 1] * gate[:, 1:2]
    new_x = x1_ref[...] + g2_ref[...] * moe

    @pl.when(i < n_ctx_tiles)
    def _():
        oc_ref[...] = new_x

    @pl.when(i >= n_ctx_tiles)
    def _():
        ol_ref[...] = new_x

    @pl.when(i == pl.num_programs(0) - 1)
    def _():
        wait(1 - slot)


def _combine(y, x1, mod3, gate, dest1, dest2, t_ctx, s_lat):
    ntok, d = x1.shape
    tm = 256
    nct = t_ctx // tm
    row = lambda i: _mod_row(i * tm, t_ctx, s_lat)
    return pl.pallas_call(
        functools.partial(_combine_kernel, n_ctx_tiles=nct),
        grid_spec=pltpu.PrefetchScalarGridSpec(
            num_scalar_prefetch=2,
            grid=(ntok // tm,),
            in_specs=[pl.BlockSpec(memory_space=pl.ANY),
                      pl.BlockSpec((tm, d), lambda i, a, b: (i, 0)),
                      pl.BlockSpec((None, 1, d), lambda i, a, b: (row(i), 0, 5)),
                      pl.BlockSpec((tm, 2), lambda i, a, b: (i, 0))],
            out_specs=[pl.BlockSpec((tm, d), lambda i, a, b: (jnp.minimum(i, nct - 1), 0)),
                       pl.BlockSpec((tm, d), lambda i, a, b: (jnp.maximum(i - nct, 0), 0))],
            scratch_shapes=[pltpu.VMEM((2, 2, tm, d), F32), pltpu.SemaphoreType.DMA((2,))]),
        out_shape=[jax.ShapeDtypeStruct((t_ctx, d), F32), jax.ShapeDtypeStruct((ntok - t_ctx, d), F32)],
        compiler_params=_cparams(("arbitrary",)),
        name="combine",
    )(dest1, dest2, y, x1, mod3, gate)


def _dispatch(route, counts, tile):
    ntok = route.shape[1]
    e = route[0:2].astype(I32)
    pos = route[4:6].astype(I32)
    gate = route[2:4].T
    counts = counts[:, 0].astype(I32)
    padded = (counts + EXPERT_ROWS - 1) // EXPERT_ROWS * EXPERT_ROWS
    pad_end = jnp.cumsum(padded)
    pad_start = pad_end - padded
    onehot = (e[:, :, None] == jnp.arange(N_EXPERTS, dtype=I32)).astype(I32)
    dest = jnp.sum(onehot * pad_start, axis=-1) + pos
    n_rows = -(-(2 * ntok) // EXPERT_ROWS) * EXPERT_ROWS + N_EXPERTS * EXPERT_ROWS
    nb = n_rows // EXPERT_ROWS
    tok = jnp.broadcast_to(jnp.arange(ntok, dtype=I32)[None, :], (2, ntok))
    row_tok = jnp.zeros((n_rows + EXPERT_ROWS,), I32).at[dest.reshape(-1)].set(tok.reshape(-1), unique_indices=True)
    blk_start = jnp.arange(nb, dtype=I32) * EXPERT_ROWS
    block_e = jnp.minimum(jnp.sum((pad_end[None, :] <= blk_start[:, None]).astype(I32), axis=1), N_EXPERTS - 1)
    n_used = pad_end[-1:] // EXPERT_ROWS
    dest = jnp.pad(dest, ((0, 0), (0, tile)))
    return row_tok, block_e, n_used, gate, dest[0], dest[1]


def kernel(x_prompt, x_sample, c, cache_attn_k, cache_attn_v, state_mlstm_C, state_mlstm_n, state_mlstm_m, state_rwkv, c_ctx, norm1_g, norm2_g, w_mod, b_mod, w_in, w_out, attn_q_norm, attn_k_norm, mlstm_i_bias, mlstm_f_bias, mlstm_norm_g, rwkv_mu, rwkv_w0, rwkv_w_up, rwkv_a0, rwkv_a_up, rwkv_g_up, rwkv_k_k, rwkv_k_a, rwkv_r_k, rwkv_ln_g, rwkv_ln_b, router_w, router_b, exp_w1, exp_w3, exp_w2):
    b_ctx, s_ctx, d = x_prompt.shape
    b_lat, s_lat, _ = x_sample.shape
    depth = w_in.shape[0]
    t_ctx = b_ctx * s_ctx
    ntok = t_ctx + b_lat * s_lat
    assert b_lat + 1 <= 8 and s_lat % 1024 == 0 and t_ctx % 1024 == 0 and t_ctx % s_lat == 0
    past = cache_attn_k.shape[2]

    x = (x_prompt.reshape(t_ctx, d), x_sample.reshape(b_lat * s_lat, d))
    c_all = jnp.zeros((8, d), F32).at[0].set(c_ctx).at[1:1 + b_lat].set(c)
    mod = _modulation(c_all, w_mod, b_mod)
    cos, sin = _rope_tables(s_lat)
    rw_t = router_w.T
    rb = router_b.reshape(N_EXPERTS, 1)
    r_width = R_HEADS * R_HD
    n_in = w_in.shape[2]
    gate_hi = COL_R + 4 * M_HEADS
    w_in_t = jnp.swapaxes(w_in, 1, 2)
    w_tail_t = jnp.concatenate([w_in_t[:, gate_hi:], w_in_t[:, COL_R:gate_hi],
                                jnp.zeros((depth, N_IN_PAD - n_in, d), F32)], axis=1).astype(BF16)

    ks, vs, cs, ns, ms, rs = [], [], [], [], [], []
    for l in range(depth):
        mod3 = mod[l].reshape(8, 1, 6 * d)
        u = _in_proj(x[0], x[1], norm1_g[l][None], mod3, w_in_t, l, w_tail_t[l], s_lat)

        ck = cache_attn_k[:, l].reshape(b_lat, past, ATT_KV_HEADS * ATT_HD)
        cv = cache_attn_v[:, l].reshape(b_lat, past, ATT_KV_HEADS * ATT_HD)
        att_c, att_l, k_ctx, v_ctx = _attention(u, attn_q_norm[l][None], attn_k_norm[l][None], ck, cv, cos, sin,
                                                b_ctx, s_ctx, b_lat, s_lat)
        ks.append(k_ctx.reshape(b_ctx, s_ctx, ATT_KV_HEADS, ATT_HD))
        vs.append(v_ctx.reshape(b_ctx, s_ctx, ATT_KV_HEADS, ATT_HD))

        gcol = u[:, COL_GATE:COL_GATE + 4 * M_HEADS]
        gt = gcol.reshape(ntok // CHUNK, CHUNK, 4 * M_HEADS).transpose(0, 2, 1)
        bias = jnp.stack([mlstm_i_bias[l], mlstm_f_bias[l]], axis=1)
        ng = mlstm_norm_g[l][None]
        m_c_out, cx_c, m_c = _mlstm(u, gcol, gt, bias, ng, None, 0, b_ctx, s_ctx)
        n_col = jnp.pad(state_mlstm_n[:, l][..., None], ((0, 0),) * 4 + ((0, M_HD - 1),))
        lat_states = (jnp.concatenate([state_mlstm_C[:, l], n_col], axis=-1),
                      jnp.broadcast_to(state_mlstm_m[:, l][..., None, None], (b_lat, 2, M_HEADS, 1, M_HD)))
        m_l_out, _, _ = _mlstm(u, gcol, gt, bias, ng, lat_states, t_ctx // s_lat, b_lat, s_lat)
        cs.append(cx_c[..., :M_HD])
        ns.append(cx_c[..., M_HD])
        ms.append(m_c[:, :, :, 0, 0])

        rp = dict(mu=rwkv_mu[l][None], w0=rwkv_w0[l].reshape(2, 1, r_width), w_up=rwkv_w_up[l],
                  a0=rwkv_a0[l].reshape(2, 1, r_width), a_up=rwkv_a_up[l], g_up=rwkv_g_up[l],
                  k_k=rwkv_k_k[l][None], k_a=rwkv_k_a[l][None], r_k=rwkv_r_k[l].reshape(1, r_width),
                  ln_g=rwkv_ln_g[l][None], ln_b=rwkv_ln_b[l][None])
        r_c_out, r_c = _rwkv(u, rp, None, 0, b_ctx, s_ctx, RWKV_HEADS_PER_STEP_CTX)
        r_l_out, _ = _rwkv(u, rp, state_rwkv[:, l], t_ctx // s_lat, b_lat, s_lat, RWKV_HEADS_PER_STEP_LAT)
        rs.append(r_c)

        x1, h2, route, counts = _out_proj((att_c, att_l), (m_c_out, m_l_out), (r_c_out, r_l_out), x,
                                          w_out[l].astype(BF16), mod3, norm2_g[l][None], rw_t, rb, s_lat)
        row_tok, block_e, n_used, gate, dest1, dest2 = _dispatch(route, counts, 256)
        y = _experts(h2, block_e, row_tok, n_used, exp_w1, exp_w3, exp_w2, l)
        x = _combine(y, x1, mod3, gate, dest1, dest2, t_ctx, s_lat)

    y_prompt = x[0].reshape(b_ctx, s_ctx, d)
    y_sample = x[1].reshape(b_lat, s_lat, d)
    return (y_prompt, y_sample, jnp.stack(ks, axis=1), jnp.stack(vs, axis=1), jnp.stack(cs, axis=1),
            jnp.stack(ns, axis=1), jnp.stack(ms, axis=1), jnp.stack(rs, axis=1))
```

```python
import functools
import math

import jax
import jax.numpy as jnp
from jax import lax
from jax.experimental import pallas as pl
from jax.experimental.pallas import tpu as pltpu

F32 = jnp.float32
BF16 = jnp.bfloat16
I32 = jnp.int32
U32 = jnp.uint32

NORM_EPS = 1e-6
GN_EPS = 64e-5
M_INIT = -1e30
GRID_W = 64
ROPE_THETA = 10000.0
ATT_HD = 128
ATT_GROUPS = 4
ATT_KV_HEADS = 2
M_HD = 128
M_HEADS = 4
R_HD = 64
R_HEADS = 8
N_EXPERTS = 16
N_EXPERT_GROUPS = 4
EXPERTS_PER_GROUP = 4
CHUNK = 64
LANES = 128
EXPERT_ROWS = 256
VMEM_LIMIT = 58 * 1024 * 1024
RWKV_HEADS_PER_STEP_CTX = 8
RWKV_HEADS_PER_STEP_LAT = 8

COL_ATT = 0
COL_M = 1536
COL_R = 3584
COL_GATE = 5376
N_IN_PAD = 5632


def _cparams(sem):
    return pltpu.CompilerParams(dimension_semantics=sem, vmem_limit_bytes=VMEM_LIMIT)


def _dot(a, b):
    return jnp.dot(a.astype(BF16), b.astype(BF16), preferred_element_type=F32)


def _dg(a, b, dims):
    return lax.dot_general(a, b, (dims, ((), ())), preferred_element_type=F32)


_NN = ((1,), (0,))
_NT = ((1,), (1,))
_TN = ((0,), (0,))


def _split(a):
    hi = a.astype(BF16)
    lo = (a - hi.astype(F32)).astype(BF16)
    return hi, lo


def _dot3(a, b, dims=_NN):
    ah, al = _split(a)
    bh, bl = _split(b)
    return _dg(ah, bh, dims) + (_dg(ah, bl, dims) + _dg(al, bh, dims))


def _dot1(a, b, dims=_NN):
    return _dg(a.astype(BF16), b.astype(BF16), dims)


def _rms(x, g):
    return x * lax.rsqrt(jnp.mean(x * x, axis=-1, keepdims=True) + NORM_EPS) * g


def _pack_bf16_pairs(x):
    half = x.shape[1] // 2
    bits = pltpu.bitcast(x.astype(BF16).astype(F32), U32)
    return (bits[:, :half] >> 16) | (bits[:, half:] & jnp.uint32(0xFFFF0000))


def _unpack_bf16_pairs(words):
    return pltpu.bitcast(words << 16, F32), pltpu.bitcast(words & jnp.uint32(0xFFFF0000), F32)


def _sigmoid(x):
    return 1.0 / (1.0 + jnp.exp(-x))


def _mod_kernel(c_ref, w_ref, b_ref, o_ref):
    c = c_ref[...]
    o_ref[...] = _dot(c * _sigmoid(c), w_ref[...]) + b_ref[...]


def _modulation(c_all, w_mod, b_mod):
    depth, d, n = w_mod.shape
    tn = 1024
    return pl.pallas_call(
        _mod_kernel,
        grid=(depth, n // tn),
        in_specs=[pl.BlockSpec((8, d), lambda l, j: (0, 0)),
                  pl.BlockSpec((None, d, tn), lambda l, j: (l, 0, j)),
                  pl.BlockSpec((None, 1, tn), lambda l, j: (l, 0, j))],
        out_specs=pl.BlockSpec((None, 8, tn), lambda l, j: (l, 0, j)),
        out_shape=jax.ShapeDtypeStruct((depth, 8, n), F32),
        compiler_params=_cparams(("arbitrary", "arbitrary")),
        name="modulation",
    )(c_all, w_mod, b_mod.reshape(depth, 1, n))


def _mod_row(tok0, t_ctx, s_lat):
    return jnp.where(tok0 < t_ctx, 0, 1 + (tok0 - t_ctx) // s_lat)


def _in_kernel(xc_ref, xl_ref, g_ref, sh_ref, sc_ref, wa_ref, wb_ref, o_ref, h_ref, *, n_ctx_tiles, n_head_tiles):
    i = pl.program_id(0)
    j = pl.program_id(1)

    def normalise(x_ref):
        slab = 256
        for r0 in range(0, x_ref.shape[0], slab):
            h = _rms(x_ref[r0:r0 + slab, :], g_ref[...]) * (1.0 + sc_ref[...]) + sh_ref[...]
            h_ref[r0:r0 + slab, :] = h.astype(BF16)

    @pl.when(jnp.logical_and(j == 0, i < n_ctx_tiles))
    def _():
        normalise(xc_ref)

    @pl.when(jnp.logical_and(j == 0, i >= n_ctx_tiles))
    def _():
        normalise(xl_ref)

    @pl.when(j < n_head_tiles)
    def _():
        o_ref[...] = _dg(h_ref[...], wa_ref[...].astype(BF16), _NT)

    @pl.when(j >= n_head_tiles)
    def _():
        o_ref[...] = _dg(h_ref[...], wb_ref[...], _NT)


def _in_proj(xc, xl, g1, mod3, w_in_t, layer, w_tail_t, s_lat):
    t_ctx, d = xc.shape
    ntok = t_ctx + xl.shape[0]
    tm, tn = 1024, 512
    na = COL_R // tn
    n = COL_R + w_tail_t.shape[0]
    nct = t_ctx // tm
    row = lambda i: _mod_row(i * tm, t_ctx, s_lat)
    return pl.pallas_call(
        functools.partial(_in_kernel, n_ctx_tiles=nct, n_head_tiles=na),
        grid=(ntok // tm, n // tn),
        in_specs=[pl.BlockSpec((tm, d), lambda i, j: (jnp.minimum(i, nct - 1), 0)),
                  pl.BlockSpec((tm, d), lambda i, j: (jnp.maximum(i - nct, 0), 0)),
                  pl.BlockSpec((1, d), lambda i, j: (0, 0)),
                  pl.BlockSpec((None, 1, d), lambda i, j: (row(i), 0, 0)),
                  pl.BlockSpec((None, 1, d), lambda i, j: (row(i), 0, 1)),
                  pl.BlockSpec((None, tn, d), lambda i, j: (layer, jnp.minimum(j, na - 1), 0)),
                  pl.BlockSpec((tn, d), lambda i, j: (jnp.maximum(j - na, 0), 0))],
        out_specs=pl.BlockSpec((tm, tn), lambda i, j: (i, j)),
        out_shape=jax.ShapeDtypeStruct((ntok, n), F32),
        scratch_shapes=[pltpu.VMEM((tm, d), BF16)],
        compiler_params=_cparams(("arbitrary", "arbitrary")),
        name="in_proj",
    )(xc, xl, g1, mod3, mod3, w_in_t, w_tail_t)


def _softmax_av(q, kb, vb):
    s = _dg(q.astype(BF16), kb, _NT) * (ATT_HD ** -0.5)
    p = jnp.exp(s - jnp.max(s, axis=-1, keepdims=True))
    l = jnp.sum(p, axis=-1, keepdims=True)
    return jnp.dot(p.astype(BF16), vb, preferred_element_type=F32) / l


def _att_ctx_kernel(q_ref, k_ref, v_ref, qn_ref, kn_ref, o_ref, ko_ref, vo_ref):
    k = _rms(k_ref[...], kn_ref[...])
    ko_ref[...] = k
    vo_ref[...] = v_ref[...]
    kb = k.astype(BF16)
    vb = v_ref[...].astype(BF16)
    for g in range(ATT_GROUPS):
        q = _rms(q_ref[:, g * ATT_HD:(g + 1) * ATT_HD], qn_ref[...])
        o_ref[:, g * ATT_HD:(g + 1) * ATT_HD] = _softmax_av(q, kb, vb)


def _rope(x, cos, sin):
    lane = lax.broadcasted_iota(I32, x.shape, 1)
    first = (lane % (ATT_HD // 2)) < (ATT_HD // 4)
    partner = jnp.where(first, pltpu.roll(x, ATT_HD - ATT_HD // 4, 1), pltpu.roll(x, ATT_HD // 4, 1))
    return x * cos + partner * sin


def _att_lat_kernel(q_ref, k_ref, v_ref, ck_ref, cv_ref, qn_ref, kn_ref, cosq_ref, sinq_ref, cosk_ref, sink_ref,
                    o_ref, kb_ref, vb_ref, *, past):
    @pl.when(pl.program_id(2) == 0)
    def _():
        k = _rope(_rms(k_ref[...], kn_ref[...]), cosk_ref[...], sink_ref[...])
        kb_ref[0:past, :] = ck_ref[...].astype(BF16)
        kb_ref[past:, :] = k.astype(BF16)
        vb_ref[0:past, :] = cv_ref[...].astype(BF16)
        vb_ref[past:, :] = v_ref[...].astype(BF16)

    kb = kb_ref[...]
    vb = vb_ref[...]
    for g in range(ATT_GROUPS):
        q = _rope(_rms(q_ref[:, g * ATT_HD:(g + 1) * ATT_HD], qn_ref[...]), cosq_ref[...], sinq_ref[...])
        o_ref[:, g * ATT_HD:(g + 1) * ATT_HD] = _softmax_av(q, kb, vb)


def _attention(u, qn, kn, cache_k, cache_v, cos, sin, b_ctx, s_ctx, b_lat, s_lat):
    ntok = u.shape[0]
    t_ctx = b_ctx * s_ctx
    gw = ATT_GROUPS * ATT_HD
    kcol = (ATT_KV_HEADS * gw) // ATT_HD
    vcol = kcol + ATT_KV_HEADS
    kv_spec = pl.BlockSpec((s_ctx, ATT_HD), lambda b, h: (b, h))
    att_c, k_ctx, v_ctx = pl.pallas_call(
        _att_ctx_kernel,
        grid=(b_ctx, ATT_KV_HEADS),
        in_specs=[pl.BlockSpec((s_ctx, gw), lambda b, h: (b, h)),
                  pl.BlockSpec((s_ctx, ATT_HD), lambda b, h: (b, kcol + h)),
                  pl.BlockSpec((s_ctx, ATT_HD), lambda b, h: (b, vcol + h)),
                  pl.BlockSpec((1, ATT_HD), lambda b, h: (0, 0)),
                  pl.BlockSpec((1, ATT_HD), lambda b, h: (0, 0))],
        out_specs=[pl.BlockSpec((s_ctx, gw), lambda b, h: (b, h)), kv_spec, kv_spec],
        out_shape=[jax.ShapeDtypeStruct((t_ctx, ATT_KV_HEADS * gw), F32),
                   jax.ShapeDtypeStruct((t_ctx, ATT_KV_HEADS * ATT_HD), F32),
                   jax.ShapeDtypeStruct((t_ctx, ATT_KV_HEADS * ATT_HD), F32)],
        compiler_params=_cparams(("arbitrary", "arbitrary")),
        name="att_ctx",
    )(u, u, u, qn, kn)

    tq = 256
    nqb = s_lat // tq
    past = cache_k.shape[1]
    qrow0 = t_ctx // tq
    krow0 = t_ctx // s_lat
    att_l = pl.pallas_call(
        functools.partial(_att_lat_kernel, past=past),
        grid=(b_lat, ATT_KV_HEADS, nqb),
        in_specs=[pl.BlockSpec((tq, gw), lambda b, h, i: (qrow0 + b * nqb + i, h)),
                  pl.BlockSpec((s_lat, ATT_HD), lambda b, h, i: (krow0 + b, kcol + h)),
                  pl.BlockSpec((s_lat, ATT_HD), lambda b, h, i: (krow0 + b, vcol + h)),
                  pl.BlockSpec((None, past, ATT_HD), lambda b, h, i: (b, 0, h)),
                  pl.BlockSpec((None, past, ATT_HD), lambda b, h, i: (b, 0, h)),
                  pl.BlockSpec((1, ATT_HD), lambda b, h, i: (0, 0)),
                  pl.BlockSpec((1, ATT_HD), lambda b, h, i: (0, 0)),
                  pl.BlockSpec((tq, ATT_HD), lambda b, h, i: (i, 0)),
                  pl.BlockSpec((tq, ATT_HD), lambda b, h, i: (i, 0)),
                  pl.BlockSpec((s_lat, ATT_HD), lambda b, h, i: (0, 0)),
                  pl.BlockSpec((s_lat, ATT_HD), lambda b, h, i: (0, 0))],
        out_specs=pl.BlockSpec((tq, gw), lambda b, h, i: (b * nqb + i, h)),
        out_shape=jax.ShapeDtypeStruct((ntok - t_ctx, ATT_KV_HEADS * gw), F32),
        scratch_shapes=[pltpu.VMEM((past + s_lat, ATT_HD), BF16), pltpu.VMEM((past + s_lat, ATT_HD), BF16)],
        compiler_params=_cparams(("arbitrary", "arbitrary", "arbitrary")),
        name="att_lat",
    )(u, u, u, cache_k, cache_v, qn, kn, cos, sin, cos, sin)
    return att_c, att_l, k_ctx, v_ctx


def _rope_tables(n_tokens):
    pos = jnp.arange(n_tokens)
    row = (pos // GRID_W).astype(F32)
    col = (pos % GRID_W).astype(F32)
    n_freq = ATT_HD // 4
    inv_freq = ROPE_THETA ** (-jnp.arange(n_freq, dtype=F32) / n_freq)
    ang_r = row[:, None] * inv_freq[None, :]
    ang_c = col[:, None] * inv_freq[None, :]
    cos = jnp.concatenate([jnp.cos(ang_r), jnp.cos(ang_r), jnp.cos(ang_c), jnp.cos(ang_c)], axis=-1)
    sin = jnp.concatenate([-jnp.sin(ang_r), jnp.sin(ang_r), -jnp.sin(ang_c), jnp.sin(ang_c)], axis=-1)
    return cos, sin


def _log_sigmoid(x):
    return jnp.minimum(x, 0.0) - jnp.log1p(jnp.exp(-jnp.abs(x)))


def _mlstm_kernel(*refs, zero_init, seq):
    if zero_init:
        (q_ref, k_ref, v_ref, o_ref, gc_ref, gt_ref, br_ref, bc_ref, ng_ref,
         out_ref, co_ref, mo_ref, h_scr, c_scr, m_scr) = refs
    else:
        (q_ref, k_ref, v_ref, o_ref, gc_ref, gt_ref, br_ref, bc_ref, ng_ref, c0_ref, m0_ref,
         out_ref, co_ref, mo_ref, h_scr, c_scr, m_scr) = refs
    nc = seq // CHUNK
    if zero_init:
        c_scr[...] = jnp.zeros(c_scr.shape, F32)
        m_scr[...] = jnp.full(m_scr.shape, M_INIT, F32)
    else:
        c_scr[...] = c0_ref[...]
        m_scr[...] = m0_ref[...]

    ti = lax.broadcasted_iota(I32, (CHUNK, CHUNK), 0)
    si = lax.broadcasted_iota(I32, (CHUNK, CHUNK), 1)
    ones_col = jnp.where(lax.broadcasted_iota(I32, (CHUNK, M_HD), 1) == 0, 1.0, 0.0).astype(BF16)
    eye_d = jnp.where(lax.broadcasted_iota(I32, (M_HD, M_HD), 0) == lax.broadcasted_iota(I32, (M_HD, M_HD), 1),
                      1.0, 0.0).astype(BF16)

    def cumsum3(tri, x, tri_left):
        hi = x.astype(BF16)
        r1 = x - hi.astype(F32)
        mid = r1.astype(BF16)
        lo = (r1 - mid.astype(F32)).astype(BF16)
        if tri_left:
            return _dg(tri, hi, _NN) + (_dg(tri, mid, _NN) + _dg(tri, lo, _NN))
        return _dg(hi, tri, _NN) + (_dg(mid, tri, _NN) + _dg(lo, tri, _NN))
    gate_i = lambda d, h: 2 * M_HEADS * d + h
    gate_f = lambda d, h: 2 * M_HEADS * d + M_HEADS + h

    def chunk(c, carry):
        ch = []
        for d in (0, 1):
            cc = c if d == 0 else nc - 1 - c
            rows = pl.ds(pl.multiple_of(cc * CHUNK, CHUNK), CHUNK)
            causal = (si <= ti) if d == 0 else (si >= ti)
            tri = jnp.where(causal, 1.0, 0.0).astype(BF16)
            tri_t = jnp.where((ti <= si) if d == 0 else (ti >= si), 1.0, 0.0).astype(BF16)
            q_all = q_ref[rows, :] * (M_HD ** -0.5)
            k_all = k_ref[rows, :]
            v_all = v_ref[rows, :]
            gcol = gc_ref[rows, :] + br_ref[...]
            grow = gt_ref[cc] + bc_ref[...]
            bcum_c_all = cumsum3(tri, _log_sigmoid(gcol), True)
            bcum_r_all = cumsum3(tri_t, _log_sigmoid(grow), False)
            for h in range(M_HEADS):
                sl = slice(h * M_HD, (h + 1) * M_HD)
                q, k, v = q_all[:, sl], k_all[:, sl], v_all[:, sl]
                ig_c = gcol[:, gate_i(d, h):gate_i(d, h) + 1]
                ig_r = grow[gate_i(d, h):gate_i(d, h) + 1, :]
                bcum_c = bcum_c_all[:, gate_f(d, h):gate_f(d, h) + 1]
                bcum_r = bcum_r_all[gate_f(d, h):gate_f(d, h) + 1, :]
                m_st = m_scr[d, h][:, 0:1]
                dmat = jnp.where(causal, bcum_c - bcum_r + ig_r, -jnp.inf)
                inter = bcum_c + m_st
                m_t = jnp.maximum(inter, jnp.max(dmat, axis=1, keepdims=True))
                b_last = bcum_c[CHUNK - 1:CHUNK, :] if d == 0 else bcum_c[0:1, :]
                g_c = b_last - bcum_c + ig_c
                m_new = jnp.maximum(b_last + m_st, jnp.max(g_c, axis=0, keepdims=True))
                ch.append(dict(d=d, h=h, rows=rows, sl=sl, qb=q.astype(BF16), kb=k.astype(BF16),
                               vx=jnp.concatenate([v.astype(BF16), ones_col], axis=1),
                               c_st=c_scr[d, h], m_t=m_t, m_new=m_new,
                               w_intra=jnp.exp(dmat - m_t), w_inter=jnp.exp(inter - m_t),
                               decay=jnp.exp(b_last + m_st - m_new), kw=(k * jnp.exp(g_c - m_new)).astype(BF16)))
        for x in ch:
            x['s_qk'] = _dg(x['qb'], x['kb'], _NT) * x['w_intra']
        for x in ch:
            x['qc'] = _dg(x['qb'], x['c_st'].astype(BF16), _NN)
        for x in ch:
            x['kw_t'] = _dg(eye_d, x['kw'], _NT).astype(BF16)
        for x in ch:
            x['sv'] = _dg(x['s_qk'].astype(BF16), x['vx'], _NN)
        for x in ch:
            x['upd'] = _dg(x['kw_t'], x['vx'], _NN)
        for x in ch:
            d, h = x['d'], x['h']
            tot = x['w_inter'] * x['qc'] + x['sv']
            den = tot[:, M_HD:M_HD + 1]
            h_scr[d, x['rows'], x['sl']] = tot[:, :M_HD] / jnp.maximum(jnp.abs(den), jnp.exp(-x['m_t']))
            c_scr[d, h] = x['decay'] * x['c_st'] + x['upd']
            m_scr[d, h] = jnp.broadcast_to(x['m_new'], (1, M_HD))
        return carry

    lax.fori_loop(0, nc, chunk, 0)
    hsum = h_scr[0] + h_scr[1]
    hn = jnp.concatenate(
        [hsum[:, h * M_HD:(h + 1) * M_HD]
         * lax.rsqrt(jnp.mean(hsum[:, h * M_HD:(h + 1) * M_HD] ** 2, axis=-1, keepdims=True) + NORM_EPS)
         for h in range(M_HEADS)], axis=1)
    out_ref[...] = hn * ng_ref[...] * _sigmoid(o_ref[...])
    co_ref[...] = c_scr[...]
    mo_ref[...] = m_scr[...]


def _mlstm(u, gcol, gt, bias, norm_g, states, row0, batch, seq):
    zero_init = states is None
    width = M_HEADS * M_HD
    qc = COL_M // width
    ngate = 4 * M_HEADS
    blk = lambda off: pl.BlockSpec((seq, width), lambda b: (row0 + b, off))
    st_c = pl.BlockSpec((None, 2, M_HEADS, M_HD, 2 * M_HD), lambda b: (b, 0, 0, 0, 0))
    st_v = pl.BlockSpec((None, 2, M_HEADS, 1, M_HD), lambda b: (b, 0, 0, 0, 0))
    in_specs = [blk(qc), blk(qc + 1), blk(qc + 2), blk(qc + 3),
                pl.BlockSpec((seq, ngate), lambda b: (row0 + b, 0)),
                pl.BlockSpec((seq // CHUNK, ngate, CHUNK), lambda b: (row0 + b, 0, 0)),
                pl.BlockSpec((1, ngate), lambda b: (0, 0)),
                pl.BlockSpec((ngate, 1), lambda b: (0, 0)),
                pl.BlockSpec((1, width), lambda b: (0, 0))]
    args = [u, u, u, u, gcol, gt, bias.reshape(1, ngate), bias.reshape(ngate, 1), norm_g]
    if not zero_init:
        in_specs += [st_c, st_v]
        args += list(states)
    return pl.pallas_call(
        functools.partial(_mlstm_kernel, zero_init=zero_init, seq=seq),
        grid=(batch,),
        in_specs=in_specs,
        out_specs=[pl.BlockSpec((seq, width), lambda b: (b, 0)), st_c, st_v],
        out_shape=[jax.ShapeDtypeStruct((batch * seq, width), F32),
                   jax.ShapeDtypeStruct((batch, 2, M_HEADS, M_HD, 2 * M_HD), F32),
                   jax.ShapeDtypeStruct((batch, 2, M_HEADS, 1, M_HD), F32)],
        scratch_shapes=[pltpu.VMEM((2, seq, width), F32), pltpu.VMEM((2, M_HEADS, M_HD, 2 * M_HD), F32),
                        pltpu.VMEM((2, M_HEADS, 1, M_HD), F32)],
        compiler_params=_cparams(("arbitrary",)),
        name="mlstm_ctx" if zero_init else "mlstm_lat",
    )(*args)


def _rwkv_kernel(*refs, zero_init, seq):
    if zero_init:
        (r_ref, k_ref, v_ref, xl_ref, xg_ref, mur_ref, muk_ref, muv_ref, mul_ref, mug_ref,
         w0_ref, wup_ref, a0_ref, aup_ref, gup_ref, kk_ref, ka_ref, rk_ref, lng_ref, lnb_ref,
         out_ref, so_ref, r_scr, v_scr, kk_scr, g_scr, bonus_scr, lw_scr, kd_scr, b_scr, y_scr, s_scr) = refs
    else:
        (r_ref, k_ref, v_ref, xl_ref, xg_ref, mur_ref, muk_ref, muv_ref, mul_ref, mug_ref,
         w0_ref, wup_ref, a0_ref, aup_ref, gup_ref, kk_ref, ka_ref, rk_ref, lng_ref, lnb_ref, s0_ref,
         out_ref, so_ref, r_scr, v_scr, kk_scr, g_scr, bonus_scr, lw_scr, kd_scr, b_scr, y_scr, s_scr) = refs
    nc = seq // CHUNK
    hd = R_HD
    nh = r_ref.shape[1] // hd

    def tshift(x_ref, mu_ref):
        x = x_ref[...]
        row = lax.broadcasted_iota(I32, x.shape, 0)
        prev = jnp.where(row == 0, 0.0, pltpu.roll(x, 1, 0))
        nxt = jnp.where(row == seq - 1, 0.0, pltpu.roll(x, seq - 1, 0))
        return x + mu_ref[...] * (0.5 * (prev + nxt) - x)

    r = tshift(r_ref, mur_ref)
    k = tshift(k_ref, muk_ref)
    v = tshift(v_ref, muv_ref)
    xl = tshift(xl_ref, mul_ref)
    xg = tshift(xg_ref, mug_ref)
    g = _dot(_sigmoid(xg), gup_ref[...])
    wlanes = nh * hd
    same_head = (lax.broadcasted_iota(I32, (wlanes, wlanes), 0) // hd
                 == lax.broadcasted_iota(I32, (wlanes, wlanes), 1) // hd)
    head_ones = jnp.where(same_head, 1.0, 0.0).astype(BF16)

    def head_sum(x):
        hi = x.astype(BF16)
        r1 = x - hi.astype(F32)
        mid = r1.astype(BF16)
        lo = (r1 - mid.astype(F32)).astype(BF16)
        return _dg(hi, head_ones, _NN) + (_dg(mid, head_ones, _NN) + _dg(lo, head_ones, _NN))

    kkp = k * kk_ref[...]
    kk = kkp * lax.rsqrt(jnp.maximum(head_sum(kkp * kkp), 1e-24))
    tw = jnp.tanh(xl[:, 0:hd])
    xa = xl[:, hd:2 * hd]
    r_scr[...] = r
    v_scr[...] = v
    kk_scr[...] = kk
    g_scr[...] = g
    bonus_scr[...] = head_sum(r * k * rk_ref[...]) * v
    for d in (0, 1):
        lw_scr[d] = -math.exp(-0.5) * _sigmoid(w0_ref[d] + _dot(tw, wup_ref[d]))
        ad = _sigmoid(a0_ref[d] + _dot(xa, aup_ref[d]))
        kd_scr[d] = k * (1.0 + (ad - 1.0) * ka_ref[...])
        b_scr[d] = kk * ad
    npair = nh // 2
    pw = 2 * hd
    zero_blk = jnp.zeros((hd, hd), F32)
    for d in (0, 1):
        for p in range(npair):
            if zero_init:
                s_scr[d, p] = jnp.zeros((pw, pw), F32)
            else:
                s_scr[d, p] = jnp.concatenate(
                    [jnp.concatenate([s0_ref[d, 2 * p], zero_blk], axis=1),
                     jnp.concatenate([zero_blk, s0_ref[d, 2 * p + 1]], axis=1)], axis=0)
    y_scr[...] = jnp.zeros(y_scr.shape, F32)

    ti = lax.broadcasted_iota(I32, (CHUNK, pw), 0)
    si = lax.broadcasted_iota(I32, (CHUNK, pw), 1) % CHUNK
    eye2 = (ti == si).astype(F32)
    tri_i = lax.broadcasted_iota(I32, (CHUNK, CHUNK), 0)
    tri_j = lax.broadcasted_iota(I32, (CHUNK, CHUNK), 1)
    first = lax.broadcasted_iota(I32, (1, pw), 1) < hd
    diag_blk = ((lax.broadcasted_iota(I32, (pw, pw), 0) < hd) == (lax.broadcasted_iota(I32, (pw, pw), 1) < hd))

    def bd(x):
        xb = x.astype(BF16)
        zero = jnp.zeros_like(xb)
        return jnp.concatenate([jnp.where(first, xb, zero), jnp.where(first, zero, xb)], axis=0)

    def chunk(c, carry):
        chains = []
        for d in (0, 1):
            cc = c if d == 0 else nc - 1 - c
            rows = pl.ds(pl.multiple_of(cc * CHUNK, CHUNK), CHUNK)
            incl = (si <= ti) if d == 0 else (si >= ti)
            strict = (si < ti) if d == 0 else (si > ti)
            lw = lw_scr[d, rows, :]
            lw_hi, lw_lo = _split(lw)
            tri = jnp.where((tri_j <= tri_i) if d == 0 else (tri_j >= tri_i), 1.0, 0.0).astype(BF16)
            lc = _dg(tri, lw_hi, _NN) + _dg(tri, lw_lo, _NN)
            l_last = lc[CHUNK - 1:CHUNK, :] if d == 0 else lc[0:1, :]
            e_neg = jnp.exp(-lc)
            e_end = jnp.exp(l_last - lc)
            vc = v_scr[rows, :]
            kdc = kd_scr[d, rows, :]
            bc = b_scr[d, rows, :]
            rt = r_scr[rows, :] * jnp.exp(lc)
            kkt = kk_scr[rows, :] * jnp.exp(lc - lw)
            kh = kdc * e_neg
            bh = bc * e_neg
            kbar = kdc * e_end
            bbar = bc * e_end
            w_end = jnp.exp(l_last)
            for p in range(npair):
                sl = slice(p * pw, (p + 1) * pw)
                chains.append(dict(
                    d=d, pair=p, sl=sl, rows=rows, incl=incl, strict=strict,
                    lhs=jnp.concatenate([kkt[:, sl], rt[:, sl]], axis=0).astype(BF16),
                    rhs=jnp.concatenate([bd(bh[:, sl]), bd(kh[:, sl])], axis=0),
                    end=jnp.concatenate([kbar[:, sl], bbar[:, sl]], axis=0),
                    v=vc[:, sl], w_end=w_end[:, sl], s0=s_scr[d, p]))
        for ch in chains:
            ch['ab'] = _dg(ch['lhs'], ch['rhs'], _NT)
        for ch in chains:
            ch['proj'] = _dg(ch['lhs'], ch['s0'].astype(BF16), _NT)
        for ch in chains:
            ab = ch['ab']
            ch['a_kb'] = jnp.where(ch['strict'], ab[:CHUNK, :pw], 0.0)
            ch['b_rb'] = jnp.where(ch['incl'], ab[CHUNK:, :pw], 0.0)
            ch['akk_brk'] = jnp.concatenate([jnp.where(ch['strict'], ab[:CHUNK, pw:], 0.0),
                                             jnp.where(ch['incl'], ab[CHUNK:, pw:], 0.0)], axis=0)
        for ch in chains:
            ch['p'] = _dg(ch['a_kb'].astype(BF16), bd(ch['a_kb']), _NN)
        for ch in chains:
            ch['abv'] = _dg(ch['akk_brk'].astype(BF16), bd(ch['v']), _NN)
        for ch in chains:
            inv = eye2 - ch['a_kb']
            ch['inv'] = inv + _dg(inv.astype(BF16), bd(ch['p']), _NN)
        span = 4
        while span < CHUNK:
            for ch in chains:
                ch['p'] = _dg(ch['p'].astype(BF16), bd(ch['p']), _NN)
            for ch in chains:
                ch['inv'] = ch['inv'] + _dg(ch['inv'].astype(BF16), bd(ch['p']), _NN)
            span *= 2
        for ch in chains:
            ch['u'] = _dg(ch['inv'].astype(BF16), bd(ch['proj'][:CHUNK] + ch['abv'][:CHUNK]), _NN)
        for ch in chains:
            y = ch['proj'][CHUNK:] + ch['abv'][CHUNK:] - _dg(ch['b_rb'].astype(BF16), bd(ch['u']), _NN)
            y_scr[ch['rows'], ch['sl']] += y
        for ch in chains:
            upd = _dot3(jnp.concatenate([ch['v'], -ch['u']], axis=0), ch['end'], _TN)
            s_scr[ch['d'], ch['pair']] = ch['s0'] * ch['w_end'] + jnp.where(diag_blk, upd, 0.0)
        return carry

    lax.fori_loop(0, nc, chunk, 0)

    y = y_scr[...]
    dev = y - head_sum(y) * (1.0 / hd)
    yn = dev * lax.rsqrt(head_sum(dev * dev) * (1.0 / hd) + GN_EPS)
    out_ref[...] = (yn * lng_ref[...] + lnb_ref[...] + bonus_scr[...]) * g_scr[...]
    for d in (0, 1):
        for p in range(npair):
            s = s_scr[d, p]
            so_ref[d, 2 * p] = s[:hd, :hd]
            so_ref[d, 2 * p + 1] = s[hd:, hd:]


def _rwkv(u, p, state, row0, batch, seq, heads_per_step):
    zero_init = state is None
    nh = heads_per_step
    wd = nh * R_HD
    nsteps = R_HEADS // nh
    width = R_HEADS * R_HD
    rc = COL_R // wd
    sec = width // wd
    lc = (COL_R + 3 * width) // LANES
    ublk = lambda off: pl.BlockSpec((seq, wd), lambda b, h: (row0 + b, rc + off + h))
    ufix = lambda blk: pl.BlockSpec((seq, LANES), lambda b, h: (row0 + b, blk))
    mblk = lambda off: pl.BlockSpec((1, wd), lambda b, h: (0, off + h))
    mfix = lambda blk: pl.BlockSpec((1, LANES), lambda b, h: (0, blk))
    vec = pl.BlockSpec((1, wd), lambda b, h: (0, h))
    in_specs = [ublk(0), ublk(sec), ublk(2 * sec), ufix(lc), ufix(lc + 1),
                mblk(0), mblk(sec), mblk(2 * sec), mfix(lc - COL_R // LANES), mfix(lc - COL_R // LANES + 1),
                pl.BlockSpec((2, 1, wd), lambda b, h: (0, 0, h)),
                pl.BlockSpec((2, R_HD, wd), lambda b, h: (0, 0, h)),
                pl.BlockSpec((2, 1, wd), lambda b, h: (0, 0, h)),
                pl.BlockSpec((2, R_HD, wd), lambda b, h: (0, 0, h)),
                pl.BlockSpec((LANES, wd), lambda b, h: (0, h)),
                vec, vec, vec, vec, vec]
    args = [u, u, u, u, u, p['mu'], p['mu'], p['mu'], p['mu'], p['mu'],
            p['w0'], p['w_up'], p['a0'], p['a_up'], p['g_up'], p['k_k'], p['k_a'], p['r_k'], p['ln_g'], p['ln_b']]
    if not zero_init:
        in_specs.append(pl.BlockSpec((None, 2, nh, R_HD, R_HD), lambda b, h: (b, 0, h, 0, 0)))
        args.append(state)
    big = lambda n: pltpu.VMEM((n, seq, wd), F32)
    return pl.pallas_call(
        functools.partial(_rwkv_kernel, zero_init=zero_init, seq=seq),
        grid=(batch, nsteps),
        in_specs=in_specs,
        out_specs=[pl.BlockSpec((seq, wd), lambda b, h: (b, h)),
                   pl.BlockSpec((None, 2, nh, R_HD, R_HD), lambda b, h: (b, 0, h, 0, 0))],
        out_shape=[jax.ShapeDtypeStruct((batch * seq, width), F32),
                   jax.ShapeDtypeStruct((batch, 2, R_HEADS, R_HD, R_HD), F32)],
        scratch_shapes=[pltpu.VMEM((seq, wd), F32)] * 5
                       + [big(2), big(2), big(2), pltpu.VMEM((seq, wd), F32),
                          pltpu.VMEM((2, nh // 2, 2 * R_HD, 2 * R_HD), F32)],
        compiler_params=_cparams(("arbitrary", "arbitrary")),
        name="rwkv_ctx" if zero_init else "rwkv_lat",
    )(*args)


def _top2_sum(a, b, c, d):
    m1, n1 = jnp.maximum(a, b), jnp.minimum(a, b)
    m2, n2 = jnp.maximum(c, d), jnp.minimum(c, d)
    return jnp.maximum(m1, m2) + jnp.maximum(jnp.minimum(m1, m2), jnp.maximum(n1, n2))


def _first_argmax(vals):
    best = functools.reduce(jnp.maximum, vals)
    idx = jnp.full(best.shape, len(vals) - 1, I32)
    for j in range(len(vals) - 2, -1, -1):
        idx = jnp.where(vals[j] == best, j, idx)
    return best, idx


def _out_kernel(attc_ref, attl_ref, mc_ref, ml_ref, rc_ref, rl_ref, xc_ref, xl_ref, w_ref, g1_ref, sh2_ref, sc2_ref,
                n2_ref, rw_ref, rb_ref, x1_ref, h2_ref, route_ref, cnt_ref, cnt_scr, *, n_ctx_tiles):
    @pl.when(pl.program_id(0) == 0)
    def _():
        cnt_scr[...] = jnp.zeros(cnt_scr.shape, F32)

    is_ctx = pl.program_id(0) < n_ctx_tiles
    both = lambda c_ref, l_ref: jnp.where(is_ctx, c_ref[...], l_ref[...])
    na = attc_ref.shape[1]
    nm = mc_ref.shape[1]
    mix = (jnp.dot(both(attc_ref, attl_ref).astype(BF16), w_ref[0:na, :], preferred_element_type=F32)
           + jnp.dot(both(mc_ref, ml_ref).astype(BF16), w_ref[na:na + nm, :], preferred_element_type=F32)
           + jnp.dot(both(rc_ref, rl_ref).astype(BF16), w_ref[na + nm:, :], preferred_element_type=F32))
    x1 = both(xc_ref, xl_ref) + g1_ref[...] * mix
    x1_ref[...] = x1
    h2 = _rms(x1, n2_ref[...]) * (1.0 + sc2_ref[...]) + sh2_ref[...]
    h2_ref[...] = _pack_bf16_pairs(h2)
    logits = _dot1(rw_ref[...], h2, _NT)
    s = _sigmoid(logits)
    ssel = s + rb_ref[...]
    srow = [s[e:e + 1, :] for e in range(N_EXPERTS)]
    brow = [ssel[e:e + 1, :] for e in range(N_EXPERTS)]
    gscore = [_top2_sum(*brow[EXPERTS_PER_GROUP * g:EXPERTS_PER_GROUP * (g + 1)]) for g in range(N_EXPERT_GROUPS)]
    _, gidx = _first_argmax(gscore)
    pick = lambda rows, j: functools.reduce(
        lambda acc, g: jnp.where(gidx == g, rows[EXPERTS_PER_GROUP * g + j], acc),
        range(N_EXPERT_GROUPS - 2, -1, -1), rows[EXPERTS_PER_GROUP * (N_EXPERT_GROUPS - 1) + j])
    ing = [pick(brow, j) for j in range(EXPERTS_PER_GROUP)]
    sin_ = [pick(srow, j) for j in range(EXPERTS_PER_GROUP)]
    _, l1 = _first_argmax(ing)
    _, l2 = _first_argmax([jnp.where(l1 == j, -jnp.inf, ing[j]) for j in range(EXPERTS_PER_GROUP)])
    sel = lambda l: functools.reduce(lambda acc, j: jnp.where(l == j, sin_[j], acc),
                                     range(EXPERTS_PER_GROUP - 2, -1, -1), sin_[EXPERTS_PER_GROUP - 1])
    w1, w2 = sel(l1), sel(l2)
    tot = w1 + w2
    e1 = gidx * EXPERTS_PER_GROUP + l1
    e2 = gidx * EXPERTS_PER_GROUP + l2
    tm = e1.shape[1]
    eid = lax.broadcasted_iota(I32, (N_EXPERTS, tm), 0)
    oh1 = eid == e1
    oh2 = eid == e2
    picked = jnp.where(jnp.logical_or(oh1, oh2), 1.0, 0.0)
    earlier = jnp.where(lax.broadcasted_iota(I32, (tm, tm), 0) < lax.broadcasted_iota(I32, (tm, tm), 1), 1.0, 0.0)
    rank = cnt_scr[:, 0:1] + jnp.dot(picked.astype(BF16), earlier.astype(BF16), preferred_element_type=F32)
    pos1 = jnp.sum(jnp.where(oh1, rank, 0.0), axis=0, keepdims=True)
    pos2 = jnp.sum(jnp.where(oh2, rank, 0.0), axis=0, keepdims=True)
    cnt = cnt_scr[...] + jnp.sum(picked, axis=1, keepdims=True)
    cnt_scr[...] = cnt
    cnt_ref[...] = cnt
    zero = jnp.zeros_like(w1)
    route_ref[...] = jnp.concatenate([e1.astype(F32), e2.astype(F32), w1 / tot, w2 / tot, pos1, pos2, zero, zero],
                                     axis=0)


def _out_proj(att, m_out, r_out, x, w_out, mod3, n2g, rw_t, rb, s_lat):
    t_ctx, d = x[0].shape
    ntok = t_ctx + x[1].shape[0]
    tm = 256
    nct = t_ctx // tm
    row = lambda i: _mod_row(i * tm, t_ctx, s_lat)
    modblk = lambda j: pl.BlockSpec((None, 1, d), lambda i: (row(i), 0, j))
    pair = lambda a: [pl.BlockSpec((tm, a[0].shape[1]), lambda i: (jnp.minimum(i, nct - 1), 0)),
                      pl.BlockSpec((tm, a[1].shape[1]), lambda i: (jnp.maximum(i - nct, 0), 0))]
    return pl.pallas_call(
        functools.partial(_out_kernel, n_ctx_tiles=nct),
        grid=(ntok // tm,),
        in_specs=pair(att) + pair(m_out) + pair(r_out) + pair(x) + [
                  pl.BlockSpec(w_out.shape, lambda i: (0, 0)),
                  modblk(2), modblk(3), modblk(4),
                  pl.BlockSpec((1, d), lambda i: (0, 0)),
                  pl.BlockSpec(rw_t.shape, lambda i: (0, 0)),
                  pl.BlockSpec(rb.shape, lambda i: (0, 0))],
        out_specs=[pl.BlockSpec((tm, d), lambda i: (i, 0)),
                   pl.BlockSpec((tm, d // 2), lambda i: (i, 0)),
                   pl.BlockSpec((8, tm), lambda i: (0, i)),
                   pl.BlockSpec((N_EXPERTS, LANES), lambda i: (0, 0))],
        out_shape=[jax.ShapeDtypeStruct((ntok, d), F32), jax.ShapeDtypeStruct((ntok, d // 2), U32),
                   jax.ShapeDtypeStruct((8, ntok), F32), jax.ShapeDtypeStruct((N_EXPERTS, LANES), F32)],
        scratch_shapes=[pltpu.VMEM((N_EXPERTS, LANES), F32)],
        compiler_params=_cparams(("arbitrary",)),
        name="out_proj",
    )(*att, *m_out, *r_out, *x, w_out, mod3, mod3, mod3, n2g, rw_t, rb)


def _row_gather(src_hbm, idx_ref, base, dst, sem, n, unrolled, lo=0):
    def start(j):
        pltpu.make_async_copy(src_hbm.at[pl.ds(idx_ref[base + j], 1), :], dst.at[pl.ds(j, 1), :], sem).start()

    if unrolled:
        for j in range(lo, n):
            start(j)
    else:
        def body(j, c):
            start(j)
            return c
        lax.fori_loop(lo, n, body, 0)


def _expert_kernel(be_ref, tok_ref, nused_ref, h_hbm, w1_ref, w3_ref, w2_ref, y_ref, xbuf, w1b, w3b, w2b, sem):
    i = pl.program_id(0)
    n_used = nused_ref[0]
    wait = lambda s: pltpu.make_async_copy(h_hbm.at[pl.ds(0, EXPERT_ROWS), :], xbuf.at[s], sem.at[s]).wait()

    @pl.when(i < n_used)
    def _():
        slot = i % 2

        @pl.when(i == 0)
        def _():
            _row_gather(h_hbm, tok_ref, 0, xbuf.at[0], sem.at[0], EXPERT_ROWS, unrolled=False)

        @pl.when(jnp.logical_or(i == 0, be_ref[i] != be_ref[jnp.maximum(i - 1, 0)]))
        def _():
            w1b[...] = w1_ref[...].astype(BF16)
            w3b[...] = w3_ref[...].astype(BF16)
            w2b[...] = w2_ref[...].astype(BF16)

        wait(slot)
        de = w1b.shape[1]
        half = de // 2
        groups = 6
        per = EXPERT_ROWS // groups

        def next_rows(gi):
            hi = EXPERT_ROWS if gi == groups - 1 else (gi + 1) * per
            _row_gather(h_hbm, tok_ref, (i + 1) * EXPERT_ROWS, xbuf.at[1 - slot], sem.at[1 - slot], hi,
                        unrolled=True, lo=gi * per)

        x_lo, x_hi = (h.astype(BF16) for h in _unpack_bf16_pairs(xbuf[slot]))
        kh = x_lo.shape[1]
        xw = lambda w, cols: (jnp.dot(x_lo, w[0:kh, cols], preferred_element_type=F32)
                              + jnp.dot(x_hi, w[kh:, cols], preferred_element_type=F32))
        hmid = []
        for n in range(2):
            cols = slice(n * half, (n + 1) * half)
            next_rows(2 * n)
            a = xw(w1b, cols)
            next_rows(2 * n + 1)
            b = xw(w3b, cols)
            hmid.append(((a * _sigmoid(a)) * b).astype(BF16))
        next_rows(4)
        y = jnp.dot(hmid[0], w2b[0:half, :], preferred_element_type=F32)
        next_rows(5)
        y = y + jnp.dot(hmid[1], w2b[half:, :], preferred_element_type=F32)
        y_ref[...] = _pack_bf16_pairs(y)

        @pl.when(i == n_used - 1)
        def _():
            wait(1 - slot)

    @pl.when(i >= n_used)
    def _():
        y_ref[...] = jnp.zeros(y_ref.shape, U32)


def _experts(h2, block_e, row_tok, n_used, w1, w3, w2, layer):
    d = w1.shape[2]
    de = w1.shape[3]
    n_rows = row_tok.shape[0] - EXPERT_ROWS
    nb = n_rows // EXPERT_ROWS
    return pl.pallas_call(
        _expert_kernel,
        grid_spec=pltpu.PrefetchScalarGridSpec(
            num_scalar_prefetch=3,
            grid=(nb,),
            in_specs=[pl.BlockSpec(memory_space=pl.ANY),
                      pl.BlockSpec((None, None, d, de), lambda i, be, tok, nu: (layer, be[i], 0, 0)),
                      pl.BlockSpec((None, None, d, de), lambda i, be, tok, nu: (layer, be[i], 0, 0)),
                      pl.BlockSpec((None, None, de, d), lambda i, be, tok, nu: (layer, be[i], 0, 0))],
            out_specs=pl.BlockSpec((EXPERT_ROWS, d // 2), lambda i, be, tok, nu: (i, 0)),
            scratch_shapes=[pltpu.VMEM((2, EXPERT_ROWS, d // 2), U32), pltpu.VMEM((d, de), BF16),
                            pltpu.VMEM((d, de), BF16), pltpu.VMEM((de, d), BF16), pltpu.SemaphoreType.DMA((2,))]),
        out_shape=jax.ShapeDtypeStruct((n_rows, d // 2), U32),
        compiler_params=_cparams(("arbitrary",)),
        name="experts",
    )(block_e, row_tok, n_used, h2, w1, w3, w2)


def _combine_kernel(d1_ref, d2_ref, y_hbm, x1_ref, g2_ref, gate_ref, oc_ref, ol_ref, ybuf, sem, *, n_ctx_tiles):
    tm = x1_ref.shape[0]
    i = pl.program_id(0)
    slot = i % 2

    def gather(tile, s, unrolled):
        _row_gather(y_hbm, d1_ref, tile * tm, ybuf.at[s, 0], sem.at[s], tm, unrolled)
        _row_gather(y_hbm, d2_ref, tile * tm, ybuf.at[s, 1], sem.at[s], tm, unrolled)

    def wait(s):
        pltpu.make_async_copy(y_hbm.at[pl.ds(0, tm), :], ybuf.at[s, 0], sem.at[s]).wait()
        pltpu.make_async_copy(y_hbm.at[pl.ds(0, tm), :], ybuf.at[s, 1], sem.at[s]).wait()

    @pl.when(i == 0)
    def _():
        gather(0, 0, False)

    wait(slot)
    gather(i + 1, 1 - slot, True)
    gate = gate_ref[...]
    lo1, hi1 = _unpack_bf16_pairs(ybuf[slot, 0])
    lo2, hi2 = _unpack_bf16_pairs(ybuf[slot, 1])
    moe = jnp.concatenate([lo1 * gate[:, 0:1] + lo2 * gate[:, 1:2], hi1 * gate[:, 0:1] + hi2 * gate[:, 1:2]], axis=1)
    new_x = x1_ref[...] + g2_ref[...] * moe

    @pl.when(i < n_ctx_tiles)
    def _():
        oc_ref[...] = new_x

    @pl.when(i >= n_ctx_tiles)
    def _():
        ol_ref[...] = new_x

    @pl.when(i == pl.num_programs(0) - 1)
    def _():
        wait(1 - slot)


def _combine(y, x1, mod3, gate, dest1, dest2, t_ctx, s_lat):
    ntok, d = x1.shape
    tm = 256
    nct = t_ctx // tm
    row = lambda i: _mod_row(i * tm, t_ctx, s_lat)
    return pl.pallas_call(
        functools.partial(_combine_kernel, n_ctx_tiles=nct),
        grid_spec=pltpu.PrefetchScalarGridSpec(
            num_scalar_prefetch=2,
            grid=(ntok // tm,),
            in_specs=[pl.BlockSpec(memory_space=pl.ANY),
                      pl.BlockSpec((tm, d), lambda i, a, b: (i, 0)),
                      pl.BlockSpec((None, 1, d), lambda i, a, b: (row(i), 0, 5)),
                      pl.BlockSpec((tm, 2), lambda i, a, b: (i, 0))],
            out_specs=[pl.BlockSpec((tm, d), lambda i, a, b: (jnp.minimum(i, nct - 1), 0)),
                       pl.BlockSpec((tm, d), lambda i, a, b: (jnp.maximum(i - nct, 0), 0))],
            scratch_shapes=[pltpu.VMEM((2, 2, tm, d // 2), U32), pltpu.SemaphoreType.DMA((2,))]),
        out_shape=[jax.ShapeDtypeStruct((t_ctx, d), F32), jax.ShapeDtypeStruct((ntok - t_ctx, d), F32)],
        compiler_params=_cparams(("arbitrary",)),
        name="combine",
    )(dest1, dest2, y, x1, mod3, gate)


def _dispatch(route, counts, tile):
    ntok = route.shape[1]
    e = route[0:2].astype(I32)
    pos = route[4:6].astype(I32)
    gate = route[2:4].T
    counts = counts[:, 0].astype(I32)
    padded = (counts + EXPERT_ROWS - 1) // EXPERT_ROWS * EXPERT_ROWS
    pad_end = jnp.cumsum(padded)
    pad_start = pad_end - padded
    onehot = (e[:, :, None] == jnp.arange(N_EXPERTS, dtype=I32)).astype(I32)
    dest = jnp.sum(onehot * pad_start, axis=-1) + pos
    n_rows = -(-(2 * ntok) // EXPERT_ROWS) * EXPERT_ROWS + N_EXPERTS * EXPERT_ROWS
    nb = n_rows // EXPERT_ROWS
    tok = jnp.broadcast_to(jnp.arange(ntok, dtype=I32)[None, :], (2, ntok))
    row_tok = jnp.zeros((n_rows + EXPERT_ROWS,), I32).at[dest.reshape(-1)].set(tok.reshape(-1), unique_indices=True)
    blk_start = jnp.arange(nb, dtype=I32) * EXPERT_ROWS
    block_e = jnp.minimum(jnp.sum((pad_end[None, :] <= blk_start[:, None]).astype(I32), axis=1), N_EXPERTS - 1)
    n_used = pad_end[-1:] // EXPERT_ROWS
    dest = jnp.pad(dest, ((0, 0), (0, tile)))
    return row_tok, block_e, n_used, gate, dest[0], dest[1]


def kernel(x_prompt, x_sample, c, cache_attn_k, cache_attn_v, state_mlstm_C, state_mlstm_n, state_mlstm_m, state_rwkv, c_ctx, norm1_g, norm2_g, w_mod, b_mod, w_in, w_out, attn_q_norm, attn_k_norm, mlstm_i_bias, mlstm_f_bias, mlstm_norm_g, rwkv_mu, rwkv_w0, rwkv_w_up, rwkv_a0, rwkv_a_up, rwkv_g_up, rwkv_k_k, rwkv_k_a, rwkv_r_k, rwkv_ln_g, rwkv_ln_b, router_w, router_b, exp_w1, exp_w3, exp_w2):
    b_ctx, s_ctx, d = x_prompt.shape
    b_lat, s_lat, _ = x_sample.shape
    depth = w_in.shape[0]
    t_ctx = b_ctx * s_ctx
    ntok = t_ctx + b_lat * s_lat
    assert b_lat + 1 <= 8 and s_lat % 1024 == 0 and t_ctx % 1024 == 0 and t_ctx % s_lat == 0
    past = cache_attn_k.shape[2]

    x = (x_prompt.reshape(t_ctx, d), x_sample.reshape(b_lat * s_lat, d))
    c_all = jnp.zeros((8, d), F32).at[0].set(c_ctx).at[1:1 + b_lat].set(c)
    mod = _modulation(c_all, w_mod, b_mod)
    cos, sin = _rope_tables(s_lat)
    rw_t = router_w.T
    rb = router_b.reshape(N_EXPERTS, 1)
    r_width = R_HEADS * R_HD
    n_in = w_in.shape[2]
    gate_hi = COL_R + 4 * M_HEADS
    w_in_t = jnp.swapaxes(w_in, 1, 2)
    w_tail_t = jnp.concatenate([w_in_t[:, gate_hi:], w_in_t[:, COL_R:gate_hi],
                                jnp.zeros((depth, N_IN_PAD - n_in, d), F32)], axis=1).astype(BF16)

    ks, vs, cs, ns, ms, rs = [], [], [], [], [], []
    for l in range(depth):
        mod3 = mod[l].reshape(8, 1, 6 * d)
        u = _in_proj(x[0], x[1], norm1_g[l][None], mod3, w_in_t, l, w_tail_t[l], s_lat)

        ck = cache_attn_k[:, l].reshape(b_lat, past, ATT_KV_HEADS * ATT_HD)
        cv = cache_attn_v[:, l].reshape(b_lat, past, ATT_KV_HEADS * ATT_HD)
        att_c, att_l, k_ctx, v_ctx = _attention(u, attn_q_norm[l][None], attn_k_norm[l][None], ck, cv, cos, sin,
                                                b_ctx, s_ctx, b_lat, s_lat)
        ks.append(k_ctx.reshape(b_ctx, s_ctx, ATT_KV_HEADS, ATT_HD))
        vs.append(v_ctx.reshape(b_ctx, s_ctx, ATT_KV_HEADS, ATT_HD))

        gcol = u[:, COL_GATE:COL_GATE + 4 * M_HEADS]
        gt = gcol.reshape(ntok // CHUNK, CHUNK, 4 * M_HEADS).transpose(0, 2, 1)
        bias = jnp.stack([mlstm_i_bias[l], mlstm_f_bias[l]], axis=1)
        ng = mlstm_norm_g[l][None]
        m_c_out, cx_c, m_c = _mlstm(u, gcol, gt, bias, ng, None, 0, b_ctx, s_ctx)
        n_col = jnp.pad(state_mlstm_n[:, l][..., None], ((0, 0),) * 4 + ((0, M_HD - 1),))
        lat_states = (jnp.concatenate([state_mlstm_C[:, l], n_col], axis=-1),
                      jnp.broadcast_to(state_mlstm_m[:, l][..., None, None], (b_lat, 2, M_HEADS, 1, M_HD)))
        m_l_out, _, _ = _mlstm(u, gcol, gt, bias, ng, lat_states, t_ctx // s_lat, b_lat, s_lat)
        cs.append(cx_c[..., :M_HD])
        ns.append(cx_c[..., M_HD])
        ms.append(m_c[:, :, :, 0, 0])

        rp = dict(mu=rwkv_mu[l][None], w0=rwkv_w0[l].reshape(2, 1, r_width), w_up=rwkv_w_up[l],
                  a0=rwkv_a0[l].reshape(2, 1, r_width), a_up=rwkv_a_up[l], g_up=rwkv_g_up[l],
                  k_k=rwkv_k_k[l][None], k_a=rwkv_k_a[l][None], r_k=rwkv_r_k[l].reshape(1, r_width),
                  ln_g=rwkv_ln_g[l][None], ln_b=rwkv_ln_b[l][None])
        r_c_out, r_c = _rwkv(u, rp, None, 0, b_ctx, s_ctx, RWKV_HEADS_PER_STEP_CTX)
        r_l_out, _ = _rwkv(u, rp, state_rwkv[:, l], t_ctx // s_lat, b_lat, s_lat, RWKV_HEADS_PER_STEP_LAT)
        rs.append(r_c)

        x1, h2, route, counts = _out_proj((att_c, att_l), (m_c_out, m_l_out), (r_c_out, r_l_out), x,
                                          w_out[l].astype(BF16), mod3, norm2_g[l][None], rw_t, rb, s_lat)
        row_tok, block_e, n_used, gate, dest1, dest2 = _dispatch(route, counts, 256)
        y = _experts(h2, block_e, row_tok, n_used, exp_w1, exp_w3, exp_w2, l)
        x = _combine(y, x1, mod3, gate, dest1, dest2, t_ctx, s_lat)

    y_prompt = x[0].reshape(b_ctx, s_ctx, d)
    y_sample = x[1].reshape(b_lat, s_lat, d)
    return (y_prompt, y_sample, jnp.stack(ks, axis=1), jnp.stack(vs, axis=1), jnp.stack(cs, axis=1),
            jnp.stack(ns, axis=1), jnp.stack(ms, axis=1), jnp.stack(rs, axis=1))
```

```python
import functools
import math

import jax
import jax.numpy as jnp
from jax import lax
from jax.experimental import pallas as pl
from jax.experimental.pallas import tpu as pltpu

F32 = jnp.float32
BF16 = jnp.bfloat16
I32 = jnp.int32

NORM_EPS = 1e-6
GN_EPS = 64e-5
M_INIT = -1e30
GRID_W = 64
ROPE_THETA = 10000.0
ATT_HD = 128
ATT_GROUPS = 4
ATT_KV_HEADS = 2
M_HD = 128
M_HEADS = 4
R_HD = 64
R_HEADS = 8
N_EXPERTS = 16
N_EXPERT_GROUPS = 4
EXPERTS_PER_GROUP = 4
CHUNK = 64
LANES = 128
EXPERT_ROWS = 512
VMEM_LIMIT = 58 * 1024 * 1024
RWKV_HEADS_PER_STEP_CTX = 8
RWKV_HEADS_PER_STEP_LAT = 8

COL_ATT = 0
COL_M = 1536
COL_R = 3584
COL_GATE = 5376
N_IN_PAD = 5632


def _cparams(sem):
    return pltpu.CompilerParams(dimension_semantics=sem, vmem_limit_bytes=VMEM_LIMIT)


def _dot(a, b):
    return jnp.dot(a.astype(BF16), b.astype(BF16), preferred_element_type=F32)


def _dg(a, b, dims):
    return lax.dot_general(a, b, (dims, ((), ())), preferred_element_type=F32)


_NN = ((1,), (0,))
_NT = ((1,), (1,))
_TN = ((0,), (0,))


def _split(a):
    hi = a.astype(BF16)
    lo = (a - hi.astype(F32)).astype(BF16)
    return hi, lo


def _dot3(a, b, dims=_NN):
    ah, al = _split(a)
    bh, bl = _split(b)
    return _dg(ah, bh, dims) + (_dg(ah, bl, dims) + _dg(al, bh, dims))


def _dot1(a, b, dims=_NN):
    return _dg(a.astype(BF16), b.astype(BF16), dims)


def _rms(x, g):
    return x * lax.rsqrt(jnp.mean(x * x, axis=-1, keepdims=True) + NORM_EPS) * g


def _sigmoid(x):
    return 1.0 / (1.0 + jnp.exp(-x))


def _mod_kernel(c_ref, w_ref, b_ref, o_ref):
    c = c_ref[...]
    o_ref[...] = _dot(c * _sigmoid(c), w_ref[...]) + b_ref[...]


def _modulation(c_all, w_mod, b_mod):
    depth, d, n = w_mod.shape
    tn = 1024
    return pl.pallas_call(
        _mod_kernel,
        grid=(depth, n // tn),
        in_specs=[pl.BlockSpec((8, d), lambda l, j: (0, 0)),
                  pl.BlockSpec((None, d, tn), lambda l, j: (l, 0, j)),
                  pl.BlockSpec((None, 1, tn), lambda l, j: (l, 0, j))],
        out_specs=pl.BlockSpec((None, 8, tn), lambda l, j: (l, 0, j)),
        out_shape=jax.ShapeDtypeStruct((depth, 8, n), F32),
        compiler_params=_cparams(("arbitrary", "arbitrary")),
        name="modulation",
    )(c_all, w_mod, b_mod.reshape(depth, 1, n))


def _mod_row(tok0, t_ctx, s_lat):
    return jnp.where(tok0 < t_ctx, 0, 1 + (tok0 - t_ctx) // s_lat)


def _in_kernel(xc_ref, xl_ref, g_ref, sh_ref, sc_ref, wa_ref, wb_ref, o_ref, h_ref, *, n_ctx_tiles, n_head_tiles):
    i = pl.program_id(0)
    j = pl.program_id(1)

    def normalise(x_ref):
        slab = 256
        for r0 in range(0, x_ref.shape[0], slab):
            h = _rms(x_ref[r0:r0 + slab, :], g_ref[...]) * (1.0 + sc_ref[...]) + sh_ref[...]
            h_ref[r0:r0 + slab, :] = h.astype(BF16)

    @pl.when(jnp.logical_and(j == 0, i < n_ctx_tiles))
    def _():
        normalise(xc_ref)

    @pl.when(jnp.logical_and(j == 0, i >= n_ctx_tiles))
    def _():
        normalise(xl_ref)

    @pl.when(j < n_head_tiles)
    def _():
        o_ref[...] = _dg(h_ref[...], wa_ref[...].astype(BF16), _NT)

    @pl.when(j >= n_head_tiles)
    def _():
        o_ref[...] = _dg(h_ref[...], wb_ref[...], _NT)


def _in_proj(xc, xl, g1, mod3, w_in_t, layer, w_tail_t, s_lat):
    t_ctx, d = xc.shape
    ntok = t_ctx + xl.shape[0]
    tm, tn = 1024, 512
    na = COL_R // tn
    n = COL_R + w_tail_t.shape[0]
    nct = t_ctx // tm
    row = lambda i: _mod_row(i * tm, t_ctx, s_lat)
    return pl.pallas_call(
        functools.partial(_in_kernel, n_ctx_tiles=nct, n_head_tiles=na),
        grid=(ntok // tm, n // tn),
        in_specs=[pl.BlockSpec((tm, d), lambda i, j: (jnp.minimum(i, nct - 1), 0)),
                  pl.BlockSpec((tm, d), lambda i, j: (jnp.maximum(i - nct, 0), 0)),
                  pl.BlockSpec((1, d), lambda i, j: (0, 0)),
                  pl.BlockSpec((None, 1, d), lambda i, j: (row(i), 0, 0)),
                  pl.BlockSpec((None, 1, d), lambda i, j: (row(i), 0, 1)),
                  pl.BlockSpec((None, tn, d), lambda i, j: (layer, jnp.minimum(j, na - 1), 0)),
                  pl.BlockSpec((tn, d), lambda i, j: (jnp.maximum(j - na, 0), 0))],
        out_specs=pl.BlockSpec((tm, tn), lambda i, j: (i, j)),
        out_shape=jax.ShapeDtypeStruct((ntok, n), F32),
        scratch_shapes=[pltpu.VMEM((tm, d), BF16)],
        compiler_params=_cparams(("arbitrary", "arbitrary")),
        name="in_proj",
    )(xc, xl, g1, mod3, mod3, w_in_t, w_tail_t)


def _softmax_av(q, kb, vb):
    s = _dg(q.astype(BF16), kb, _NT) * (ATT_HD ** -0.5)
    p = jnp.exp(s - jnp.max(s, axis=-1, keepdims=True))
    l = jnp.sum(p, axis=-1, keepdims=True)
    return jnp.dot(p.astype(BF16), vb, preferred_element_type=F32) / l


def _att_ctx_kernel(q_ref, k_ref, v_ref, qn_ref, kn_ref, o_ref, ko_ref, vo_ref):
    k = _rms(k_ref[...], kn_ref[...])
    ko_ref[...] = k
    vo_ref[...] = v_ref[...]
    kb = k.astype(BF16)
    vb = v_ref[...].astype(BF16)
    for g in range(ATT_GROUPS):
        q = _rms(q_ref[:, g * ATT_HD:(g + 1) * ATT_HD], qn_ref[...])
        o_ref[:, g * ATT_HD:(g + 1) * ATT_HD] = _softmax_av(q, kb, vb)


def _rope(x, cos, sin):
    lane = lax.broadcasted_iota(I32, x.shape, 1)
    first = (lane % (ATT_HD // 2)) < (ATT_HD // 4)
    partner = jnp.where(first, pltpu.roll(x, ATT_HD - ATT_HD // 4, 1), pltpu.roll(x, ATT_HD // 4, 1))
    return x * cos + partner * sin


def _att_lat_kernel(q_ref, k_ref, v_ref, ck_ref, cv_ref, qn_ref, kn_ref, cosq_ref, sinq_ref, cosk_ref, sink_ref,
                    o_ref, kb_ref, vb_ref, *, past):
    @pl.when(pl.program_id(2) == 0)
    def _():
        k = _rope(_rms(k_ref[...], kn_ref[...]), cosk_ref[...], sink_ref[...])
        kb_ref[0:past, :] = ck_ref[...].astype(BF16)
        kb_ref[past:, :] = k.astype(BF16)
        vb_ref[0:past, :] = cv_ref[...].astype(BF16)
        vb_ref[past:, :] = v_ref[...].astype(BF16)

    kb = kb_ref[...]
    vb = vb_ref[...]
    for g in range(ATT_GROUPS):
        q = _rope(_rms(q_ref[:, g * ATT_HD:(g + 1) * ATT_HD], qn_ref[...]), cosq_ref[...], sinq_ref[...])
        o_ref[:, g * ATT_HD:(g + 1) * ATT_HD] = _softmax_av(q, kb, vb)


def _attention(u, qn, kn, cache_k, cache_v, cos, sin, b_ctx, s_ctx, b_lat, s_lat):
    ntok = u.shape[0]
    t_ctx = b_ctx * s_ctx
    gw = ATT_GROUPS * ATT_HD
    kcol = (ATT_KV_HEADS * gw) // ATT_HD
    vcol = kcol + ATT_KV_HEADS
    kv_spec = pl.BlockSpec((s_ctx, ATT_HD), lambda b, h: (b, h))
    att_c, k_ctx, v_ctx = pl.pallas_call(
        _att_ctx_kernel,
        grid=(b_ctx, ATT_KV_HEADS),
        in_specs=[pl.BlockSpec((s_ctx, gw), lambda b, h: (b, h)),
                  pl.BlockSpec((s_ctx, ATT_HD), lambda b, h: (b, kcol + h)),
                  pl.BlockSpec((s_ctx, ATT_HD), lambda b, h: (b, vcol + h)),
                  pl.BlockSpec((1, ATT_HD), lambda b, h: (0, 0)),
                  pl.BlockSpec((1, ATT_HD), lambda b, h: (0, 0))],
        out_specs=[pl.BlockSpec((s_ctx, gw), lambda b, h: (b, h)), kv_spec, kv_spec],
        out_shape=[jax.ShapeDtypeStruct((t_ctx, ATT_KV_HEADS * gw), F32),
                   jax.ShapeDtypeStruct((t_ctx, ATT_KV_HEADS * ATT_HD), F32),
                   jax.ShapeDtypeStruct((t_ctx, ATT_KV_HEADS * ATT_HD), F32)],
        compiler_params=_cparams(("arbitrary", "arbitrary")),
        name="att_ctx",
    )(u, u, u, qn, kn)

    tq = 256
    nqb = s_lat // tq
    past = cache_k.shape[1]
    qrow0 = t_ctx // tq
    krow0 = t_ctx // s_lat
    att_l = pl.pallas_call(
        functools.partial(_att_lat_kernel, past=past),
        grid=(b_lat, ATT_KV_HEADS, nqb),
        in_specs=[pl.BlockSpec((tq, gw), lambda b, h, i: (qrow0 + b * nqb + i, h)),
                  pl.BlockSpec((s_lat, ATT_HD), lambda b, h, i: (krow0 + b, kcol + h)),
                  pl.BlockSpec((s_lat, ATT_HD), lambda b, h, i: (krow0 + b, vcol + h)),
                  pl.BlockSpec((None, past, ATT_HD), lambda b, h, i: (b, 0, h)),
                  pl.BlockSpec((None, past, ATT_HD), lambda b, h, i: (b, 0, h)),
                  pl.BlockSpec((1, ATT_HD), lambda b, h, i: (0, 0)),
                  pl.BlockSpec((1, ATT_HD), lambda b, h, i: (0, 0)),
                  pl.BlockSpec((tq, ATT_HD), lambda b, h, i: (i, 0)),
                  pl.BlockSpec((tq, ATT_HD), lambda b, h, i: (i, 0)),
                  pl.BlockSpec((s_lat, ATT_HD), lambda b, h, i: (0, 0)),
                  pl.BlockSpec((s_lat, ATT_HD), lambda b, h, i: (0, 0))],
        out_specs=pl.BlockSpec((tq, gw), lambda b, h, i: (b * nqb + i, h)),
        out_shape=jax.ShapeDtypeStruct((ntok - t_ctx, ATT_KV_HEADS * gw), F32),
        scratch_shapes=[pltpu.VMEM((past + s_lat, ATT_HD), BF16), pltpu.VMEM((past + s_lat, ATT_HD), BF16)],
        compiler_params=_cparams(("arbitrary", "arbitrary", "arbitrary")),
        name="att_lat",
    )(u, u, u, cache_k, cache_v, qn, kn, cos, sin, cos, sin)
    return att_c, att_l, k_ctx, v_ctx


def _rope_tables(n_tokens):
    pos = jnp.arange(n_tokens)
    row = (pos // GRID_W).astype(F32)
    col = (pos % GRID_W).astype(F32)
    n_freq = ATT_HD // 4
    inv_freq = ROPE_THETA ** (-jnp.arange(n_freq, dtype=F32) / n_freq)
    ang_r = row[:, None] * inv_freq[None, :]
    ang_c = col[:, None] * inv_freq[None, :]
    cos = jnp.concatenate([jnp.cos(ang_r), jnp.cos(ang_r), jnp.cos(ang_c), jnp.cos(ang_c)], axis=-1)
    sin = jnp.concatenate([-jnp.sin(ang_r), jnp.sin(ang_r), -jnp.sin(ang_c), jnp.sin(ang_c)], axis=-1)
    return cos, sin


def _log_sigmoid(x):
    return jnp.minimum(x, 0.0) - jnp.log1p(jnp.exp(-jnp.abs(x)))


def _mlstm_kernel(*refs, zero_init, seq):
    if zero_init:
        (q_ref, k_ref, v_ref, o_ref, gc_ref, gt_ref, br_ref, bc_ref, ng_ref,
         out_ref, co_ref, mo_ref, h_scr, c_scr, m_scr) = refs
    else:
        (q_ref, k_ref, v_ref, o_ref, gc_ref, gt_ref, br_ref, bc_ref, ng_ref, c0_ref, m0_ref,
         out_ref, co_ref, mo_ref, h_scr, c_scr, m_scr) = refs
    nc = seq // CHUNK
    if zero_init:
        c_scr[...] = jnp.zeros(c_scr.shape, F32)
        m_scr[...] = jnp.full(m_scr.shape, M_INIT, F32)
    else:
        c_scr[...] = c0_ref[...]
        m_scr[...] = m0_ref[...]

    ti = lax.broadcasted_iota(I32, (CHUNK, CHUNK), 0)
    si = lax.broadcasted_iota(I32, (CHUNK, CHUNK), 1)
    ones_col = jnp.where(lax.broadcasted_iota(I32, (CHUNK, M_HD), 1) == 0, 1.0, 0.0).astype(BF16)
    eye_d = jnp.where(lax.broadcasted_iota(I32, (M_HD, M_HD), 0) == lax.broadcasted_iota(I32, (M_HD, M_HD), 1),
                      1.0, 0.0).astype(BF16)

    def cumsum3(tri, x, tri_left):
        hi = x.astype(BF16)
        r1 = x - hi.astype(F32)
        mid = r1.astype(BF16)
        lo = (r1 - mid.astype(F32)).astype(BF16)
        if tri_left:
            return _dg(tri, hi, _NN) + (_dg(tri, mid, _NN) + _dg(tri, lo, _NN))
        return _dg(hi, tri, _NN) + (_dg(mid, tri, _NN) + _dg(lo, tri, _NN))
    gate_i = lambda d, h: 2 * M_HEADS * d + h
    gate_f = lambda d, h: 2 * M_HEADS * d + M_HEADS + h

    def chunk(c, carry):
        ch = []
        for d in (0, 1):
            cc = c if d == 0 else nc - 1 - c
            rows = pl.ds(pl.multiple_of(cc * CHUNK, CHUNK), CHUNK)
            causal = (si <= ti) if d == 0 else (si >= ti)
            tri = jnp.where(causal, 1.0, 0.0).astype(BF16)
            tri_t = jnp.where((ti <= si) if d == 0 else (ti >= si), 1.0, 0.0).astype(BF16)
            q_all = q_ref[rows, :] * (M_HD ** -0.5)
            k_all = k_ref[rows, :]
            v_all = v_ref[rows, :]
            gcol = gc_ref[rows, :] + br_ref[...]
            grow = gt_ref[cc] + bc_ref[...]
            bcum_c_all = cumsum3(tri, _log_sigmoid(gcol), True)
            bcum_r_all = cumsum3(tri_t, _log_sigmoid(grow), False)
            for h in range(M_HEADS):
                sl = slice(h * M_HD, (h + 1) * M_HD)
                q, k, v = q_all[:, sl], k_all[:, sl], v_all[:, sl]
                ig_c = gcol[:, gate_i(d, h):gate_i(d, h) + 1]
                ig_r = grow[gate_i(d, h):gate_i(d, h) + 1, :]
                bcum_c = bcum_c_all[:, gate_f(d, h):gate_f(d, h) + 1]
                bcum_r = bcum_r_all[gate_f(d, h):gate_f(d, h) + 1, :]
                m_st = m_scr[d, h][:, 0:1]
                dmat = jnp.where(causal, bcum_c - bcum_r + ig_r, -jnp.inf)
                inter = bcum_c + m_st
                m_t = jnp.maximum(inter, jnp.max(dmat, axis=1, keepdims=True))
                b_last = bcum_c[CHUNK - 1:CHUNK, :] if d == 0 else bcum_c[0:1, :]
                g_c = b_last - bcum_c + ig_c
                m_new = jnp.maximum(b_last + m_st, jnp.max(g_c, axis=0, keepdims=True))
                ch.append(dict(d=d, h=h, rows=rows, sl=sl, qb=q.astype(BF16), kb=k.astype(BF16),
                               vx=jnp.concatenate([v.astype(BF16), ones_col], axis=1),
                               c_st=c_scr[d, h], m_t=m_t, m_new=m_new,
                               w_intra=jnp.exp(dmat - m_t), w_inter=jnp.exp(inter - m_t),
                               decay=jnp.exp(b_last + m_st - m_new), kw=(k * jnp.exp(g_c - m_new)).astype(BF16)))
        for x in ch:
            x['s_qk'] = _dg(x['qb'], x['kb'], _NT) * x['w_intra']
        for x in ch:
            x['qc'] = _dg(x['qb'], x['c_st'].astype(BF16), _NN)
        for x in ch:
            x['kw_t'] = _dg(eye_d, x['kw'], _NT).astype(BF16)
        for x in ch:
            x['sv'] = _dg(x['s_qk'].astype(BF16), x['vx'], _NN)
        for x in ch:
            x['upd'] = _dg(x['kw_t'], x['vx'], _NN)
        for x in ch:
            d, h = x['d'], x['h']
            tot = x['w_inter'] * x['qc'] + x['sv']
            den = tot[:, M_HD:M_HD + 1]
            h_scr[d, x['rows'], x['sl']] = tot[:, :M_HD] / jnp.maximum(jnp.abs(den), jnp.exp(-x['m_t']))
            c_scr[d, h] = x['decay'] * x['c_st'] + x['upd']
            m_scr[d, h] = jnp.broadcast_to(x['m_new'], (1, M_HD))
        return carry

    lax.fori_loop(0, nc, chunk, 0)
    hsum = h_scr[0] + h_scr[1]
    hn = jnp.concatenate(
        [hsum[:, h * M_HD:(h + 1) * M_HD]
         * lax.rsqrt(jnp.mean(hsum[:, h * M_HD:(h + 1) * M_HD] ** 2, axis=-1, keepdims=True) + NORM_EPS)
         for h in range(M_HEADS)], axis=1)
    out_ref[...] = hn * ng_ref[...] * _sigmoid(o_ref[...])
    co_ref[...] = c_scr[...]
    mo_ref[...] = m_scr[...]


def _mlstm(u, gcol, gt, bias, norm_g, states, row0, batch, seq):
    zero_init = states is None
    width = M_HEADS * M_HD
    qc = COL_M // width
    ngate = 4 * M_HEADS
    blk = lambda off: pl.BlockSpec((seq, width), lambda b: (row0 + b, off))
    st_c = pl.BlockSpec((None, 2, M_HEADS, M_HD, 2 * M_HD), lambda b: (b, 0, 0, 0, 0))
    st_v = pl.BlockSpec((None, 2, M_HEADS, 1, M_HD), lambda b: (b, 0, 0, 0, 0))
    in_specs = [blk(qc), blk(qc + 1), blk(qc + 2), blk(qc + 3),
                pl.BlockSpec((seq, ngate), lambda b: (row0 + b, 0)),
                pl.BlockSpec((seq // CHUNK, ngate, CHUNK), lambda b: (row0 + b, 0, 0)),
                pl.BlockSpec((1, ngate), lambda b: (0, 0)),
                pl.BlockSpec((ngate, 1), lambda b: (0, 0)),
                pl.BlockSpec((1, width), lambda b: (0, 0))]
    args = [u, u, u, u, gcol, gt, bias.reshape(1, ngate), bias.reshape(ngate, 1), norm_g]
    if not zero_init:
        in_specs += [st_c, st_v]
        args += list(states)
    return pl.pallas_call(
        functools.partial(_mlstm_kernel, zero_init=zero_init, seq=seq),
        grid=(batch,),
        in_specs=in_specs,
        out_specs=[pl.BlockSpec((seq, width), lambda b: (b, 0)), st_c, st_v],
        out_shape=[jax.ShapeDtypeStruct((batch * seq, width), F32),
                   jax.ShapeDtypeStruct((batch, 2, M_HEADS, M_HD, 2 * M_HD), F32),
                   jax.ShapeDtypeStruct((batch, 2, M_HEADS, 1, M_HD), F32)],
        scratch_shapes=[pltpu.VMEM((2, seq, width), F32), pltpu.VMEM((2, M_HEADS, M_HD, 2 * M_HD), F32),
                        pltpu.VMEM((2, M_HEADS, 1, M_HD), F32)],
        compiler_params=_cparams(("arbitrary",)),
        name="mlstm_ctx" if zero_init else "mlstm_lat",
    )(*args)


def _rwkv_kernel(*refs, zero_init, seq):
    if zero_init:
        (r_ref, k_ref, v_ref, xl_ref, xg_ref, mur_ref, muk_ref, muv_ref, mul_ref, mug_ref,
         w0_ref, wup_ref, a0_ref, aup_ref, gup_ref, kk_ref, ka_ref, rk_ref, lng_ref, lnb_ref,
         out_ref, so_ref, r_scr, v_scr, kk_scr, g_scr, bonus_scr, lw_scr, kd_scr, b_scr, y_scr, s_scr) = refs
    else:
        (r_ref, k_ref, v_ref, xl_ref, xg_ref, mur_ref, muk_ref, muv_ref, mul_ref, mug_ref,
         w0_ref, wup_ref, a0_ref, aup_ref, gup_ref, kk_ref, ka_ref, rk_ref, lng_ref, lnb_ref, s0_ref,
         out_ref, so_ref, r_scr, v_scr, kk_scr, g_scr, bonus_scr, lw_scr, kd_scr, b_scr, y_scr, s_scr) = refs
    nc = seq // CHUNK
    hd = R_HD
    nh = r_ref.shape[1] // hd

    def tshift(x_ref, mu_ref):
        x = x_ref[...]
        row = lax.broadcasted_iota(I32, x.shape, 0)
        prev = jnp.where(row == 0, 0.0, pltpu.roll(x, 1, 0))
        nxt = jnp.where(row == seq - 1, 0.0, pltpu.roll(x, seq - 1, 0))
        return x + mu_ref[...] * (0.5 * (prev + nxt) - x)

    r = tshift(r_ref, mur_ref)
    k = tshift(k_ref, muk_ref)
    v = tshift(v_ref, muv_ref)
    xl = tshift(xl_ref, mul_ref)
    xg = tshift(xg_ref, mug_ref)
    g = _dot(_sigmoid(xg), gup_ref[...])
    wlanes = nh * hd
    same_head = (lax.broadcasted_iota(I32, (wlanes, wlanes), 0) // hd
                 == lax.broadcasted_iota(I32, (wlanes, wlanes), 1) // hd)
    head_ones = jnp.where(same_head, 1.0, 0.0).astype(BF16)

    def head_sum(x):
        hi = x.astype(BF16)
        r1 = x - hi.astype(F32)
        mid = r1.astype(BF16)
        lo = (r1 - mid.astype(F32)).astype(BF16)
        return _dg(hi, head_ones, _NN) + (_dg(mid, head_ones, _NN) + _dg(lo, head_ones, _NN))

    kkp = k * kk_ref[...]
    kk = kkp * lax.rsqrt(jnp.maximum(head_sum(kkp * kkp), 1e-24))
    tw = jnp.tanh(xl[:, 0:hd])
    xa = xl[:, hd:2 * hd]
    r_scr[...] = r
    v_scr[...] = v
    kk_scr[...] = kk
    g_scr[...] = g
    bonus_scr[...] = head_sum(r * k * rk_ref[...]) * v
    for d in (0, 1):
        lw_scr[d] = -math.exp(-0.5) * _sigmoid(w0_ref[d] + _dot(tw, wup_ref[d]))
        ad = _sigmoid(a0_ref[d] + _dot(xa, aup_ref[d]))
        kd_scr[d] = k * (1.0 + (ad - 1.0) * ka_ref[...])
        b_scr[d] = kk * ad
    npair = nh // 2
    pw = 2 * hd
    zero_blk = jnp.zeros((hd, hd), F32)
    for d in (0, 1):
        for p in range(npair):
            if zero_init:
                s_scr[d, p] = jnp.zeros((pw, pw), F32)
            else:
                s_scr[d, p] = jnp.concatenate(
                    [jnp.concatenate([s0_ref[d, 2 * p], zero_blk], axis=1),
                     jnp.concatenate([zero_blk, s0_ref[d, 2 * p + 1]], axis=1)], axis=0)
    y_scr[...] = jnp.zeros(y_scr.shape, F32)

    ti = lax.broadcasted_iota(I32, (CHUNK, pw), 0)
    si = lax.broadcasted_iota(I32, (CHUNK, pw), 1) % CHUNK
    eye2 = (ti == si).astype(F32)
    tri_i = lax.broadcasted_iota(I32, (CHUNK, CHUNK), 0)
    tri_j = lax.broadcasted_iota(I32, (CHUNK, CHUNK), 1)
    first = lax.broadcasted_iota(I32, (1, pw), 1) < hd
    diag_blk = ((lax.broadcasted_iota(I32, (pw, pw), 0) < hd) == (lax.broadcasted_iota(I32, (pw, pw), 1) < hd))

    def bd(x):
        xb = x.astype(BF16)
        zero = jnp.zeros_like(xb)
        return jnp.concatenate([jnp.where(first, xb, zero), jnp.where(first, zero, xb)], axis=0)

    def chunk(c, carry):
        chains = []
        for d in (0, 1):
            cc = c if d == 0 else nc - 1 - c
            rows = pl.ds(pl.multiple_of(cc * CHUNK, CHUNK), CHUNK)
            incl = (si <= ti) if d == 0 else (si >= ti)
            strict = (si < ti) if d == 0 else (si > ti)
            lw = lw_scr[d, rows, :]
            lw_hi, lw_lo = _split(lw)
            tri = jnp.where((tri_j <= tri_i) if d == 0 else (tri_j >= tri_i), 1.0, 0.0).astype(BF16)
            lc = _dg(tri, lw_hi, _NN) + _dg(tri, lw_lo, _NN)
            l_last = lc[CHUNK - 1:CHUNK, :] if d == 0 else lc[0:1, :]
            e_neg = jnp.exp(-lc)
            e_end = jnp.exp(l_last - lc)
            vc = v_scr[rows, :]
            kdc = kd_scr[d, rows, :]
            bc = b_scr[d, rows, :]
            rt = r_scr[rows, :] * jnp.exp(lc)
            kkt = kk_scr[rows, :] * jnp.exp(lc - lw)
            kh = kdc * e_neg
            bh = bc * e_neg
            kbar = kdc * e_end
            bbar = bc * e_end
            w_end = jnp.exp(l_last)
            for p in range(npair):
                sl = slice(p * pw, (p + 1) * pw)
                chains.append(dict(
                    d=d, pair=p, sl=sl, rows=rows, incl=incl, strict=strict,
                    lhs=jnp.concatenate([kkt[:, sl], rt[:, sl]], axis=0).astype(BF16),
                    rhs=jnp.concatenate([bd(bh[:, sl]), bd(kh[:, sl])], axis=0),
                    end=jnp.concatenate([kbar[:, sl], bbar[:, sl]], axis=0),
                    v=vc[:, sl], w_end=w_end[:, sl], s0=s_scr[d, p]))
        for ch in chains:
            ch['ab'] = _dg(ch['lhs'], ch['rhs'], _NT)
        for ch in chains:
            ch['proj'] = _dg(ch['lhs'], ch['s0'].astype(BF16), _NT)
        for ch in chains:
            ab = ch['ab']
            ch['a_kb'] = jnp.where(ch['strict'], ab[:CHUNK, :pw], 0.0)
            ch['b_rb'] = jnp.where(ch['incl'], ab[CHUNK:, :pw], 0.0)
            ch['akk_brk'] = jnp.concatenate([jnp.where(ch['strict'], ab[:CHUNK, pw:], 0.0),
                                             jnp.where(ch['incl'], ab[CHUNK:, pw:], 0.0)], axis=0)
        for ch in chains:
            ch['p'] = _dg(ch['a_kb'].astype(BF16), bd(ch['a_kb']), _NN)
        for ch in chains:
            ch['abv'] = _dg(ch['akk_brk'].astype(BF16), bd(ch['v']), _NN)
        for ch in chains:
            inv = eye2 - ch['a_kb']
            ch['inv'] = inv + _dg(inv.astype(BF16), bd(ch['p']), _NN)
        span = 4
        while span < CHUNK:
            for ch in chains:
                ch['p'] = _dg(ch['p'].astype(BF16), bd(ch['p']), _NN)
            for ch in chains:
                ch['inv'] = ch['inv'] + _dg(ch['inv'].astype(BF16), bd(ch['p']), _NN)
            span *= 2
        for ch in chains:
            ch['u'] = _dg(ch['inv'].astype(BF16), bd(ch['proj'][:CHUNK] + ch['abv'][:CHUNK]), _NN)
        for ch in chains:
            y = ch['proj'][CHUNK:] + ch['abv'][CHUNK:] - _dg(ch['b_rb'].astype(BF16), bd(ch['u']), _NN)
            y_scr[ch['rows'], ch['sl']] += y
        for ch in chains:
            upd = _dot3(jnp.concatenate([ch['v'], -ch['u']], axis=0), ch['end'], _TN)
            s_scr[ch['d'], ch['pair']] = ch['s0'] * ch['w_end'] + jnp.where(diag_blk, upd, 0.0)
        return carry

    lax.fori_loop(0, nc, chunk, 0)

    y = y_scr[...]
    dev = y - head_sum(y) * (1.0 / hd)
    yn = dev * lax.rsqrt(head_sum(dev * dev) * (1.0 / hd) + GN_EPS)
    out_ref[...] = (yn * lng_ref[...] + lnb_ref[...] + bonus_scr[...]) * g_scr[...]
    for d in (0, 1):
        for p in range(npair):
            s = s_scr[d, p]
            so_ref[d, 2 * p] = s[:hd, :hd]
            so_ref[d, 2 * p + 1] = s[hd:, hd:]


def _rwkv(u, p, state, row0, batch, seq, heads_per_step):
    zero_init = state is None
    nh = heads_per_step
    wd = nh * R_HD
    nsteps = R_HEADS // nh
    width = R_HEADS * R_HD
    rc = COL_R // wd
    sec = width // wd
    lc = (COL_R + 3 * width) // LANES
    ublk = lambda off: pl.BlockSpec((seq, wd), lambda b, h: (row0 + b, rc + off + h))
    ufix = lambda blk: pl.BlockSpec((seq, LANES), lambda b, h: (row0 + b, blk))
    mblk = lambda off: pl.BlockSpec((1, wd), lambda b, h: (0, off + h))
    mfix = lambda blk: pl.BlockSpec((1, LANES), lambda b, h: (0, blk))
    vec = pl.BlockSpec((1, wd), lambda b, h: (0, h))
    in_specs = [ublk(0), ublk(sec), ublk(2 * sec), ufix(lc), ufix(lc + 1),
                mblk(0), mblk(sec), mblk(2 * sec), mfix(lc - COL_R // LANES), mfix(lc - COL_R // LANES + 1),
                pl.BlockSpec((2, 1, wd), lambda b, h: (0, 0, h)),
                pl.BlockSpec((2, R_HD, wd), lambda b, h: (0, 0, h)),
                pl.BlockSpec((2, 1, wd), lambda b, h: (0, 0, h)),
                pl.BlockSpec((2, R_HD, wd), lambda b, h: (0, 0, h)),
                pl.BlockSpec((LANES, wd), lambda b, h: (0, h)),
                vec, vec, vec, vec, vec]
    args = [u, u, u, u, u, p['mu'], p['mu'], p['mu'], p['mu'], p['mu'],
            p['w0'], p['w_up'], p['a0'], p['a_up'], p['g_up'], p['k_k'], p['k_a'], p['r_k'], p['ln_g'], p['ln_b']]
    if not zero_init:
        in_specs.append(pl.BlockSpec((None, 2, nh, R_HD, R_HD), lambda b, h: (b, 0, h, 0, 0)))
        args.append(state)
    big = lambda n: pltpu.VMEM((n, seq, wd), F32)
    return pl.pallas_call(
        functools.partial(_rwkv_kernel, zero_init=zero_init, seq=seq),
        grid=(batch, nsteps),
        in_specs=in_specs,
        out_specs=[pl.BlockSpec((seq, wd), lambda b, h: (b, h)),
                   pl.BlockSpec((None, 2, nh, R_HD, R_HD), lambda b, h: (b, 0, h, 0, 0))],
        out_shape=[jax.ShapeDtypeStruct((batch * seq, width), F32),
                   jax.ShapeDtypeStruct((batch, 2, R_HEADS, R_HD, R_HD), F32)],
        scratch_shapes=[pltpu.VMEM((seq, wd), F32)] * 5
                       + [big(2), big(2), big(2), pltpu.VMEM((seq, wd), F32),
                          pltpu.VMEM((2, nh // 2, 2 * R_HD, 2 * R_HD), F32)],
        compiler_params=_cparams(("arbitrary", "arbitrary")),
        name="rwkv_ctx" if zero_init else "rwkv_lat",
    )(*args)


def _top2_sum(a, b, c, d):
    m1, n1 = jnp.maximum(a, b), jnp.minimum(a, b)
    m2, n2 = jnp.maximum(c, d), jnp.minimum(c, d)
    return jnp.maximum(m1, m2) + jnp.maximum(jnp.minimum(m1, m2), jnp.maximum(n1, n2))


def _first_argmax(vals):
    best = functools.reduce(jnp.maximum, vals)
    idx = jnp.full(best.shape, len(vals) - 1, I32)
    for j in range(len(vals) - 2, -1, -1):
        idx = jnp.where(vals[j] == best, j, idx)
    return best, idx


def _out_kernel(attc_ref, attl_ref, mc_ref, ml_ref, rc_ref, rl_ref, xc_ref, xl_ref, w_ref, g1_ref, sh2_ref, sc2_ref,
                n2_ref, rw_ref, rb_ref, x1_ref, h2_ref, route_ref, cnt_ref, cnt_scr, *, n_ctx_tiles):
    @pl.when(pl.program_id(0) == 0)
    def _():
        cnt_scr[...] = jnp.zeros(cnt_scr.shape, F32)

    is_ctx = pl.program_id(0) < n_ctx_tiles
    both = lambda c_ref, l_ref: jnp.where(is_ctx, c_ref[...], l_ref[...])
    na = attc_ref.shape[1]
    nm = mc_ref.shape[1]
    mix = (jnp.dot(both(attc_ref, attl_ref).astype(BF16), w_ref[0:na, :], preferred_element_type=F32)
           + jnp.dot(both(mc_ref, ml_ref).astype(BF16), w_ref[na:na + nm, :], preferred_element_type=F32)
           + jnp.dot(both(rc_ref, rl_ref).astype(BF16), w_ref[na + nm:, :], preferred_element_type=F32))
    x1 = both(xc_ref, xl_ref) + g1_ref[...] * mix
    x1_ref[...] = x1
    h2 = _rms(x1, n2_ref[...]) * (1.0 + sc2_ref[...]) + sh2_ref[...]
    h2_ref[...] = h2
    logits = _dot1(rw_ref[...], h2, _NT)
    s = _sigmoid(logits)
    ssel = s + rb_ref[...]
    srow = [s[e:e + 1, :] for e in range(N_EXPERTS)]
    brow = [ssel[e:e + 1, :] for e in range(N_EXPERTS)]
    gscore = [_top2_sum(*brow[EXPERTS_PER_GROUP * g:EXPERTS_PER_GROUP * (g + 1)]) for g in range(N_EXPERT_GROUPS)]
    _, gidx = _first_argmax(gscore)
    pick = lambda rows, j: functools.reduce(
        lambda acc, g: jnp.where(gidx == g, rows[EXPERTS_PER_GROUP * g + j], acc),
        range(N_EXPERT_GROUPS - 2, -1, -1), rows[EXPERTS_PER_GROUP * (N_EXPERT_GROUPS - 1) + j])
    ing = [pick(brow, j) for j in range(EXPERTS_PER_GROUP)]
    sin_ = [pick(srow, j) for j in range(EXPERTS_PER_GROUP)]
    _, l1 = _first_argmax(ing)
    _, l2 = _first_argmax([jnp.where(l1 == j, -jnp.inf, ing[j]) for j in range(EXPERTS_PER_GROUP)])
    sel = lambda l: functools.reduce(lambda acc, j: jnp.where(l == j, sin_[j], acc),
                                     range(EXPERTS_PER_GROUP - 2, -1, -1), sin_[EXPERTS_PER_GROUP - 1])
    w1, w2 = sel(l1), sel(l2)
    tot = w1 + w2
    e1 = gidx * EXPERTS_PER_GROUP + l1
    e2 = gidx * EXPERTS_PER_GROUP + l2
    tm = e1.shape[1]
    eid = lax.broadcasted_iota(I32, (N_EXPERTS, tm), 0)
    oh1 = eid == e1
    oh2 = eid == e2
    picked = jnp.where(jnp.logical_or(oh1, oh2), 1.0, 0.0)
    earlier = jnp.where(lax.broadcasted_iota(I32, (tm, tm), 0) < lax.broadcasted_iota(I32, (tm, tm), 1), 1.0, 0.0)
    rank = cnt_scr[:, 0:1] + jnp.dot(picked.astype(BF16), earlier.astype(BF16), preferred_element_type=F32)
    pos1 = jnp.sum(jnp.where(oh1, rank, 0.0), axis=0, keepdims=True)
    pos2 = jnp.sum(jnp.where(oh2, rank, 0.0), axis=0, keepdims=True)
    cnt = cnt_scr[...] + jnp.sum(picked, axis=1, keepdims=True)
    cnt_scr[...] = cnt
    cnt_ref[...] = cnt
    zero = jnp.zeros_like(w1)
    route_ref[...] = jnp.concatenate([e1.astype(F32), e2.astype(F32), w1 / tot, w2 / tot, pos1, pos2, zero, zero],
                                     axis=0)


def _out_proj(att, m_out, r_out, x, w_out, mod3, n2g, rw_t, rb, s_lat):
    t_ctx, d = x[0].shape
    ntok = t_ctx + x[1].shape[0]
    tm = 256
    nct = t_ctx // tm
    row = lambda i: _mod_row(i * tm, t_ctx, s_lat)
    modblk = lambda j: pl.BlockSpec((None, 1, d), lambda i: (row(i), 0, j))
    pair = lambda a: [pl.BlockSpec((tm, a[0].shape[1]), lambda i: (jnp.minimum(i, nct - 1), 0)),
                      pl.BlockSpec((tm, a[1].shape[1]), lambda i: (jnp.maximum(i - nct, 0), 0))]
    return pl.pallas_call(
        functools.partial(_out_kernel, n_ctx_tiles=nct),
        grid=(ntok // tm,),
        in_specs=pair(att) + pair(m_out) + pair(r_out) + pair(x) + [
                  pl.BlockSpec(w_out.shape, lambda i: (0, 0)),
                  modblk(2), modblk(3), modblk(4),
                  pl.BlockSpec((1, d), lambda i: (0, 0)),
                  pl.BlockSpec(rw_t.shape, lambda i: (0, 0)),
                  pl.BlockSpec(rb.shape, lambda i: (0, 0))],
        out_specs=[pl.BlockSpec((tm, d), lambda i: (i, 0)),
                   pl.BlockSpec((tm, d), lambda i: (i, 0)),
                   pl.BlockSpec((8, tm), lambda i: (0, i)),
                   pl.BlockSpec((N_EXPERTS, LANES), lambda i: (0, 0))],
        out_shape=[jax.ShapeDtypeStruct((ntok, d), F32), jax.ShapeDtypeStruct((ntok, d), F32),
                   jax.ShapeDtypeStruct((8, ntok), F32), jax.ShapeDtypeStruct((N_EXPERTS, LANES), F32)],
        scratch_shapes=[pltpu.VMEM((N_EXPERTS, LANES), F32)],
        compiler_params=_cparams(("arbitrary",)),
        name="out_proj",
    )(*att, *m_out, *r_out, *x, w_out, mod3, mod3, mod3, n2g, rw_t, rb)


def _row_gather(src_hbm, idx_ref, base, dst, sem, n, unrolled, lo=0):
    def start(j):
        pltpu.make_async_copy(src_hbm.at[pl.ds(idx_ref[base + j], 1), :], dst.at[pl.ds(j, 1), :], sem).start()

    if unrolled:
        for j in range(lo, n):
            start(j)
    else:
        def body(j, c):
            start(j)
            return c
        lax.fori_loop(lo, n, body, 0)


def _expert_kernel(be_ref, tok_ref, nused_ref, h_hbm, w1_ref, w3_ref, w2_ref, y_ref, xbuf, w1b, w3b, w2b, sem):
    i = pl.program_id(0)
    n_used = nused_ref[0]
    wait = lambda s: pltpu.make_async_copy(h_hbm.at[pl.ds(0, EXPERT_ROWS), :], xbuf.at[s], sem.at[s]).wait()

    @pl.when(i < n_used)
    def _():
        slot = i % 2

        @pl.when(i == 0)
        def _():
            _row_gather(h_hbm, tok_ref, 0, xbuf.at[0], sem.at[0], EXPERT_ROWS, unrolled=False)

        @pl.when(jnp.logical_or(i == 0, be_ref[i] != be_ref[jnp.maximum(i - 1, 0)]))
        def _():
            w1b[...] = w1_ref[...].astype(BF16)
            w3b[...] = w3_ref[...].astype(BF16)
            w2b[...] = w2_ref[...].astype(BF16)

        wait(slot)
        de = w1b.shape[1]
        half = de // 2
        groups = 6
        per = EXPERT_ROWS // groups

        def next_rows(gi):
            hi = EXPERT_ROWS if gi == groups - 1 else (gi + 1) * per
            _row_gather(h_hbm, tok_ref, (i + 1) * EXPERT_ROWS, xbuf.at[1 - slot], sem.at[1 - slot], hi,
                        unrolled=True, lo=gi * per)

        xb = xbuf[slot].astype(BF16)
        hmid = []
        for n in range(2):
            cols = slice(n * half, (n + 1) * half)
            next_rows(2 * n)
            a = jnp.dot(xb, w1b[:, cols], preferred_element_type=F32)
            next_rows(2 * n + 1)
            b = jnp.dot(xb, w3b[:, cols], preferred_element_type=F32)
            hmid.append(((a * _sigmoid(a)) * b).astype(BF16))
        next_rows(4)
        y = jnp.dot(hmid[0], w2b[0:half, :], preferred_element_type=F32)
        next_rows(5)
        y_ref[...] = y + jnp.dot(hmid[1], w2b[half:, :], preferred_element_type=F32)

        @pl.when(i == n_used - 1)
        def _():
            wait(1 - slot)

    @pl.when(i >= n_used)
    def _():
        y_ref[...] = jnp.zeros(y_ref.shape, F32)


def _experts(h2, block_e, row_tok, n_used, w1, w3, w2, layer):
    ntok, d = h2.shape
    de = w1.shape[3]
    n_rows = row_tok.shape[0] - EXPERT_ROWS
    nb = n_rows // EXPERT_ROWS
    return pl.pallas_call(
        _expert_kernel,
        grid_spec=pltpu.PrefetchScalarGridSpec(
            num_scalar_prefetch=3,
            grid=(nb,),
            in_specs=[pl.BlockSpec(memory_space=pl.ANY),
                      pl.BlockSpec((None, None, d, de), lambda i, be, tok, nu: (layer, be[i], 0, 0)),
                      pl.BlockSpec((None, None, d, de), lambda i, be, tok, nu: (layer, be[i], 0, 0)),
                      pl.BlockSpec((None, None, de, d), lambda i, be, tok, nu: (layer, be[i], 0, 0))],
            out_specs=pl.BlockSpec((EXPERT_ROWS, d), lambda i, be, tok, nu: (i, 0)),
            scratch_shapes=[pltpu.VMEM((2, EXPERT_ROWS, d), F32), pltpu.VMEM((d, de), BF16),
                            pltpu.VMEM((d, de), BF16), pltpu.VMEM((de, d), BF16), pltpu.SemaphoreType.DMA((2,))]),
        out_shape=jax.ShapeDtypeStruct((n_rows, d), F32),
        compiler_params=_cparams(("arbitrary",)),
        name="experts",
    )(block_e, row_tok, n_used, h2, w1, w3, w2)


def _combine_kernel(d1_ref, d2_ref, y_hbm, x1_ref, g2_ref, gate_ref, oc_ref, ol_ref, ybuf, sem, *, n_ctx_tiles):
    tm = x1_ref.shape[0]
    i = pl.program_id(0)
    slot = i % 2

    def gather(tile, s, unrolled):
        _row_gather(y_hbm, d1_ref, tile * tm, ybuf.at[s, 0], sem.at[s], tm, unrolled)
        _row_gather(y_hbm, d2_ref, tile * tm, ybuf.at[s, 1], sem.at[s], tm, unrolled)

    def wait(s):
        pltpu.make_async_copy(y_hbm.at[pl.ds(0, tm), :], ybuf.at[s, 0], sem.at[s]).wait()
        pltpu.make_async_copy(y_hbm.at[pl.ds(0, tm), :], ybuf.at[s, 1], sem.at[s]).wait()

    @pl.when(i == 0)
    def _():
        gather(0, 0, False)

    wait(slot)
    gather(i + 1, 1 - slot, True)
    gate = gate_ref[...]
    moe = ybuf[slot, 0] * gate[:, 0:1] + ybuf[slot, 1] * gate[:, 1:2]
    new_x = x1_ref[...] + g2_ref[...] * moe

    @pl.when(i < n_ctx_tiles)
    def _():
        oc_ref[...] = new_x

    @pl.when(i >= n_ctx_tiles)
    def _():
        ol_ref[...] = new_x

    @pl.when(i == pl.num_programs(0) - 1)
    def _():
        wait(1 - slot)


def _combine(y, x1, mod3, gate, dest1, dest2, t_ctx, s_lat):
    ntok, d = x1.shape
    tm = 256
    nct = t_ctx // tm
    row = lambda i: _mod_row(i * tm, t_ctx, s_lat)
    return pl.pallas_call(
        functools.partial(_combine_kernel, n_ctx_tiles=nct),
        grid_spec=pltpu.PrefetchScalarGridSpec(
            num_scalar_prefetch=2,
            grid=(ntok // tm,),
            in_specs=[pl.BlockSpec(memory_space=pl.ANY),
                      pl.BlockSpec((tm, d), lambda i, a, b: (i, 0)),
                      pl.BlockSpec((None, 1, d), lambda i, a, b: (row(i), 0, 5)),
                      pl.BlockSpec((tm, 2), lambda i, a, b: (i, 0))],
            out_specs=[pl.BlockSpec((tm, d), lambda i, a, b: (jnp.minimum(i, nct - 1), 0)),
                       pl.BlockSpec((tm, d), lambda i, a, b: (jnp.maximum(i - nct, 0), 0))],
            scratch_shapes=[pltpu.VMEM((2, 2, tm, d), F32), pltpu.SemaphoreType.DMA((2,))]),
        out_shape=[jax.ShapeDtypeStruct((t_ctx, d), F32), jax.ShapeDtypeStruct((ntok - t_ctx, d), F32)],
        compiler_params=_cparams(("arbitrary",)),
        name="combine",
    )(dest1, dest2, y, x1, mod3, gate)


def _dispatch(route, counts, tile):
    ntok = route.shape[1]
    e = route[0:2].astype(I32)
    pos = route[4:6].astype(I32)
    gate = route[2:4].T
    counts = counts[:, 0].astype(I32)
    padded = (counts + EXPERT_ROWS - 1) // EXPERT_ROWS * EXPERT_ROWS
    pad_end = jnp.cumsum(padded)
    pad_start = pad_end - padded
    onehot = (e[:, :, None] == jnp.arange(N_EXPERTS, dtype=I32)).astype(I32)
    dest = jnp.sum(onehot * pad_start, axis=-1) + pos
    n_rows = -(-(2 * ntok) // EXPERT_ROWS) * EXPERT_ROWS + N_EXPERTS * EXPERT_ROWS
    nb = n_rows // EXPERT_ROWS
    tok = jnp.broadcast_to(jnp.arange(ntok, dtype=I32)[None, :], (2, ntok))
    row_tok = jnp.zeros((n_rows + EXPERT_ROWS,), I32).at[dest.reshape(-1)].set(tok.reshape(-1), unique_indices=True)
    blk_start = jnp.arange(nb, dtype=I32) * EXPERT_ROWS
    block_e = jnp.minimum(jnp.sum((pad_end[None, :] <= blk_start[:, None]).astype(I32), axis=1), N_EXPERTS - 1)
    n_used = pad_end[-1:] // EXPERT_ROWS
    dest = jnp.pad(dest, ((0, 0), (0, tile)))
    return row_tok, block_e, n_used, gate, dest[0], dest[1]


def kernel(x_prompt, x_sample, c, cache_attn_k, cache_attn_v, state_mlstm_C, state_mlstm_n, state_mlstm_m, state_rwkv, c_ctx, norm1_g, norm2_g, w_mod, b_mod, w_in, w_out, attn_q_norm, attn_k_norm, mlstm_i_bias, mlstm_f_bias, mlstm_norm_g, rwkv_mu, rwkv_w0, rwkv_w_up, rwkv_a0, rwkv_a_up, rwkv_g_up, rwkv_k_k, rwkv_k_a, rwkv_r_k, rwkv_ln_g, rwkv_ln_b, router_w, router_b, exp_w1, exp_w3, exp_w2):
    b_ctx, s_ctx, d = x_prompt.shape
    b_lat, s_lat, _ = x_sample.shape
    depth = w_in.shape[0]
    t_ctx = b_ctx * s_ctx
    ntok = t_ctx + b_lat * s_lat
    assert b_lat + 1 <= 8 and s_lat % 1024 == 0 and t_ctx % 1024 == 0 and t_ctx % s_lat == 0
    past = cache_attn_k.shape[2]

    x = (x_prompt.reshape(t_ctx, d), x_sample.reshape(b_lat * s_lat, d))
    c_all = jnp.zeros((8, d), F32).at[0].set(c_ctx).at[1:1 + b_lat].set(c)
    mod = _modulation(c_all, w_mod, b_mod)
    cos, sin = _rope_tables(s_lat)
    rw_t = router_w.T
    rb = router_b.reshape(N_EXPERTS, 1)
    r_width = R_HEADS * R_HD
    n_in = w_in.shape[2]
    gate_hi = COL_R + 4 * M_HEADS
    w_in_t = jnp.swapaxes(w_in, 1, 2)
    w_tail_t = jnp.concatenate([w_in_t[:, gate_hi:], w_in_t[:, COL_R:gate_hi],
                                jnp.zeros((depth, N_IN_PAD - n_in, d), F32)], axis=1).astype(BF16)

    ks, vs, cs, ns, ms, rs = [], [], [], [], [], []
    for l in range(depth):
        mod3 = mod[l].reshape(8, 1, 6 * d)
        u = _in_proj(x[0], x[1], norm1_g[l][None], mod3, w_in_t, l, w_tail_t[l], s_lat)

        ck = cache_attn_k[:, l].reshape(b_lat, past, ATT_KV_HEADS * ATT_HD)
        cv = cache_attn_v[:, l].reshape(b_lat, past, ATT_KV_HEADS * ATT_HD)
        att_c, att_l, k_ctx, v_ctx = _attention(u, attn_q_norm[l][None], attn_k_norm[l][None], ck, cv, cos, sin,
                                                b_ctx, s_ctx, b_lat, s_lat)
        ks.append(k_ctx.reshape(b_ctx, s_ctx, ATT_KV_HEADS, ATT_HD))
        vs.append(v_ctx.reshape(b_ctx, s_ctx, ATT_KV_HEADS, ATT_HD))

        gcol = u[:, COL_GATE:COL_GATE + 4 * M_HEADS]
        gt = gcol.reshape(ntok // CHUNK, CHUNK, 4 * M_HEADS).transpose(0, 2, 1)
        bias = jnp.stack([mlstm_i_bias[l], mlstm_f_bias[l]], axis=1)
        ng = mlstm_norm_g[l][None]
        m_c_out, cx_c, m_c = _mlstm(u, gcol, gt, bias, ng, None, 0, b_ctx, s_ctx)
        n_col = jnp.pad(state_mlstm_n[:, l][..., None], ((0, 0),) * 4 + ((0, M_HD - 1),))
        lat_states = (jnp.concatenate([state_mlstm_C[:, l], n_col], axis=-1),
                      jnp.broadcast_to(state_mlstm_m[:, l][..., None, None], (b_lat, 2, M_HEADS, 1, M_HD)))
        m_l_out, _, _ = _mlstm(u, gcol, gt, bias, ng, lat_states, t_ctx // s_lat, b_lat, s_lat)
        cs.append(cx_c[..., :M_HD])
        ns.append(cx_c[..., M_HD])
        ms.append(m_c[:, :, :, 0, 0])

        rp = dict(mu=rwkv_mu[l][None], w0=rwkv_w0[l].reshape(2, 1, r_width), w_up=rwkv_w_up[l],
                  a0=rwkv_a0[l].reshape(2, 1, r_width), a_up=rwkv_a_up[l], g_up=rwkv_g_up[l],
                  k_k=rwkv_k_k[l][None], k_a=rwkv_k_a[l][None], r_k=rwkv_r_k[l].reshape(1, r_width),
                  ln_g=rwkv_ln_g[l][None], ln_b=rwkv_ln_b[l][None])
        r_c_out, r_c = _rwkv(u, rp, None, 0, b_ctx, s_ctx, RWKV_HEADS_PER_STEP_CTX)
        r_l_out, _ = _rwkv(u, rp, state_rwkv[:, l], t_ctx // s_lat, b_lat, s_lat, RWKV_HEADS_PER_STEP_LAT)
        rs.append(r_c)

        x1, h2, route, counts = _out_proj((att_c, att_l), (m_c_out, m_l_out), (r_c_out, r_l_out), x,
                                          w_out[l].astype(BF16), mod3, norm2_g[l][None], rw_t, rb, s_lat)
        row_tok, block_e, n_used, gate, dest1, dest2 = _dispatch(route, counts, 256)
        y = _experts(h2, block_e, row_tok, n_used, exp_w1, exp_w3, exp_w2, l)
        x = _combine(y, x1, mod3, gate, dest1, dest2, t_ctx, s_lat)

    y_prompt = x[0].reshape(b_ctx, s_ctx, d)
    y_sample = x[1].reshape(b_lat, s_lat, d)
    return (y_prompt, y_sample, jnp.stack(ks, axis=1), jnp.stack(vs, axis=1), jnp.stack(cs, axis=1),
            jnp.stack(ns, axis=1), jnp.stack(ms, axis=1), jnp.stack(rs, axis=1))
```

```python
import functools
import math

import jax
import jax.numpy as jnp
from jax import lax
from jax.experimental import pallas as pl
from jax.experimental.pallas import tpu as pltpu

F32 = jnp.float32
BF16 = jnp.bfloat16
I32 = jnp.int32

NORM_EPS = 1e-6
GN_EPS = 64e-5
M_INIT = -1e30
GRID_W = 64
ROPE_THETA = 10000.0
ATT_HD = 128
ATT_GROUPS = 4
ATT_KV_HEADS = 2
M_HD = 128
M_HEADS = 4
R_HD = 64
R_HEADS = 8
N_EXPERTS = 16
N_EXPERT_GROUPS = 4
EXPERTS_PER_GROUP = 4
CHUNK = 64
LANES = 128
EXPERT_ROWS = 128
VMEM_LIMIT = 58 * 1024 * 1024
RWKV_HEADS_PER_STEP_CTX = 8
RWKV_HEADS_PER_STEP_LAT = 8

COL_ATT = 0
COL_M = 1536
COL_R = 3584
COL_GATE = 5376
N_IN_PAD = 5632


def _cparams(sem):
    return pltpu.CompilerParams(dimension_semantics=sem, vmem_limit_bytes=VMEM_LIMIT)


def _dot(a, b):
    return jnp.dot(a.astype(BF16), b.astype(BF16), preferred_element_type=F32)


def _dg(a, b, dims):
    return lax.dot_general(a, b, (dims, ((), ())), preferred_element_type=F32)


_NN = ((1,), (0,))
_NT = ((1,), (1,))
_TN = ((0,), (0,))


def _split(a):
    hi = a.astype(BF16)
    lo = (a - hi.astype(F32)).astype(BF16)
    return hi, lo


def _dot3(a, b, dims=_NN):
    ah, al = _split(a)
    bh, bl = _split(b)
    return _dg(ah, bh, dims) + (_dg(ah, bl, dims) + _dg(al, bh, dims))


def _dot1(a, b, dims=_NN):
    return _dg(a.astype(BF16), b.astype(BF16), dims)


def _rms(x, g):
    return x * lax.rsqrt(jnp.mean(x * x, axis=-1, keepdims=True) + NORM_EPS) * g


def _sigmoid(x):
    return 1.0 / (1.0 + jnp.exp(-x))


def _mod_kernel(c_ref, w_ref, b_ref, o_ref):
    c = c_ref[...]
    o_ref[...] = _dot(c * _sigmoid(c), w_ref[...]) + b_ref[...]


def _modulation(c_all, w_mod, b_mod):
    depth, d, n = w_mod.shape
    tn = 1024
    return pl.pallas_call(
        _mod_kernel,
        grid=(depth, n // tn),
        in_specs=[pl.BlockSpec((8, d), lambda l, j: (0, 0)),
                  pl.BlockSpec((None, d, tn), lambda l, j: (l, 0, j)),
                  pl.BlockSpec((None, 1, tn), lambda l, j: (l, 0, j))],
        out_specs=pl.BlockSpec((None, 8, tn), lambda l, j: (l, 0, j)),
        out_shape=jax.ShapeDtypeStruct((depth, 8, n), F32),
        compiler_params=_cparams(("arbitrary", "arbitrary")),
        name="modulation",
    )(c_all, w_mod, b_mod.reshape(depth, 1, n))


def _mod_row(tok0, t_ctx, s_lat):
    return jnp.where(tok0 < t_ctx, 0, 1 + (tok0 - t_ctx) // s_lat)


def _in_kernel(xc_ref, xl_ref, g_ref, sh_ref, sc_ref, wa_ref, wb_ref, o_ref, h_ref, *, n_ctx_tiles, n_head_tiles):
    i = pl.program_id(0)
    j = pl.program_id(1)

    def normalise(x_ref):
        slab = 256
        for r0 in range(0, x_ref.shape[0], slab):
            h = _rms(x_ref[r0:r0 + slab, :], g_ref[...]) * (1.0 + sc_ref[...]) + sh_ref[...]
            h_ref[r0:r0 + slab, :] = h.astype(BF16)

    @pl.when(jnp.logical_and(j == 0, i < n_ctx_tiles))
    def _():
        normalise(xc_ref)

    @pl.when(jnp.logical_and(j == 0, i >= n_ctx_tiles))
    def _():
        normalise(xl_ref)

    @pl.when(j < n_head_tiles)
    def _():
        o_ref[...] = _dg(h_ref[...], wa_ref[...].astype(BF16), _NT)

    @pl.when(j >= n_head_tiles)
    def _():
        o_ref[...] = _dg(h_ref[...], wb_ref[...], _NT)


def _in_proj(xc, xl, g1, mod3, w_in_t, layer, w_tail_t, s_lat):
    t_ctx, d = xc.shape
    ntok = t_ctx + xl.shape[0]
    tm, tn = 1024, 512
    na = COL_R // tn
    n = COL_R + w_tail_t.shape[0]
    nct = t_ctx // tm
    row = lambda i: _mod_row(i * tm, t_ctx, s_lat)
    return pl.pallas_call(
        functools.partial(_in_kernel, n_ctx_tiles=nct, n_head_tiles=na),
        grid=(ntok // tm, n // tn),
        in_specs=[pl.BlockSpec((tm, d), lambda i, j: (jnp.minimum(i, nct - 1), 0)),
                  pl.BlockSpec((tm, d), lambda i, j: (jnp.maximum(i - nct, 0), 0)),
                  pl.BlockSpec((1, d), lambda i, j: (0, 0)),
                  pl.BlockSpec((None, 1, d), lambda i, j: (row(i), 0, 0)),
                  pl.BlockSpec((None, 1, d), lambda i, j: (row(i), 0, 1)),
                  pl.BlockSpec((None, tn, d), lambda i, j: (layer, jnp.minimum(j, na - 1), 0)),
                  pl.BlockSpec((tn, d), lambda i, j: (jnp.maximum(j - na, 0), 0))],
        out_specs=pl.BlockSpec((tm, tn), lambda i, j: (i, j)),
        out_shape=jax.ShapeDtypeStruct((ntok, n), F32),
        scratch_shapes=[pltpu.VMEM((tm, d), BF16)],
        compiler_params=_cparams(("arbitrary", "arbitrary")),
        name="in_proj",
    )(xc, xl, g1, mod3, mod3, w_in_t, w_tail_t)


def _softmax_av(q, kb, vb):
    s = _dg(q.astype(BF16), kb, _NT) * (ATT_HD ** -0.5)
    p = jnp.exp(s - jnp.max(s, axis=-1, keepdims=True))
    l = jnp.sum(p, axis=-1, keepdims=True)
    return jnp.dot(p.astype(BF16), vb, preferred_element_type=F32) / l


def _att_ctx_kernel(q_ref, k_ref, v_ref, qn_ref, kn_ref, o_ref, ko_ref, vo_ref):
    k = _rms(k_ref[...], kn_ref[...])
    ko_ref[...] = k
    vo_ref[...] = v_ref[...]
    kb = k.astype(BF16)
    vb = v_ref[...].astype(BF16)
    for g in range(ATT_GROUPS):
        q = _rms(q_ref[:, g * ATT_HD:(g + 1) * ATT_HD], qn_ref[...])
        o_ref[:, g * ATT_HD:(g + 1) * ATT_HD] = _softmax_av(q, kb, vb)


def _rope(x, cos, sin):
    lane = lax.broadcasted_iota(I32, x.shape, 1)
    first = (lane % (ATT_HD // 2)) < (ATT_HD // 4)
    partner = jnp.where(first, pltpu.roll(x, ATT_HD - ATT_HD // 4, 1), pltpu.roll(x, ATT_HD // 4, 1))
    return x * cos + partner * sin


def _att_lat_kernel(q_ref, k_ref, v_ref, ck_ref, cv_ref, qn_ref, kn_ref, cosq_ref, sinq_ref, cosk_ref, sink_ref,
                    o_ref, kb_ref, vb_ref, *, past):
    @pl.when(pl.program_id(2) == 0)
    def _():
        k = _rope(_rms(k_ref[...], kn_ref[...]), cosk_ref[...], sink_ref[...])
        kb_ref[0:past, :] = ck_ref[...].astype(BF16)
        kb_ref[past:, :] = k.astype(BF16)
        vb_ref[0:past, :] = cv_ref[...].astype(BF16)
        vb_ref[past:, :] = v_ref[...].astype(BF16)

    kb = kb_ref[...]
    vb = vb_ref[...]
    for g in range(ATT_GROUPS):
        q = _rope(_rms(q_ref[:, g * ATT_HD:(g + 1) * ATT_HD], qn_ref[...]), cosq_ref[...], sinq_ref[...])
        o_ref[:, g * ATT_HD:(g + 1) * ATT_HD] = _softmax_av(q, kb, vb)


def _attention(u, qn, kn, cache_k, cache_v, cos, sin, b_ctx, s_ctx, b_lat, s_lat):
    ntok = u.shape[0]
    t_ctx = b_ctx * s_ctx
    gw = ATT_GROUPS * ATT_HD
    kcol = (ATT_KV_HEADS * gw) // ATT_HD
    vcol = kcol + ATT_KV_HEADS
    kv_spec = pl.BlockSpec((s_ctx, ATT_HD), lambda b, h: (b, h))
    att_c, k_ctx, v_ctx = pl.pallas_call(
        _att_ctx_kernel,
        grid=(b_ctx, ATT_KV_HEADS),
        in_specs=[pl.BlockSpec((s_ctx, gw), lambda b, h: (b, h)),
                  pl.BlockSpec((s_ctx, ATT_HD), lambda b, h: (b, kcol + h)),
                  pl.BlockSpec((s_ctx, ATT_HD), lambda b, h: (b, vcol + h)),
                  pl.BlockSpec((1, ATT_HD), lambda b, h: (0, 0)),
                  pl.BlockSpec((1, ATT_HD), lambda b, h: (0, 0))],
        out_specs=[pl.BlockSpec((s_ctx, gw), lambda b, h: (b, h)), kv_spec, kv_spec],
        out_shape=[jax.ShapeDtypeStruct((t_ctx, ATT_KV_HEADS * gw), F32),
                   jax.ShapeDtypeStruct((t_ctx, ATT_KV_HEADS * ATT_HD), F32),
                   jax.ShapeDtypeStruct((t_ctx, ATT_KV_HEADS * ATT_HD), F32)],
        compiler_params=_cparams(("arbitrary", "arbitrary")),
        name="att_ctx",
    )(u, u, u, qn, kn)

    tq = 256
    nqb = s_lat // tq
    past = cache_k.shape[1]
    qrow0 = t_ctx // tq
    krow0 = t_ctx // s_lat
    att_l = pl.pallas_call(
        functools.partial(_att_lat_kernel, past=past),
        grid=(b_lat, ATT_KV_HEADS, nqb),
        in_specs=[pl.BlockSpec((tq, gw), lambda b, h, i: (qrow0 + b * nqb + i, h)),
                  pl.BlockSpec((s_lat, ATT_HD), lambda b, h, i: (krow0 + b, kcol + h)),
                  pl.BlockSpec((s_lat, ATT_HD), lambda b, h, i: (krow0 + b, vcol + h)),
                  pl.BlockSpec((None, past, ATT_HD), lambda b, h, i: (b, 0, h)),
                  pl.BlockSpec((None, past, ATT_HD), lambda b, h, i: (b, 0, h)),
                  pl.BlockSpec((1, ATT_HD), lambda b, h, i: (0, 0)),
                  pl.BlockSpec((1, ATT_HD), lambda b, h, i: (0, 0)),
                  pl.BlockSpec((tq, ATT_HD), lambda b, h, i: (i, 0)),
                  pl.BlockSpec((tq, ATT_HD), lambda b, h, i: (i, 0)),
                  pl.BlockSpec((s_lat, ATT_HD), lambda b, h, i: (0, 0)),
                  pl.BlockSpec((s_lat, ATT_HD), lambda b, h, i: (0, 0))],
        out_specs=pl.BlockSpec((tq, gw), lambda b, h, i: (b * nqb + i, h)),
        out_shape=jax.ShapeDtypeStruct((ntok - t_ctx, ATT_KV_HEADS * gw), F32),
        scratch_shapes=[pltpu.VMEM((past + s_lat, ATT_HD), BF16), pltpu.VMEM((past + s_lat, ATT_HD), BF16)],
        compiler_params=_cparams(("arbitrary", "arbitrary", "arbitrary")),
        name="att_lat",
    )(u, u, u, cache_k, cache_v, qn, kn, cos, sin, cos, sin)
    return att_c, att_l, k_ctx, v_ctx


def _rope_tables(n_tokens):
    pos = jnp.arange(n_tokens)
    row = (pos // GRID_W).astype(F32)
    col = (pos % GRID_W).astype(F32)
    n_freq = ATT_HD // 4
    inv_freq = ROPE_THETA ** (-jnp.arange(n_freq, dtype=F32) / n_freq)
    ang_r = row[:, None] * inv_freq[None, :]
    ang_c = col[:, None] * inv_freq[None, :]
    cos = jnp.concatenate([jnp.cos(ang_r), jnp.cos(ang_r), jnp.cos(ang_c), jnp.cos(ang_c)], axis=-1)
    sin = jnp.concatenate([-jnp.sin(ang_r), jnp.sin(ang_r), -jnp.sin(ang_c), jnp.sin(ang_c)], axis=-1)
    return cos, sin


def _log_sigmoid(x):
    return jnp.minimum(x, 0.0) - jnp.log1p(jnp.exp(-jnp.abs(x)))


def _mlstm_kernel(*refs, zero_init, seq):
    if zero_init:
        (q_ref, k_ref, v_ref, o_ref, gc_ref, gt_ref, br_ref, bc_ref, ng_ref,
         out_ref, co_ref, mo_ref, h_scr, c_scr, m_scr) = refs
    else:
        (q_ref, k_ref, v_ref, o_ref, gc_ref, gt_ref, br_ref, bc_ref, ng_ref, c0_ref, m0_ref,
         out_ref, co_ref, mo_ref, h_scr, c_scr, m_scr) = refs
    nc = seq // CHUNK
    if zero_init:
        c_scr[...] = jnp.zeros(c_scr.shape, F32)
        m_scr[...] = jnp.full(m_scr.shape, M_INIT, F32)
    else:
        c_scr[...] = c0_ref[...]
        m_scr[...] = m0_ref[...]

    ti = lax.broadcasted_iota(I32, (CHUNK, CHUNK), 0)
    si = lax.broadcasted_iota(I32, (CHUNK, CHUNK), 1)
    ones_col = jnp.where(lax.broadcasted_iota(I32, (CHUNK, M_HD), 1) == 0, 1.0, 0.0).astype(BF16)
    eye_d = jnp.where(lax.broadcasted_iota(I32, (M_HD, M_HD), 0) == lax.broadcasted_iota(I32, (M_HD, M_HD), 1),
                      1.0, 0.0).astype(BF16)

    def cumsum3(tri, x, tri_left):
        hi = x.astype(BF16)
        r1 = x - hi.astype(F32)
        mid = r1.astype(BF16)
        lo = (r1 - mid.astype(F32)).astype(BF16)
        if tri_left:
            return _dg(tri, hi, _NN) + (_dg(tri, mid, _NN) + _dg(tri, lo, _NN))
        return _dg(hi, tri, _NN) + (_dg(mid, tri, _NN) + _dg(lo, tri, _NN))
    gate_i = lambda d, h: 2 * M_HEADS * d + h
    gate_f = lambda d, h: 2 * M_HEADS * d + M_HEADS + h

    def chunk(c, carry):
        ch = []
        for d in (0, 1):
            cc = c if d == 0 else nc - 1 - c
            rows = pl.ds(pl.multiple_of(cc * CHUNK, CHUNK), CHUNK)
            causal = (si <= ti) if d == 0 else (si >= ti)
            tri = jnp.where(causal, 1.0, 0.0).astype(BF16)
            tri_t = jnp.where((ti <= si) if d == 0 else (ti >= si), 1.0, 0.0).astype(BF16)
            q_all = q_ref[rows, :] * (M_HD ** -0.5)
            k_all = k_ref[rows, :]
            v_all = v_ref[rows, :]
            gcol = gc_ref[rows, :] + br_ref[...]
            grow = gt_ref[cc] + bc_ref[...]
            bcum_c_all = cumsum3(tri, _log_sigmoid(gcol), True)
            bcum_r_all = cumsum3(tri_t, _log_sigmoid(grow), False)
            for h in range(M_HEADS):
                sl = slice(h * M_HD, (h + 1) * M_HD)
                q, k, v = q_all[:, sl], k_all[:, sl], v_all[:, sl]
                ig_c = gcol[:, gate_i(d, h):gate_i(d, h) + 1]
                ig_r = grow[gate_i(d, h):gate_i(d, h) + 1, :]
                bcum_c = bcum_c_all[:, gate_f(d, h):gate_f(d, h) + 1]
                bcum_r = bcum_r_all[gate_f(d, h):gate_f(d, h) + 1, :]
                m_st = m_scr[d, h][:, 0:1]
                dmat = jnp.where(causal, bcum_c - bcum_r + ig_r, -jnp.inf)
                inter = bcum_c + m_st
                m_t = jnp.maximum(inter, jnp.max(dmat, axis=1, keepdims=True))
                b_last = bcum_c[CHUNK - 1:CHUNK, :] if d == 0 else bcum_c[0:1, :]
                g_c = b_last - bcum_c + ig_c
                m_new = jnp.maximum(b_last + m_st, jnp.max(g_c, axis=0, keepdims=True))
                ch.append(dict(d=d, h=h, rows=rows, sl=sl, qb=q.astype(BF16), kb=k.astype(BF16),
                               vx=jnp.concatenate([v.astype(BF16), ones_col], axis=1),
                               c_st=c_scr[d, h], m_t=m_t, m_new=m_new,
                               w_intra=jnp.exp(dmat - m_t), w_inter=jnp.exp(inter - m_t),
                               decay=jnp.exp(b_last + m_st - m_new), kw=(k * jnp.exp(g_c - m_new)).astype(BF16)))
        for x in ch:
            x['s_qk'] = _dg(x['qb'], x['kb'], _NT) * x['w_intra']
        for x in ch:
            x['qc'] = _dg(x['qb'], x['c_st'].astype(BF16), _NN)
        for x in ch:
            x['kw_t'] = _dg(eye_d, x['kw'], _NT).astype(BF16)
        for x in ch:
            x['sv'] = _dg(x['s_qk'].astype(BF16), x['vx'], _NN)
        for x in ch:
            x['upd'] = _dg(x['kw_t'], x['vx'], _NN)
        for x in ch:
            d, h = x['d'], x['h']
            tot = x['w_inter'] * x['qc'] + x['sv']
            den = tot[:, M_HD:M_HD + 1]
            h_scr[d, x['rows'], x['sl']] = tot[:, :M_HD] / jnp.maximum(jnp.abs(den), jnp.exp(-x['m_t']))
            c_scr[d, h] = x['decay'] * x['c_st'] + x['upd']
            m_scr[d, h] = jnp.broadcast_to(x['m_new'], (1, M_HD))
        return carry

    lax.fori_loop(0, nc, chunk, 0)
    hsum = h_scr[0] + h_scr[1]
    hn = jnp.concatenate(
        [hsum[:, h * M_HD:(h + 1) * M_HD]
         * lax.rsqrt(jnp.mean(hsum[:, h * M_HD:(h + 1) * M_HD] ** 2, axis=-1, keepdims=True) + NORM_EPS)
         for h in range(M_HEADS)], axis=1)
    out_ref[...] = hn * ng_ref[...] * _sigmoid(o_ref[...])
    co_ref[...] = c_scr[...]
    mo_ref[...] = m_scr[...]


def _mlstm(u, gcol, gt, bias, norm_g, states, row0, batch, seq):
    zero_init = states is None
    width = M_HEADS * M_HD
    qc = COL_M // width
    ngate = 4 * M_HEADS
    blk = lambda off: pl.BlockSpec((seq, width), lambda b: (row0 + b, off))
    st_c = pl.BlockSpec((None, 2, M_HEADS, M_HD, 2 * M_HD), lambda b: (b, 0, 0, 0, 0))
    st_v = pl.BlockSpec((None, 2, M_HEADS, 1, M_HD), lambda b: (b, 0, 0, 0, 0))
    in_specs = [blk(qc), blk(qc + 1), blk(qc + 2), blk(qc + 3),
                pl.BlockSpec((seq, ngate), lambda b: (row0 + b, 0)),
                pl.BlockSpec((seq // CHUNK, ngate, CHUNK), lambda b: (row0 + b, 0, 0)),
                pl.BlockSpec((1, ngate), lambda b: (0, 0)),
                pl.BlockSpec((ngate, 1), lambda b: (0, 0)),
                pl.BlockSpec((1, width), lambda b: (0, 0))]
    args = [u, u, u, u, gcol, gt, bias.reshape(1, ngate), bias.reshape(ngate, 1), norm_g]
    if not zero_init:
        in_specs += [st_c, st_v]
        args += list(states)
    return pl.pallas_call(
        functools.partial(_mlstm_kernel, zero_init=zero_init, seq=seq),
        grid=(batch,),
        in_specs=in_specs,
        out_specs=[pl.BlockSpec((seq, width), lambda b: (b, 0)), st_c, st_v],
        out_shape=[jax.ShapeDtypeStruct((batch * seq, width), F32),
                   jax.ShapeDtypeStruct((batch, 2, M_HEADS, M_HD, 2 * M_HD), F32),
                   jax.ShapeDtypeStruct((batch, 2, M_HEADS, 1, M_HD), F32)],
        scratch_shapes=[pltpu.VMEM((2, seq, width), F32), pltpu.VMEM((2, M_HEADS, M_HD, 2 * M_HD), F32),
                        pltpu.VMEM((2, M_HEADS, 1, M_HD), F32)],
        compiler_params=_cparams(("arbitrary",)),
        name="mlstm_ctx" if zero_init else "mlstm_lat",
    )(*args)


def _rwkv_kernel(*refs, zero_init, seq):
    if zero_init:
        (r_ref, k_ref, v_ref, xl_ref, xg_ref, mur_ref, muk_ref, muv_ref, mul_ref, mug_ref,
         w0_ref, wup_ref, a0_ref, aup_ref, gup_ref, kk_ref, ka_ref, rk_ref, lng_ref, lnb_ref,
         out_ref, so_ref, r_scr, v_scr, kk_scr, g_scr, bonus_scr, lw_scr, kd_scr, b_scr, y_scr, s_scr) = refs
    else:
        (r_ref, k_ref, v_ref, xl_ref, xg_ref, mur_ref, muk_ref, muv_ref, mul_ref, mug_ref,
         w0_ref, wup_ref, a0_ref, aup_ref, gup_ref, kk_ref, ka_ref, rk_ref, lng_ref, lnb_ref, s0_ref,
         out_ref, so_ref, r_scr, v_scr, kk_scr, g_scr, bonus_scr, lw_scr, kd_scr, b_scr, y_scr, s_scr) = refs
    nc = seq // CHUNK
    hd = R_HD
    nh = r_ref.shape[1] // hd

    def tshift(x_ref, mu_ref):
        x = x_ref[...]
        row = lax.broadcasted_iota(I32, x.shape, 0)
        prev = jnp.where(row == 0, 0.0, pltpu.roll(x, 1, 0))
        nxt = jnp.where(row == seq - 1, 0.0, pltpu.roll(x, seq - 1, 0))
        return x + mu_ref[...] * (0.5 * (prev + nxt) - x)

    r = tshift(r_ref, mur_ref)
    k = tshift(k_ref, muk_ref)
    v = tshift(v_ref, muv_ref)
    xl = tshift(xl_ref, mul_ref)
    xg = tshift(xg_ref, mug_ref)
    g = _dot(_sigmoid(xg), gup_ref[...])
    wlanes = nh * hd
    same_head = (lax.broadcasted_iota(I32, (wlanes, wlanes), 0) // hd
                 == lax.broadcasted_iota(I32, (wlanes, wlanes), 1) // hd)
    head_ones = jnp.where(same_head, 1.0, 0.0).astype(BF16)

    def head_sum(x):
        hi = x.astype(BF16)
        r1 = x - hi.astype(F32)
        mid = r1.astype(BF16)
        lo = (r1 - mid.astype(F32)).astype(BF16)
        return _dg(hi, head_ones, _NN) + (_dg(mid, head_ones, _NN) + _dg(lo, head_ones, _NN))

    kkp = k * kk_ref[...]
    kk = kkp * lax.rsqrt(jnp.maximum(head_sum(kkp * kkp), 1e-24))
    tw = jnp.tanh(xl[:, 0:hd])
    xa = xl[:, hd:2 * hd]
    r_scr[...] = r
    v_scr[...] = v
    kk_scr[...] = kk
    g_scr[...] = g
    bonus_scr[...] = head_sum(r * k * rk_ref[...]) * v
    for d in (0, 1):
        lw_scr[d] = -math.exp(-0.5) * _sigmoid(w0_ref[d] + _dot(tw, wup_ref[d]))
        ad = _sigmoid(a0_ref[d] + _dot(xa, aup_ref[d]))
        kd_scr[d] = k * (1.0 + (ad - 1.0) * ka_ref[...])
        b_scr[d] = kk * ad
    npair = nh // 2
    pw = 2 * hd
    zero_blk = jnp.zeros((hd, hd), F32)
    for d in (0, 1):
        for p in range(npair):
            if zero_init:
                s_scr[d, p] = jnp.zeros((pw, pw), F32)
            else:
                s_scr[d, p] = jnp.concatenate(
                    [jnp.concatenate([s0_ref[d, 2 * p], zero_blk], axis=1),
                     jnp.concatenate([zero_blk, s0_ref[d, 2 * p + 1]], axis=1)], axis=0)
    y_scr[...] = jnp.zeros(y_scr.shape, F32)

    ti = lax.broadcasted_iota(I32, (CHUNK, pw), 0)
    si = lax.broadcasted_iota(I32, (CHUNK, pw), 1) % CHUNK
    eye2 = (ti == si).astype(F32)
    tri_i = lax.broadcasted_iota(I32, (CHUNK, CHUNK), 0)
    tri_j = lax.broadcasted_iota(I32, (CHUNK, CHUNK), 1)
    first = lax.broadcasted_iota(I32, (1, pw), 1) < hd
    diag_blk = ((lax.broadcasted_iota(I32, (pw, pw), 0) < hd) == (lax.broadcasted_iota(I32, (pw, pw), 1) < hd))

    def bd(x):
        xb = x.astype(BF16)
        zero = jnp.zeros_like(xb)
        return jnp.concatenate([jnp.where(first, xb, zero), jnp.where(first, zero, xb)], axis=0)

    def chunk(c, carry):
        chains = []
        for d in (0, 1):
            cc = c if d == 0 else nc - 1 - c
            rows = pl.ds(pl.multiple_of(cc * CHUNK, CHUNK), CHUNK)
            incl = (si <= ti) if d == 0 else (si >= ti)
            strict = (si < ti) if d == 0 else (si > ti)
            lw = lw_scr[d, rows, :]
            lw_hi, lw_lo = _split(lw)
            tri = jnp.where((tri_j <= tri_i) if d == 0 else (tri_j >= tri_i), 1.0, 0.0).astype(BF16)
            lc = _dg(tri, lw_hi, _NN) + _dg(tri, lw_lo, _NN)
            l_last = lc[CHUNK - 1:CHUNK, :] if d == 0 else lc[0:1, :]
            e_neg = jnp.exp(-lc)
            e_end = jnp.exp(l_last - lc)
            vc = v_scr[rows, :]
            kdc = kd_scr[d, rows, :]
            bc = b_scr[d, rows, :]
            rt = r_scr[rows, :] * jnp.exp(lc)
            kkt = kk_scr[rows, :] * jnp.exp(lc - lw)
            kh = kdc * e_neg
            bh = bc * e_neg
            kbar = kdc * e_end
            bbar = bc * e_end
            w_end = jnp.exp(l_last)
            for p in range(npair):
                sl = slice(p * pw, (p + 1) * pw)
                chains.append(dict(
                    d=d, pair=p, sl=sl, rows=rows, incl=incl, strict=strict,
                    lhs=jnp.concatenate([kkt[:, sl], rt[:, sl]], axis=0).astype(BF16),
                    rhs=jnp.concatenate([bd(bh[:, sl]), bd(kh[:, sl])], axis=0),
                    end=jnp.concatenate([kbar[:, sl], bbar[:, sl]], axis=0),
                    v=vc[:, sl], w_end=w_end[:, sl], s0=s_scr[d, p]))
        for ch in chains:
            ch['ab'] = _dg(ch['lhs'], ch['rhs'], _NT)
        for ch in chains:
            ch['proj'] = _dg(ch['lhs'], ch['s0'].astype(BF16), _NT)
        for ch in chains:
            ab = ch['ab']
            ch['a_kb'] = jnp.where(ch['strict'], ab[:CHUNK, :pw], 0.0)
            ch['b_rb'] = jnp.where(ch['incl'], ab[CHUNK:, :pw], 0.0)
            ch['akk_brk'] = jnp.concatenate([jnp.where(ch['strict'], ab[:CHUNK, pw:], 0.0),
                                             jnp.where(ch['incl'], ab[CHUNK:, pw:], 0.0)], axis=0)
        for ch in chains:
            ch['p'] = _dg(ch['a_kb'].astype(BF16), bd(ch['a_kb']), _NN)
        for ch in chains:
            ch['abv'] = _dg(ch['akk_brk'].astype(BF16), bd(ch['v']), _NN)
        for ch in chains:
            inv = eye2 - ch['a_kb']
            ch['inv'] = inv + _dg(inv.astype(BF16), bd(ch['p']), _NN)
        span = 4
        while span < CHUNK:
            for ch in chains:
                ch['p'] = _dg(ch['p'].astype(BF16), bd(ch['p']), _NN)
            for ch in chains:
                ch['inv'] = ch['inv'] + _dg(ch['inv'].astype(BF16), bd(ch['p']), _NN)
            span *= 2
        for ch in chains:
            ch['u'] = _dg(ch['inv'].astype(BF16), bd(ch['proj'][:CHUNK] + ch['abv'][:CHUNK]), _NN)
        for ch in chains:
            y = ch['proj'][CHUNK:] + ch['abv'][CHUNK:] - _dg(ch['b_rb'].astype(BF16), bd(ch['u']), _NN)
            y_scr[ch['rows'], ch['sl']] += y
        for ch in chains:
            upd = _dot3(jnp.concatenate([ch['v'], -ch['u']], axis=0), ch['end'], _TN)
            s_scr[ch['d'], ch['pair']] = ch['s0'] * ch['w_end'] + jnp.where(diag_blk, upd, 0.0)
        return carry

    lax.fori_loop(0, nc, chunk, 0)

    y = y_scr[...]
    dev = y - head_sum(y) * (1.0 / hd)
    yn = dev * lax.rsqrt(head_sum(dev * dev) * (1.0 / hd) + GN_EPS)
    out_ref[...] = (yn * lng_ref[...] + lnb_ref[...] + bonus_scr[...]) * g_scr[...]
    for d in (0, 1):
        for p in range(npair):
            s = s_scr[d, p]
            so_ref[d, 2 * p] = s[:hd, :hd]
            so_ref[d, 2 * p + 1] = s[hd:, hd:]


def _rwkv(u, p, state, row0, batch, seq, heads_per_step):
    zero_init = state is None
    nh = heads_per_step
    wd = nh * R_HD
    nsteps = R_HEADS // nh
    width = R_HEADS * R_HD
    rc = COL_R // wd
    sec = width // wd
    lc = (COL_R + 3 * width) // LANES
    ublk = lambda off: pl.BlockSpec((seq, wd), lambda b, h: (row0 + b, rc + off + h))
    ufix = lambda blk: pl.BlockSpec((seq, LANES), lambda b, h: (row0 + b, blk))
    mblk = lambda off: pl.BlockSpec((1, wd), lambda b, h: (0, off + h))
    mfix = lambda blk: pl.BlockSpec((1, LANES), lambda b, h: (0, blk))
    vec = pl.BlockSpec((1, wd), lambda b, h: (0, h))
    in_specs = [ublk(0), ublk(sec), ublk(2 * sec), ufix(lc), ufix(lc + 1),
                mblk(0), mblk(sec), mblk(2 * sec), mfix(lc - COL_R // LANES), mfix(lc - COL_R // LANES + 1),
                pl.BlockSpec((2, 1, wd), lambda b, h: (0, 0, h)),
                pl.BlockSpec((2, R_HD, wd), lambda b, h: (0, 0, h)),
                pl.BlockSpec((2, 1, wd), lambda b, h: (0, 0, h)),
                pl.BlockSpec((2, R_HD, wd), lambda b, h: (0, 0, h)),
                pl.BlockSpec((LANES, wd), lambda b, h: (0, h)),
                vec, vec, vec, vec, vec]
    args = [u, u, u, u, u, p['mu'], p['mu'], p['mu'], p['mu'], p['mu'],
            p['w0'], p['w_up'], p['a0'], p['a_up'], p['g_up'], p['k_k'], p['k_a'], p['r_k'], p['ln_g'], p['ln_b']]
    if not zero_init:
        in_specs.append(pl.BlockSpec((None, 2, nh, R_HD, R_HD), lambda b, h: (b, 0, h, 0, 0)))
        args.append(state)
    big = lambda n: pltpu.VMEM((n, seq, wd), F32)
    return pl.pallas_call(
        functools.partial(_rwkv_kernel, zero_init=zero_init, seq=seq),
        grid=(batch, nsteps),
        in_specs=in_specs,
        out_specs=[pl.BlockSpec((seq, wd), lambda b, h: (b, h)),
                   pl.BlockSpec((None, 2, nh, R_HD, R_HD), lambda b, h: (b, 0, h, 0, 0))],
        out_shape=[jax.ShapeDtypeStruct((batch * seq, width), F32),
                   jax.ShapeDtypeStruct((batch, 2, R_HEADS, R_HD, R_HD), F32)],
        scratch_shapes=[pltpu.VMEM((seq, wd), F32)] * 5
                       + [big(2), big(2), big(2), pltpu.VMEM((seq, wd), F32),
                          pltpu.VMEM((2, nh // 2, 2 * R_HD, 2 * R_HD), F32)],
        compiler_params=_cparams(("arbitrary", "arbitrary")),
        name="rwkv_ctx" if zero_init else "rwkv_lat",
    )(*args)


def _top2_sum(a, b, c, d):
    m1, n1 = jnp.maximum(a, b), jnp.minimum(a, b)
    m2, n2 = jnp.maximum(c, d), jnp.minimum(c, d)
    return jnp.maximum(m1, m2) + jnp.maximum(jnp.minimum(m1, m2), jnp.maximum(n1, n2))


def _first_argmax(vals):
    best = functools.reduce(jnp.maximum, vals)
    idx = jnp.full(best.shape, len(vals) - 1, I32)
    for j in range(len(vals) - 2, -1, -1):
        idx = jnp.where(vals[j] == best, j, idx)
    return best, idx


def _out_kernel(attc_ref, attl_ref, mc_ref, ml_ref, rc_ref, rl_ref, xc_ref, xl_ref, w_ref, g1_ref, sh2_ref, sc2_ref,
                n2_ref, rw_ref, rb_ref, x1_ref, h2_ref, route_ref, cnt_ref, cnt_scr, *, n_ctx_tiles):
    @pl.when(pl.program_id(0) == 0)
    def _():
        cnt_scr[...] = jnp.zeros(cnt_scr.shape, F32)

    is_ctx = pl.program_id(0) < n_ctx_tiles
    both = lambda c_ref, l_ref: jnp.where(is_ctx, c_ref[...], l_ref[...])
    na = attc_ref.shape[1]
    nm = mc_ref.shape[1]
    mix = (jnp.dot(both(attc_ref, attl_ref).astype(BF16), w_ref[0:na, :], preferred_element_type=F32)
           + jnp.dot(both(mc_ref, ml_ref).astype(BF16), w_ref[na:na + nm, :], preferred_element_type=F32)
           + jnp.dot(both(rc_ref, rl_ref).astype(BF16), w_ref[na + nm:, :], preferred_element_type=F32))
    x1 = both(xc_ref, xl_ref) + g1_ref[...] * mix
    x1_ref[...] = x1
    h2 = _rms(x1, n2_ref[...]) * (1.0 + sc2_ref[...]) + sh2_ref[...]
    h2_ref[...] = h2
    logits = _dot1(rw_ref[...], h2, _NT)
    s = _sigmoid(logits)
    ssel = s + rb_ref[...]
    srow = [s[e:e + 1, :] for e in range(N_EXPERTS)]
    brow = [ssel[e:e + 1, :] for e in range(N_EXPERTS)]
    gscore = [_top2_sum(*brow[EXPERTS_PER_GROUP * g:EXPERTS_PER_GROUP * (g + 1)]) for g in range(N_EXPERT_GROUPS)]
    _, gidx = _first_argmax(gscore)
    pick = lambda rows, j: functools.reduce(
        lambda acc, g: jnp.where(gidx == g, rows[EXPERTS_PER_GROUP * g + j], acc),
        range(N_EXPERT_GROUPS - 2, -1, -1), rows[EXPERTS_PER_GROUP * (N_EXPERT_GROUPS - 1) + j])
    ing = [pick(brow, j) for j in range(EXPERTS_PER_GROUP)]
    sin_ = [pick(srow, j) for j in range(EXPERTS_PER_GROUP)]
    _, l1 = _first_argmax(ing)
    _, l2 = _first_argmax([jnp.where(l1 == j, -jnp.inf, ing[j]) for j in range(EXPERTS_PER_GROUP)])
    sel = lambda l: functools.reduce(lambda acc, j: jnp.where(l == j, sin_[j], acc),
                                     range(EXPERTS_PER_GROUP - 2, -1, -1), sin_[EXPERTS_PER_GROUP - 1])
    w1, w2 = sel(l1), sel(l2)
    tot = w1 + w2
    e1 = gidx * EXPERTS_PER_GROUP + l1
    e2 = gidx * EXPERTS_PER_GROUP + l2
    tm = e1.shape[1]
    eid = lax.broadcasted_iota(I32, (N_EXPERTS, tm), 0)
    oh1 = eid == e1
    oh2 = eid == e2
    picked = jnp.where(jnp.logical_or(oh1, oh2), 1.0, 0.0)
    earlier = jnp.where(lax.broadcasted_iota(I32, (tm, tm), 0) < lax.broadcasted_iota(I32, (tm, tm), 1), 1.0, 0.0)
    rank = cnt_scr[:, 0:1] + jnp.dot(picked.astype(BF16), earlier.astype(BF16), preferred_element_type=F32)
    pos1 = jnp.sum(jnp.where(oh1, rank, 0.0), axis=0, keepdims=True)
    pos2 = jnp.sum(jnp.where(oh2, rank, 0.0), axis=0, keepdims=True)
    cnt = cnt_scr[...] + jnp.sum(picked, axis=1, keepdims=True)
    cnt_scr[...] = cnt
    cnt_ref[...] = cnt
    zero = jnp.zeros_like(w1)
    route_ref[...] = jnp.concatenate([e1.astype(F32), e2.astype(F32), w1 / tot, w2 / tot, pos1, pos2, zero, zero],
                                     axis=0)


def _out_proj(att, m_out, r_out, x, w_out, mod3, n2g, rw_t, rb, s_lat):
    t_ctx, d = x[0].shape
    ntok = t_ctx + x[1].shape[0]
    tm = 256
    nct = t_ctx // tm
    row = lambda i: _mod_row(i * tm, t_ctx, s_lat)
    modblk = lambda j: pl.BlockSpec((None, 1, d), lambda i: (row(i), 0, j))
    pair = lambda a: [pl.BlockSpec((tm, a[0].shape[1]), lambda i: (jnp.minimum(i, nct - 1), 0)),
                      pl.BlockSpec((tm, a[1].shape[1]), lambda i: (jnp.maximum(i - nct, 0), 0))]
    return pl.pallas_call(
        functools.partial(_out_kernel, n_ctx_tiles=nct),
        grid=(ntok // tm,),
        in_specs=pair(att) + pair(m_out) + pair(r_out) + pair(x) + [
                  pl.BlockSpec(w_out.shape, lambda i: (0, 0)),
                  modblk(2), modblk(3), modblk(4),
                  pl.BlockSpec((1, d), lambda i: (0, 0)),
                  pl.BlockSpec(rw_t.shape, lambda i: (0, 0)),
                  pl.BlockSpec(rb.shape, lambda i: (0, 0))],
        out_specs=[pl.BlockSpec((tm, d), lambda i: (i, 0)),
                   pl.BlockSpec((tm, d), lambda i: (i, 0)),
                   pl.BlockSpec((8, tm), lambda i: (0, i)),
                   pl.BlockSpec((N_EXPERTS, LANES), lambda i: (0, 0))],
        out_shape=[jax.ShapeDtypeStruct((ntok, d), F32), jax.ShapeDtypeStruct((ntok, d), F32),
                   jax.ShapeDtypeStruct((8, ntok), F32), jax.ShapeDtypeStruct((N_EXPERTS, LANES), F32)],
        scratch_shapes=[pltpu.VMEM((N_EXPERTS, LANES), F32)],
        compiler_params=_cparams(("arbitrary",)),
        name="out_proj",
    )(*att, *m_out, *r_out, *x, w_out, mod3, mod3, mod3, n2g, rw_t, rb)


def _row_gather(src_hbm, idx_ref, base, dst, sem, n, unrolled, lo=0):
    def start(j):
        pltpu.make_async_copy(src_hbm.at[pl.ds(idx_ref[base + j], 1), :], dst.at[pl.ds(j, 1), :], sem).start()

    if unrolled:
        for j in range(lo, n):
            start(j)
    else:
        def body(j, c):
            start(j)
            return c
        lax.fori_loop(lo, n, body, 0)


def _expert_kernel(be_ref, tok_ref, nused_ref, h_hbm, w1_ref, w3_ref, w2_ref, y_ref, xbuf, w1b, w3b, w2b, sem):
    i = pl.program_id(0)
    n_used = nused_ref[0]
    wait = lambda s: pltpu.make_async_copy(h_hbm.at[pl.ds(0, EXPERT_ROWS), :], xbuf.at[s], sem.at[s]).wait()

    @pl.when(i < n_used)
    def _():
        slot = i % 2

        @pl.when(i == 0)
        def _():
            _row_gather(h_hbm, tok_ref, 0, xbuf.at[0], sem.at[0], EXPERT_ROWS, unrolled=False)

        @pl.when(jnp.logical_or(i == 0, be_ref[i] != be_ref[jnp.maximum(i - 1, 0)]))
        def _():
            w1b[...] = w1_ref[...].astype(BF16)
            w3b[...] = w3_ref[...].astype(BF16)
            w2b[...] = w2_ref[...].astype(BF16)

        wait(slot)
        de = w1b.shape[1]
        half = de // 2
        groups = 6
        per = EXPERT_ROWS // groups

        def next_rows(gi):
            hi = EXPERT_ROWS if gi == groups - 1 else (gi + 1) * per
            _row_gather(h_hbm, tok_ref, (i + 1) * EXPERT_ROWS, xbuf.at[1 - slot], sem.at[1 - slot], hi,
                        unrolled=True, lo=gi * per)

        xb = xbuf[slot].astype(BF16)
        hmid = []
        for n in range(2):
            cols = slice(n * half, (n + 1) * half)
            next_rows(2 * n)
            a = jnp.dot(xb, w1b[:, cols], preferred_element_type=F32)
            next_rows(2 * n + 1)
            b = jnp.dot(xb, w3b[:, cols], preferred_element_type=F32)
            hmid.append(((a * _sigmoid(a)) * b).astype(BF16))
        next_rows(4)
        y = jnp.dot(hmid[0], w2b[0:half, :], preferred_element_type=F32)
        next_rows(5)
        y_ref[...] = y + jnp.dot(hmid[1], w2b[half:, :], preferred_element_type=F32)

        @pl.when(i == n_used - 1)
        def _():
            wait(1 - slot)

    @pl.when(i >= n_used)
    def _():
        y_ref[...] = jnp.zeros(y_ref.shape, F32)


def _experts(h2, block_e, row_tok, n_used, w1, w3, w2, layer):
    ntok, d = h2.shape
    de = w1.shape[3]
    n_rows = row_tok.shape[0] - EXPERT_ROWS
    nb = n_rows // EXPERT_ROWS
    return pl.pallas_call(
        _expert_kernel,
        grid_spec=pltpu.PrefetchScalarGridSpec(
            num_scalar_prefetch=3,
            grid=(nb,),
            in_specs=[pl.BlockSpec(memory_space=pl.ANY),
                      pl.BlockSpec((None, None, d, de), lambda i, be, tok, nu: (layer, be[i], 0, 0)),
                      pl.BlockSpec((None, None, d, de), lambda i, be, tok, nu: (layer, be[i], 0, 0)),
                      pl.BlockSpec((None, None, de, d), lambda i, be, tok, nu: (layer, be[i], 0, 0))],
            out_specs=pl.BlockSpec((EXPERT_ROWS, d), lambda i, be, tok, nu: (i, 0)),
            scratch_shapes=[pltpu.VMEM((2, EXPERT_ROWS, d), F32), pltpu.VMEM((d, de), BF16),
                            pltpu.VMEM((d, de), BF16), pltpu.VMEM((de, d), BF16), pltpu.SemaphoreType.DMA((2,))]),
        out_shape=jax.ShapeDtypeStruct((n_rows, d), F32),
        compiler_params=_cparams(("arbitrary",)),
        name="experts",
    )(block_e, row_tok, n_used, h2, w1, w3, w2)


def _combine_kernel(d1_ref, d2_ref, y_hbm, x1_ref, g2_ref, gate_ref, oc_ref, ol_ref, ybuf, sem, *, n_ctx_tiles):
    tm = x1_ref.shape[0]
    i = pl.program_id(0)
    slot = i % 2

    def gather(tile, s, unrolled):
        _row_gather(y_hbm, d1_ref, tile * tm, ybuf.at[s, 0], sem.at[s], tm, unrolled)
        _row_gather(y_hbm, d2_ref, tile * tm, ybuf.at[s, 1], sem.at[s], tm, unrolled)

    def wait(s):
        pltpu.make_async_copy(y_hbm.at[pl.ds(0, tm), :], ybuf.at[s, 0], sem.at[s]).wait()
        pltpu.make_async_copy(y_hbm.at[pl.ds(0, tm), :], ybuf.at[s, 1], sem.at[s]).wait()

    @pl.when(i == 0)
    def _():
        gather(0, 0, False)

    wait(slot)
    gather(i + 1, 1 - slot, True)
    gate = gate_ref[...]
    moe = ybuf[slot, 0] * gate[:, 0:1] + ybuf[slot, 1] * gate[:, 1:2]
    new_x = x1_ref[...] + g2_ref[...] * moe

    @pl.when(i < n_ctx_tiles)
    def _():
        oc_ref[...] = new_x

    @pl.when(i >= n_ctx_tiles)
    def _():
        ol_ref[...] = new_x

    @pl.when(i == pl.num_programs(0) - 1)
    def _():
        wait(1 - slot)


def _combine(y, x1, mod3, gate, dest1, dest2, t_ctx, s_lat):
    ntok, d = x1.shape
    tm = 256
    nct = t_ctx // tm
    row = lambda i: _mod_row(i * tm, t_ctx, s_lat)
    return pl.pallas_call(
        functools.partial(_combine_kernel, n_ctx_tiles=nct),
        grid_spec=pltpu.PrefetchScalarGridSpec(
            num_scalar_prefetch=2,
            grid=(ntok // tm,),
            in_specs=[pl.BlockSpec(memory_space=pl.ANY),
                      pl.BlockSpec((tm, d), lambda i, a, b: (i, 0)),
                      pl.BlockSpec((None, 1, d), lambda i, a, b: (row(i), 0, 5)),
                      pl.BlockSpec((tm, 2), lambda i, a, b: (i, 0))],
            out_specs=[pl.BlockSpec((tm, d), lambda i, a, b: (jnp.minimum(i, nct - 1), 0)),
                       pl.BlockSpec((tm, d), lambda i, a, b: (jnp.maximum(i - nct, 0), 0))],
            scratch_shapes=[pltpu.VMEM((2, 2, tm, d), F32), pltpu.SemaphoreType.DMA((2,))]),
        out_shape=[jax.ShapeDtypeStruct((t_ctx, d), F32), jax.ShapeDtypeStruct((ntok - t_ctx, d), F32)],
        compiler_params=_cparams(("arbitrary",)),
        name="combine",
    )(dest1, dest2, y, x1, mod3, gate)


def _dispatch(route, counts, tile):
    ntok = route.shape[1]
    e = route[0:2].astype(I32)
    pos = route[4:6].astype(I32)
    gate = route[2:4].T
    counts = counts[:, 0].astype(I32)
    padded = (counts + EXPERT_ROWS - 1) // EXPERT_ROWS * EXPERT_ROWS
    pad_end = jnp.cumsum(padded)
    pad_start = pad_end - padded
    onehot = (e[:, :, None] == jnp.arange(N_EXPERTS, dtype=I32)).astype(I32)
    dest = jnp.sum(onehot * pad_start, axis=-1) + pos
    n_rows = -(-(2 * ntok) // EXPERT_ROWS) * EXPERT_ROWS + N_EXPERTS * EXPERT_ROWS
    nb = n_rows // EXPERT_ROWS
    tok = jnp.broadcast_to(jnp.arange(ntok, dtype=I32)[None, :], (2, ntok))
    row_tok = jnp.zeros((n_rows + EXPERT_ROWS,), I32).at[dest.reshape(-1)].set(tok.reshape(-1), unique_indices=True)
    blk_start = jnp.arange(nb, dtype=I32) * EXPERT_ROWS
    block_e = jnp.minimum(jnp.sum((pad_end[None, :] <= blk_start[:, None]).astype(I32), axis=1), N_EXPERTS - 1)
    n_used = pad_end[-1:] // EXPERT_ROWS
    dest = jnp.pad(dest, ((0, 0), (0, tile)))
    return row_tok, block_e, n_used, gate, dest[0], dest[1]


def kernel(x_prompt, x_sample, c, cache_attn_k, cache_attn_v, state_mlstm_C, state_mlstm_n, state_mlstm_m, state_rwkv, c_ctx, norm1_g, norm2_g, w_mod, b_mod, w_in, w_out, attn_q_norm, attn_k_norm, mlstm_i_bias, mlstm_f_bias, mlstm_norm_g, rwkv_mu, rwkv_w0, rwkv_w_up, rwkv_a0, rwkv_a_up, rwkv_g_up, rwkv_k_k, rwkv_k_a, rwkv_r_k, rwkv_ln_g, rwkv_ln_b, router_w, router_b, exp_w1, exp_w3, exp_w2):
    b_ctx, s_ctx, d = x_prompt.shape
    b_lat, s_lat, _ = x_sample.shape
    depth = w_in.shape[0]
    t_ctx = b_ctx * s_ctx
    ntok = t_ctx + b_lat * s_lat
    assert b_lat + 1 <= 8 and s_lat % 1024 == 0 and t_ctx % 1024 == 0 and t_ctx % s_lat == 0
    past = cache_attn_k.shape[2]

    x = (x_prompt.reshape(t_ctx, d), x_sample.reshape(b_lat * s_lat, d))
    c_all = jnp.zeros((8, d), F32).at[0].set(c_ctx).at[1:1 + b_lat].set(c)
    mod = _modulation(c_all, w_mod, b_mod)
    cos, sin = _rope_tables(s_lat)
    rw_t = router_w.T
    rb = router_b.reshape(N_EXPERTS, 1)
    r_width = R_HEADS * R_HD
    n_in = w_in.shape[2]
    gate_hi = COL_R + 4 * M_HEADS
    w_in_t = jnp.swapaxes(w_in, 1, 2)
    w_tail_t = jnp.concatenate([w_in_t[:, gate_hi:], w_in_t[:, COL_R:gate_hi],
                                jnp.zeros((depth, N_IN_PAD - n_in, d), F32)], axis=1).astype(BF16)

    ks, vs, cs, ns, ms, rs = [], [], [], [], [], []
    for l in range(depth):
        mod3 = mod[l].reshape(8, 1, 6 * d)
        u = _in_proj(x[0], x[1], norm1_g[l][None], mod3, w_in_t, l, w_tail_t[l], s_lat)

        ck = cache_attn_k[:, l].reshape(b_lat, past, ATT_KV_HEADS * ATT_HD)
        cv = cache_attn_v[:, l].reshape(b_lat, past, ATT_KV_HEADS * ATT_HD)
        att_c, att_l, k_ctx, v_ctx = _attention(u, attn_q_norm[l][None], attn_k_norm[l][None], ck, cv, cos, sin,
                                                b_ctx, s_ctx, b_lat, s_lat)
        ks.append(k_ctx.reshape(b_ctx, s_ctx, ATT_KV_HEADS, ATT_HD))
        vs.append(v_ctx.reshape(b_ctx, s_ctx, ATT_KV_HEADS, ATT_HD))

        gcol = u[:, COL_GATE:COL_GATE + 4 * M_HEADS]
        gt = gcol.reshape(ntok // CHUNK, CHUNK, 4 * M_HEADS).transpose(0, 2, 1)
        bias = jnp.stack([mlstm_i_bias[l], mlstm_f_bias[l]], axis=1)
        ng = mlstm_norm_g[l][None]
        m_c_out, cx_c, m_c = _mlstm(u, gcol, gt, bias, ng, None, 0, b_ctx, s_ctx)
        n_col = jnp.pad(state_mlstm_n[:, l][..., None], ((0, 0),) * 4 + ((0, M_HD - 1),))
        lat_states = (jnp.concatenate([state_mlstm_C[:, l], n_col], axis=-1),
                      jnp.broadcast_to(state_mlstm_m[:, l][..., None, None], (b_lat, 2, M_HEADS, 1, M_HD)))
        m_l_out, _, _ = _mlstm(u, gcol, gt, bias, ng, lat_states, t_ctx // s_lat, b_lat, s_lat)
        cs.append(cx_c[..., :M_HD])
        ns.append(cx_c[..., M_HD])
        ms.append(m_c[:, :, :, 0, 0])

        rp = dict(mu=rwkv_mu[l][None], w0=rwkv_w0[l].reshape(2, 1, r_width), w_up=rwkv_w_up[l],
                  a0=rwkv_a0[l].reshape(2, 1, r_width), a_up=rwkv_a_up[l], g_up=rwkv_g_up[l],
                  k_k=rwkv_k_k[l][None], k_a=rwkv_k_a[l][None], r_k=rwkv_r_k[l].reshape(1, r_width),
                  ln_g=rwkv_ln_g[l][None], ln_b=rwkv_ln_b[l][None])
        r_c_out, r_c = _rwkv(u, rp, None, 0, b_ctx, s_ctx, RWKV_HEADS_PER_STEP_CTX)
        r_l_out, _ = _rwkv(u, rp, state_rwkv[:, l], t_ctx // s_lat, b_lat, s_lat, RWKV_HEADS_PER_STEP_LAT)
        rs.append(r_c)

        x1, h2, route, counts = _out_proj((att_c, att_l), (m_c_out, m_l_out), (r_c_out, r_l_out), x,
                                          w_out[l].astype(BF16), mod3, norm2_g[l][None], rw_t, rb, s_lat)
        row_tok, block_e, n_used, gate, dest1, dest2 = _dispatch(route, counts, 256)
        y = _experts(h2, block_e, row_tok, n_used, exp_w1, exp_w3, exp_w2, l)
        x = _combine(y, x1, mod3, gate, dest1, dest2, t_ctx, s_lat)

    y_prompt = x[0].reshape(b_ctx, s_ctx, d)
    y_sample = x[1].reshape(b_lat, s_lat, d)
    return (y_prompt, y_sample, jnp.stack(ks, axis=1), jnp.stack(vs, axis=1), jnp.stack(cs, axis=1),
            jnp.stack(ns, axis=1), jnp.stack(ms, axis=1), jnp.stack(rs, axis=1))
```

```python
import functools
import math

import jax
import jax.numpy as jnp
from jax import lax
from jax.experimental import pallas as pl
from jax.experimental.pallas import tpu as pltpu

F32 = jnp.float32
BF16 = jnp.bfloat16
I32 = jnp.int32

NORM_EPS = 1e-6
GN_EPS = 64e-5
M_INIT = -1e30
GRID_W = 64
ROPE_THETA = 10000.0
ATT_HD = 128
ATT_GROUPS = 4
ATT_KV_HEADS = 2
M_HD = 128
M_HEADS = 4
R_HD = 64
R_HEADS = 8
N_EXPERTS = 16
N_EXPERT_GROUPS = 4
EXPERTS_PER_GROUP = 4
CHUNK = 64
LANES = 128
EXPERT_ROWS = 256
VMEM_LIMIT = 58 * 1024 * 1024
RWKV_HEADS_PER_STEP_CTX = 8
RWKV_SEQS_PER_STEP_CTX = 2
MLSTM_SEQS_PER_STEP_CTX = 1
RWKV_HEADS_PER_STEP_LAT = 8

COL_ATT = 0
COL_M = 1536
COL_R = 3584
COL_GATE = 5376
N_IN_PAD = 5632


def _cparams(sem):
    return pltpu.CompilerParams(dimension_semantics=sem, vmem_limit_bytes=VMEM_LIMIT)


def _dot(a, b):
    return jnp.dot(a.astype(BF16), b.astype(BF16), preferred_element_type=F32)


def _dg(a, b, dims):
    return lax.dot_general(a, b, (dims, ((), ())), preferred_element_type=F32)


_NN = ((1,), (0,))
_NT = ((1,), (1,))
_TN = ((0,), (0,))


def _split(a):
    hi = a.astype(BF16)
    lo = (a - hi.astype(F32)).astype(BF16)
    return hi, lo


def _dot3(a, b, dims=_NN):
    ah, al = _split(a)
    bh, bl = _split(b)
    return _dg(ah, bh, dims) + (_dg(ah, bl, dims) + _dg(al, bh, dims))


def _dot1(a, b, dims=_NN):
    return _dg(a.astype(BF16), b.astype(BF16), dims)


def _rms(x, g):
    return x * lax.rsqrt(jnp.mean(x * x, axis=-1, keepdims=True) + NORM_EPS) * g


def _sigmoid(x):
    return 1.0 / (1.0 + jnp.exp(-x))


def _mod_kernel(c_ref, w_ref, b_ref, o_ref):
    c = c_ref[...]
    o_ref[...] = _dot(c * _sigmoid(c), w_ref[...]) + b_ref[...]


def _modulation(c_all, w_mod, b_mod):
    depth, d, n = w_mod.shape
    tn = 1024
    return pl.pallas_call(
        _mod_kernel,
        grid=(depth, n // tn),
        in_specs=[pl.BlockSpec((8, d), lambda l, j: (0, 0)),
                  pl.BlockSpec((None, d, tn), lambda l, j: (l, 0, j)),
                  pl.BlockSpec((None, 1, tn), lambda l, j: (l, 0, j))],
        out_specs=pl.BlockSpec((None, 8, tn), lambda l, j: (l, 0, j)),
        out_shape=jax.ShapeDtypeStruct((depth, 8, n), F32),
        compiler_params=_cparams(("arbitrary", "arbitrary")),
        name="modulation",
    )(c_all, w_mod, b_mod.reshape(depth, 1, n))


def _mod_row(tok0, t_ctx, s_lat):
    return jnp.where(tok0 < t_ctx, 0, 1 + (tok0 - t_ctx) // s_lat)


def _in_kernel(xc_ref, xl_ref, g_ref, sh_ref, sc_ref, wa_ref, wb_ref, o_ref, h_ref, *, n_ctx_tiles, n_head_tiles):
    i = pl.program_id(0)
    j = pl.program_id(1)

    def normalise(x_ref):
        slab = 256
        for r0 in range(0, x_ref.shape[0], slab):
            h = _rms(x_ref[r0:r0 + slab, :], g_ref[...]) * (1.0 + sc_ref[...]) + sh_ref[...]
            h_ref[r0:r0 + slab, :] = h.astype(BF16)

    @pl.when(jnp.logical_and(j == 0, i < n_ctx_tiles))
    def _():
        normalise(xc_ref)

    @pl.when(jnp.logical_and(j == 0, i >= n_ctx_tiles))
    def _():
        normalise(xl_ref)

    @pl.when(j < n_head_tiles)
    def _():
        o_ref[...] = _dg(h_ref[...], wa_ref[...].astype(BF16), _NT)

    @pl.when(j >= n_head_tiles)
    def _():
        o_ref[...] = _dg(h_ref[...], wb_ref[...], _NT)


def _in_proj(xc, xl, g1, mod3, w_in_t, layer, w_tail_t, s_lat):
    t_ctx, d = xc.shape
    ntok = t_ctx + xl.shape[0]
    tm, tn = 1024, 512
    na = COL_R // tn
    n = COL_R + w_tail_t.shape[0]
    nct = t_ctx // tm
    row = lambda i: _mod_row(i * tm, t_ctx, s_lat)
    return pl.pallas_call(
        functools.partial(_in_kernel, n_ctx_tiles=nct, n_head_tiles=na),
        grid=(ntok // tm, n // tn),
        in_specs=[pl.BlockSpec((tm, d), lambda i, j: (jnp.minimum(i, nct - 1), 0)),
                  pl.BlockSpec((tm, d), lambda i, j: (jnp.maximum(i - nct, 0), 0)),
                  pl.BlockSpec((1, d), lambda i, j: (0, 0)),
                  pl.BlockSpec((None, 1, d), lambda i, j: (row(i), 0, 0)),
                  pl.BlockSpec((None, 1, d), lambda i, j: (row(i), 0, 1)),
                  pl.BlockSpec((None, tn, d), lambda i, j: (layer, jnp.minimum(j, na - 1), 0)),
                  pl.BlockSpec((tn, d), lambda i, j: (jnp.maximum(j - na, 0), 0))],
        out_specs=pl.BlockSpec((tm, tn), lambda i, j: (i, j)),
        out_shape=jax.ShapeDtypeStruct((ntok, n), F32),
        scratch_shapes=[pltpu.VMEM((tm, d), BF16)],
        compiler_params=_cparams(("arbitrary", "arbitrary")),
        name="in_proj",
    )(xc, xl, g1, mod3, mod3, w_in_t, w_tail_t)


def _softmax_av(q, kb, vb):
    s = _dg(q.astype(BF16), kb, _NT) * (ATT_HD ** -0.5)
    p = jnp.exp(s - jnp.max(s, axis=-1, keepdims=True))
    l = jnp.sum(p, axis=-1, keepdims=True)
    return jnp.dot(p.astype(BF16), vb, preferred_element_type=F32) / l


def _att_ctx_kernel(q_ref, k_ref, v_ref, qn_ref, kn_ref, o_ref, ko_ref, vo_ref):
    k = _rms(k_ref[...], kn_ref[...])
    ko_ref[...] = k
    vo_ref[...] = v_ref[...]
    kb = k.astype(BF16)
    vb = v_ref[...].astype(BF16)
    for g in range(ATT_GROUPS):
        q = _rms(q_ref[:, g * ATT_HD:(g + 1) * ATT_HD], qn_ref[...])
        o_ref[:, g * ATT_HD:(g + 1) * ATT_HD] = _softmax_av(q, kb, vb)


def _rope(x, cos, sin):
    lane = lax.broadcasted_iota(I32, x.shape, 1)
    first = (lane % (ATT_HD // 2)) < (ATT_HD // 4)
    partner = jnp.where(first, pltpu.roll(x, ATT_HD - ATT_HD // 4, 1), pltpu.roll(x, ATT_HD // 4, 1))
    return x * cos + partner * sin


def _att_lat_kernel(q_ref, k_ref, v_ref, ck_ref, cv_ref, qn_ref, kn_ref, cosq_ref, sinq_ref, cosk_ref, sink_ref,
                    o_ref, kb_ref, vb_ref, *, past):
    @pl.when(pl.program_id(2) == 0)
    def _():
        k = _rope(_rms(k_ref[...], kn_ref[...]), cosk_ref[...], sink_ref[...])
        kb_ref[0:past, :] = ck_ref[...].astype(BF16)
        kb_ref[past:, :] = k.astype(BF16)
        vb_ref[0:past, :] = cv_ref[...].astype(BF16)
        vb_ref[past:, :] = v_ref[...].astype(BF16)

    kb = kb_ref[...]
    vb = vb_ref[...]
    for g in range(ATT_GROUPS):
        q = _rope(_rms(q_ref[:, g * ATT_HD:(g + 1) * ATT_HD], qn_ref[...]), cosq_ref[...], sinq_ref[...])
        o_ref[:, g * ATT_HD:(g + 1) * ATT_HD] = _softmax_av(q, kb, vb)


def _attention(u, qn, kn, cache_k, cache_v, cos, sin, b_ctx, s_ctx, b_lat, s_lat):
    ntok = u.shape[0]
    t_ctx = b_ctx * s_ctx
    gw = ATT_GROUPS * ATT_HD
    kcol = (ATT_KV_HEADS * gw) // ATT_HD
    vcol = kcol + ATT_KV_HEADS
    kv_spec = pl.BlockSpec((s_ctx, ATT_HD), lambda b, h: (b, h))
    att_c, k_ctx, v_ctx = pl.pallas_call(
        _att_ctx_kernel,
        grid=(b_ctx, ATT_KV_HEADS),
        in_specs=[pl.BlockSpec((s_ctx, gw), lambda b, h: (b, h)),
                  pl.BlockSpec((s_ctx, ATT_HD), lambda b, h: (b, kcol + h)),
                  pl.BlockSpec((s_ctx, ATT_HD), lambda b, h: (b, vcol + h)),
                  pl.BlockSpec((1, ATT_HD), lambda b, h: (0, 0)),
                  pl.BlockSpec((1, ATT_HD), lambda b, h: (0, 0))],
        out_specs=[pl.BlockSpec((s_ctx, gw), lambda b, h: (b, h)), kv_spec, kv_spec],
        out_shape=[jax.ShapeDtypeStruct((t_ctx, ATT_KV_HEADS * gw), F32),
                   jax.ShapeDtypeStruct((t_ctx, ATT_KV_HEADS * ATT_HD), F32),
                   jax.ShapeDtypeStruct((t_ctx, ATT_KV_HEADS * ATT_HD), F32)],
        compiler_params=_cparams(("arbitrary", "arbitrary")),
        name="att_ctx",
    )(u, u, u, qn, kn)

    tq = 256
    nqb = s_lat // tq
    past = cache_k.shape[1]
    qrow0 = t_ctx // tq
    krow0 = t_ctx // s_lat
    att_l = pl.pallas_call(
        functools.partial(_att_lat_kernel, past=past),
        grid=(b_lat, ATT_KV_HEADS, nqb),
        in_specs=[pl.BlockSpec((tq, gw), lambda b, h, i: (qrow0 + b * nqb + i, h)),
                  pl.BlockSpec((s_lat, ATT_HD), lambda b, h, i: (krow0 + b, kcol + h)),
                  pl.BlockSpec((s_lat, ATT_HD), lambda b, h, i: (krow0 + b, vcol + h)),
                  pl.BlockSpec((None, past, ATT_HD), lambda b, h, i: (b, 0, h)),
                  pl.BlockSpec((None, past, ATT_HD), lambda b, h, i: (b, 0, h)),
                  pl.BlockSpec((1, ATT_HD), lambda b, h, i: (0, 0)),
                  pl.BlockSpec((1, ATT_HD), lambda b, h, i: (0, 0)),
                  pl.BlockSpec((tq, ATT_HD), lambda b, h, i: (i, 0)),
                  pl.BlockSpec((tq, ATT_HD), lambda b, h, i: (i, 0)),
                  pl.BlockSpec((s_lat, ATT_HD), lambda b, h, i: (0, 0)),
                  pl.BlockSpec((s_lat, ATT_HD), lambda b, h, i: (0, 0))],
        out_specs=pl.BlockSpec((tq, gw), lambda b, h, i: (b * nqb + i, h)),
        out_shape=jax.ShapeDtypeStruct((ntok - t_ctx, ATT_KV_HEADS * gw), F32),
        scratch_shapes=[pltpu.VMEM((past + s_lat, ATT_HD), BF16), pltpu.VMEM((past + s_lat, ATT_HD), BF16)],
        compiler_params=_cparams(("arbitrary", "arbitrary", "arbitrary")),
        name="att_lat",
    )(u, u, u, cache_k, cache_v, qn, kn, cos, sin, cos, sin)
    return att_c, att_l, k_ctx, v_ctx


def _rope_tables(n_tokens):
    pos = jnp.arange(n_tokens)
    row = (pos // GRID_W).astype(F32)
    col = (pos % GRID_W).astype(F32)
    n_freq = ATT_HD // 4
    inv_freq = ROPE_THETA ** (-jnp.arange(n_freq, dtype=F32) / n_freq)
    ang_r = row[:, None] * inv_freq[None, :]
    ang_c = col[:, None] * inv_freq[None, :]
    cos = jnp.concatenate([jnp.cos(ang_r), jnp.cos(ang_r), jnp.cos(ang_c), jnp.cos(ang_c)], axis=-1)
    sin = jnp.concatenate([-jnp.sin(ang_r), jnp.sin(ang_r), -jnp.sin(ang_c), jnp.sin(ang_c)], axis=-1)
    return cos, sin


def _log_sigmoid(x):
    return jnp.minimum(x, 0.0) - jnp.log1p(jnp.exp(-jnp.abs(x)))


def _mlstm_kernel(*refs, zero_init, seq):
    if zero_init:
        (q_ref, k_ref, v_ref, o_ref, gc_ref, gt_ref, br_ref, bc_ref, ng_ref,
         out_ref, co_ref, mo_ref, h_scr, c_scr, m_scr) = refs
    else:
        (q_ref, k_ref, v_ref, o_ref, gc_ref, gt_ref, br_ref, bc_ref, ng_ref, c0_ref, m0_ref,
         out_ref, co_ref, mo_ref, h_scr, c_scr, m_scr) = refs
    nc = seq // CHUNK
    nseq = q_ref.shape[0] // seq
    if zero_init:
        c_scr[...] = jnp.zeros(c_scr.shape, F32)
        m_scr[...] = jnp.full(m_scr.shape, M_INIT, F32)
    else:
        c_scr[...] = c0_ref[...]
        m_scr[...] = m0_ref[...]

    ti = lax.broadcasted_iota(I32, (CHUNK, CHUNK), 0)
    si = lax.broadcasted_iota(I32, (CHUNK, CHUNK), 1)
    ones_col = jnp.where(lax.broadcasted_iota(I32, (CHUNK, M_HD), 1) == 0, 1.0, 0.0).astype(BF16)
    eye_d = jnp.where(lax.broadcasted_iota(I32, (M_HD, M_HD), 0) == lax.broadcasted_iota(I32, (M_HD, M_HD), 1),
                      1.0, 0.0).astype(BF16)

    def cumsum3(tri, x, tri_left):
        hi = x.astype(BF16)
        r1 = x - hi.astype(F32)
        mid = r1.astype(BF16)
        lo = (r1 - mid.astype(F32)).astype(BF16)
        if tri_left:
            return _dg(tri, hi, _NN) + (_dg(tri, mid, _NN) + _dg(tri, lo, _NN))
        return _dg(hi, tri, _NN) + (_dg(mid, tri, _NN) + _dg(lo, tri, _NN))
    gate_i = lambda d, h: 2 * M_HEADS * d + h
    gate_f = lambda d, h: 2 * M_HEADS * d + M_HEADS + h

    def chunk(c, carry):
        ch = []
        for sq, d in [(sq, d) for sq in range(nseq) for d in (0, 1)]:
            cc = c if d == 0 else nc - 1 - c
            rows = pl.ds(pl.multiple_of(sq * seq + cc * CHUNK, CHUNK), CHUNK)
            causal = (si <= ti) if d == 0 else (si >= ti)
            tri = jnp.where(causal, 1.0, 0.0).astype(BF16)
            tri_t = jnp.where((ti <= si) if d == 0 else (ti >= si), 1.0, 0.0).astype(BF16)
            q_all = q_ref[rows, :] * (M_HD ** -0.5)
            k_all = k_ref[rows, :]
            v_all = v_ref[rows, :]
            gcol = gc_ref[rows, :] + br_ref[...]
            grow = gt_ref[sq * nc + cc] + bc_ref[...]
            bcum_c_all = cumsum3(tri, _log_sigmoid(gcol), True)
            bcum_r_all = cumsum3(tri_t, _log_sigmoid(grow), False)
            for h in range(M_HEADS):
                sl = slice(h * M_HD, (h + 1) * M_HD)
                q, k, v = q_all[:, sl], k_all[:, sl], v_all[:, sl]
                ig_c = gcol[:, gate_i(d, h):gate_i(d, h) + 1]
                ig_r = grow[gate_i(d, h):gate_i(d, h) + 1, :]
                bcum_c = bcum_c_all[:, gate_f(d, h):gate_f(d, h) + 1]
                bcum_r = bcum_r_all[gate_f(d, h):gate_f(d, h) + 1, :]
                m_st = m_scr[sq, d, h][:, 0:1]
                dmat = jnp.where(causal, bcum_c - bcum_r + ig_r, -jnp.inf)
                inter = bcum_c + m_st
                m_t = jnp.maximum(inter, jnp.max(dmat, axis=1, keepdims=True))
                b_last = bcum_c[CHUNK - 1:CHUNK, :] if d == 0 else bcum_c[0:1, :]
                g_c = b_last - bcum_c + ig_c
                m_new = jnp.maximum(b_last + m_st, jnp.max(g_c, axis=0, keepdims=True))
                ch.append(dict(sq=sq, d=d, h=h, rows=rows, sl=sl, qb=q.astype(BF16), kb=k.astype(BF16),
                               vx=jnp.concatenate([v.astype(BF16), ones_col], axis=1),
                               c_st=c_scr[sq, d, h], m_t=m_t, m_new=m_new,
                               w_intra=jnp.exp(dmat - m_t), w_inter=jnp.exp(inter - m_t),
                               decay=jnp.exp(b_last + m_st - m_new), kw=(k * jnp.exp(g_c - m_new)).astype(BF16)))
        for x in ch:
            x['s_qk'] = _dg(x['qb'], x['kb'], _NT) * x['w_intra']
        for x in ch:
            x['qc'] = _dg(x['qb'], x['c_st'].astype(BF16), _NN)
        for x in ch:
            x['kw_t'] = _dg(eye_d, x['kw'], _NT).astype(BF16)
        for x in ch:
            x['sv'] = _dg(x['s_qk'].astype(BF16), x['vx'], _NN)
        for x in ch:
            x['upd'] = _dg(x['kw_t'], x['vx'], _NN)
        for x in ch:
            sq, d, h = x['sq'], x['d'], x['h']
            tot = x['w_inter'] * x['qc'] + x['sv']
            den = tot[:, M_HD:M_HD + 1]
            h_scr[d, x['rows'], x['sl']] = tot[:, :M_HD] / jnp.maximum(jnp.abs(den), jnp.exp(-x['m_t']))
            c_scr[sq, d, h] = x['decay'] * x['c_st'] + x['upd']
            m_scr[sq, d, h] = jnp.broadcast_to(x['m_new'], (1, M_HD))
        return carry

    lax.fori_loop(0, nc, chunk, 0)
    hsum = h_scr[0] + h_scr[1]
    hn = jnp.concatenate(
        [hsum[:, h * M_HD:(h + 1) * M_HD]
         * lax.rsqrt(jnp.mean(hsum[:, h * M_HD:(h + 1) * M_HD] ** 2, axis=-1, keepdims=True) + NORM_EPS)
         for h in range(M_HEADS)], axis=1)
    out_ref[...] = hn * ng_ref[...] * _sigmoid(o_ref[...])
    co_ref[...] = c_scr[...]
    mo_ref[...] = m_scr[...]


def _mlstm(u, gcol, gt, bias, norm_g, states, row0, batch, seq, seqs_per_step):
    zero_init = states is None
    nq = seqs_per_step
    rows = nq * seq
    assert batch % nq == 0 and row0 % nq == 0
    rb = row0 // nq
    width = M_HEADS * M_HD
    qc = COL_M // width
    ngate = 4 * M_HEADS
    blk = lambda off: pl.BlockSpec((rows, width), lambda b: (rb + b, off))
    st_c = pl.BlockSpec((nq, 2, M_HEADS, M_HD, 2 * M_HD), lambda b: (b, 0, 0, 0, 0))
    st_v = pl.BlockSpec((nq, 2, M_HEADS, 1, M_HD), lambda b: (b, 0, 0, 0, 0))
    in_specs = [blk(qc), blk(qc + 1), blk(qc + 2), blk(qc + 3),
                pl.BlockSpec((rows, ngate), lambda b: (rb + b, 0)),
                pl.BlockSpec((rows // CHUNK, ngate, CHUNK), lambda b: (rb + b, 0, 0)),
                pl.BlockSpec((1, ngate), lambda b: (0, 0)),
                pl.BlockSpec((ngate, 1), lambda b: (0, 0)),
                pl.BlockSpec((1, width), lambda b: (0, 0))]
    args = [u, u, u, u, gcol, gt, bias.reshape(1, ngate), bias.reshape(ngate, 1), norm_g]
    if not zero_init:
        in_specs += [st_c, st_v]
        args += list(states)
    return pl.pallas_call(
        functools.partial(_mlstm_kernel, zero_init=zero_init, seq=seq),
        grid=(batch // nq,),
        in_specs=in_specs,
        out_specs=[pl.BlockSpec((rows, width), lambda b: (b, 0)), st_c, st_v],
        out_shape=[jax.ShapeDtypeStruct((batch * seq, width), F32),
                   jax.ShapeDtypeStruct((batch, 2, M_HEADS, M_HD, 2 * M_HD), F32),
                   jax.ShapeDtypeStruct((batch, 2, M_HEADS, 1, M_HD), F32)],
        scratch_shapes=[pltpu.VMEM((2, rows, width), F32), pltpu.VMEM((nq, 2, M_HEADS, M_HD, 2 * M_HD), F32),
                        pltpu.VMEM((nq, 2, M_HEADS, 1, M_HD), F32)],
        compiler_params=_cparams(("arbitrary",)),
        name="mlstm_ctx" if zero_init else "mlstm_lat",
    )(*args)


def _rwkv_kernel(*refs, zero_init, seq):
    if zero_init:
        (r_ref, k_ref, v_ref, xl_ref, xg_ref, mur_ref, muk_ref, muv_ref, mul_ref, mug_ref,
         w0_ref, wup_ref, a0_ref, aup_ref, gup_ref, kk_ref, ka_ref, rk_ref, lng_ref, lnb_ref,
         out_ref, so_ref, r_scr, v_scr, kk_scr, g_scr, bonus_scr, lw_scr, kd_scr, b_scr, y_scr, s_scr) = refs
    else:
        (r_ref, k_ref, v_ref, xl_ref, xg_ref, mur_ref, muk_ref, muv_ref, mul_ref, mug_ref,
         w0_ref, wup_ref, a0_ref, aup_ref, gup_ref, kk_ref, ka_ref, rk_ref, lng_ref, lnb_ref, s0_ref,
         out_ref, so_ref, r_scr, v_scr, kk_scr, g_scr, bonus_scr, lw_scr, kd_scr, b_scr, y_scr, s_scr) = refs
    nc = seq // CHUNK
    hd = R_HD
    nh = r_ref.shape[1] // hd
    rows_blk = r_ref.shape[0]
    nseq = rows_blk // seq

    def tshift(x_ref, mu_ref):
        x = x_ref[...]
        row = lax.broadcasted_iota(I32, x.shape, 0) % seq
        prev = jnp.where(row == 0, 0.0, pltpu.roll(x, 1, 0))
        nxt = jnp.where(row == seq - 1, 0.0, pltpu.roll(x, rows_blk - 1, 0))
        return x + mu_ref[...] * (0.5 * (prev + nxt) - x)

    r = tshift(r_ref, mur_ref)
    k = tshift(k_ref, muk_ref)
    v = tshift(v_ref, muv_ref)
    xl = tshift(xl_ref, mul_ref)
    xg = tshift(xg_ref, mug_ref)
    g = _dot(_sigmoid(xg), gup_ref[...])
    wlanes = nh * hd
    same_head = (lax.broadcasted_iota(I32, (wlanes, wlanes), 0) // hd
                 == lax.broadcasted_iota(I32, (wlanes, wlanes), 1) // hd)
    head_ones = jnp.where(same_head, 1.0, 0.0).astype(BF16)

    def head_sum(x):
        hi = x.astype(BF16)
        r1 = x - hi.astype(F32)
        mid = r1.astype(BF16)
        lo = (r1 - mid.astype(F32)).astype(BF16)
        return _dg(hi, head_ones, _NN) + (_dg(mid, head_ones, _NN) + _dg(lo, head_ones, _NN))

    kkp = k * kk_ref[...]
    kk = kkp * lax.rsqrt(jnp.maximum(head_sum(kkp * kkp), 1e-24))
    tw = jnp.tanh(xl[:, 0:hd])
    xa = xl[:, hd:2 * hd]
    r_scr[...] = r
    v_scr[...] = v
    kk_scr[...] = kk
    g_scr[...] = g
    bonus_scr[...] = head_sum(r * k * rk_ref[...]) * v
    for d in (0, 1):
        lw_scr[d] = -math.exp(-0.5) * _sigmoid(w0_ref[d] + _dot(tw, wup_ref[d]))
        ad = _sigmoid(a0_ref[d] + _dot(xa, aup_ref[d]))
        kd_scr[d] = k * (1.0 + (ad - 1.0) * ka_ref[...])
        b_scr[d] = kk * ad
    npair = nh // 2
    pw = 2 * hd
    zero_blk = jnp.zeros((hd, hd), F32)
    for q in range(nseq):
        for d in (0, 1):
            for p in range(npair):
                if zero_init:
                    s_scr[q, d, p] = jnp.zeros((pw, pw), F32)
                else:
                    s_scr[q, d, p] = jnp.concatenate(
                        [jnp.concatenate([s0_ref[q, d, 2 * p], zero_blk], axis=1),
                         jnp.concatenate([zero_blk, s0_ref[q, d, 2 * p + 1]], axis=1)], axis=0)
    y_scr[...] = jnp.zeros(y_scr.shape, F32)

    ti = lax.broadcasted_iota(I32, (CHUNK, pw), 0)
    si = lax.broadcasted_iota(I32, (CHUNK, pw), 1) % CHUNK
    eye2 = (ti == si).astype(F32)
    tri_i = lax.broadcasted_iota(I32, (CHUNK, CHUNK), 0)
    tri_j = lax.broadcasted_iota(I32, (CHUNK, CHUNK), 1)
    first = lax.broadcasted_iota(I32, (1, pw), 1) < hd
    diag_blk = ((lax.broadcasted_iota(I32, (pw, pw), 0) < hd) == (lax.broadcasted_iota(I32, (pw, pw), 1) < hd))

    def bd(x):
        xb = x.astype(BF16)
        zero = jnp.zeros_like(xb)
        return jnp.concatenate([jnp.where(first, xb, zero), jnp.where(first, zero, xb)], axis=0)

    def chunk(c, carry):
        chains = []
        for q, d in [(q, d) for q in range(nseq) for d in (0, 1)]:
            cc = c if d == 0 else nc - 1 - c
            rows = pl.ds(pl.multiple_of(q * seq + cc * CHUNK, CHUNK), CHUNK)
            incl = (si <= ti) if d == 0 else (si >= ti)
            strict = (si < ti) if d == 0 else (si > ti)
            lw = lw_scr[d, rows, :]
            lw_hi, lw_lo = _split(lw)
            tri = jnp.where((tri_j <= tri_i) if d == 0 else (tri_j >= tri_i), 1.0, 0.0).astype(BF16)
            lc = _dg(tri, lw_hi, _NN) + _dg(tri, lw_lo, _NN)
            l_last = lc[CHUNK - 1:CHUNK, :] if d == 0 else lc[0:1, :]
            e_neg = jnp.exp(-lc)
            e_end = jnp.exp(l_last - lc)
            vc = v_scr[rows, :]
            kdc = kd_scr[d, rows, :]
            bc = b_scr[d, rows, :]
            rt = r_scr[rows, :] * jnp.exp(lc)
            kkt = kk_scr[rows, :] * jnp.exp(lc - lw)
            kh = kdc * e_neg
            bh = bc * e_neg
            kbar = kdc * e_end
            bbar = bc * e_end
            w_end = jnp.exp(l_last)
            for p in range(npair):
                sl = slice(p * pw, (p + 1) * pw)
                chains.append(dict(
                    q=q, d=d, pair=p, sl=sl, rows=rows, incl=incl, strict=strict,
                    lhs=jnp.concatenate([kkt[:, sl], rt[:, sl]], axis=0).astype(BF16),
                    rhs=jnp.concatenate([bd(bh[:, sl]), bd(kh[:, sl])], axis=0),
                    end=jnp.concatenate([kbar[:, sl], bbar[:, sl]], axis=0),
                    v=vc[:, sl], w_end=w_end[:, sl], s0=s_scr[q, d, p]))
        for ch in chains:
            ch['ab'] = _dg(ch['lhs'], ch['rhs'], _NT)
        for ch in chains:
            ch['proj'] = _dg(ch['lhs'], ch['s0'].astype(BF16), _NT)
        for ch in chains:
            ab = ch['ab']
            ch['a_kb'] = jnp.where(ch['strict'], ab[:CHUNK, :pw], 0.0)
            ch['b_rb'] = jnp.where(ch['incl'], ab[CHUNK:, :pw], 0.0)
            ch['akk_brk'] = jnp.concatenate([jnp.where(ch['strict'], ab[:CHUNK, pw:], 0.0),
                                             jnp.where(ch['incl'], ab[CHUNK:, pw:], 0.0)], axis=0)
        for ch in chains:
            ch['p'] = _dg(ch['a_kb'].astype(BF16), bd(ch['a_kb']), _NN)
        for ch in chains:
            ch['abv'] = _dg(ch['akk_brk'].astype(BF16), bd(ch['v']), _NN)
        for ch in chains:
            inv = eye2 - ch['a_kb']
            ch['inv'] = inv + _dg(inv.astype(BF16), bd(ch['p']), _NN)
        span = 4
        while span < CHUNK:
            for ch in chains:
                ch['p'] = _dg(ch['p'].astype(BF16), bd(ch['p']), _NN)
            for ch in chains:
                ch['inv'] = ch['inv'] + _dg(ch['inv'].astype(BF16), bd(ch['p']), _NN)
            span *= 2
        for ch in chains:
            ch['u'] = _dg(ch['inv'].astype(BF16), bd(ch['proj'][:CHUNK] + ch['abv'][:CHUNK]), _NN)
        for ch in chains:
            y = ch['proj'][CHUNK:] + ch['abv'][CHUNK:] - _dg(ch['b_rb'].astype(BF16), bd(ch['u']), _NN)
            y_scr[ch['rows'], ch['sl']] += y
        for ch in chains:
            upd = _dot3(jnp.concatenate([ch['v'], -ch['u']], axis=0), ch['end'], _TN)
            s_scr[ch['q'], ch['d'], ch['pair']] = ch['s0'] * ch['w_end'] + jnp.where(diag_blk, upd, 0.0)
        return carry

    lax.fori_loop(0, nc, chunk, 0)

    y = y_scr[...]
    dev = y - head_sum(y) * (1.0 / hd)
    yn = dev * lax.rsqrt(head_sum(dev * dev) * (1.0 / hd) + GN_EPS)
    out_ref[...] = (yn * lng_ref[...] + lnb_ref[...] + bonus_scr[...]) * g_scr[...]
    for q in range(nseq):
        for d in (0, 1):
            for p in range(npair):
                s = s_scr[q, d, p]
                so_ref[q, d, 2 * p] = s[:hd, :hd]
                so_ref[q, d, 2 * p + 1] = s[hd:, hd:]


def _rwkv(u, p, state, row0, batch, seq, heads_per_step, seqs_per_step):
    zero_init = state is None
    nh = heads_per_step
    nq = seqs_per_step
    rows = nq * seq
    assert batch % nq == 0 and row0 % nq == 0
    wd = nh * R_HD
    nsteps = R_HEADS // nh
    width = R_HEADS * R_HD
    rc = COL_R // wd
    sec = width // wd
    lc = (COL_R + 3 * width) // LANES
    ublk = lambda off: pl.BlockSpec((rows, wd), lambda b, h: (row0 // nq + b, rc + off + h))
    ufix = lambda blk: pl.BlockSpec((rows, LANES), lambda b, h: (row0 // nq + b, blk))
    mblk = lambda off: pl.BlockSpec((1, wd), lambda b, h: (0, off + h))
    mfix = lambda blk: pl.BlockSpec((1, LANES), lambda b, h: (0, blk))
    vec = pl.BlockSpec((1, wd), lambda b, h: (0, h))
    in_specs = [ublk(0), ublk(sec), ublk(2 * sec), ufix(lc), ufix(lc + 1),
                mblk(0), mblk(sec), mblk(2 * sec), mfix(lc - COL_R // LANES), mfix(lc - COL_R // LANES + 1),
                pl.BlockSpec((2, 1, wd), lambda b, h: (0, 0, h)),
                pl.BlockSpec((2, R_HD, wd), lambda b, h: (0, 0, h)),
                pl.BlockSpec((2, 1, wd), lambda b, h: (0, 0, h)),
                pl.BlockSpec((2, R_HD, wd), lambda b, h: (0, 0, h)),
                pl.BlockSpec((LANES, wd), lambda b, h: (0, h)),
                vec, vec, vec, vec, vec]
    args = [u, u, u, u, u, p['mu'], p['mu'], p['mu'], p['mu'], p['mu'],
            p['w0'], p['w_up'], p['a0'], p['a_up'], p['g_up'], p['k_k'], p['k_a'], p['r_k'], p['ln_g'], p['ln_b']]
    st_spec = pl.BlockSpec((nq, 2, nh, R_HD, R_HD), lambda b, h: (b, 0, h, 0, 0))
    if not zero_init:
        in_specs.append(st_spec)
        args.append(state)
    big = lambda n: pltpu.VMEM((n, rows, wd), F32)
    return pl.pallas_call(
        functools.partial(_rwkv_kernel, zero_init=zero_init, seq=seq),
        grid=(batch // nq, nsteps),
        in_specs=in_specs,
        out_specs=[pl.BlockSpec((rows, wd), lambda b, h: (b, h)), st_spec],
        out_shape=[jax.ShapeDtypeStruct((batch * seq, width), F32),
                   jax.ShapeDtypeStruct((batch, 2, R_HEADS, R_HD, R_HD), F32)],
        scratch_shapes=[pltpu.VMEM((rows, wd), F32)] * 5
                       + [big(2), big(2), big(2), pltpu.VMEM((rows, wd), F32),
                          pltpu.VMEM((nq, 2, nh // 2, 2 * R_HD, 2 * R_HD), F32)],
        compiler_params=_cparams(("arbitrary", "arbitrary")),
        name="rwkv_ctx" if zero_init else "rwkv_lat",
    )(*args)


def _top2_sum(a, b, c, d):
    m1, n1 = jnp.maximum(a, b), jnp.minimum(a, b)
    m2, n2 = jnp.maximum(c, d), jnp.minimum(c, d)
    return jnp.maximum(m1, m2) + jnp.maximum(jnp.minimum(m1, m2), jnp.maximum(n1, n2))


def _first_argmax(vals):
    best = functools.reduce(jnp.maximum, vals)
    idx = jnp.full(best.shape, len(vals) - 1, I32)
    for j in range(len(vals) - 2, -1, -1):
        idx = jnp.where(vals[j] == best, j, idx)
    return best, idx


def _out_kernel(attc_ref, attl_ref, mc_ref, ml_ref, rc_ref, rl_ref, xc_ref, xl_ref, w_ref, g1_ref, sh2_ref, sc2_ref,
                n2_ref, rw_ref, rb_ref, x1_ref, h2_ref, route_ref, cnt_ref, cnt_scr, *, n_ctx_tiles):
    @pl.when(pl.program_id(0) == 0)
    def _():
        cnt_scr[...] = jnp.zeros(cnt_scr.shape, F32)

    is_ctx = pl.program_id(0) < n_ctx_tiles
    both = lambda c_ref, l_ref: jnp.where(is_ctx, c_ref[...], l_ref[...])
    na = attc_ref.shape[1]
    nm = mc_ref.shape[1]
    mix = (jnp.dot(both(attc_ref, attl_ref).astype(BF16), w_ref[0:na, :], preferred_element_type=F32)
           + jnp.dot(both(mc_ref, ml_ref).astype(BF16), w_ref[na:na + nm, :], preferred_element_type=F32)
           + jnp.dot(both(rc_ref, rl_ref).astype(BF16), w_ref[na + nm:, :], preferred_element_type=F32))
    x1 = both(xc_ref, xl_ref) + g1_ref[...] * mix
    x1_ref[...] = x1
    h2 = _rms(x1, n2_ref[...]) * (1.0 + sc2_ref[...]) + sh2_ref[...]
    h2_ref[...] = h2
    logits = _dot1(rw_ref[...], h2, _NT)
    s = _sigmoid(logits)
    ssel = s + rb_ref[...]
    srow = [s[e:e + 1, :] for e in range(N_EXPERTS)]
    brow = [ssel[e:e + 1, :] for e in range(N_EXPERTS)]
    gscore = [_top2_sum(*brow[EXPERTS_PER_GROUP * g:EXPERTS_PER_GROUP * (g + 1)]) for g in range(N_EXPERT_GROUPS)]
    _, gidx = _first_argmax(gscore)
    pick = lambda rows, j: functools.reduce(
        lambda acc, g: jnp.where(gidx == g, rows[EXPERTS_PER_GROUP * g + j], acc),
        range(N_EXPERT_GROUPS - 2, -1, -1), rows[EXPERTS_PER_GROUP * (N_EXPERT_GROUPS - 1) + j])
    ing = [pick(brow, j) for j in range(EXPERTS_PER_GROUP)]
    sin_ = [pick(srow, j) for j in range(EXPERTS_PER_GROUP)]
    _, l1 = _first_argmax(ing)
    _, l2 = _first_argmax([jnp.where(l1 == j, -jnp.inf, ing[j]) for j in range(EXPERTS_PER_GROUP)])
    sel = lambda l: functools.reduce(lambda acc, j: jnp.where(l == j, sin_[j], acc),
                                     range(EXPERTS_PER_GROUP - 2, -1, -1), sin_[EXPERTS_PER_GROUP - 1])
    w1, w2 = sel(l1), sel(l2)
    tot = w1 + w2
    e1 = gidx * EXPERTS_PER_GROUP + l1
    e2 = gidx * EXPERTS_PER_GROUP + l2
    tm = e1.shape[1]
    eid = lax.broadcasted_iota(I32, (N_EXPERTS, tm), 0)
    oh1 = eid == e1
    oh2 = eid == e2
    picked = jnp.where(jnp.logical_or(oh1, oh2), 1.0, 0.0)
    earlier = jnp.where(lax.broadcasted_iota(I32, (tm, tm), 0) < lax.broadcasted_iota(I32, (tm, tm), 1), 1.0, 0.0)
    rank = cnt_scr[:, 0:1] + jnp.dot(picked.astype(BF16), earlier.astype(BF16), preferred_element_type=F32)
    pos1 = jnp.sum(jnp.where(oh1, rank, 0.0), axis=0, keepdims=True)
    pos2 = jnp.sum(jnp.where(oh2, rank, 0.0), axis=0, keepdims=True)
    cnt = cnt_scr[...] + jnp.sum(picked, axis=1, keepdims=True)
    cnt_scr[...] = cnt
    cnt_ref[...] = cnt
    zero = jnp.zeros_like(w1)
    route_ref[...] = jnp.concatenate([e1.astype(F32), e2.astype(F32), w1 / tot, w2 / tot, pos1, pos2, zero, zero],
                                     axis=0)


def _out_proj(att, m_out, r_out, x, w_out, mod3, n2g, rw_t, rb, s_lat):
    t_ctx, d = x[0].shape
    ntok = t_ctx + x[1].shape[0]
    tm = 256
    nct = t_ctx // tm
    row = lambda i: _mod_row(i * tm, t_ctx, s_lat)
    modblk = lambda j: pl.BlockSpec((None, 1, d), lambda i: (row(i), 0, j))
    pair = lambda a: [pl.BlockSpec((tm, a[0].shape[1]), lambda i: (jnp.minimum(i, nct - 1), 0)),
                      pl.BlockSpec((tm, a[1].shape[1]), lambda i: (jnp.maximum(i - nct, 0), 0))]
    return pl.pallas_call(
        functools.partial(_out_kernel, n_ctx_tiles=nct),
        grid=(ntok // tm,),
        in_specs=pair(att) + pair(m_out) + pair(r_out) + pair(x) + [
                  pl.BlockSpec(w_out.shape, lambda i: (0, 0)),
                  modblk(2), modblk(3), modblk(4),
                  pl.BlockSpec((1, d), lambda i: (0, 0)),
                  pl.BlockSpec(rw_t.shape, lambda i: (0, 0)),
                  pl.BlockSpec(rb.shape, lambda i: (0, 0))],
        out_specs=[pl.BlockSpec((tm, d), lambda i: (i, 0)),
                   pl.BlockSpec((tm, d), lambda i: (i, 0)),
                   pl.BlockSpec((8, tm), lambda i: (0, i)),
                   pl.BlockSpec((N_EXPERTS, LANES), lambda i: (0, 0))],
        out_shape=[jax.ShapeDtypeStruct((ntok, d), F32), jax.ShapeDtypeStruct((ntok, d), F32),
                   jax.ShapeDtypeStruct((8, ntok), F32), jax.ShapeDtypeStruct((N_EXPERTS, LANES), F32)],
        scratch_shapes=[pltpu.VMEM((N_EXPERTS, LANES), F32)],
        compiler_params=_cparams(("arbitrary",)),
        name="out_proj",
    )(*att, *m_out, *r_out, *x, w_out, mod3, mod3, mod3, n2g, rw_t, rb)


def _row_gather(src_hbm, idx_ref, base, dst, sem, n, unrolled, lo=0):
    def start(j):
        pltpu.make_async_copy(src_hbm.at[pl.ds(idx_ref[base + j], 1), :], dst.at[pl.ds(j, 1), :], sem).start()

    if unrolled:
        for j in range(lo, n):
            start(j)
    else:
        def body(j, c):
            start(j)
            return c
        lax.fori_loop(lo, n, body, 0)


def _expert_kernel(be_ref, tok_ref, nused_ref, h_hbm, w1_ref, w3_ref, w2_ref, y_ref, xbuf, w1b, w3b, w2b, sem):
    i = pl.program_id(0)
    n_used = nused_ref[0]
    wait = lambda s: pltpu.make_async_copy(h_hbm.at[pl.ds(0, EXPERT_ROWS), :], xbuf.at[s], sem.at[s]).wait()

    @pl.when(i < n_used)
    def _():
        slot = i % 2

        @pl.when(i == 0)
        def _():
            _row_gather(h_hbm, tok_ref, 0, xbuf.at[0], sem.at[0], EXPERT_ROWS, unrolled=False)

        @pl.when(jnp.logical_or(i == 0, be_ref[i] != be_ref[jnp.maximum(i - 1, 0)]))
        def _():
            w1b[...] = w1_ref[...].astype(BF16)
            w3b[...] = w3_ref[...].astype(BF16)
            w2b[...] = w2_ref[...].astype(BF16)

        wait(slot)
        de = w1b.shape[1]
        half = de // 2
        groups = 6
        per = EXPERT_ROWS // groups

        def next_rows(gi):
            hi = EXPERT_ROWS if gi == groups - 1 else (gi + 1) * per
            _row_gather(h_hbm, tok_ref, (i + 1) * EXPERT_ROWS, xbuf.at[1 - slot], sem.at[1 - slot], hi,
                        unrolled=True, lo=gi * per)

        xb = xbuf[slot].astype(BF16)
        hmid = []
        for n in range(2):
            cols = slice(n * half, (n + 1) * half)
            next_rows(2 * n)
            a = jnp.dot(xb, w1b[:, cols], preferred_element_type=F32)
            next_rows(2 * n + 1)
            b = jnp.dot(xb, w3b[:, cols], preferred_element_type=F32)
            hmid.append(((a * _sigmoid(a)) * b).astype(BF16))
        next_rows(4)
        y = jnp.dot(hmid[0], w2b[0:half, :], preferred_element_type=F32)
        next_rows(5)
        y_ref[...] = y + jnp.dot(hmid[1], w2b[half:, :], preferred_element_type=F32)

        @pl.when(i == n_used - 1)
        def _():
            wait(1 - slot)

    @pl.when(i >= n_used)
    def _():
        y_ref[...] = jnp.zeros(y_ref.shape, F32)


def _experts(h2, block_e, row_tok, n_used, w1, w3, w2, layer):
    ntok, d = h2.shape
    de = w1.shape[3]
    n_rows = row_tok.shape[0] - EXPERT_ROWS
    nb = n_rows // EXPERT_ROWS
    return pl.pallas_call(
        _expert_kernel,
        grid_spec=pltpu.PrefetchScalarGridSpec(
            num_scalar_prefetch=3,
            grid=(nb,),
            in_specs=[pl.BlockSpec(memory_space=pl.ANY),
                      pl.BlockSpec((None, None, d, de), lambda i, be, tok, nu: (layer, be[i], 0, 0)),
                      pl.BlockSpec((None, None, d, de), lambda i, be, tok, nu: (layer, be[i], 0, 0)),
                      pl.BlockSpec((None, None, de, d), lambda i, be, tok, nu: (layer, be[i], 0, 0))],
            out_specs=pl.BlockSpec((EXPERT_ROWS, d), lambda i, be, tok, nu: (i, 0)),
            scratch_shapes=[pltpu.VMEM((2, EXPERT_ROWS, d), F32), pltpu.VMEM((d, de), BF16),
                            pltpu.VMEM((d, de), BF16), pltpu.VMEM((de, d), BF16), pltpu.SemaphoreType.DMA((2,))]),
        out_shape=jax.ShapeDtypeStruct((n_rows, d), F32),
        compiler_params=_cparams(("arbitrary",)),
        name="experts",
    )(block_e, row_tok, n_used, h2, w1, w3, w2)


def _combine_kernel(d1_ref, d2_ref, y_hbm, x1_ref, g2_ref, gate_ref, oc_ref, ol_ref, ybuf, sem, *, n_ctx_tiles):
    tm = x1_ref.shape[0]
    i = pl.program_id(0)
    slot = i % 2

    def gather(tile, s, unrolled):
        _row_gather(y_hbm, d1_ref, tile * tm, ybuf.at[s, 0], sem.at[s], tm, unrolled)
        _row_gather(y_hbm, d2_ref, tile * tm, ybuf.at[s, 1], sem.at[s], tm, unrolled)

    def wait(s):
        pltpu.make_async_copy(y_hbm.at[pl.ds(0, tm), :], ybuf.at[s, 0], sem.at[s]).wait()
        pltpu.make_async_copy(y_hbm.at[pl.ds(0, tm), :], ybuf.at[s, 1], sem.at[s]).wait()

    @pl.when(i == 0)
    def _():
        gather(0, 0, False)

    wait(slot)
    gather(i + 1, 1 - slot, True)
    gate = gate_ref[...]
    moe = ybuf[slot, 0] * gate[:, 0:1] + ybuf[slot, 1] * gate[:, 1:2]
    new_x = x1_ref[...] + g2_ref[...] * moe

    @pl.when(i < n_ctx_tiles)
    def _():
        oc_ref[...] = new_x

    @pl.when(i >= n_ctx_tiles)
    def _():
        ol_ref[...] = new_x

    @pl.when(i == pl.num_programs(0) - 1)
    def _():
        wait(1 - slot)


def _combine(y, x1, mod3, gate, dest1, dest2, t_ctx, s_lat):
    ntok, d = x1.shape
    tm = 256
    nct = t_ctx // tm
    row = lambda i: _mod_row(i * tm, t_ctx, s_lat)
    return pl.pallas_call(
        functools.partial(_combine_kernel, n_ctx_tiles=nct),
        grid_spec=pltpu.PrefetchScalarGridSpec(
            num_scalar_prefetch=2,
            grid=(ntok // tm,),
            in_specs=[pl.BlockSpec(memory_space=pl.ANY),
                      pl.BlockSpec((tm, d), lambda i, a, b: (i, 0)),
                      pl.BlockSpec((None, 1, d), lambda i, a, b: (row(i), 0, 5)),
                      pl.BlockSpec((tm, 2), lambda i, a, b: (i, 0))],
            out_specs=[pl.BlockSpec((tm, d), lambda i, a, b: (jnp.minimum(i, nct - 1), 0)),
                       pl.BlockSpec((tm, d), lambda i, a, b: (jnp.maximum(i - nct, 0), 0))],
            scratch_shapes=[pltpu.VMEM((2, 2, tm, d), F32), pltpu.SemaphoreType.DMA((2,))]),
        out_shape=[jax.ShapeDtypeStruct((t_ctx, d), F32), jax.ShapeDtypeStruct((ntok - t_ctx, d), F32)],
        compiler_params=_cparams(("arbitrary",)),
        name="combine",
    )(dest1, dest2, y, x1, mod3, gate)


def _dispatch(route, counts, tile):
    ntok = route.shape[1]
    e = route[0:2].astype(I32)
    pos = route[4:6].astype(I32)
    gate = route[2:4].T
    counts = counts[:, 0].astype(I32)
    padded = (counts + EXPERT_ROWS - 1) // EXPERT_ROWS * EXPERT_ROWS
    pad_end = jnp.cumsum(padded)
    pad_start = pad_end - padded
    onehot = (e[:, :, None] == jnp.arange(N_EXPERTS, dtype=I32)).astype(I32)
    dest = jnp.sum(onehot * pad_start, axis=-1) + pos
    n_rows = -(-(2 * ntok) // EXPERT_ROWS) * EXPERT_ROWS + N_EXPERTS * EXPERT_ROWS
    nb = n_rows // EXPERT_ROWS
    tok = jnp.broadcast_to(jnp.arange(ntok, dtype=I32)[None, :], (2, ntok))
    row_tok = jnp.zeros((n_rows + EXPERT_ROWS,), I32).at[dest.reshape(-1)].set(tok.reshape(-1), unique_indices=True)
    blk_start = jnp.arange(nb, dtype=I32) * EXPERT_ROWS
    block_e = jnp.minimum(jnp.sum((pad_end[None, :] <= blk_start[:, None]).astype(I32), axis=1), N_EXPERTS - 1)
    n_used = pad_end[-1:] // EXPERT_ROWS
    dest = jnp.pad(dest, ((0, 0), (0, tile)))
    return row_tok, block_e, n_used, gate, dest[0], dest[1]


def kernel(x_prompt, x_sample, c, cache_attn_k, cache_attn_v, state_mlstm_C, state_mlstm_n, state_mlstm_m, state_rwkv, c_ctx, norm1_g, norm2_g, w_mod, b_mod, w_in, w_out, attn_q_norm, attn_k_norm, mlstm_i_bias, mlstm_f_bias, mlstm_norm_g, rwkv_mu, rwkv_w0, rwkv_w_up, rwkv_a0, rwkv_a_up, rwkv_g_up, rwkv_k_k, rwkv_k_a, rwkv_r_k, rwkv_ln_g, rwkv_ln_b, router_w, router_b, exp_w1, exp_w3, exp_w2):
    b_ctx, s_ctx, d = x_prompt.shape
    b_lat, s_lat, _ = x_sample.shape
    depth = w_in.shape[0]
    t_ctx = b_ctx * s_ctx
    ntok = t_ctx + b_lat * s_lat
    assert b_lat + 1 <= 8 and s_lat % 1024 == 0 and t_ctx % 1024 == 0 and t_ctx % s_lat == 0
    past = cache_attn_k.shape[2]

    x = (x_prompt.reshape(t_ctx, d), x_sample.reshape(b_lat * s_lat, d))
    c_all = jnp.zeros((8, d), F32).at[0].set(c_ctx).at[1:1 + b_lat].set(c)
    mod = _modulation(c_all, w_mod, b_mod)
    cos, sin = _rope_tables(s_lat)
    rw_t = router_w.T
    rb = router_b.reshape(N_EXPERTS, 1)
    r_width = R_HEADS * R_HD
    n_in = w_in.shape[2]
    gate_hi = COL_R + 4 * M_HEADS
    w_in_t = jnp.swapaxes(w_in, 1, 2)
    w_tail_t = jnp.concatenate([w_in_t[:, gate_hi:], w_in_t[:, COL_R:gate_hi],
                                jnp.zeros((depth, N_IN_PAD - n_in, d), F32)], axis=1).astype(BF16)

    ks, vs, cs, ns, ms, rs = [], [], [], [], [], []
    for l in range(depth):
        mod3 = mod[l].reshape(8, 1, 6 * d)
        u = _in_proj(x[0], x[1], norm1_g[l][None], mod3, w_in_t, l, w_tail_t[l], s_lat)

        ck = cache_attn_k[:, l].reshape(b_lat, past, ATT_KV_HEADS * ATT_HD)
        cv = cache_attn_v[:, l].reshape(b_lat, past, ATT_KV_HEADS * ATT_HD)
        att_c, att_l, k_ctx, v_ctx = _attention(u, attn_q_norm[l][None], attn_k_norm[l][None], ck, cv, cos, sin,
                                                b_ctx, s_ctx, b_lat, s_lat)
        ks.append(k_ctx.reshape(b_ctx, s_ctx, ATT_KV_HEADS, ATT_HD))
        vs.append(v_ctx.reshape(b_ctx, s_ctx, ATT_KV_HEADS, ATT_HD))

        gcol = u[:, COL_GATE:COL_GATE + 4 * M_HEADS]
        gt = gcol.reshape(ntok // CHUNK, CHUNK, 4 * M_HEADS).transpose(0, 2, 1)
        bias = jnp.stack([mlstm_i_bias[l], mlstm_f_bias[l]], axis=1)
        ng = mlstm_norm_g[l][None]
        m_c_out, cx_c, m_c = _mlstm(u, gcol, gt, bias, ng, None, 0, b_ctx, s_ctx, MLSTM_SEQS_PER_STEP_CTX)
        n_col = jnp.pad(state_mlstm_n[:, l][..., None], ((0, 0),) * 4 + ((0, M_HD - 1),))
        lat_states = (jnp.concatenate([state_mlstm_C[:, l], n_col], axis=-1),
                      jnp.broadcast_to(state_mlstm_m[:, l][..., None, None], (b_lat, 2, M_HEADS, 1, M_HD)))
        m_l_out, _, _ = _mlstm(u, gcol, gt, bias, ng, lat_states, t_ctx // s_lat, b_lat, s_lat, 1)
        cs.append(cx_c[..., :M_HD])
        ns.append(cx_c[..., M_HD])
        ms.append(m_c[:, :, :, 0, 0])

        rp = dict(mu=rwkv_mu[l][None], w0=rwkv_w0[l].reshape(2, 1, r_width), w_up=rwkv_w_up[l],
                  a0=rwkv_a0[l].reshape(2, 1, r_width), a_up=rwkv_a_up[l], g_up=rwkv_g_up[l],
                  k_k=rwkv_k_k[l][None], k_a=rwkv_k_a[l][None], r_k=rwkv_r_k[l].reshape(1, r_width),
                  ln_g=rwkv_ln_g[l][None], ln_b=rwkv_ln_b[l][None])
        r_c_out, r_c = _rwkv(u, rp, None, 0, b_ctx, s_ctx, RWKV_HEADS_PER_STEP_CTX, RWKV_SEQS_PER_STEP_CTX)
        r_l_out, _ = _rwkv(u, rp, state_rwkv[:, l], t_ctx // s_lat, b_lat, s_lat, RWKV_HEADS_PER_STEP_LAT, 1)
        rs.append(r_c)

        x1, h2, route, counts = _out_proj((att_c, att_l), (m_c_out, m_l_out), (r_c_out, r_l_out), x,
                                          w_out[l].astype(BF16), mod3, norm2_g[l][None], rw_t, rb, s_lat)
        row_tok, block_e, n_used, gate, dest1, dest2 = _dispatch(route, counts, 256)
        y = _experts(h2, block_e, row_tok, n_used, exp_w1, exp_w3, exp_w2, l)
        x = _combine(y, x1, mod3, gate, dest1, dest2, t_ctx, s_lat)

    y_prompt = x[0].reshape(b_ctx, s_ctx, d)
    y_sample = x[1].reshape(b_lat, s_lat, d)
    return (y_prompt, y_sample, jnp.stack(ks, axis=1), jnp.stack(vs, axis=1), jnp.stack(cs, axis=1),
            jnp.stack(ns, axis=1), jnp.stack(ms, axis=1), jnp.stack(rs, axis=1))
```

```python
import functools
import math

import jax
import jax.numpy as jnp
from jax import lax
from jax.experimental import pallas as pl
from jax.experimental.pallas import tpu as pltpu

F32 = jnp.float32
BF16 = jnp.bfloat16
I32 = jnp.int32

NORM_EPS = 1e-6
GN_EPS = 64e-5
M_INIT = -1e30
GRID_W = 64
ROPE_THETA = 10000.0
ATT_HD = 128
ATT_GROUPS = 4
ATT_KV_HEADS = 2
M_HD = 128
M_HEADS = 4
R_HD = 64
R_HEADS = 8
N_EXPERTS = 16
N_EXPERT_GROUPS = 4
EXPERTS_PER_GROUP = 4
CHUNK = 64
LANES = 128
EXPERT_ROWS = 256
VMEM_LIMIT = 58 * 1024 * 1024
RWKV_HEADS_PER_STEP_CTX = 8
RWKV_SEQS_PER_STEP_CTX = 2
MLSTM_SEQS_PER_STEP_CTX = 1
RWKV_HEADS_PER_STEP_LAT = 8

COL_ATT = 0
COL_M = 1536
COL_R = 3584
COL_GATE = 5376
N_IN_PAD = 5632


def _cparams(sem):
    return pltpu.CompilerParams(dimension_semantics=sem, vmem_limit_bytes=VMEM_LIMIT)


def _dot(a, b):
    return jnp.dot(a.astype(BF16), b.astype(BF16), preferred_element_type=F32)


def _dg(a, b, dims):
    return lax.dot_general(a, b, (dims, ((), ())), preferred_element_type=F32)


_NN = ((1,), (0,))
_NT = ((1,), (1,))
_TN = ((0,), (0,))


def _split(a):
    hi = a.astype(BF16)
    lo = (a - hi.astype(F32)).astype(BF16)
    return hi, lo


def _dot3(a, b, dims=_NN):
    ah, al = _split(a)
    bh, bl = _split(b)
    return _dg(ah, bh, dims) + (_dg(ah, bl, dims) + _dg(al, bh, dims))


def _dot1(a, b, dims=_NN):
    return _dg(a.astype(BF16), b.astype(BF16), dims)


def _rms(x, g):
    return x * lax.rsqrt(jnp.mean(x * x, axis=-1, keepdims=True) + NORM_EPS) * g


def _sigmoid(x):
    return 1.0 / (1.0 + jnp.exp(-x))


def _mod_kernel(c_ref, w_ref, b_ref, o_ref):
    c = c_ref[...]
    o_ref[...] = _dot(c * _sigmoid(c), w_ref[...]) + b_ref[...]


def _modulation(c_all, w_mod, b_mod):
    depth, d, n = w_mod.shape
    tn = 1024
    return pl.pallas_call(
        _mod_kernel,
        grid=(depth, n // tn),
        in_specs=[pl.BlockSpec((8, d), lambda l, j: (0, 0)),
                  pl.BlockSpec((None, d, tn), lambda l, j: (l, 0, j)),
                  pl.BlockSpec((None, 1, tn), lambda l, j: (l, 0, j))],
        out_specs=pl.BlockSpec((None, 8, tn), lambda l, j: (l, 0, j)),
        out_shape=jax.ShapeDtypeStruct((depth, 8, n), F32),
        compiler_params=_cparams(("arbitrary", "arbitrary")),
        name="modulation",
    )(c_all, w_mod, b_mod.reshape(depth, 1, n))


def _mod_row(tok0, t_ctx, s_lat):
    return jnp.where(tok0 < t_ctx, 0, 1 + (tok0 - t_ctx) // s_lat)


def _in_kernel(xc_ref, xl_ref, g_ref, sh_ref, sc_ref, wa_ref, wb_ref, o_ref, h_ref, *, n_ctx_tiles, n_head_tiles):
    i = pl.program_id(0)
    j = pl.program_id(1)

    def normalise(x_ref):
        slab = 256
        for r0 in range(0, x_ref.shape[0], slab):
            h = _rms(x_ref[r0:r0 + slab, :], g_ref[...]) * (1.0 + sc_ref[...]) + sh_ref[...]
            h_ref[r0:r0 + slab, :] = h.astype(BF16)

    @pl.when(jnp.logical_and(j == 0, i < n_ctx_tiles))
    def _():
        normalise(xc_ref)

    @pl.when(jnp.logical_and(j == 0, i >= n_ctx_tiles))
    def _():
        normalise(xl_ref)

    @pl.when(j < n_head_tiles)
    def _():
        o_ref[...] = _dg(h_ref[...], wa_ref[...].astype(BF16), _NT)

    @pl.when(j >= n_head_tiles)
    def _():
        o_ref[...] = _dg(h_ref[...], wb_ref[...], _NT)


def _in_proj(xc, xl, g1, mod3, w_in_t, layer, w_tail_t, s_lat):
    t_ctx, d = xc.shape
    ntok = t_ctx + xl.shape[0]
    tm, tn = 1024, 512
    na = COL_R // tn
    n = COL_R + w_tail_t.shape[0]
    nct = t_ctx // tm
    row = lambda i: _mod_row(i * tm, t_ctx, s_lat)
    return pl.pallas_call(
        functools.partial(_in_kernel, n_ctx_tiles=nct, n_head_tiles=na),
        grid=(ntok // tm, n // tn),
        in_specs=[pl.BlockSpec((tm, d), lambda i, j: (jnp.minimum(i, nct - 1), 0)),
                  pl.BlockSpec((tm, d), lambda i, j: (jnp.maximum(i - nct, 0), 0)),
                  pl.BlockSpec((1, d), lambda i, j: (0, 0)),
                  pl.BlockSpec((None, 1, d), lambda i, j: (row(i), 0, 0)),
                  pl.BlockSpec((None, 1, d), lambda i, j: (row(i), 0, 1)),
                  pl.BlockSpec((None, tn, d), lambda i, j: (layer, jnp.minimum(j, na - 1), 0)),
                  pl.BlockSpec((tn, d), lambda i, j: (jnp.maximum(j - na, 0), 0))],
        out_specs=pl.BlockSpec((tm, tn), lambda i, j: (i, j)),
        out_shape=jax.ShapeDtypeStruct((ntok, n), F32),
        scratch_shapes=[pltpu.VMEM((tm, d), BF16)],
        compiler_params=_cparams(("arbitrary", "arbitrary")),
        name="in_proj",
    )(xc, xl, g1, mod3, mod3, w_in_t, w_tail_t)


def _softmax_av(q, kb, vb):
    s = _dg(q.astype(BF16), kb, _NT) * (ATT_HD ** -0.5)
    p = jnp.exp(s - jnp.max(s, axis=-1, keepdims=True))
    l = jnp.sum(p, axis=-1, keepdims=True)
    return jnp.dot(p.astype(BF16), vb, preferred_element_type=F32) / l


def _att_ctx_kernel(q_ref, k_ref, v_ref, qn_ref, kn_ref, o_ref, ko_ref, vo_ref):
    k = _rms(k_ref[...], kn_ref[...])
    ko_ref[...] = k
    vo_ref[...] = v_ref[...]
    kb = k.astype(BF16)
    vb = v_ref[...].astype(BF16)
    for g in range(ATT_GROUPS):
        q = _rms(q_ref[:, g * ATT_HD:(g + 1) * ATT_HD], qn_ref[...])
        o_ref[:, g * ATT_HD:(g + 1) * ATT_HD] = _softmax_av(q, kb, vb)


def _rope(x, cos, sin):
    lane = lax.broadcasted_iota(I32, x.shape, 1)
    first = (lane % (ATT_HD // 2)) < (ATT_HD // 4)
    partner = jnp.where(first, pltpu.roll(x, ATT_HD - ATT_HD // 4, 1), pltpu.roll(x, ATT_HD // 4, 1))
    return x * cos + partner * sin


def _att_lat_kernel(q_ref, k_ref, v_ref, ck_ref, cv_ref, qn_ref, kn_ref, cosq_ref, sinq_ref, cosk_ref, sink_ref,
                    o_ref, kb_ref, vb_ref, *, past):
    @pl.when(pl.program_id(2) == 0)
    def _():
        k = _rope(_rms(k_ref[...], kn_ref[...]), cosk_ref[...], sink_ref[...])
        kb_ref[0:past, :] = ck_ref[...].astype(BF16)
        kb_ref[past:, :] = k.astype(BF16)
        vb_ref[0:past, :] = cv_ref[...].astype(BF16)
        vb_ref[past:, :] = v_ref[...].astype(BF16)

    kb = kb_ref[...]
    vb = vb_ref[...]
    for g in range(ATT_GROUPS):
        q = _rope(_rms(q_ref[:, g * ATT_HD:(g + 1) * ATT_HD], qn_ref[...]), cosq_ref[...], sinq_ref[...])
        o_ref[:, g * ATT_HD:(g + 1) * ATT_HD] = _softmax_av(q, kb, vb)


def _attention(u, qn, kn, cache_k, cache_v, cos, sin, b_ctx, s_ctx, b_lat, s_lat):
    ntok = u.shape[0]
    t_ctx = b_ctx * s_ctx
    gw = ATT_GROUPS * ATT_HD
    kcol = (ATT_KV_HEADS * gw) // ATT_HD
    vcol = kcol + ATT_KV_HEADS
    kv_spec = pl.BlockSpec((s_ctx, ATT_HD), lambda b, h: (b, h))
    att_c, k_ctx, v_ctx = pl.pallas_call(
        _att_ctx_kernel,
        grid=(b_ctx, ATT_KV_HEADS),
        in_specs=[pl.BlockSpec((s_ctx, gw), lambda b, h: (b, h)),
                  pl.BlockSpec((s_ctx, ATT_HD), lambda b, h: (b, kcol + h)),
                  pl.BlockSpec((s_ctx, ATT_HD), lambda b, h: (b, vcol + h)),
                  pl.BlockSpec((1, ATT_HD), lambda b, h: (0, 0)),
                  pl.BlockSpec((1, ATT_HD), lambda b, h: (0, 0))],
        out_specs=[pl.BlockSpec((s_ctx, gw), lambda b, h: (b, h)), kv_spec, kv_spec],
        out_shape=[jax.ShapeDtypeStruct((t_ctx, ATT_KV_HEADS * gw), F32),
                   jax.ShapeDtypeStruct((t_ctx, ATT_KV_HEADS * ATT_HD), F32),
                   jax.ShapeDtypeStruct((t_ctx, ATT_KV_HEADS * ATT_HD), F32)],
        compiler_params=_cparams(("arbitrary", "arbitrary")),
        name="att_ctx",
    )(u, u, u, qn, kn)

    tq = 256
    nqb = s_lat // tq
    past = cache_k.shape[1]
    qrow0 = t_ctx // tq
    krow0 = t_ctx // s_lat
    att_l = pl.pallas_call(
        functools.partial(_att_lat_kernel, past=past),
        grid=(b_lat, ATT_KV_HEADS, nqb),
        in_specs=[pl.BlockSpec((tq, gw), lambda b, h, i: (qrow0 + b * nqb + i, h)),
                  pl.BlockSpec((s_lat, ATT_HD), lambda b, h, i: (krow0 + b, kcol + h)),
                  pl.BlockSpec((s_lat, ATT_HD), lambda b, h, i: (krow0 + b, vcol + h)),
                  pl.BlockSpec((None, past, ATT_HD), lambda b, h, i: (b, 0, h)),
                  pl.BlockSpec((None, past, ATT_HD), lambda b, h, i: (b, 0, h)),
                  pl.BlockSpec((1, ATT_HD), lambda b, h, i: (0, 0)),
                  pl.BlockSpec((1, ATT_HD), lambda b, h, i: (0, 0)),
                  pl.BlockSpec((tq, ATT_HD), lambda b, h, i: (i, 0)),
                  pl.BlockSpec((tq, ATT_HD), lambda b, h, i: (i, 0)),
                  pl.BlockSpec((s_lat, ATT_HD), lambda b, h, i: (0, 0)),
                  pl.BlockSpec((s_lat, ATT_HD), lambda b, h, i: (0, 0))],
        out_specs=pl.BlockSpec((tq, gw), lambda b, h, i: (b * nqb + i, h)),
        out_shape=jax.ShapeDtypeStruct((ntok - t_ctx, ATT_KV_HEADS * gw), F32),
        scratch_shapes=[pltpu.VMEM((past + s_lat, ATT_HD), BF16), pltpu.VMEM((past + s_lat, ATT_HD), BF16)],
        compiler_params=_cparams(("arbitrary", "arbitrary", "arbitrary")),
        name="att_lat",
    )(u, u, u, cache_k, cache_v, qn, kn, cos, sin, cos, sin)
    return att_c, att_l, k_ctx, v_ctx


def _rope_tables(n_tokens):
    pos = jnp.arange(n_tokens)
    row = (pos // GRID_W).astype(F32)
    col = (pos % GRID_W).astype(F32)
    n_freq = ATT_HD // 4
    inv_freq = ROPE_THETA ** (-jnp.arange(n_freq, dtype=F32) / n_freq)
    ang_r = row[:, None] * inv_freq[None, :]
    ang_c = col[:, None] * inv_freq[None, :]
    cos = jnp.concatenate([jnp.cos(ang_r), jnp.cos(ang_r), jnp.cos(ang_c), jnp.cos(ang_c)], axis=-1)
    sin = jnp.concatenate([-jnp.sin(ang_r), jnp.sin(ang_r), -jnp.sin(ang_c), jnp.sin(ang_c)], axis=-1)
    return cos, sin


def _log_sigmoid(x):
    return jnp.minimum(x, 0.0) - jnp.log1p(jnp.exp(-jnp.abs(x)))


def _mlstm_kernel(*refs, zero_init, seq):
    if zero_init:
        (q_ref, k_ref, v_ref, o_ref, gc_ref, gt_ref, br_ref, bc_ref, ng_ref,
         out_ref, co_ref, mo_ref, h_scr, c_scr, m_scr) = refs
    else:
        (q_ref, k_ref, v_ref, o_ref, gc_ref, gt_ref, br_ref, bc_ref, ng_ref, c0_ref, m0_ref,
         out_ref, co_ref, mo_ref, h_scr, c_scr, m_scr) = refs
    nc = seq // CHUNK
    nseq = q_ref.shape[0] // seq
    if zero_init:
        c_scr[...] = jnp.zeros(c_scr.shape, F32)
        m_scr[...] = jnp.full(m_scr.shape, M_INIT, F32)
    else:
        c_scr[...] = c0_ref[...]
        m_scr[...] = m0_ref[...]

    ti = lax.broadcasted_iota(I32, (CHUNK, CHUNK), 0)
    si = lax.broadcasted_iota(I32, (CHUNK, CHUNK), 1)
    ones_col = jnp.where(lax.broadcasted_iota(I32, (CHUNK, M_HD), 1) == 0, 1.0, 0.0).astype(BF16)
    eye_d = jnp.where(lax.broadcasted_iota(I32, (M_HD, M_HD), 0) == lax.broadcasted_iota(I32, (M_HD, M_HD), 1),
                      1.0, 0.0).astype(BF16)

    def cumsum3(tri, x, tri_left):
        hi = x.astype(BF16)
        r1 = x - hi.astype(F32)
        mid = r1.astype(BF16)
        lo = (r1 - mid.astype(F32)).astype(BF16)
        if tri_left:
            return _dg(tri, hi, _NN) + (_dg(tri, mid, _NN) + _dg(tri, lo, _NN))
        return _dg(hi, tri, _NN) + (_dg(mid, tri, _NN) + _dg(lo, tri, _NN))
    gate_i = lambda d, h: 2 * M_HEADS * d + h
    gate_f = lambda d, h: 2 * M_HEADS * d + M_HEADS + h

    def chunk(c, carry):
        ch = []
        for sq, d in [(sq, d) for sq in range(nseq) for d in (0, 1)]:
            cc = c if d == 0 else nc - 1 - c
            rows = pl.ds(pl.multiple_of(sq * seq + cc * CHUNK, CHUNK), CHUNK)
            causal = (si <= ti) if d == 0 else (si >= ti)
            tri = jnp.where(causal, 1.0, 0.0).astype(BF16)
            tri_t = jnp.where((ti <= si) if d == 0 else (ti >= si), 1.0, 0.0).astype(BF16)
            q_all = q_ref[rows, :] * (M_HD ** -0.5)
            k_all = k_ref[rows, :]
            v_all = v_ref[rows, :]
            gcol = gc_ref[rows, :] + br_ref[...]
            grow = gt_ref[sq * nc + cc] + bc_ref[...]
            bcum_c_all = cumsum3(tri, _log_sigmoid(gcol), True)
            bcum_r_all = cumsum3(tri_t, _log_sigmoid(grow), False)
            for h in range(M_HEADS):
                sl = slice(h * M_HD, (h + 1) * M_HD)
                q, k, v = q_all[:, sl], k_all[:, sl], v_all[:, sl]
                ig_c = gcol[:, gate_i(d, h):gate_i(d, h) + 1]
                ig_r = grow[gate_i(d, h):gate_i(d, h) + 1, :]
                bcum_c = bcum_c_all[:, gate_f(d, h):gate_f(d, h) + 1]
                bcum_r = bcum_r_all[gate_f(d, h):gate_f(d, h) + 1, :]
                m_st = m_scr[sq, d, h][:, 0:1]
                dmat = jnp.where(causal, bcum_c - bcum_r + ig_r, -jnp.inf)
                inter = bcum_c + m_st
                m_t = jnp.maximum(inter, jnp.max(dmat, axis=1, keepdims=True))
                b_last = bcum_c[CHUNK - 1:CHUNK, :] if d == 0 else bcum_c[0:1, :]
                g_c = b_last - bcum_c + ig_c
                m_new = jnp.maximum(b_last + m_st, jnp.max(g_c, axis=0, keepdims=True))
                ch.append(dict(sq=sq, d=d, h=h, rows=rows, sl=sl, qb=q.astype(BF16), kb=k.astype(BF16),
                               vx=jnp.concatenate([v.astype(BF16), ones_col], axis=1),
                               c_st=c_scr[sq, d, h], m_t=m_t, m_new=m_new,
                               w_intra=jnp.exp(dmat - m_t), w_inter=jnp.exp(inter - m_t),
                               decay=jnp.exp(b_last + m_st - m_new), kw=(k * jnp.exp(g_c - m_new)).astype(BF16)))
        for x in ch:
            x['s_qk'] = _dg(x['qb'], x['kb'], _NT) * x['w_intra']
        for x in ch:
            x['qc'] = _dg(x['qb'], x['c_st'].astype(BF16), _NN)
        for x in ch:
            x['kw_t'] = _dg(eye_d, x['kw'], _NT).astype(BF16)
        for x in ch:
            x['sv'] = _dg(x['s_qk'].astype(BF16), x['vx'], _NN)
        for x in ch:
            x['upd'] = _dg(x['kw_t'], x['vx'], _NN)
        for x in ch:
            sq, d, h = x['sq'], x['d'], x['h']
            tot = x['w_inter'] * x['qc'] + x['sv']
            den = tot[:, M_HD:M_HD + 1]
            h_scr[d, x['rows'], x['sl']] = tot[:, :M_HD] / jnp.maximum(jnp.abs(den), jnp.exp(-x['m_t']))
            c_scr[sq, d, h] = x['decay'] * x['c_st'] + x['upd']
            m_scr[sq, d, h] = jnp.broadcast_to(x['m_new'], (1, M_HD))
        return carry

    lax.fori_loop(0, nc, chunk, 0)
    hsum = h_scr[0] + h_scr[1]
    hn = jnp.concatenate(
        [hsum[:, h * M_HD:(h + 1) * M_HD]
         * lax.rsqrt(jnp.mean(hsum[:, h * M_HD:(h + 1) * M_HD] ** 2, axis=-1, keepdims=True) + NORM_EPS)
         for h in range(M_HEADS)], axis=1)
    out_ref[...] = hn * ng_ref[...] * _sigmoid(o_ref[...])
    co_ref[...] = c_scr[...]
    mo_ref[...] = m_scr[...]


def _mlstm(u, gcol, gt, bias, norm_g, states, row0, batch, seq, seqs_per_step):
    zero_init = states is None
    nq = seqs_per_step
    rows = nq * seq
    assert batch % nq == 0 and row0 % nq == 0
    rb = row0 // nq
    width = M_HEADS * M_HD
    qc = COL_M // width
    ngate = 4 * M_HEADS
    blk = lambda off: pl.BlockSpec((rows, width), lambda b: (rb + b, off))
    st_c = pl.BlockSpec((nq, 2, M_HEADS, M_HD, 2 * M_HD), lambda b: (b, 0, 0, 0, 0))
    st_v = pl.BlockSpec((nq, 2, M_HEADS, 1, M_HD), lambda b: (b, 0, 0, 0, 0))
    in_specs = [blk(qc), blk(qc + 1), blk(qc + 2), blk(qc + 3),
                pl.BlockSpec((rows, ngate), lambda b: (rb + b, 0)),
                pl.BlockSpec((rows // CHUNK, ngate, CHUNK), lambda b: (rb + b, 0, 0)),
                pl.BlockSpec((1, ngate), lambda b: (0, 0)),
                pl.BlockSpec((ngate, 1), lambda b: (0, 0)),
                pl.BlockSpec((1, width), lambda b: (0, 0))]
    args = [u, u, u, u, gcol, gt, bias.reshape(1, ngate), bias.reshape(ngate, 1), norm_g]
    if not zero_init:
        in_specs += [st_c, st_v]
        args += list(states)
    return pl.pallas_call(
        functools.partial(_mlstm_kernel, zero_init=zero_init, seq=seq),
        grid=(batch // nq,),
        in_specs=in_specs,
        out_specs=[pl.BlockSpec((rows, width), lambda b: (b, 0)), st_c, st_v],
        out_shape=[jax.ShapeDtypeStruct((batch * seq, width), F32),
                   jax.ShapeDtypeStruct((batch, 2, M_HEADS, M_HD, 2 * M_HD), F32),
                   jax.ShapeDtypeStruct((batch, 2, M_HEADS, 1, M_HD), F32)],
        scratch_shapes=[pltpu.VMEM((2, rows, width), F32), pltpu.VMEM((nq, 2, M_HEADS, M_HD, 2 * M_HD), F32),
                        pltpu.VMEM((nq, 2, M_HEADS, 1, M_HD), F32)],
        compiler_params=_cparams(("arbitrary",)),
        name="mlstm_ctx" if zero_init else "mlstm_lat",
    )(*args)


def _rwkv_kernel(*refs, zero_init, seq):
    if zero_init:
        (r_ref, k_ref, v_ref, xl_ref, xg_ref, mur_ref, muk_ref, muv_ref, mul_ref, mug_ref,
         w0_ref, wup_ref, a0_ref, aup_ref, gup_ref, kk_ref, ka_ref, rk_ref, lng_ref, lnb_ref,
         out_ref, so_ref, r_scr, v_scr, kk_scr, g_scr, bonus_scr, lw_scr, kd_scr, b_scr, y_scr, s_scr) = refs
    else:
        (r_ref, k_ref, v_ref, xl_ref, xg_ref, mur_ref, muk_ref, muv_ref, mul_ref, mug_ref,
         w0_ref, wup_ref, a0_ref, aup_ref, gup_ref, kk_ref, ka_ref, rk_ref, lng_ref, lnb_ref, s0_ref,
         out_ref, so_ref, r_scr, v_scr, kk_scr, g_scr, bonus_scr, lw_scr, kd_scr, b_scr, y_scr, s_scr) = refs
    nc = seq // CHUNK
    hd = R_HD
    nh = r_ref.shape[1] // hd
    rows_blk = r_ref.shape[0]
    nseq = rows_blk // seq

    def tshift(x_ref, mu_ref):
        x = x_ref[...]
        row = lax.broadcasted_iota(I32, x.shape, 0) % seq
        prev = jnp.where(row == 0, 0.0, pltpu.roll(x, 1, 0))
        nxt = jnp.where(row == seq - 1, 0.0, pltpu.roll(x, rows_blk - 1, 0))
        return x + mu_ref[...] * (0.5 * (prev + nxt) - x)

    r = tshift(r_ref, mur_ref)
    k = tshift(k_ref, muk_ref)
    v = tshift(v_ref, muv_ref)
    xl = tshift(xl_ref, mul_ref)
    xg = tshift(xg_ref, mug_ref)
    g = _dot(_sigmoid(xg), gup_ref[...])
    wlanes = nh * hd
    same_head = (lax.broadcasted_iota(I32, (wlanes, wlanes), 0) // hd
                 == lax.broadcasted_iota(I32, (wlanes, wlanes), 1) // hd)
    head_ones = jnp.where(same_head, 1.0, 0.0).astype(BF16)

    def head_sum(x):
        hi = x.astype(BF16)
        r1 = x - hi.astype(F32)
        mid = r1.astype(BF16)
        lo = (r1 - mid.astype(F32)).astype(BF16)
        return _dg(hi, head_ones, _NN) + (_dg(mid, head_ones, _NN) + _dg(lo, head_ones, _NN))

    kkp = k * kk_ref[...]
    kk = kkp * lax.rsqrt(jnp.maximum(head_sum(kkp * kkp), 1e-24))
    tw = jnp.tanh(xl[:, 0:hd])
    xa = xl[:, hd:2 * hd]
    r_scr[...] = r
    v_scr[...] = v
    kk_scr[...] = kk
    g_scr[...] = g
    bonus_scr[...] = head_sum(r * k * rk_ref[...]) * v
    for d in (0, 1):
        lw_scr[d] = -math.exp(-0.5) * _sigmoid(w0_ref[d] + _dot(tw, wup_ref[d]))
        ad = _sigmoid(a0_ref[d] + _dot(xa, aup_ref[d]))
        kd_scr[d] = k * (1.0 + (ad - 1.0) * ka_ref[...])
        b_scr[d] = kk * ad
    npair = nh // 2
    pw = 2 * hd
    zero_blk = jnp.zeros((hd, hd), F32)
    for q in range(nseq):
        for d in (0, 1):
            for p in range(npair):
                if zero_init:
                    s_scr[q, d, p] = jnp.zeros((pw, pw), F32)
                else:
                    s_scr[q, d, p] = jnp.concatenate(
                        [jnp.concatenate([s0_ref[q, d, 2 * p], zero_blk], axis=1),
                         jnp.concatenate([zero_blk, s0_ref[q, d, 2 * p + 1]], axis=1)], axis=0)
    y_scr[...] = jnp.zeros(y_scr.shape, F32)

    ti = lax.broadcasted_iota(I32, (CHUNK, pw), 0)
    si = lax.broadcasted_iota(I32, (CHUNK, pw), 1) % CHUNK
    eye2 = (ti == si).astype(F32)
    tri_i = lax.broadcasted_iota(I32, (CHUNK, CHUNK), 0)
    tri_j = lax.broadcasted_iota(I32, (CHUNK, CHUNK), 1)
    first = lax.broadcasted_iota(I32, (1, pw), 1) < hd
    diag_blk = ((lax.broadcasted_iota(I32, (pw, pw), 0) < hd) == (lax.broadcasted_iota(I32, (pw, pw), 1) < hd))

    def bd(x):
        xb = x.astype(BF16)
        zero = jnp.zeros_like(xb)
        return jnp.concatenate([jnp.where(first, xb, zero), jnp.where(first, zero, xb)], axis=0)

    def chunk(c, carry):
        chains = []
        for q, d in [(q, d) for q in range(nseq) for d in (0, 1)]:
            cc = c if d == 0 else nc - 1 - c
            rows = pl.ds(pl.multiple_of(q * seq + cc * CHUNK, CHUNK), CHUNK)
            incl = (si <= ti) if d == 0 else (si >= ti)
            strict = (si < ti) if d == 0 else (si > ti)
            lw = lw_scr[d, rows, :]
            lw_hi, lw_lo = _split(lw)
            tri = jnp.where((tri_j <= tri_i) if d == 0 else (tri_j >= tri_i), 1.0, 0.0).astype(BF16)
            lc = _dg(tri, lw_hi, _NN) + _dg(tri, lw_lo, _NN)
            l_last = lc[CHUNK - 1:CHUNK, :] if d == 0 else lc[0:1, :]
            e_neg = jnp.exp(-lc)
            e_end = jnp.exp(l_last - lc)
            vc = v_scr[rows, :]
            kdc = kd_scr[d, rows, :]
            bc = b_scr[d, rows, :]
            rt = r_scr[rows, :] * jnp.exp(lc)
            kkt = kk_scr[rows, :] * jnp.exp(lc - lw)
            kh = kdc * e_neg
            bh = bc * e_neg
            kbar = kdc * e_end
            bbar = bc * e_end
            w_end = jnp.exp(l_last)
            for p in range(npair):
                sl = slice(p * pw, (p + 1) * pw)
                chains.append(dict(
                    q=q, d=d, pair=p, sl=sl, rows=rows, incl=incl, strict=strict,
                    lhs=jnp.concatenate([kkt[:, sl], rt[:, sl]], axis=0).astype(BF16),
                    rhs=jnp.concatenate([bd(bh[:, sl]), bd(kh[:, sl])], axis=0),
                    end=jnp.concatenate([kbar[:, sl], bbar[:, sl]], axis=0),
                    v=vc[:, sl], w_end=w_end[:, sl], s0=s_scr[q, d, p]))
        for ch in chains:
            ch['ab'] = _dg(ch['lhs'], ch['rhs'], _NT)
        for ch in chains:
            ch['proj'] = _dg(ch['lhs'], ch['s0'].astype(BF16), _NT)
        for ch in chains:
            ab = ch['ab']
            ch['a_kb'] = jnp.where(ch['strict'], ab[:CHUNK, :pw], 0.0)
            ch['b_rb'] = jnp.where(ch['incl'], ab[CHUNK:, :pw], 0.0)
            ch['akk_brk'] = jnp.concatenate([jnp.where(ch['strict'], ab[:CHUNK, pw:], 0.0),
                                             jnp.where(ch['incl'], ab[CHUNK:, pw:], 0.0)], axis=0)
        for ch in chains:
            ch['p'] = _dg(ch['a_kb'].astype(BF16), bd(ch['a_kb']), _NN)
        for ch in chains:
            ch['abv'] = _dg(ch['akk_brk'].astype(BF16), bd(ch['v']), _NN)
        for ch in chains:
            inv = eye2 - ch['a_kb']
            ch['inv'] = inv + _dg(inv.astype(BF16), bd(ch['p']), _NN)
        span = 4
        while span < CHUNK:
            for ch in chains:
                ch['p'] = _dg(ch['p'].astype(BF16), bd(ch['p']), _NN)
            for ch in chains:
                ch['inv'] = ch['inv'] + _dg(ch['inv'].astype(BF16), bd(ch['p']), _NN)
            span *= 2
        for ch in chains:
            ch['u'] = _dg(ch['inv'].astype(BF16), bd(ch['proj'][:CHUNK] + ch['abv'][:CHUNK]), _NN)
        for ch in chains:
            y = ch['proj'][CHUNK:] + ch['abv'][CHUNK:] - _dg(ch['b_rb'].astype(BF16), bd(ch['u']), _NN)
            y_scr[ch['rows'], ch['sl']] += y
        for ch in chains:
            upd = _dot3(jnp.concatenate([ch['v'], -ch['u']], axis=0), ch['end'], _TN)
            s_scr[ch['q'], ch['d'], ch['pair']] = ch['s0'] * ch['w_end'] + jnp.where(diag_blk, upd, 0.0)
        return carry

    lax.fori_loop(0, nc, chunk, 0)

    y = y_scr[...]
    dev = y - head_sum(y) * (1.0 / hd)
    yn = dev * lax.rsqrt(head_sum(dev * dev) * (1.0 / hd) + GN_EPS)
    out_ref[...] = (yn * lng_ref[...] + lnb_ref[...] + bonus_scr[...]) * g_scr[...]
    for q in range(nseq):
        for d in (0, 1):
            for p in range(npair):
                s = s_scr[q, d, p]
                so_ref[q, d, 2 * p] = s[:hd, :hd]
                so_ref[q, d, 2 * p + 1] = s[hd:, hd:]


def _rwkv(u, p, state, row0, batch, seq, heads_per_step, seqs_per_step):
    zero_init = state is None
    nh = heads_per_step
    nq = seqs_per_step
    rows = nq * seq
    assert batch % nq == 0 and row0 % nq == 0
    wd = nh * R_HD
    nsteps = R_HEADS // nh
    width = R_HEADS * R_HD
    rc = COL_R // wd
    sec = width // wd
    lc = (COL_R + 3 * width) // LANES
    ublk = lambda off: pl.BlockSpec((rows, wd), lambda b, h: (row0 // nq + b, rc + off + h))
    ufix = lambda blk: pl.BlockSpec((rows, LANES), lambda b, h: (row0 // nq + b, blk))
    mblk = lambda off: pl.BlockSpec((1, wd), lambda b, h: (0, off + h))
    mfix = lambda blk: pl.BlockSpec((1, LANES), lambda b, h: (0, blk))
    vec = pl.BlockSpec((1, wd), lambda b, h: (0, h))
    in_specs = [ublk(0), ublk(sec), ublk(2 * sec), ufix(lc), ufix(lc + 1),
                mblk(0), mblk(sec), mblk(2 * sec), mfix(lc - COL_R // LANES), mfix(lc - COL_R // LANES + 1),
                pl.BlockSpec((2, 1, wd), lambda b, h: (0, 0, h)),
                pl.BlockSpec((2, R_HD, wd), lambda b, h: (0, 0, h)),
                pl.BlockSpec((2, 1, wd), lambda b, h: (0, 0, h)),
                pl.BlockSpec((2, R_HD, wd), lambda b, h: (0, 0, h)),
                pl.BlockSpec((LANES, wd), lambda b, h: (0, h)),
                vec, vec, vec, vec, vec]
    args = [u, u, u, u, u, p['mu'], p['mu'], p['mu'], p['mu'], p['mu'],
            p['w0'], p['w_up'], p['a0'], p['a_up'], p['g_up'], p['k_k'], p['k_a'], p['r_k'], p['ln_g'], p['ln_b']]
    st_spec = pl.BlockSpec((nq, 2, nh, R_HD, R_HD), lambda b, h: (b, 0, h, 0, 0))
    if not zero_init:
        in_specs.append(st_spec)
        args.append(state)
    big = lambda n: pltpu.VMEM((n, rows, wd), F32)
    return pl.pallas_call(
        functools.partial(_rwkv_kernel, zero_init=zero_init, seq=seq),
        grid=(batch // nq, nsteps),
        in_specs=in_specs,
        out_specs=[pl.BlockSpec((rows, wd), lambda b, h: (b, h)), st_spec],
        out_shape=[jax.ShapeDtypeStruct((batch * seq, width), F32),
                   jax.ShapeDtypeStruct((batch, 2, R_HEADS, R_HD, R_HD), F32)],
        scratch_shapes=[pltpu.VMEM((rows, wd), F32)] * 5
                       + [big(2), big(2), big(2), pltpu.VMEM((rows, wd), F32),
                          pltpu.VMEM((nq, 2, nh // 2, 2 * R_HD, 2 * R_HD), F32)],
        compiler_params=_cparams(("arbitrary", "arbitrary")),
        name="rwkv_ctx" if zero_init else "rwkv_lat",
    )(*args)


def _top2_sum(a, b, c, d):
    m1, n1 = jnp.maximum(a, b), jnp.minimum(a, b)
    m2, n2 = jnp.maximum(c, d), jnp.minimum(c, d)
    return jnp.maximum(m1, m2) + jnp.maximum(jnp.minimum(m1, m2), jnp.maximum(n1, n2))


def _first_argmax(vals):
    best = functools.reduce(jnp.maximum, vals)
    idx = jnp.full(best.shape, len(vals) - 1, I32)
    for j in range(len(vals) - 2, -1, -1):
        idx = jnp.where(vals[j] == best, j, idx)
    return best, idx


def _out_kernel(attc_ref, attl_ref, mc_ref, ml_ref, rc_ref, rl_ref, xc_ref, xl_ref, w_ref, g1_ref, sh2_ref, sc2_ref,
                n2_ref, rw_ref, rb_ref, x1_ref, h2_ref, route_ref, cnt_ref, cnt_scr, *, n_ctx_tiles):
    @pl.when(pl.program_id(0) == 0)
    def _():
        cnt_scr[...] = jnp.zeros(cnt_scr.shape, F32)

    is_ctx = pl.program_id(0) < n_ctx_tiles
    both = lambda c_ref, l_ref: jnp.where(is_ctx, c_ref[...], l_ref[...])
    na = attc_ref.shape[1]
    nm = mc_ref.shape[1]
    mix = (jnp.dot(both(attc_ref, attl_ref).astype(BF16), w_ref[0:na, :], preferred_element_type=F32)
           + jnp.dot(both(mc_ref, ml_ref).astype(BF16), w_ref[na:na + nm, :], preferred_element_type=F32)
           + jnp.dot(both(rc_ref, rl_ref).astype(BF16), w_ref[na + nm:, :], preferred_element_type=F32))
    x1 = both(xc_ref, xl_ref) + g1_ref[...] * mix
    x1_ref[...] = x1
    h2 = _rms(x1, n2_ref[...]) * (1.0 + sc2_ref[...]) + sh2_ref[...]
    h2_ref[...] = h2
    logits = _dot1(rw_ref[...], h2, _NT)
    s = _sigmoid(logits)
    ssel = s + rb_ref[...]
    srow = [s[e:e + 1, :] for e in range(N_EXPERTS)]
    brow = [ssel[e:e + 1, :] for e in range(N_EXPERTS)]
    gscore = [_top2_sum(*brow[EXPERTS_PER_GROUP * g:EXPERTS_PER_GROUP * (g + 1)]) for g in range(N_EXPERT_GROUPS)]
    _, gidx = _first_argmax(gscore)
    pick = lambda rows, j: functools.reduce(
        lambda acc, g: jnp.where(gidx == g, rows[EXPERTS_PER_GROUP * g + j], acc),
        range(N_EXPERT_GROUPS - 2, -1, -1), rows[EXPERTS_PER_GROUP * (N_EXPERT_GROUPS - 1) + j])
    ing = [pick(brow, j) for j in range(EXPERTS_PER_GROUP)]
    sin_ = [pick(srow, j) for j in range(EXPERTS_PER_GROUP)]
    _, l1 = _first_argmax(ing)
    _, l2 = _first_argmax([jnp.where(l1 == j, -jnp.inf, ing[j]) for j in range(EXPERTS_PER_GROUP)])
    sel = lambda l: functools.reduce(lambda acc, j: jnp.where(l == j, sin_[j], acc),
                                     range(EXPERTS_PER_GROUP - 2, -1, -1), sin_[EXPERTS_PER_GROUP - 1])
    w1, w2 = sel(l1), sel(l2)
    tot = w1 + w2
    e1 = gidx * EXPERTS_PER_GROUP + l1
    e2 = gidx * EXPERTS_PER_GROUP + l2
    tm = e1.shape[1]
    eid = lax.broadcasted_iota(I32, (N_EXPERTS, tm), 0)
    oh1 = eid == e1
    oh2 = eid == e2
    picked = jnp.where(jnp.logical_or(oh1, oh2), 1.0, 0.0)
    earlier = jnp.where(lax.broadcasted_iota(I32, (tm, tm), 0) < lax.broadcasted_iota(I32, (tm, tm), 1), 1.0, 0.0)
    rank = cnt_scr[:, 0:1] + jnp.dot(picked.astype(BF16), earlier.astype(BF16), preferred_element_type=F32)
    pos1 = jnp.sum(jnp.where(oh1, rank, 0.0), axis=0, keepdims=True)
    pos2 = jnp.sum(jnp.where(oh2, rank, 0.0), axis=0, keepdims=True)
    cnt = cnt_scr[...] + jnp.sum(picked, axis=1, keepdims=True)
    cnt_scr[...] = cnt
    cnt_ref[...] = cnt
    zero = jnp.zeros_like(w1)
    route_ref[...] = jnp.concatenate([e1.astype(F32), e2.astype(F32), w1 / tot, w2 / tot, pos1, pos2, zero, zero],
                                     axis=0)


def _out_proj(att, m_out, r_out, x, w_out, mod3, n2g, rw_t, rb, s_lat):
    t_ctx, d = x[0].shape
    ntok = t_ctx + x[1].shape[0]
    tm = 256
    nct = t_ctx // tm
    row = lambda i: _mod_row(i * tm, t_ctx, s_lat)
    modblk = lambda j: pl.BlockSpec((None, 1, d), lambda i: (row(i), 0, j))
    pair = lambda a: [pl.BlockSpec((tm, a[0].shape[1]), lambda i: (jnp.minimum(i, nct - 1), 0)),
                      pl.BlockSpec((tm, a[1].shape[1]), lambda i: (jnp.maximum(i - nct, 0), 0))]
    return pl.pallas_call(
        functools.partial(_out_kernel, n_ctx_tiles=nct),
        grid=(ntok // tm,),
        in_specs=pair(att) + pair(m_out) + pair(r_out) + pair(x) + [
                  pl.BlockSpec(w_out.shape, lambda i: (0, 0)),
                  modblk(2), modblk(3), modblk(4),
                  pl.BlockSpec((1, d), lambda i: (0, 0)),
                  pl.BlockSpec(rw_t.shape, lambda i: (0, 0)),
                  pl.BlockSpec(rb.shape, lambda i: (0, 0))],
        out_specs=[pl.BlockSpec((tm, d), lambda i: (i, 0)),
                   pl.BlockSpec((tm, d), lambda i: (i, 0)),
                   pl.BlockSpec((8, tm), lambda i: (0, i)),
                   pl.BlockSpec((N_EXPERTS, LANES), lambda i: (0, 0))],
        out_shape=[jax.ShapeDtypeStruct((ntok, d), F32), jax.ShapeDtypeStruct((ntok, d), F32),
                   jax.ShapeDtypeStruct((8, ntok), F32), jax.ShapeDtypeStruct((N_EXPERTS, LANES), F32)],
        scratch_shapes=[pltpu.VMEM((N_EXPERTS, LANES), F32)],
        compiler_params=_cparams(("arbitrary",)),
        name="out_proj",
    )(*att, *m_out, *r_out, *x, w_out, mod3, mod3, mod3, n2g, rw_t, rb)


def _row_gather(src_hbm, idx_ref, base, dst, sem, n, unrolled, lo=0):
    def start(j, priority):
        pltpu.make_async_copy(src_hbm.at[pl.ds(idx_ref[base + j], 1), :], dst.at[pl.ds(j, 1), :],
                              sem).start(priority=priority)

    if unrolled:
        for j in range(lo, n):
            start(j, j % 2)
    else:
        def body(j, c):
            start(j, 0)
            return c
        lax.fori_loop(lo, n, body, 0)


def _expert_kernel(be_ref, tok_ref, nused_ref, h_hbm, w1_ref, w3_ref, w2_ref, y_ref, xbuf, w1b, w3b, w2b, sem):
    i = pl.program_id(0)
    n_used = nused_ref[0]
    wait = lambda s: pltpu.make_async_copy(h_hbm.at[pl.ds(0, EXPERT_ROWS), :], xbuf.at[s], sem.at[s]).wait()

    @pl.when(i < n_used)
    def _():
        slot = i % 2

        @pl.when(i == 0)
        def _():
            _row_gather(h_hbm, tok_ref, 0, xbuf.at[0], sem.at[0], EXPERT_ROWS, unrolled=False)

        @pl.when(jnp.logical_or(i == 0, be_ref[i] != be_ref[jnp.maximum(i - 1, 0)]))
        def _():
            w1b[...] = w1_ref[...].astype(BF16)
            w3b[...] = w3_ref[...].astype(BF16)
            w2b[...] = w2_ref[...].astype(BF16)

        wait(slot)
        de = w1b.shape[1]
        half = de // 2
        groups = 6
        per = EXPERT_ROWS // groups

        def next_rows(gi):
            hi = EXPERT_ROWS if gi == groups - 1 else (gi + 1) * per
            _row_gather(h_hbm, tok_ref, (i + 1) * EXPERT_ROWS, xbuf.at[1 - slot], sem.at[1 - slot], hi,
                        unrolled=True, lo=gi * per)

        xb = xbuf[slot].astype(BF16)
        hmid = []
        for n in range(2):
            cols = slice(n * half, (n + 1) * half)
            next_rows(2 * n)
            a = jnp.dot(xb, w1b[:, cols], preferred_element_type=F32)
            next_rows(2 * n + 1)
            b = jnp.dot(xb, w3b[:, cols], preferred_element_type=F32)
            hmid.append(((a * _sigmoid(a)) * b).astype(BF16))
        next_rows(4)
        y = jnp.dot(hmid[0], w2b[0:half, :], preferred_element_type=F32)
        next_rows(5)
        y_ref[...] = y + jnp.dot(hmid[1], w2b[half:, :], preferred_element_type=F32)

        @pl.when(i == n_used - 1)
        def _():
            wait(1 - slot)

    @pl.when(i >= n_used)
    def _():
        y_ref[...] = jnp.zeros(y_ref.shape, F32)


def _experts(h2, block_e, row_tok, n_used, w1, w3, w2, layer):
    ntok, d = h2.shape
    de = w1.shape[3]
    n_rows = row_tok.shape[0] - EXPERT_ROWS
    nb = n_rows // EXPERT_ROWS
    return pl.pallas_call(
        _expert_kernel,
        grid_spec=pltpu.PrefetchScalarGridSpec(
            num_scalar_prefetch=3,
            grid=(nb,),
            in_specs=[pl.BlockSpec(memory_space=pl.ANY),
                      pl.BlockSpec((None, None, d, de), lambda i, be, tok, nu: (layer, be[i], 0, 0)),
                      pl.BlockSpec((None, None, d, de), lambda i, be, tok, nu: (layer, be[i], 0, 0)),
                      pl.BlockSpec((None, None, de, d), lambda i, be, tok, nu: (layer, be[i], 0, 0))],
            out_specs=pl.BlockSpec((EXPERT_ROWS, d), lambda i, be, tok, nu: (i, 0)),
            scratch_shapes=[pltpu.VMEM((2, EXPERT_ROWS, d), F32), pltpu.VMEM((d, de), BF16),
                            pltpu.VMEM((d, de), BF16), pltpu.VMEM((de, d), BF16), pltpu.SemaphoreType.DMA((2,))]),
        out_shape=jax.ShapeDtypeStruct((n_rows, d), F32),
        compiler_params=_cparams(("arbitrary",)),
        name="experts",
    )(block_e, row_tok, n_used, h2, w1, w3, w2)


def _combine_kernel(d1_ref, d2_ref, y_hbm, x1_ref, g2_ref, gate_ref, oc_ref, ol_ref, ybuf, sem, *, n_ctx_tiles):
    tm = x1_ref.shape[0]
    i = pl.program_id(0)
    slot = i % 2

    def gather(tile, s, unrolled):
        _row_gather(y_hbm, d1_ref, tile * tm, ybuf.at[s, 0], sem.at[s], tm, unrolled)
        _row_gather(y_hbm, d2_ref, tile * tm, ybuf.at[s, 1], sem.at[s], tm, unrolled)

    def wait(s):
        pltpu.make_async_copy(y_hbm.at[pl.ds(0, tm), :], ybuf.at[s, 0], sem.at[s]).wait()
        pltpu.make_async_copy(y_hbm.at[pl.ds(0, tm), :], ybuf.at[s, 1], sem.at[s]).wait()

    @pl.when(i == 0)
    def _():
        gather(0, 0, False)

    wait(slot)
    gather(i + 1, 1 - slot, True)
    gate = gate_ref[...]
    moe = ybuf[slot, 0] * gate[:, 0:1] + ybuf[slot, 1] * gate[:, 1:2]
    new_x = x1_ref[...] + g2_ref[...] * moe

    @pl.when(i < n_ctx_tiles)
    def _():
        oc_ref[...] = new_x

    @pl.when(i >= n_ctx_tiles)
    def _():
        ol_ref[...] = new_x

    @pl.when(i == pl.num_programs(0) - 1)
    def _():
        wait(1 - slot)


def _combine(y, x1, mod3, gate, dest1, dest2, t_ctx, s_lat):
    ntok, d = x1.shape
    tm = 256
    nct = t_ctx // tm
    row = lambda i: _mod_row(i * tm, t_ctx, s_lat)
    return pl.pallas_call(
        functools.partial(_combine_kernel, n_ctx_tiles=nct),
        grid_spec=pltpu.PrefetchScalarGridSpec(
            num_scalar_prefetch=2,
            grid=(ntok // tm,),
            in_specs=[pl.BlockSpec(memory_space=pl.ANY),
                      pl.BlockSpec((tm, d), lambda i, a, b: (i, 0)),
                      pl.BlockSpec((None, 1, d), lambda i, a, b: (row(i), 0, 5)),
                      pl.BlockSpec((tm, 2), lambda i, a, b: (i, 0))],
            out_specs=[pl.BlockSpec((tm, d), lambda i, a, b: (jnp.minimum(i, nct - 1), 0)),
                       pl.BlockSpec((tm, d), lambda i, a, b: (jnp.maximum(i - nct, 0), 0))],
            scratch_shapes=[pltpu.VMEM((2, 2, tm, d), F32), pltpu.SemaphoreType.DMA((2,))]),
        out_shape=[jax.ShapeDtypeStruct((t_ctx, d), F32), jax.ShapeDtypeStruct((ntok - t_ctx, d), F32)],
        compiler_params=_cparams(("arbitrary",)),
        name="combine",
    )(dest1, dest2, y, x1, mod3, gate)


def _dispatch(route, counts, tile):
    ntok = route.shape[1]
    e = route[0:2].astype(I32)
    pos = route[4:6].astype(I32)
    gate = route[2:4].T
    counts = counts[:, 0].astype(I32)
    padded = (counts + EXPERT_ROWS - 1) // EXPERT_ROWS * EXPERT_ROWS
    pad_end = jnp.cumsum(padded)
    pad_start = pad_end - padded
    onehot = (e[:, :, None] == jnp.arange(N_EXPERTS, dtype=I32)).astype(I32)
    dest = jnp.sum(onehot * pad_start, axis=-1) + pos
    n_rows = -(-(2 * ntok) // EXPERT_ROWS) * EXPERT_ROWS + N_EXPERTS * EXPERT_ROWS
    nb = n_rows // EXPERT_ROWS
    tok = jnp.broadcast_to(jnp.arange(ntok, dtype=I32)[None, :], (2, ntok))
    row_tok = jnp.zeros((n_rows + EXPERT_ROWS,), I32).at[dest.reshape(-1)].set(tok.reshape(-1), unique_indices=True)
    blk_start = jnp.arange(nb, dtype=I32) * EXPERT_ROWS
    block_e = jnp.minimum(jnp.sum((pad_end[None, :] <= blk_start[:, None]).astype(I32), axis=1), N_EXPERTS - 1)
    n_used = pad_end[-1:] // EXPERT_ROWS
    dest = jnp.pad(dest, ((0, 0), (0, tile)))
    return row_tok, block_e, n_used, gate, dest[0], dest[1]


def kernel(x_prompt, x_sample, c, cache_attn_k, cache_attn_v, state_mlstm_C, state_mlstm_n, state_mlstm_m, state_rwkv, c_ctx, norm1_g, norm2_g, w_mod, b_mod, w_in, w_out, attn_q_norm, attn_k_norm, mlstm_i_bias, mlstm_f_bias, mlstm_norm_g, rwkv_mu, rwkv_w0, rwkv_w_up, rwkv_a0, rwkv_a_up, rwkv_g_up, rwkv_k_k, rwkv_k_a, rwkv_r_k, rwkv_ln_g, rwkv_ln_b, router_w, router_b, exp_w1, exp_w3, exp_w2):
    b_ctx, s_ctx, d = x_prompt.shape
    b_lat, s_lat, _ = x_sample.shape
    depth = w_in.shape[0]
    t_ctx = b_ctx * s_ctx
    ntok = t_ctx + b_lat * s_lat
    assert b_lat + 1 <= 8 and s_lat % 1024 == 0 and t_ctx % 1024 == 0 and t_ctx % s_lat == 0
    past = cache_attn_k.shape[2]

    x = (x_prompt.reshape(t_ctx, d), x_sample.reshape(b_lat * s_lat, d))
    c_all = jnp.zeros((8, d), F32).at[0].set(c_ctx).at[1:1 + b_lat].set(c)
    mod = _modulation(c_all, w_mod, b_mod)
    cos, sin = _rope_tables(s_lat)
    rw_t = router_w.T
    rb = router_b.reshape(N_EXPERTS, 1)
    r_width = R_HEADS * R_HD
    n_in = w_in.shape[2]
    gate_hi = COL_R + 4 * M_HEADS
    w_in_t = jnp.swapaxes(w_in, 1, 2)
    w_tail_t = jnp.concatenate([w_in_t[:, gate_hi:], w_in_t[:, COL_R:gate_hi],
                                jnp.zeros((depth, N_IN_PAD - n_in, d), F32)], axis=1).astype(BF16)

    ks, vs, cs, ns, ms, rs = [], [], [], [], [], []
    for l in range(depth):
        mod3 = mod[l].reshape(8, 1, 6 * d)
        u = _in_proj(x[0], x[1], norm1_g[l][None], mod3, w_in_t, l, w_tail_t[l], s_lat)

        ck = cache_attn_k[:, l].reshape(b_lat, past, ATT_KV_HEADS * ATT_HD)
        cv = cache_attn_v[:, l].reshape(b_lat, past, ATT_KV_HEADS * ATT_HD)
        att_c, att_l, k_ctx, v_ctx = _attention(u, attn_q_norm[l][None], attn_k_norm[l][None], ck, cv, cos, sin,
                                                b_ctx, s_ctx, b_lat, s_lat)
        ks.append(k_ctx.reshape(b_ctx, s_ctx, ATT_KV_HEADS, ATT_HD))
        vs.append(v_ctx.reshape(b_ctx, s_ctx, ATT_KV_HEADS, ATT_HD))

        gcol = u[:, COL_GATE:COL_GATE + 4 * M_HEADS]
        gt = gcol.reshape(ntok // CHUNK, CHUNK, 4 * M_HEADS).transpose(0, 2, 1)
        bias = jnp.stack([mlstm_i_bias[l], mlstm_f_bias[l]], axis=1)
        ng = mlstm_norm_g[l][None]
        m_c_out, cx_c, m_c = _mlstm(u, gcol, gt, bias, ng, None, 0, b_ctx, s_ctx, MLSTM_SEQS_PER_STEP_CTX)
        n_col = jnp.pad(state_mlstm_n[:, l][..., None], ((0, 0),) * 4 + ((0, M_HD - 1),))
        lat_states = (jnp.concatenate([state_mlstm_C[:, l], n_col], axis=-1),
                      jnp.broadcast_to(state_mlstm_m[:, l][..., None, None], (b_lat, 2, M_HEADS, 1, M_HD)))
        m_l_out, _, _ = _mlstm(u, gcol, gt, bias, ng, lat_states, t_ctx // s_lat, b_lat, s_lat, 1)
        cs.append(cx_c[..., :M_HD])
        ns.append(cx_c[..., M_HD])
        ms.append(m_c[:, :, :, 0, 0])

        rp = dict(mu=rwkv_mu[l][None], w0=rwkv_w0[l].reshape(2, 1, r_width), w_up=rwkv_w_up[l],
                  a0=rwkv_a0[l].reshape(2, 1, r_width), a_up=rwkv_a_up[l], g_up=rwkv_g_up[l],
                  k_k=rwkv_k_k[l][None], k_a=rwkv_k_a[l][None], r_k=rwkv_r_k[l].reshape(1, r_width),
                  ln_g=rwkv_ln_g[l][None], ln_b=rwkv_ln_b[l][None])
        r_c_out, r_c = _rwkv(u, rp, None, 0, b_ctx, s_ctx, RWKV_HEADS_PER_STEP_CTX, RWKV_SEQS_PER_STEP_CTX)
        r_l_out, _ = _rwkv(u, rp, state_rwkv[:, l], t_ctx // s_lat, b_lat, s_lat, RWKV_HEADS_PER_STEP_LAT, 1)
        rs.append(r_c)

        x1, h2, route, counts = _out_proj((att_c, att_l), (m_c_out, m_l_out), (r_c_out, r_l_out), x,
                                          w_out[l].astype(BF16), mod3, norm2_g[l][None], rw_t, rb, s_lat)
        row_tok, block_e, n_used, gate, dest1, dest2 = _dispatch(route, counts, 256)
        y = _experts(h2, block_e, row_tok, n_used, exp_w1, exp_w3, exp_w2, l)
        x = _combine(y, x1, mod3, gate, dest1, dest2, t_ctx, s_lat)

    y_prompt = x[0].reshape(b_ctx, s_ctx, d)
    y_sample = x[1].reshape(b_lat, s_lat, d)
    return (y_prompt, y_sample, jnp.stack(ks, axis=1), jnp.stack(vs, axis=1), jnp.stack(cs, axis=1),
            jnp.stack(ns, axis=1), jnp.stack(ms, axis=1), jnp.stack(rs, axis=1))
```

```python
import functools
import math

import jax
import jax.numpy as jnp
from jax import lax
from jax.experimental import pallas as pl
from jax.experimental.pallas import tpu as pltpu

F32 = jnp.float32
BF16 = jnp.bfloat16
I32 = jnp.int32

NORM_EPS = 1e-6
GN_EPS = 64e-5
M_INIT = -1e30
GRID_W = 64
ROPE_THETA = 10000.0
ATT_HD = 128
ATT_GROUPS = 4
ATT_KV_HEADS = 2
M_HD = 128
M_HEADS = 4
R_HD = 64
R_HEADS = 8
N_EXPERTS = 16
N_EXPERT_GROUPS = 4
EXPERTS_PER_GROUP = 4
CHUNK = 64
LANES = 128
EXPERT_ROWS = 256
VMEM_LIMIT = 58 * 1024 * 1024
RWKV_HEADS_PER_STEP_CTX = 8
RWKV_SEQS_PER_STEP_CTX = 2
RWKV_HEADS_PER_STEP_LAT = 8
MLSTM_SEQS_PER_STEP_CTX = 1

COL_M = 1536
COL_R = 3584
COL_GATE = 5376
N_IN_PAD = 5632


def _cparams(sem):
    return pltpu.CompilerParams(dimension_semantics=sem, vmem_limit_bytes=VMEM_LIMIT)


def _dot(a, b):
    return jnp.dot(a.astype(BF16), b.astype(BF16), preferred_element_type=F32)


def _dg(a, b, dims):
    return lax.dot_general(a, b, (dims, ((), ())), preferred_element_type=F32)


_NN = ((1,), (0,))
_NT = ((1,), (1,))
_TN = ((0,), (0,))


def _split(a):
    hi = a.astype(BF16)
    lo = (a - hi.astype(F32)).astype(BF16)
    return hi, lo


def _dot3(a, b, dims=_NN):
    ah, al = _split(a)
    bh, bl = _split(b)
    return _dg(ah, bh, dims) + (_dg(ah, bl, dims) + _dg(al, bh, dims))


def _dot1(a, b, dims=_NN):
    return _dg(a.astype(BF16), b.astype(BF16), dims)


def _rms(x, g):
    return x * lax.rsqrt(jnp.mean(x * x, axis=-1, keepdims=True) + NORM_EPS) * g


def _sigmoid(x):
    return 1.0 / (1.0 + jnp.exp(-x))


def _mod_kernel(c_ref, w_ref, b_ref, o_ref):
    c = c_ref[...]
    o_ref[...] = _dot(c * _sigmoid(c), w_ref[...]) + b_ref[...]


def _modulation(c_all, w_mod, b_mod):
    depth, d, n = w_mod.shape
    tn = 1024
    return pl.pallas_call(
        _mod_kernel,
        grid=(depth, n // tn),
        in_specs=[pl.BlockSpec((8, d), lambda l, j: (0, 0)),
                  pl.BlockSpec((None, d, tn), lambda l, j: (l, 0, j)),
                  pl.BlockSpec((None, 1, tn), lambda l, j: (l, 0, j))],
        out_specs=pl.BlockSpec((None, 8, tn), lambda l, j: (l, 0, j)),
        out_shape=jax.ShapeDtypeStruct((depth, 8, n), F32),
        compiler_params=_cparams(("arbitrary", "arbitrary")),
        name="modulation",
    )(c_all, w_mod, b_mod.reshape(depth, 1, n))


def _mod_row(tok0, t_ctx, s_lat):
    return jnp.where(tok0 < t_ctx, 0, 1 + (tok0 - t_ctx) // s_lat)


def _in_kernel(xc_ref, xl_ref, g_ref, sh_ref, sc_ref, wa_ref, wb_ref, o_ref, h_ref, *, n_ctx_tiles, n_head_tiles):
    i = pl.program_id(0)
    j = pl.program_id(1)

    def normalise(x_ref):
        slab = 256
        for r0 in range(0, x_ref.shape[0], slab):
            h = _rms(x_ref[r0:r0 + slab, :], g_ref[...]) * (1.0 + sc_ref[...]) + sh_ref[...]
            h_ref[r0:r0 + slab, :] = h.astype(BF16)

    @pl.when(jnp.logical_and(j == 0, i < n_ctx_tiles))
    def _():
        normalise(xc_ref)

    @pl.when(jnp.logical_and(j == 0, i >= n_ctx_tiles))
    def _():
        normalise(xl_ref)

    @pl.when(j < n_head_tiles)
    def _():
        o_ref[...] = _dg(h_ref[...], wa_ref[...].astype(BF16), _NT)

    @pl.when(j >= n_head_tiles)
    def _():
        o_ref[...] = _dg(h_ref[...], wb_ref[...], _NT)


def _in_proj(xc, xl, g1, mod3, w_in_t, layer, w_tail_t, s_lat):
    t_ctx, d = xc.shape
    ntok = t_ctx + xl.shape[0]
    tm, tn = 1024, 512
    na = COL_R // tn
    n = COL_R + w_tail_t.shape[0]
    nct = t_ctx // tm
    row = lambda i: _mod_row(i * tm, t_ctx, s_lat)
    return pl.pallas_call(
        functools.partial(_in_kernel, n_ctx_tiles=nct, n_head_tiles=na),
        grid=(ntok // tm, n // tn),
        in_specs=[pl.BlockSpec((tm, d), lambda i, j: (jnp.minimum(i, nct - 1), 0)),
                  pl.BlockSpec((tm, d), lambda i, j: (jnp.maximum(i - nct, 0), 0)),
                  pl.BlockSpec((1, d), lambda i, j: (0, 0)),
                  pl.BlockSpec((None, 1, d), lambda i, j: (row(i), 0, 0)),
                  pl.BlockSpec((None, 1, d), lambda i, j: (row(i), 0, 1)),
                  pl.BlockSpec((None, tn, d), lambda i, j: (layer, jnp.minimum(j, na - 1), 0)),
                  pl.BlockSpec((tn, d), lambda i, j: (jnp.maximum(j - na, 0), 0))],
        out_specs=pl.BlockSpec((tm, tn), lambda i, j: (i, j)),
        out_shape=jax.ShapeDtypeStruct((ntok, n), F32),
        scratch_shapes=[pltpu.VMEM((tm, d), BF16)],
        compiler_params=_cparams(("arbitrary", "arbitrary")),
        name="in_proj",
    )(xc, xl, g1, mod3, mod3, w_in_t, w_tail_t)


def _softmax_av(q, kb, vb):
    s = _dg(q.astype(BF16), kb, _NT) * (ATT_HD ** -0.5)
    p = jnp.exp(s - jnp.max(s, axis=-1, keepdims=True))
    l = jnp.sum(p, axis=-1, keepdims=True)
    return jnp.dot(p.astype(BF16), vb, preferred_element_type=F32) / l


def _att_ctx_kernel(q_ref, k_ref, v_ref, qn_ref, kn_ref, o_ref, ko_ref, vo_ref):
    k = _rms(k_ref[...], kn_ref[...])
    ko_ref[...] = k
    vo_ref[...] = v_ref[...]
    kb = k.astype(BF16)
    vb = v_ref[...].astype(BF16)
    for g in range(ATT_GROUPS):
        q = _rms(q_ref[:, g * ATT_HD:(g + 1) * ATT_HD], qn_ref[...])
        o_ref[:, g * ATT_HD:(g + 1) * ATT_HD] = _softmax_av(q, kb, vb)


def _rope(x, cos, sin):
    lane = lax.broadcasted_iota(I32, x.shape, 1)
    first = (lane % (ATT_HD // 2)) < (ATT_HD // 4)
    partner = jnp.where(first, pltpu.roll(x, ATT_HD - ATT_HD // 4, 1), pltpu.roll(x, ATT_HD // 4, 1))
    return x * cos + partner * sin


def _att_lat_kernel(q_ref, k_ref, v_ref, ck_ref, cv_ref, qn_ref, kn_ref, cosq_ref, sinq_ref, cosk_ref, sink_ref,
                    o_ref, kb_ref, vb_ref, *, past):
    @pl.when(pl.program_id(2) == 0)
    def _():
        k = _rope(_rms(k_ref[...], kn_ref[...]), cosk_ref[...], sink_ref[...])
        kb_ref[0:past, :] = ck_ref[...].astype(BF16)
        kb_ref[past:, :] = k.astype(BF16)
        vb_ref[0:past, :] = cv_ref[...].astype(BF16)
        vb_ref[past:, :] = v_ref[...].astype(BF16)

    kb = kb_ref[...]
    vb = vb_ref[...]
    for g in range(ATT_GROUPS):
        q = _rope(_rms(q_ref[:, g * ATT_HD:(g + 1) * ATT_HD], qn_ref[...]), cosq_ref[...], sinq_ref[...])
        o_ref[:, g * ATT_HD:(g + 1) * ATT_HD] = _softmax_av(q, kb, vb)


def _attention(u, qn, kn, cache_k, cache_v, cos, sin, b_ctx, s_ctx, b_lat, s_lat):
    ntok = u.shape[0]
    t_ctx = b_ctx * s_ctx
    gw = ATT_GROUPS * ATT_HD
    kcol = (ATT_KV_HEADS * gw) // ATT_HD
    vcol = kcol + ATT_KV_HEADS
    kv_spec = pl.BlockSpec((s_ctx, ATT_HD), lambda b, h: (b, h))
    att_c, k_ctx, v_ctx = pl.pallas_call(
        _att_ctx_kernel,
        grid=(b_ctx, ATT_KV_HEADS),
        in_specs=[pl.BlockSpec((s_ctx, gw), lambda b, h: (b, h)),
                  pl.BlockSpec((s_ctx, ATT_HD), lambda b, h: (b, kcol + h)),
                  pl.BlockSpec((s_ctx, ATT_HD), lambda b, h: (b, vcol + h)),
                  pl.BlockSpec((1, ATT_HD), lambda b, h: (0, 0)),
                  pl.BlockSpec((1, ATT_HD), lambda b, h: (0, 0))],
        out_specs=[pl.BlockSpec((s_ctx, gw), lambda b, h: (b, h)), kv_spec, kv_spec],
        out_shape=[jax.ShapeDtypeStruct((t_ctx, ATT_KV_HEADS * gw), F32),
                   jax.ShapeDtypeStruct((t_ctx, ATT_KV_HEADS * ATT_HD), F32),
                   jax.ShapeDtypeStruct((t_ctx, ATT_KV_HEADS * ATT_HD), F32)],
        compiler_params=_cparams(("arbitrary", "arbitrary")),
        name="att_ctx",
    )(u, u, u, qn, kn)

    tq = 256
    nqb = s_lat // tq
    past = cache_k.shape[1]
    qrow0 = t_ctx // tq
    krow0 = t_ctx // s_lat
    att_l = pl.pallas_call(
        functools.partial(_att_lat_kernel, past=past),
        grid=(b_lat, ATT_KV_HEADS, nqb),
        in_specs=[pl.BlockSpec((tq, gw), lambda b, h, i: (qrow0 + b * nqb + i, h)),
                  pl.BlockSpec((s_lat, ATT_HD), lambda b, h, i: (krow0 + b, kcol + h)),
                  pl.BlockSpec((s_lat, ATT_HD), lambda b, h, i: (krow0 + b, vcol + h)),
                  pl.BlockSpec((None, past, ATT_HD), lambda b, h, i: (b, 0, h)),
                  pl.BlockSpec((None, past, ATT_HD), lambda b, h, i: (b, 0, h)),
                  pl.BlockSpec((1, ATT_HD), lambda b, h, i: (0, 0)),
                  pl.BlockSpec((1, ATT_HD), lambda b, h, i: (0, 0)),
                  pl.BlockSpec((tq, ATT_HD), lambda b, h, i: (i, 0)),
                  pl.BlockSpec((tq, ATT_HD), lambda b, h, i: (i, 0)),
                  pl.BlockSpec((s_lat, ATT_HD), lambda b, h, i: (0, 0)),
                  pl.BlockSpec((s_lat, ATT_HD), lambda b, h, i: (0, 0))],
        out_specs=pl.BlockSpec((tq, gw), lambda b, h, i: (b * nqb + i, h)),
        out_shape=jax.ShapeDtypeStruct((ntok - t_ctx, ATT_KV_HEADS * gw), F32),
        scratch_shapes=[pltpu.VMEM((past + s_lat, ATT_HD), BF16), pltpu.VMEM((past + s_lat, ATT_HD), BF16)],
        compiler_params=_cparams(("arbitrary", "arbitrary", "arbitrary")),
        name="att_lat",
    )(u, u, u, cache_k, cache_v, qn, kn, cos, sin, cos, sin)
    return att_c, att_l, k_ctx, v_ctx


def _rope_tables(n_tokens):
    pos = jnp.arange(n_tokens)
    row = (pos // GRID_W).astype(F32)
    col = (pos % GRID_W).astype(F32)
    n_freq = ATT_HD // 4
    inv_freq = ROPE_THETA ** (-jnp.arange(n_freq, dtype=F32) / n_freq)
    ang_r = row[:, None] * inv_freq[None, :]
    ang_c = col[:, None] * inv_freq[None, :]
    cos = jnp.concatenate([jnp.cos(ang_r), jnp.cos(ang_r), jnp.cos(ang_c), jnp.cos(ang_c)], axis=-1)
    sin = jnp.concatenate([-jnp.sin(ang_r), jnp.sin(ang_r), -jnp.sin(ang_c), jnp.sin(ang_c)], axis=-1)
    return cos, sin


def _log_sigmoid(x):
    return jnp.minimum(x, 0.0) - jnp.log1p(jnp.exp(-jnp.abs(x)))


def _mlstm_kernel(*refs, zero_init, seq):
    if zero_init:
        (q_ref, k_ref, v_ref, o_ref, gc_ref, gt_ref, br_ref, bc_ref, ng_ref,
         out_ref, co_ref, mo_ref, h_scr, c_scr, m_scr) = refs
    else:
        (q_ref, k_ref, v_ref, o_ref, gc_ref, gt_ref, br_ref, bc_ref, ng_ref, c0_ref, m0_ref,
         out_ref, co_ref, mo_ref, h_scr, c_scr, m_scr) = refs
    nc = seq // CHUNK
    nseq = q_ref.shape[0] // seq
    if zero_init:
        c_scr[...] = jnp.zeros(c_scr.shape, F32)
        m_scr[...] = jnp.full(m_scr.shape, M_INIT, F32)
    else:
        c_scr[...] = c0_ref[...]
        m_scr[...] = m0_ref[...]

    ti = lax.broadcasted_iota(I32, (CHUNK, CHUNK), 0)
    si = lax.broadcasted_iota(I32, (CHUNK, CHUNK), 1)
    ones_col = jnp.where(lax.broadcasted_iota(I32, (CHUNK, M_HD), 1) == 0, 1.0, 0.0).astype(BF16)
    eye_d = jnp.where(lax.broadcasted_iota(I32, (M_HD, M_HD), 0) == lax.broadcasted_iota(I32, (M_HD, M_HD), 1),
                      1.0, 0.0).astype(BF16)

    def cumsum3(tri, x, tri_left):
        hi = x.astype(BF16)
        r1 = x - hi.astype(F32)
        mid = r1.astype(BF16)
        lo = (r1 - mid.astype(F32)).astype(BF16)
        if tri_left:
            return _dg(tri, hi, _NN) + (_dg(tri, mid, _NN) + _dg(tri, lo, _NN))
        return _dg(hi, tri, _NN) + (_dg(mid, tri, _NN) + _dg(lo, tri, _NN))
    gate_i = lambda d, h: 2 * M_HEADS * d + h
    gate_f = lambda d, h: 2 * M_HEADS * d + M_HEADS + h

    def chunk(c, carry):
        ch = []
        for sq, d in [(sq, d) for sq in range(nseq) for d in (0, 1)]:
            cc = c if d == 0 else nc - 1 - c
            rows = pl.ds(pl.multiple_of(sq * seq + cc * CHUNK, CHUNK), CHUNK)
            causal = (si <= ti) if d == 0 else (si >= ti)
            tri = jnp.where(causal, 1.0, 0.0).astype(BF16)
            tri_t = jnp.where((ti <= si) if d == 0 else (ti >= si), 1.0, 0.0).astype(BF16)
            q_all = q_ref[rows, :] * (M_HD ** -0.5)
            k_all = k_ref[rows, :]
            v_all = v_ref[rows, :]
            gcol = gc_ref[rows, :] + br_ref[...]
            grow = gt_ref[sq * nc + cc] + bc_ref[...]
            bcum_c_all = cumsum3(tri, _log_sigmoid(gcol), True)
            bcum_r_all = cumsum3(tri_t, _log_sigmoid(grow), False)
            for h in range(M_HEADS):
                sl = slice(h * M_HD, (h + 1) * M_HD)
                q, k, v = q_all[:, sl], k_all[:, sl], v_all[:, sl]
                ig_c = gcol[:, gate_i(d, h):gate_i(d, h) + 1]
                ig_r = grow[gate_i(d, h):gate_i(d, h) + 1, :]
                bcum_c = bcum_c_all[:, gate_f(d, h):gate_f(d, h) + 1]
                bcum_r = bcum_r_all[gate_f(d, h):gate_f(d, h) + 1, :]
                m_st = m_scr[sq, d, h][:, 0:1]
                dmat = jnp.where(causal, bcum_c - bcum_r + ig_r, -jnp.inf)
                inter = bcum_c + m_st
                m_t = jnp.maximum(inter, jnp.max(dmat, axis=1, keepdims=True))
                b_last = bcum_c[CHUNK - 1:CHUNK, :] if d == 0 else bcum_c[0:1, :]
                g_c = b_last - bcum_c + ig_c
                m_new = jnp.maximum(b_last + m_st, jnp.max(g_c, axis=0, keepdims=True))
                ch.append(dict(sq=sq, d=d, h=h, rows=rows, sl=sl, qb=q.astype(BF16), kb=k.astype(BF16),
                               vx=jnp.concatenate([v.astype(BF16), ones_col], axis=1),
                               c_st=c_scr[sq, d, h], m_t=m_t, m_new=m_new,
                               w_intra=jnp.exp(dmat - m_t), w_inter=jnp.exp(inter - m_t),
                               decay=jnp.exp(b_last + m_st - m_new), kw=(k * jnp.exp(g_c - m_new)).astype(BF16)))
        for x in ch:
            x['s_qk'] = _dg(x['qb'], x['kb'], _NT) * x['w_intra']
        for x in ch:
            x['qc'] = _dg(x['qb'], x['c_st'].astype(BF16), _NN)
        for x in ch:
            x['kw_t'] = _dg(eye_d, x['kw'], _NT).astype(BF16)
        for x in ch:
            x['sv'] = _dg(x['s_qk'].astype(BF16), x['vx'], _NN)
        for x in ch:
            x['upd'] = _dg(x['kw_t'], x['vx'], _NN)
        for x in ch:
            sq, d, h = x['sq'], x['d'], x['h']
            tot = x['w_inter'] * x['qc'] + x['sv']
            den = tot[:, M_HD:M_HD + 1]
            h_scr[d, x['rows'], x['sl']] = tot[:, :M_HD] / jnp.maximum(jnp.abs(den), jnp.exp(-x['m_t']))
            c_scr[sq, d, h] = x['decay'] * x['c_st'] + x['upd']
            m_scr[sq, d, h] = jnp.broadcast_to(x['m_new'], (1, M_HD))
        return carry

    lax.fori_loop(0, nc, chunk, 0)
    hsum = h_scr[0] + h_scr[1]
    hn = jnp.concatenate(
        [hsum[:, h * M_HD:(h + 1) * M_HD]
         * lax.rsqrt(jnp.mean(hsum[:, h * M_HD:(h + 1) * M_HD] ** 2, axis=-1, keepdims=True) + NORM_EPS)
         for h in range(M_HEADS)], axis=1)
    out_ref[...] = hn * ng_ref[...] * _sigmoid(o_ref[...])
    co_ref[...] = c_scr[...]
    mo_ref[...] = m_scr[...]


def _mlstm(u, gcol, gt, bias, norm_g, states, row0, batch, seq, seqs_per_step):
    zero_init = states is None
    nq = seqs_per_step
    rows = nq * seq
    assert batch % nq == 0 and row0 % nq == 0
    rb = row0 // nq
    width = M_HEADS * M_HD
    qc = COL_M // width
    ngate = 4 * M_HEADS
    blk = lambda off: pl.BlockSpec((rows, width), lambda b: (rb + b, off))
    st_c = pl.BlockSpec((nq, 2, M_HEADS, M_HD, 2 * M_HD), lambda b: (b, 0, 0, 0, 0))
    st_v = pl.BlockSpec((nq, 2, M_HEADS, 1, M_HD), lambda b: (b, 0, 0, 0, 0))
    in_specs = [blk(qc), blk(qc + 1), blk(qc + 2), blk(qc + 3),
                pl.BlockSpec((rows, ngate), lambda b: (rb + b, 0)),
                pl.BlockSpec((rows // CHUNK, ngate, CHUNK), lambda b: (rb + b, 0, 0)),
                pl.BlockSpec((1, ngate), lambda b: (0, 0)),
                pl.BlockSpec((ngate, 1), lambda b: (0, 0)),
                pl.BlockSpec((1, width), lambda b: (0, 0))]
    args = [u, u, u, u, gcol, gt, bias.reshape(1, ngate), bias.reshape(ngate, 1), norm_g]
    if not zero_init:
        in_specs += [st_c, st_v]
        args += list(states)
    return pl.pallas_call(
        functools.partial(_mlstm_kernel, zero_init=zero_init, seq=seq),
        grid=(batch // nq,),
        in_specs=in_specs,
        out_specs=[pl.BlockSpec((rows, width), lambda b: (b, 0)), st_c, st_v],
        out_shape=[jax.ShapeDtypeStruct((batch * seq, width), F32),
                   jax.ShapeDtypeStruct((batch, 2, M_HEADS, M_HD, 2 * M_HD), F32),
                   jax.ShapeDtypeStruct((batch, 2, M_HEADS, 1, M_HD), F32)],
        scratch_shapes=[pltpu.VMEM((2, rows, width), F32), pltpu.VMEM((nq, 2, M_HEADS, M_HD, 2 * M_HD), F32),
                        pltpu.VMEM((nq, 2, M_HEADS, 1, M_HD), F32)],
        compiler_params=_cparams(("arbitrary",)),
        name="mlstm_ctx" if zero_init else "mlstm_lat",
    )(*args)


def _rwkv_kernel(*refs, zero_init, seq):
    if zero_init:
        (r_ref, k_ref, v_ref, xl_ref, xg_ref, mur_ref, muk_ref, muv_ref, mul_ref, mug_ref,
         w0_ref, wup_ref, a0_ref, aup_ref, gup_ref, kk_ref, ka_ref, rk_ref, lng_ref, lnb_ref,
         out_ref, so_ref, r_scr, v_scr, kk_scr, g_scr, bonus_scr, lw_scr, kd_scr, b_scr, y_scr, s_scr) = refs
    else:
        (r_ref, k_ref, v_ref, xl_ref, xg_ref, mur_ref, muk_ref, muv_ref, mul_ref, mug_ref,
         w0_ref, wup_ref, a0_ref, aup_ref, gup_ref, kk_ref, ka_ref, rk_ref, lng_ref, lnb_ref, s0_ref,
         out_ref, so_ref, r_scr, v_scr, kk_scr, g_scr, bonus_scr, lw_scr, kd_scr, b_scr, y_scr, s_scr) = refs
    nc = seq // CHUNK
    hd = R_HD
    nh = r_ref.shape[1] // hd
    rows_blk = r_ref.shape[0]
    nseq = rows_blk // seq

    def tshift(x_ref, mu_ref):
        x = x_ref[...]
        row = lax.broadcasted_iota(I32, x.shape, 0) % seq
        prev = jnp.where(row == 0, 0.0, pltpu.roll(x, 1, 0))
        nxt = jnp.where(row == seq - 1, 0.0, pltpu.roll(x, rows_blk - 1, 0))
        return x + mu_ref[...] * (0.5 * (prev + nxt) - x)

    r = tshift(r_ref, mur_ref)
    k = tshift(k_ref, muk_ref)
    v = tshift(v_ref, muv_ref)
    xl = tshift(xl_ref, mul_ref)
    xg = tshift(xg_ref, mug_ref)
    g = _dot(_sigmoid(xg), gup_ref[...])
    wlanes = nh * hd
    same_head = (lax.broadcasted_iota(I32, (wlanes, wlanes), 0) // hd
                 == lax.broadcasted_iota(I32, (wlanes, wlanes), 1) // hd)
    head_ones = jnp.where(same_head, 1.0, 0.0).astype(BF16)

    def head_sum(x):
        hi = x.astype(BF16)
        r1 = x - hi.astype(F32)
        mid = r1.astype(BF16)
        lo = (r1 - mid.astype(F32)).astype(BF16)
        return _dg(hi, head_ones, _NN) + (_dg(mid, head_ones, _NN) + _dg(lo, head_ones, _NN))

    kkp = k * kk_ref[...]
    kk = kkp * lax.rsqrt(jnp.maximum(head_sum(kkp * kkp), 1e-24))
    tw = jnp.tanh(xl[:, 0:hd])
    xa = xl[:, hd:2 * hd]
    r_scr[...] = r
    v_scr[...] = v
    kk_scr[...] = kk
    g_scr[...] = g
    bonus_scr[...] = head_sum(r * k * rk_ref[...]) * v
    for d in (0, 1):
        lw_scr[d] = -math.exp(-0.5) * _sigmoid(w0_ref[d] + _dot(tw, wup_ref[d]))
        ad = _sigmoid(a0_ref[d] + _dot(xa, aup_ref[d]))
        kd_scr[d] = k * (1.0 + (ad - 1.0) * ka_ref[...])
        b_scr[d] = kk * ad
    npair = nh // 2
    pw = 2 * hd
    zero_blk = jnp.zeros((hd, hd), F32)
    for q in range(nseq):
        for d in (0, 1):
            for p in range(npair):
                if zero_init:
                    s_scr[q, d, p] = jnp.zeros((pw, pw), F32)
                else:
                    s_scr[q, d, p] = jnp.concatenate(
                        [jnp.concatenate([s0_ref[q, d, 2 * p], zero_blk], axis=1),
                         jnp.concatenate([zero_blk, s0_ref[q, d, 2 * p + 1]], axis=1)], axis=0)
    y_scr[...] = jnp.zeros(y_scr.shape, F32)

    ti = lax.broadcasted_iota(I32, (CHUNK, pw), 0)
    si = lax.broadcasted_iota(I32, (CHUNK, pw), 1) % CHUNK
    eye2 = (ti == si).astype(F32)
    tri_i = lax.broadcasted_iota(I32, (CHUNK, CHUNK), 0)
    tri_j = lax.broadcasted_iota(I32, (CHUNK, CHUNK), 1)
    first = lax.broadcasted_iota(I32, (1, pw), 1) < hd
    diag_blk = ((lax.broadcasted_iota(I32, (pw, pw), 0) < hd) == (lax.broadcasted_iota(I32, (pw, pw), 1) < hd))

    def bd(x):
        xb = x.astype(BF16)
        zero = jnp.zeros_like(xb)
        return jnp.concatenate([jnp.where(first, xb, zero), jnp.where(first, zero, xb)], axis=0)

    def chunk(c, carry):
        chains = []
        for q, d in [(q, d) for q in range(nseq) for d in (0, 1)]:
            cc = c if d == 0 else nc - 1 - c
            rows = pl.ds(pl.multiple_of(q * seq + cc * CHUNK, CHUNK), CHUNK)
            incl = (si <= ti) if d == 0 else (si >= ti)
            strict = (si < ti) if d == 0 else (si > ti)
            lw = lw_scr[d, rows, :]
            lw_hi, lw_lo = _split(lw)
            tri = jnp.where((tri_j <= tri_i) if d == 0 else (tri_j >= tri_i), 1.0, 0.0).astype(BF16)
            lc = _dg(tri, lw_hi, _NN) + _dg(tri, lw_lo, _NN)
            l_last = lc[CHUNK - 1:CHUNK, :] if d == 0 else lc[0:1, :]
            e_neg = jnp.exp(-lc)
            e_end = jnp.exp(l_last - lc)
            vc = v_scr[rows, :]
            kdc = kd_scr[d, rows, :]
            bc = b_scr[d, rows, :]
            rt = r_scr[rows, :] * jnp.exp(lc)
            kkt = kk_scr[rows, :] * jnp.exp(lc - lw)
            kh = kdc * e_neg
            bh = bc * e_neg
            kbar = kdc * e_end
            bbar = bc * e_end
            w_end = jnp.exp(l_last)
            for p in range(npair):
                sl = slice(p * pw, (p + 1) * pw)
                chains.append(dict(
                    q=q, d=d, pair=p, sl=sl, rows=rows, incl=incl, strict=strict,
                    lhs=jnp.concatenate([kkt[:, sl], rt[:, sl]], axis=0).astype(BF16),
                    rhs=jnp.concatenate([bd(bh[:, sl]), bd(kh[:, sl])], axis=0),
                    end=jnp.concatenate([kbar[:, sl], bbar[:, sl]], axis=0),
                    v=vc[:, sl], w_end=w_end[:, sl], s0=s_scr[q, d, p]))
        for ch in chains:
            ch['ab'] = _dg(ch['lhs'], ch['rhs'], _NT)
        for ch in chains:
            ch['proj'] = _dg(ch['lhs'], ch['s0'].astype(BF16), _NT)
        for ch in chains:
            ab = ch['ab']
            ch['a_kb'] = jnp.where(ch['strict'], ab[:CHUNK, :pw], 0.0)
            ch['b_rb'] = jnp.where(ch['incl'], ab[CHUNK:, :pw], 0.0)
            ch['akk_brk'] = jnp.concatenate([jnp.where(ch['strict'], ab[:CHUNK, pw:], 0.0),
                                             jnp.where(ch['incl'], ab[CHUNK:, pw:], 0.0)], axis=0)
        for ch in chains:
            ch['p'] = _dg(ch['a_kb'].astype(BF16), bd(ch['a_kb']), _NN)
        for ch in chains:
            ch['abv'] = _dg(ch['akk_brk'].astype(BF16), bd(ch['v']), _NN)
        for ch in chains:
            inv = eye2 - ch['a_kb']
            ch['inv'] = inv + _dg(inv.astype(BF16), bd(ch['p']), _NN)
        span = 4
        while span < CHUNK:
            for ch in chains:
                ch['p'] = _dg(ch['p'].astype(BF16), bd(ch['p']), _NN)
            for ch in chains:
                ch['inv'] = ch['inv'] + _dg(ch['inv'].astype(BF16), bd(ch['p']), _NN)
            span *= 2
        for ch in chains:
            ch['u'] = _dg(ch['inv'].astype(BF16), bd(ch['proj'][:CHUNK] + ch['abv'][:CHUNK]), _NN)
        for ch in chains:
            y = ch['proj'][CHUNK:] + ch['abv'][CHUNK:] - _dg(ch['b_rb'].astype(BF16), bd(ch['u']), _NN)
            y_scr[ch['rows'], ch['sl']] += y
        for ch in chains:
            upd = _dot3(jnp.concatenate([ch['v'], -ch['u']], axis=0), ch['end'], _TN)
            s_scr[ch['q'], ch['d'], ch['pair']] = ch['s0'] * ch['w_end'] + jnp.where(diag_blk, upd, 0.0)
        return carry

    lax.fori_loop(0, nc, chunk, 0)

    y = y_scr[...]
    dev = y - head_sum(y) * (1.0 / hd)
    yn = dev * lax.rsqrt(head_sum(dev * dev) * (1.0 / hd) + GN_EPS)
    out_ref[...] = (yn * lng_ref[...] + lnb_ref[...] + bonus_scr[...]) * g_scr[...]
    for q in range(nseq):
        for d in (0, 1):
            for p in range(npair):
                s = s_scr[q, d, p]
                so_ref[q, d, 2 * p] = s[:hd, :hd]
                so_ref[q, d, 2 * p + 1] = s[hd:, hd:]


def _rwkv(u, p, state, row0, batch, seq, heads_per_step, seqs_per_step):
    zero_init = state is None
    nh = heads_per_step
    nq = seqs_per_step
    rows = nq * seq
    assert batch % nq == 0 and row0 % nq == 0
    wd = nh * R_HD
    nsteps = R_HEADS // nh
    width = R_HEADS * R_HD
    rc = COL_R // wd
    sec = width // wd
    lc = (COL_R + 3 * width) // LANES
    ublk = lambda off: pl.BlockSpec((rows, wd), lambda b, h: (row0 // nq + b, rc + off + h))
    ufix = lambda blk: pl.BlockSpec((rows, LANES), lambda b, h: (row0 // nq + b, blk))
    mblk = lambda off: pl.BlockSpec((1, wd), lambda b, h: (0, off + h))
    mfix = lambda blk: pl.BlockSpec((1, LANES), lambda b, h: (0, blk))
    vec = pl.BlockSpec((1, wd), lambda b, h: (0, h))
    in_specs = [ublk(0), ublk(sec), ublk(2 * sec), ufix(lc), ufix(lc + 1),
                mblk(0), mblk(sec), mblk(2 * sec), mfix(lc - COL_R // LANES), mfix(lc - COL_R // LANES + 1),
                pl.BlockSpec((2, 1, wd), lambda b, h: (0, 0, h)),
                pl.BlockSpec((2, R_HD, wd), lambda b, h: (0, 0, h)),
                pl.BlockSpec((2, 1, wd), lambda b, h: (0, 0, h)),
                pl.BlockSpec((2, R_HD, wd), lambda b, h: (0, 0, h)),
                pl.BlockSpec((LANES, wd), lambda b, h: (0, h)),
                vec, vec, vec, vec, vec]
    args = [u, u, u, u, u, p['mu'], p['mu'], p['mu'], p['mu'], p['mu'],
            p['w0'], p['w_up'], p['a0'], p['a_up'], p['g_up'], p['k_k'], p['k_a'], p['r_k'], p['ln_g'], p['ln_b']]
    st_spec = pl.BlockSpec((nq, 2, nh, R_HD, R_HD), lambda b, h: (b, 0, h, 0, 0))
    if not zero_init:
        in_specs.append(st_spec)
        args.append(state)
    big = lambda n: pltpu.VMEM((n, rows, wd), F32)
    return pl.pallas_call(
        functools.partial(_rwkv_kernel, zero_init=zero_init, seq=seq),
        grid=(batch // nq, nsteps),
        in_specs=in_specs,
        out_specs=[pl.BlockSpec((rows, wd), lambda b, h: (b, h)), st_spec],
        out_shape=[jax.ShapeDtypeStruct((batch * seq, width), F32),
                   jax.ShapeDtypeStruct((batch, 2, R_HEADS, R_HD, R_HD), F32)],
        scratch_shapes=[pltpu.VMEM((rows, wd), F32)] * 5
                       + [big(2), big(2), big(2), pltpu.VMEM((rows, wd), F32),
                          pltpu.VMEM((nq, 2, nh // 2, 2 * R_HD, 2 * R_HD), F32)],
        compiler_params=_cparams(("arbitrary", "arbitrary")),
        name="rwkv_ctx" if zero_init else "rwkv_lat",
    )(*args)


def _top2_sum(a, b, c, d):
    m1, n1 = jnp.maximum(a, b), jnp.minimum(a, b)
    m2, n2 = jnp.maximum(c, d), jnp.minimum(c, d)
    return jnp.maximum(m1, m2) + jnp.maximum(jnp.minimum(m1, m2), jnp.maximum(n1, n2))


def _first_argmax(vals):
    best = functools.reduce(jnp.maximum, vals)
    idx = jnp.full(best.shape, len(vals) - 1, I32)
    for j in range(len(vals) - 2, -1, -1):
        idx = jnp.where(vals[j] == best, j, idx)
    return best, idx


def _out_kernel(attc_ref, attl_ref, mc_ref, ml_ref, rc_ref, rl_ref, xc_ref, xl_ref, w_ref, g1_ref, sh2_ref, sc2_ref,
                n2_ref, rw_ref, rb_ref, x1_ref, h2_ref, route_ref, cnt_ref, cnt_scr, *, n_ctx_tiles):
    @pl.when(pl.program_id(0) == 0)
    def _():
        cnt_scr[...] = jnp.zeros(cnt_scr.shape, F32)

    is_ctx = pl.program_id(0) < n_ctx_tiles
    both = lambda c_ref, l_ref: jnp.where(is_ctx, c_ref[...], l_ref[...])
    na = attc_ref.shape[1]
    nm = mc_ref.shape[1]
    mix = (jnp.dot(both(attc_ref, attl_ref).astype(BF16), w_ref[0:na, :], preferred_element_type=F32)
           + jnp.dot(both(mc_ref, ml_ref).astype(BF16), w_ref[na:na + nm, :], preferred_element_type=F32)
           + jnp.dot(both(rc_ref, rl_ref).astype(BF16), w_ref[na + nm:, :], preferred_element_type=F32))
    x1 = both(xc_ref, xl_ref) + g1_ref[...] * mix
    x1_ref[...] = x1
    h2 = _rms(x1, n2_ref[...]) * (1.0 + sc2_ref[...]) + sh2_ref[...]
    h2_ref[...] = h2
    logits = _dot1(rw_ref[...], h2, _NT)
    s = _sigmoid(logits)
    ssel = s + rb_ref[...]
    srow = [s[e:e + 1, :] for e in range(N_EXPERTS)]
    brow = [ssel[e:e + 1, :] for e in range(N_EXPERTS)]
    gscore = [_top2_sum(*brow[EXPERTS_PER_GROUP * g:EXPERTS_PER_GROUP * (g + 1)]) for g in range(N_EXPERT_GROUPS)]
    _, gidx = _first_argmax(gscore)
    pick = lambda rows, j: functools.reduce(
        lambda acc, g: jnp.where(gidx == g, rows[EXPERTS_PER_GROUP * g + j], acc),
        range(N_EXPERT_GROUPS - 2, -1, -1), rows[EXPERTS_PER_GROUP * (N_EXPERT_GROUPS - 1) + j])
    ing = [pick(brow, j) for j in range(EXPERTS_PER_GROUP)]
    sin_ = [pick(srow, j) for j in range(EXPERTS_PER_GROUP)]
    _, l1 = _first_argmax(ing)
    _, l2 = _first_argmax([jnp.where(l1 == j, -jnp.inf, ing[j]) for j in range(EXPERTS_PER_GROUP)])
    sel = lambda l: functools.reduce(lambda acc, j: jnp.where(l == j, sin_[j], acc),
                                     range(EXPERTS_PER_GROUP - 2, -1, -1), sin_[EXPERTS_PER_GROUP - 1])
    w1, w2 = sel(l1), sel(l2)
    tot = w1 + w2
    e1 = gidx * EXPERTS_PER_GROUP + l1
    e2 = gidx * EXPERTS_PER_GROUP + l2
    tm = e1.shape[1]
    eid = lax.broadcasted_iota(I32, (N_EXPERTS, tm), 0)
    oh1 = eid == e1
    oh2 = eid == e2
    picked = jnp.where(jnp.logical_or(oh1, oh2), 1.0, 0.0)
    earlier = jnp.where(lax.broadcasted_iota(I32, (tm, tm), 0) < lax.broadcasted_iota(I32, (tm, tm), 1), 1.0, 0.0)
    rank = cnt_scr[:, 0:1] + jnp.dot(picked.astype(BF16), earlier.astype(BF16), preferred_element_type=F32)
    pos1 = jnp.sum(jnp.where(oh1, rank, 0.0), axis=0, keepdims=True)
    pos2 = jnp.sum(jnp.where(oh2, rank, 0.0), axis=0, keepdims=True)
    cnt = cnt_scr[...] + jnp.sum(picked, axis=1, keepdims=True)
    cnt_scr[...] = cnt
    cnt_ref[...] = cnt
    zero = jnp.zeros_like(w1)
    route_ref[...] = jnp.concatenate([e1.astype(F32), e2.astype(F32), w1 / tot, w2 / tot, pos1, pos2, zero, zero],
                                     axis=0)


def _out_proj(att, m_out, r_out, x, w_out, mod3, n2g, rw_t, rb, s_lat):
    t_ctx, d = x[0].shape
    ntok = t_ctx + x[1].shape[0]
    tm = 256
    nct = t_ctx // tm
    row = lambda i: _mod_row(i * tm, t_ctx, s_lat)
    modblk = lambda j: pl.BlockSpec((None, 1, d), lambda i: (row(i), 0, j))
    pair = lambda a: [pl.BlockSpec((tm, a[0].shape[1]), lambda i: (jnp.minimum(i, nct - 1), 0)),
                      pl.BlockSpec((tm, a[1].shape[1]), lambda i: (jnp.maximum(i - nct, 0), 0))]
    return pl.pallas_call(
        functools.partial(_out_kernel, n_ctx_tiles=nct),
        grid=(ntok // tm,),
        in_specs=pair(att) + pair(m_out) + pair(r_out) + pair(x) + [
                  pl.BlockSpec(w_out.shape, lambda i: (0, 0)),
                  modblk(2), modblk(3), modblk(4),
                  pl.BlockSpec((1, d), lambda i: (0, 0)),
                  pl.BlockSpec(rw_t.shape, lambda i: (0, 0)),
                  pl.BlockSpec(rb.shape, lambda i: (0, 0))],
        out_specs=[pl.BlockSpec((tm, d), lambda i: (i, 0)),
                   pl.BlockSpec((tm, d), lambda i: (i, 0)),
                   pl.BlockSpec((8, tm), lambda i: (0, i)),
                   pl.BlockSpec((N_EXPERTS, LANES), lambda i: (0, 0))],
        out_shape=[jax.ShapeDtypeStruct((ntok, d), F32), jax.ShapeDtypeStruct((ntok, d), F32),
                   jax.ShapeDtypeStruct((8, ntok), F32), jax.ShapeDtypeStruct((N_EXPERTS, LANES), F32)],
        scratch_shapes=[pltpu.VMEM((N_EXPERTS, LANES), F32)],
        compiler_params=_cparams(("arbitrary",)),
        name="out_proj",
    )(*att, *m_out, *r_out, *x, w_out, mod3, mod3, mod3, n2g, rw_t, rb)


def _row_gather(src_hbm, idx_ref, base, dst, sem, n, unrolled, lo=0):
    def start(j):
        pltpu.make_async_copy(src_hbm.at[pl.ds(idx_ref[base + j], 1), :], dst.at[pl.ds(j, 1), :], sem).start()

    if unrolled:
        for j in range(lo, n):
            start(j)
    else:
        def body(j, c):
            start(j)
            return c
        lax.fori_loop(lo, n, body, 0)


def _expert_kernel(be_ref, tok_ref, nused_ref, h_hbm, w1_ref, w3_ref, w2_ref, y_ref, xbuf, w1b, w3b, w2b, sem):
    i = pl.program_id(0)
    n_used = nused_ref[0]
    wait = lambda s: pltpu.make_async_copy(h_hbm.at[pl.ds(0, EXPERT_ROWS), :], xbuf.at[s], sem.at[s]).wait()

    @pl.when(i < n_used)
    def _():
        slot = i % 2

        @pl.when(i == 0)
        def _():
            _row_gather(h_hbm, tok_ref, 0, xbuf.at[0], sem.at[0], EXPERT_ROWS, unrolled=False)

        @pl.when(jnp.logical_or(i == 0, be_ref[i] != be_ref[jnp.maximum(i - 1, 0)]))
        def _():
            w1b[...] = w1_ref[...].astype(BF16)
            w3b[...] = w3_ref[...].astype(BF16)
            w2b[...] = w2_ref[...].astype(BF16)

        wait(slot)
        de = w1b.shape[1]
        half = de // 2
        groups = 6
        per = EXPERT_ROWS // groups

        def next_rows(gi):
            hi = EXPERT_ROWS if gi == groups - 1 else (gi + 1) * per
            _row_gather(h_hbm, tok_ref, (i + 1) * EXPERT_ROWS, xbuf.at[1 - slot], sem.at[1 - slot], hi,
                        unrolled=True, lo=gi * per)

        xb = xbuf[slot].astype(BF16)
        hmid = []
        for n in range(2):
            cols = slice(n * half, (n + 1) * half)
            next_rows(2 * n)
            a = jnp.dot(xb, w1b[:, cols], preferred_element_type=F32)
            next_rows(2 * n + 1)
            b = jnp.dot(xb, w3b[:, cols], preferred_element_type=F32)
            hmid.append(((a * _sigmoid(a)) * b).astype(BF16))
        next_rows(4)
        y = jnp.dot(hmid[0], w2b[0:half, :], preferred_element_type=F32)
        next_rows(5)
        y_ref[...] = y + jnp.dot(hmid[1], w2b[half:, :], preferred_element_type=F32)

        @pl.when(i == n_used - 1)
        def _():
            wait(1 - slot)

    @pl.when(i >= n_used)
    def _():
        y_ref[...] = jnp.zeros(y_ref.shape, F32)


def _experts(h2, block_e, row_tok, n_used, w1, w3, w2, layer):
    ntok, d = h2.shape
    de = w1.shape[3]
    n_rows = row_tok.shape[0] - EXPERT_ROWS
    nb = n_rows // EXPERT_ROWS
    return pl.pallas_call(
        _expert_kernel,
        grid_spec=pltpu.PrefetchScalarGridSpec(
            num_scalar_prefetch=3,
            grid=(nb,),
            in_specs=[pl.BlockSpec(memory_space=pl.ANY),
                      pl.BlockSpec((None, None, d, de), lambda i, be, tok, nu: (layer, be[i], 0, 0)),
                      pl.BlockSpec((None, None, d, de), lambda i, be, tok, nu: (layer, be[i], 0, 0)),
                      pl.BlockSpec((None, None, de, d), lambda i, be, tok, nu: (layer, be[i], 0, 0))],
            out_specs=pl.BlockSpec((EXPERT_ROWS, d), lambda i, be, tok, nu: (i, 0)),
            scratch_shapes=[pltpu.VMEM((2, EXPERT_ROWS, d), F32), pltpu.VMEM((d, de), BF16),
                            pltpu.VMEM((d, de), BF16), pltpu.VMEM((de, d), BF16), pltpu.SemaphoreType.DMA((2,))]),
        out_shape=jax.ShapeDtypeStruct((n_rows, d), F32),
        compiler_params=_cparams(("arbitrary",)),
        name="experts",
    )(block_e, row_tok, n_used, h2, w1, w3, w2)


def _combine_kernel(d1_ref, d2_ref, y_hbm, x1_ref, g2_ref, gate_ref, oc_ref, ol_ref, ybuf, sem, *, n_ctx_tiles):
    tm = x1_ref.shape[0]
    i = pl.program_id(0)
    slot = i % 2

    def gather(tile, s, unrolled):
        _row_gather(y_hbm, d1_ref, tile * tm, ybuf.at[s, 0], sem.at[s], tm, unrolled)
        _row_gather(y_hbm, d2_ref, tile * tm, ybuf.at[s, 1], sem.at[s], tm, unrolled)

    def wait(s):
        pltpu.make_async_copy(y_hbm.at[pl.ds(0, tm), :], ybuf.at[s, 0], sem.at[s]).wait()
        pltpu.make_async_copy(y_hbm.at[pl.ds(0, tm), :], ybuf.at[s, 1], sem.at[s]).wait()

    @pl.when(i == 0)
    def _():
        gather(0, 0, False)

    wait(slot)
    gather(i + 1, 1 - slot, True)
    gate = gate_ref[...]
    moe = ybuf[slot, 0] * gate[:, 0:1] + ybuf[slot, 1] * gate[:, 1:2]
    new_x = x1_ref[...] + g2_ref[...] * moe

    @pl.when(i < n_ctx_tiles)
    def _():
        oc_ref[...] = new_x

    @pl.when(i >= n_ctx_tiles)
    def _():
        ol_ref[...] = new_x

    @pl.when(i == pl.num_programs(0) - 1)
    def _():
        wait(1 - slot)


def _combine(y, x1, mod3, gate, dest1, dest2, t_ctx, s_lat):
    ntok, d = x1.shape
    tm = 256
    nct = t_ctx // tm
    row = lambda i: _mod_row(i * tm, t_ctx, s_lat)
    return pl.pallas_call(
        functools.partial(_combine_kernel, n_ctx_tiles=nct),
        grid_spec=pltpu.PrefetchScalarGridSpec(
            num_scalar_prefetch=2,
            grid=(ntok // tm,),
            in_specs=[pl.BlockSpec(memory_space=pl.ANY),
                      pl.BlockSpec((tm, d), lambda i, a, b: (i, 0)),
                      pl.BlockSpec((None, 1, d), lambda i, a, b: (row(i), 0, 5)),
                      pl.BlockSpec((tm, 2), lambda i, a, b: (i, 0))],
            out_specs=[pl.BlockSpec((tm, d), lambda i, a, b: (jnp.minimum(i, nct - 1), 0)),
                       pl.BlockSpec((tm, d), lambda i, a, b: (jnp.maximum(i - nct, 0), 0))],
            scratch_shapes=[pltpu.VMEM((2, 2, tm, d), F32), pltpu.SemaphoreType.DMA((2,))]),
        out_shape=[jax.ShapeDtypeStruct((t_ctx, d), F32), jax.ShapeDtypeStruct((ntok - t_ctx, d), F32)],
        compiler_params=_cparams(("arbitrary",)),
        name="combine",
    )(dest1, dest2, y, x1, mod3, gate)


def _dispatch(route, counts, tile):
    ntok = route.shape[1]
    e = route[0:2].astype(I32)
    pos = route[4:6].astype(I32)
    gate = route[2:4].T
    counts = counts[:, 0].astype(I32)
    padded = (counts + EXPERT_ROWS - 1) // EXPERT_ROWS * EXPERT_ROWS
    pad_end = jnp.cumsum(padded)
    pad_start = pad_end - padded
    onehot = (e[:, :, None] == jnp.arange(N_EXPERTS, dtype=I32)).astype(I32)
    dest = jnp.sum(onehot * pad_start, axis=-1) + pos
    n_rows = -(-(2 * ntok) // EXPERT_ROWS) * EXPERT_ROWS + N_EXPERTS * EXPERT_ROWS
    nb = n_rows // EXPERT_ROWS
    tok = jnp.broadcast_to(jnp.arange(ntok, dtype=I32)[None, :], (2, ntok))
    row_tok = jnp.zeros((n_rows + EXPERT_ROWS,), I32).at[dest.reshape(-1)].set(tok.reshape(-1), unique_indices=True)
    blk_start = jnp.arange(nb, dtype=I32) * EXPERT_ROWS
    block_e = jnp.minimum(jnp.sum((pad_end[None, :] <= blk_start[:, None]).astype(I32), axis=1), N_EXPERTS - 1)
    n_used = pad_end[-1:] // EXPERT_ROWS
    dest = jnp.pad(dest, ((0, 0), (0, tile)))
    return row_tok, block_e, n_used, gate, dest[0], dest[1]


def kernel(x_prompt, x_sample, c, cache_attn_k, cache_attn_v, state_mlstm_C, state_mlstm_n, state_mlstm_m, state_rwkv, c_ctx, norm1_g, norm2_g, w_mod, b_mod, w_in, w_out, attn_q_norm, attn_k_norm, mlstm_i_bias, mlstm_f_bias, mlstm_norm_g, rwkv_mu, rwkv_w0, rwkv_w_up, rwkv_a0, rwkv_a_up, rwkv_g_up, rwkv_k_k, rwkv_k_a, rwkv_r_k, rwkv_ln_g, rwkv_ln_b, router_w, router_b, exp_w1, exp_w3, exp_w2):
    b_ctx, s_ctx, d = x_prompt.shape
    b_lat, s_lat, _ = x_sample.shape
    depth = w_in.shape[0]
    t_ctx = b_ctx * s_ctx
    ntok = t_ctx + b_lat * s_lat
    assert b_lat + 1 <= 8 and s_lat % 1024 == 0 and t_ctx % 1024 == 0 and t_ctx % s_lat == 0
    past = cache_attn_k.shape[2]

    x = (x_prompt.reshape(t_ctx, d), x_sample.reshape(b_lat * s_lat, d))
    c_all = jnp.zeros((8, d), F32).at[0].set(c_ctx).at[1:1 + b_lat].set(c)
    mod = _modulation(c_all, w_mod, b_mod)
    cos, sin = _rope_tables(s_lat)
    rw_t = router_w.T
    rb = router_b.reshape(N_EXPERTS, 1)
    r_width = R_HEADS * R_HD
    n_in = w_in.shape[2]
    gate_hi = COL_R + 4 * M_HEADS
    w_in_t = jnp.swapaxes(w_in, 1, 2)
    w_tail_t = jnp.concatenate([w_in_t[:, gate_hi:], w_in_t[:, COL_R:gate_hi],
                                jnp.zeros((depth, N_IN_PAD - n_in, d), F32)], axis=1).astype(BF16)

    ks, vs, cs, ns, ms, rs = [], [], [], [], [], []
    for l in range(depth):
        mod3 = mod[l].reshape(8, 1, 6 * d)
        u = _in_proj(x[0], x[1], norm1_g[l][None], mod3, w_in_t, l, w_tail_t[l], s_lat)

        ck = cache_attn_k[:, l].reshape(b_lat, past, ATT_KV_HEADS * ATT_HD)
        cv = cache_attn_v[:, l].reshape(b_lat, past, ATT_KV_HEADS * ATT_HD)
        att_c, att_l, k_ctx, v_ctx = _attention(u, attn_q_norm[l][None], attn_k_norm[l][None], ck, cv, cos, sin,
                                                b_ctx, s_ctx, b_lat, s_lat)
        ks.append(k_ctx.reshape(b_ctx, s_ctx, ATT_KV_HEADS, ATT_HD))
        vs.append(v_ctx.reshape(b_ctx, s_ctx, ATT_KV_HEADS, ATT_HD))

        gcol = u[:, COL_GATE:COL_GATE + 4 * M_HEADS]
        gt = gcol.reshape(ntok // CHUNK, CHUNK, 4 * M_HEADS).transpose(0, 2, 1)
        bias = jnp.stack([mlstm_i_bias[l], mlstm_f_bias[l]], axis=1)
        ng = mlstm_norm_g[l][None]
        m_c_out, cx_c, m_c = _mlstm(u, gcol, gt, bias, ng, None, 0, b_ctx, s_ctx, MLSTM_SEQS_PER_STEP_CTX)
        n_col = jnp.pad(state_mlstm_n[:, l][..., None], ((0, 0),) * 4 + ((0, M_HD - 1),))
        lat_states = (jnp.concatenate([state_mlstm_C[:, l], n_col], axis=-1),
                      jnp.broadcast_to(state_mlstm_m[:, l][..., None, None], (b_lat, 2, M_HEADS, 1, M_HD)))
        m_l_out, _, _ = _mlstm(u, gcol, gt, bias, ng, lat_states, t_ctx // s_lat, b_lat, s_lat, 1)
        cs.append(cx_c[..., :M_HD])
        ns.append(cx_c[..., M_HD])
        ms.append(m_c[:, :, :, 0, 0])

        rp = dict(mu=rwkv_mu[l][None], w0=rwkv_w0[l].reshape(2, 1, r_width), w_up=rwkv_w_up[l],
                  a0=rwkv_a0[l].reshape(2, 1, r_width), a_up=rwkv_a_up[l], g_up=rwkv_g_up[l],
                  k_k=rwkv_k_k[l][None], k_a=rwkv_k_a[l][None], r_k=rwkv_r_k[l].reshape(1, r_width),
                  ln_g=rwkv_ln_g[l][None], ln_b=rwkv_ln_b[l][None])
        r_c_out, r_c = _rwkv(u, rp, None, 0, b_ctx, s_ctx, RWKV_HEADS_PER_STEP_CTX, RWKV_SEQS_PER_STEP_CTX)
        r_l_out, _ = _rwkv(u, rp, state_rwkv[:, l], t_ctx // s_lat, b_lat, s_lat, RWKV_HEADS_PER_STEP_LAT, 1)
        rs.append(r_c)

        x1, h2, route, counts = _out_proj((att_c, att_l), (m_c_out, m_l_out), (r_c_out, r_l_out), x,
                                          w_out[l].astype(BF16), mod3, norm2_g[l][None], rw_t, rb, s_lat)
        row_tok, block_e, n_used, gate, dest1, dest2 = _dispatch(route, counts, 256)
        y = _experts(h2, block_e, row_tok, n_used, exp_w1, exp_w3, exp_w2, l)
        x = _combine(y, x1, mod3, gate, dest1, dest2, t_ctx, s_lat)

    y_prompt = x[0].reshape(b_ctx, s_ctx, d)
    y_sample = x[1].reshape(b_lat, s_lat, d)
    return (y_prompt, y_sample, jnp.stack(ks, axis=1), jnp.stack(vs, axis=1), jnp.stack(cs, axis=1),
            jnp.stack(ns, axis=1), jnp.stack(ms, axis=1), jnp.stack(rs, axis=1))
```
